```python
import jax, jax.numpy as jnp
from jax import lax
import numpy as np

D_MODEL = 1024
BATCH = 8
SEQ = 2048
DEPTH = 2

N_MIXERS = 2
N_A = (DEPTH + 1) // 2
N_B = DEPTH // 2
CONF_KERNEL = 31
CONF_EXPAND = 2
SHORT_KERNEL = 3
N_EXPERTS = 32
TOP_K = 4
D_EXPERT = D_MODEL
SWIGLU_ALPHA = 1.702
SWIGLU_LIMIT = 7.0
EXPERT_BLOCK = 128
NORM_EPS = 1e-5
N_MOD = 6

kernel_name = "hybrid_conformer_shortconv_moe_adaln"


def rms_norm(x, g):
    xf = x.astype(jnp.float32)
    y = xf * lax.rsqrt(jnp.mean(xf * xf, axis=-1, keepdims=True) + NORM_EPS)
    return (y * g.astype(jnp.float32)).astype(x.dtype)


def layer_norm(x, g, b):
    xf = x.astype(jnp.float32)
    mu = jnp.mean(xf, axis=-1, keepdims=True)
    var = jnp.mean(jnp.square(xf - mu), axis=-1, keepdims=True)
    y = (xf - mu) * lax.rsqrt(var + NORM_EPS)
    return (y * g.astype(jnp.float32) + b.astype(jnp.float32)).astype(x.dtype)


def causal_depthwise_conv(x, w):
    k = w.shape[0]
    return lax.conv_general_dilated(
        x, w[:, None, :].astype(x.dtype), window_strides=(1,),
        padding=[(k - 1, 0)], dimension_numbers=("NWC", "WIO", "NWC"),
        feature_group_count=x.shape[-1])


def conformer_conv(h, w1, b1, dw, dw_b, ln_g, ln_b, w2, b2):
    u = h @ w1 + b1
    a, g = jnp.split(u, 2, axis=-1)
    u = a * jax.nn.sigmoid(g)
    u = causal_depthwise_conv(u, dw) + dw_b
    u = jax.nn.silu(layer_norm(u, ln_g, ln_b))
    return u @ w2 + b2


def short_gated_conv(h, w_in, conv_w, w_out):
    z = h @ w_in
    b_gate, c_gate, v = jnp.split(z, 3, axis=-1)
    y = b_gate * causal_depthwise_conv(c_gate * v, conv_w)
    return y @ w_out


def clamped_swiglu(u):
    u = u.reshape(u.shape[:-1] + (D_EXPERT, 2))
    x_glu = jnp.minimum(u[..., 0], SWIGLU_LIMIT)
    x_lin = jnp.clip(u[..., 1], -SWIGLU_LIMIT, SWIGLU_LIMIT)
    return x_glu * jax.nn.sigmoid(SWIGLU_ALPHA * x_glu) * (x_lin + 1.0)


def moe_ffn(h, router_w, router_b, w1, b1, w2, b2):
    bsz, s, d = h.shape
    t = bsz * s
    xt = h.reshape(t, d)
    logits = (xt @ router_w + router_b).astype(jnp.float32)
    top_logits, top_idx = lax.top_k(logits, TOP_K)
    gates = jax.nn.softmax(top_logits, axis=-1)
    tk = t * TOP_K
    flat_e = top_idx.reshape(tk)
    flat_tok = jnp.arange(tk, dtype=jnp.int32) // TOP_K
    flat_gate = gates.reshape(tk)
    order = jnp.argsort(flat_e)
    se, stok, sgate = flat_e[order], flat_tok[order], flat_gate[order]
    counts = jnp.zeros((N_EXPERTS,), jnp.int32).at[flat_e].add(1)
    padded = (counts + EXPERT_BLOCK - 1) // EXPERT_BLOCK * EXPERT_BLOCK
    starts = jnp.cumsum(counts) - counts
    pad_ends = jnp.cumsum(padded)
    pad_starts = pad_ends - padded
    dest = pad_starts[se] + jnp.arange(tk, dtype=jnp.int32) - starts[se]
    n_rows = tk + N_EXPERTS * EXPERT_BLOCK
    n_blocks = n_rows // EXPERT_BLOCK
    row_tok = jnp.zeros((n_rows,), jnp.int32).at[dest].set(stok)
    row_gate = jnp.zeros((n_rows,), jnp.float32).at[dest].set(sgate)
    block_start = jnp.arange(n_blocks, dtype=jnp.int32) * EXPERT_BLOCK
    block_e = jnp.minimum(jnp.searchsorted(pad_ends, block_start, side="right"),
                          N_EXPERTS - 1).astype(jnp.int32)
    xs = xt[row_tok].reshape(n_blocks, EXPERT_BLOCK, d)

    def expert_block(args):
        xb, e = args
        u = xb @ w1[e] + b1[e]
        return clamped_swiglu(u) @ w2[e] + b2[e]

    ys = lax.map(expert_block, (xs, block_e)).reshape(n_rows, d)
    y = jnp.zeros((t, d), ys.dtype).at[row_tok].add(ys * row_gate[:, None].astype(ys.dtype))
    return y.reshape(bsz, s, d)


def setup_inputs(seed: int = 0) -> dict:
    key = jax.random.key(seed)
    ks = jax.random.split(key, 24)
    d, f, e = D_MODEL, D_EXPERT, N_EXPERTS
    nrm = lambda k, shape, scale: jax.random.normal(k, shape, jnp.float32) * scale
    mod_offset = jnp.concatenate([jnp.zeros((2 * d,)), 0.5 * jnp.ones((d,)),
                                  jnp.zeros((2 * d,)), 0.5 * jnp.ones((d,))]).astype(jnp.float32)
    return {
        "x": nrm(ks[0], (BATCH, SEQ, d), 1.0),
        "c": nrm(ks[1], (BATCH, d), 1.0),
        "mod_w": nrm(ks[2], (DEPTH, d, N_MOD * d), 0.1 * d ** -0.5),
        "mod_b": mod_offset[None, :] + nrm(ks[3], (DEPTH, N_MOD * d), 0.02),
        "norm_g": 1.0 + nrm(ks[4], (DEPTH, 2, d), 0.02),
        "conf_w1": nrm(ks[5], (N_A, d, CONF_EXPAND * d), d ** -0.5),
        "conf_b1": nrm(ks[6], (N_A, CONF_EXPAND * d), 0.02),
        "conf_dw": nrm(ks[7], (N_A, CONF_KERNEL, d), CONF_KERNEL ** -0.5),
        "conf_dw_b": nrm(ks[8], (N_A, d), 0.02),
        "conf_ln_g": 1.0 + nrm(ks[9], (N_A, d), 0.02),
        "conf_ln_b": nrm(ks[10], (N_A, d), 0.02),
        "conf_w2": nrm(ks[11], (N_A, d, d), d ** -0.5),
        "conf_b2": nrm(ks[12], (N_A, d), 0.02),
        "sc_w_in": nrm(ks[13], (N_B, d, 3 * d), d ** -0.5),
        "sc_conv": nrm(ks[14], (N_B, SHORT_KERNEL, d), SHORT_KERNEL ** -0.5),
        "sc_w_out": nrm(ks[15], (N_B, d, d), d ** -0.5),
        "router_w": nrm(ks[16], (DEPTH, d, e), d ** -0.5),
        "router_b": nrm(ks[17], (DEPTH, e), 0.01),
        "exp_w1": nrm(ks[18], (DEPTH, e, d, 2 * f), d ** -0.5),
        "exp_b1": nrm(ks[19], (DEPTH, e, 2 * f), 0.02),
        "exp_w2": nrm(ks[20], (DEPTH, e, f, d), f ** -0.5),
        "exp_b2": nrm(ks[21], (DEPTH, e, d), 0.02),
        "final_g": 1.0 + nrm(ks[22], (d,), 0.02),
    }


def reference(x, c, mod_w, mod_b, norm_g, conf_w1, conf_b1, conf_dw, conf_dw_b,
              conf_ln_g, conf_ln_b, conf_w2, conf_b2, sc_w_in, sc_conv, sc_w_out,
              router_w, router_b, exp_w1, exp_b1, exp_w2, exp_b2, final_g):
    c_act = jax.nn.silu(c)
    for i in range(DEPTH):
        mod = (c_act @ mod_w[i] + mod_b[i])[:, None, :]
        sh1, sc1, g1, sh2, sc2, g2 = jnp.split(mod, N_MOD, axis=-1)
        h = rms_norm(x, norm_g[i, 0]) * (1.0 + sc1) + sh1
        j = i // N_MIXERS
        if i % N_MIXERS == 0:
            mix = conformer_conv(h, conf_w1[j], conf_b1[j], conf_dw[j], conf_dw_b[j],
                                 conf_ln_g[j], conf_ln_b[j], conf_w2[j], conf_b2[j])
        else:
            mix = short_gated_conv(h, sc_w_in[j], sc_conv[j], sc_w_out[j])
        x = x + g1 * mix
        h = rms_norm(x, norm_g[i, 1]) * (1.0 + sc2) + sh2
        x = x + g2 * moe_ffn(h, router_w[i], router_b[i], exp_w1[i], exp_b1[i],
                             exp_w2[i], exp_b2[i])
    return rms_norm(x, final_g)
```

```python
import functools

import jax
import jax.numpy as jnp
from jax import lax
from jax.experimental import pallas as pl
from jax.experimental.pallas import tpu as pltpu

D_MODEL = 1024
BATCH = 8
SEQ = 2048
DEPTH = 2
TOKENS = BATCH * SEQ
CONF_KERNEL = 31
SHORT_KERNEL = 3
N_EXPERTS = 32
TOP_K = 4
D_EXPERT = D_MODEL
SWIGLU_ALPHA = 1.702
SWIGLU_LIMIT = 7.0
NORM_EPS = 1e-5
N_MOD = 6

LANES = 128
SUBLANES = 8
MXU_COLS = 256
VMEM_LIMIT = 56 * 1024 * 1024

ROW_CHUNKS = D_MODEL // LANES
MIX_TS = 256
CONF_HALO = 32
SHORT_HALO = 8
CONV_RC = 64
ROUTE_TM = 512
DISP_TM = 512
COMB_TM = 256
EXPERT_BM = 256
N_ROWS = TOKENS * TOP_K + N_EXPERTS * EXPERT_BM
N_BLOCKS = N_ROWS // EXPERT_BM

_F32 = jnp.float32
_BF16 = jnp.bfloat16


def _rms_mod(x, g, scale, shift):
    y = x * lax.rsqrt(jnp.mean(x * x, axis=-1, keepdims=True) + NORM_EPS)
    return (y * g) * (1.0 + scale) + shift


def _sigmoid(x):
    return 1.0 / (1.0 + jnp.exp(-x))


def _mod_kernel(c_ref, w_ref, b_ref, o_ref):
    c = c_ref[...]
    c_act = c * _sigmoid(c)
    o_ref[0] = jnp.dot(c_act, w_ref[0], preferred_element_type=_F32,
                       precision=lax.Precision.HIGHEST) + b_ref[0]


def _modulation(c, mod_w, mod_b):
    tn = 1536
    n = N_MOD * D_MODEL
    return pl.pallas_call(
        _mod_kernel,
        grid=(DEPTH, n // tn),
        in_specs=[
            pl.BlockSpec((BATCH, D_MODEL), lambda i, j: (0, 0)),
            pl.BlockSpec((1, D_MODEL, tn), lambda i, j: (i, 0, j)),
            pl.BlockSpec((1, 1, tn), lambda i, j: (i, 0, j)),
        ],
        out_specs=pl.BlockSpec((1, BATCH, tn), lambda i, j: (i, 0, j)),
        out_shape=jax.ShapeDtypeStruct((DEPTH, BATCH, n), _F32),
        compiler_params=pltpu.CompilerParams(vmem_limit_bytes=VMEM_LIMIT),
        name="modulation",
    )(c, mod_w, mod_b.reshape(DEPTH, 1, n))


def _causal_taps(cbuf, w_ref, bias_row, out_ref, ntaps, halo, ts):
    off0 = halo - (ntaps - 1)
    for c in range(ROW_CHUNKS):
        cs = slice(c * LANES, (c + 1) * LANES)
        for r in range(ts // CONV_RC):
            r0 = r * CONV_RC
            acc = None
            for k in range(ntaps):
                term = w_ref[k:k + 1, cs] * cbuf[r0 + off0 + k:r0 + off0 + k + CONV_RC, cs]
                acc = term if acc is None else acc + term
            if bias_row is not None:
                acc = acc + bias_row[:, cs]
            out_ref[r0:r0 + CONV_RC, cs] = acc


def _conformer_kernel(x_ref, mod_ref, ng_ref, w1_ref, b1_ref, dw_ref, dwb_ref, lng_ref, lnb_ref,
                      w2_ref, b2_ref, o_ref, cbuf, vbuf):
    ts = MIX_TS

    @pl.when(pl.program_id(1) == 0)
    def _():
        cbuf[0:CONF_HALO, :] = jnp.zeros((CONF_HALO, D_MODEL), _F32)

    x = x_ref[0]
    m = mod_ref[0]
    h = _rms_mod(x, ng_ref[...], m[1:2], m[0:1])
    u = jnp.dot(h.astype(_BF16), w1_ref[...], preferred_element_type=_F32) + b1_ref[...]
    cbuf[CONF_HALO:CONF_HALO + ts, :] = u[:, :D_MODEL] * _sigmoid(u[:, D_MODEL:])
    _causal_taps(cbuf, dw_ref, dwb_ref[...], vbuf, CONF_KERNEL, CONF_HALO, ts)
    cbuf[0:CONF_HALO, :] = cbuf[ts:ts + CONF_HALO, :]
    v = vbuf[...]
    mu = jnp.mean(v, axis=-1, keepdims=True)
    vc = v - mu
    var = jnp.mean(vc * vc, axis=-1, keepdims=True)
    y = vc * lax.rsqrt(var + NORM_EPS) * lng_ref[...] + lnb_ref[...]
    y = y * _sigmoid(y)
    mix = jnp.dot(y.astype(_BF16), w2_ref[...], preferred_element_type=_F32) + b2_ref[...]
    o_ref[0] = x + m[2:3] * mix


def _conformer_mixer(x, mod, norm_g, w1, b1, dw, dw_b, ln_g, ln_b, w2, b2):
    d = D_MODEL
    row = lambda a: a.reshape(1, -1)
    const = lambda shape: pl.BlockSpec(shape, lambda b, s: (0,) * len(shape))
    return pl.pallas_call(
        _conformer_kernel,
        grid=(BATCH, SEQ // MIX_TS),
        in_specs=[
            pl.BlockSpec((1, MIX_TS, d), lambda b, s: (b, s, 0)),
            pl.BlockSpec((1, N_MOD, d), lambda b, s: (b, 0, 0)),
            const((1, d)), const((d, 2 * d)), const((1, 2 * d)), const((CONF_KERNEL, d)),
            const((1, d)), const((1, d)), const((1, d)), const((d, d)), const((1, d)),
        ],
        out_specs=pl.BlockSpec((1, MIX_TS, d), lambda b, s: (b, s, 0)),
        out_shape=jax.ShapeDtypeStruct((BATCH, SEQ, d), _F32),
        scratch_shapes=[pltpu.VMEM((CONF_HALO + MIX_TS, d), _F32), pltpu.VMEM((MIX_TS, d), _F32)],
        compiler_params=pltpu.CompilerParams(
            dimension_semantics=("arbitrary", "arbitrary"), vmem_limit_bytes=VMEM_LIMIT),
        name="conformer_mixer",
    )(x, mod, row(norm_g), w1.astype(_BF16), row(b1), dw, row(dw_b), row(ln_g), row(ln_b),
      w2.astype(_BF16), row(b2))


def _shortconv_kernel(x_ref, mod_ref, ng_ref, win_ref, cw_ref, wout_ref, o_ref, cbuf, vbuf):
    ts = MIX_TS
    d = D_MODEL

    @pl.when(pl.program_id(1) == 0)
    def _():
        cbuf[0:SHORT_HALO, :] = jnp.zeros((SHORT_HALO, d), _F32)

    x = x_ref[0]
    m = mod_ref[0]
    h = _rms_mod(x, ng_ref[...], m[1:2], m[0:1])
    z = jnp.dot(h.astype(_BF16), win_ref[...], preferred_element_type=_F32)
    cbuf[SHORT_HALO:SHORT_HALO + ts, :] = z[:, d:2 * d] * z[:, 2 * d:]
    _causal_taps(cbuf, cw_ref, None, vbuf, SHORT_KERNEL, SHORT_HALO, ts)
    cbuf[0:SHORT_HALO, :] = cbuf[ts:ts + SHORT_HALO, :]
    y = z[:, :d] * vbuf[...]
    mix = jnp.dot(y.astype(_BF16), wout_ref[...], preferred_element_type=_F32)
    o_ref[0] = x + m[2:3] * mix


def _shortconv_mixer(x, mod, norm_g, w_in, conv_w, w_out):
    d = D_MODEL
    const = lambda shape: pl.BlockSpec(shape, lambda b, s: (0,) * len(shape))
    return pl.pallas_call(
        _shortconv_kernel,
        grid=(BATCH, SEQ // MIX_TS),
        in_specs=[
            pl.BlockSpec((1, MIX_TS, d), lambda b, s: (b, s, 0)),
            pl.BlockSpec((1, N_MOD, d), lambda b, s: (b, 0, 0)),
            const((1, d)), const((d, 3 * d)), const((SHORT_KERNEL, d)), const((d, d)),
        ],
        out_specs=pl.BlockSpec((1, MIX_TS, d), lambda b, s: (b, s, 0)),
        out_shape=jax.ShapeDtypeStruct((BATCH, SEQ, d), _F32),
        scratch_shapes=[pltpu.VMEM((SHORT_HALO + MIX_TS, d), _F32), pltpu.VMEM((MIX_TS, d), _F32)],
        compiler_params=pltpu.CompilerParams(
            dimension_semantics=("arbitrary", "arbitrary"), vmem_limit_bytes=VMEM_LIMIT),
        name="shortconv_mixer",
    )(x, mod, norm_g.reshape(1, d), w_in.astype(_BF16), conv_w, w_out.astype(_BF16))


def _route_kernel(x_ref, mod_ref, ng_ref, rw_ref, rb_ref, h_ref, idx_ref, rank_ref, gate_ref, cnt_ref):
    tm = ROUTE_TM

    @pl.when(pl.program_id(0) == 0)
    def _():
        cnt_ref[...] = jnp.zeros((1, N_EXPERTS), _F32)

    x = x_ref[...]
    m = mod_ref[0]
    h = _rms_mod(x, ng_ref[...], m[4:5], m[3:4])
    for j in range(ROW_CHUNKS):
        h_ref[pl.ds(j, tm, stride=ROW_CHUNKS), :] = h[:, j * LANES:(j + 1) * LANES]

    logits = jnp.dot(h, rw_ref[...], preferred_element_type=_F32,
                     precision=lax.Precision.HIGHEST) + rb_ref[...]
    lane = lax.broadcasted_iota(jnp.int32, (tm, N_EXPERTS), 1).astype(_F32)
    work = logits
    vals, idxs, hots = [], [], []
    for _ in range(TOP_K):
        top = jnp.max(work, axis=1, keepdims=True)
        pick = jnp.min(jnp.where(work == top, lane, float(N_EXPERTS)), axis=1, keepdims=True)
        hot = lane == pick
        vals.append(top)
        idxs.append(pick)
        hots.append(hot)
        work = jnp.where(hot, -jnp.inf, work)
    exps = [jnp.exp(v - vals[0]) for v in vals]
    denom = exps[0] + exps[1] + exps[2] + exps[3]

    multi = (hots[0] | hots[1] | hots[2] | hots[3]).astype(_BF16)
    r_i = lax.broadcasted_iota(jnp.int32, (tm, tm), 0)
    c_i = lax.broadcasted_iota(jnp.int32, (tm, tm), 1)
    before = (c_i < r_i).astype(_BF16)
    pos = cnt_ref[...] + jnp.dot(before, multi, preferred_element_type=_F32)
    cnt_ref[...] = cnt_ref[...] + jnp.sum(multi.astype(_F32), axis=0, keepdims=True)

    col = lax.broadcasted_iota(jnp.int32, (tm, TOP_K), 1)
    idx_out = jnp.zeros((tm, TOP_K), _F32)
    rank_out = jnp.zeros((tm, TOP_K), _F32)
    gate_out = jnp.zeros((tm, TOP_K), _F32)
    for k in range(TOP_K):
        rank_k = jnp.sum(jnp.where(hots[k], pos, 0.0), axis=1, keepdims=True)
        idx_out = jnp.where(col == k, idxs[k], idx_out)
        rank_out = jnp.where(col == k, rank_k, rank_out)
        gate_out = jnp.where(col == k, exps[k] / denom, gate_out)
    idx_ref[...] = idx_out.astype(jnp.int32)
    rank_ref[...] = rank_out.astype(jnp.int32)
    gate_ref[...] = gate_out


def _route(x1, mod, norm_g, router_w, router_b):
    d = D_MODEL
    tm = ROUTE_TM
    tiles_per_batch = SEQ // tm
    return pl.pallas_call(
        _route_kernel,
        grid=(TOKENS // tm,),
        in_specs=[
            pl.BlockSpec((tm, d), lambda i: (i, 0)),
            pl.BlockSpec((1, N_MOD, d), lambda i: (i // tiles_per_batch, 0, 0)),
            pl.BlockSpec((1, d), lambda i: (0, 0)),
            pl.BlockSpec((d, N_EXPERTS), lambda i: (0, 0)),
            pl.BlockSpec((1, N_EXPERTS), lambda i: (0, 0)),
        ],
        out_specs=[
            pl.BlockSpec((tm * ROW_CHUNKS, LANES), lambda i: (i, 0)),
            pl.BlockSpec((tm, TOP_K), lambda i: (i, 0)),
            pl.BlockSpec((tm, TOP_K), lambda i: (i, 0)),
            pl.BlockSpec((tm, TOP_K), lambda i: (i, 0)),
            pl.BlockSpec((1, N_EXPERTS), lambda i: (0, 0)),
        ],
        out_shape=[
            jax.ShapeDtypeStruct((TOKENS * ROW_CHUNKS, LANES), _F32),
            jax.ShapeDtypeStruct((TOKENS, TOP_K), jnp.int32),
            jax.ShapeDtypeStruct((TOKENS, TOP_K), jnp.int32),
            jax.ShapeDtypeStruct((TOKENS, TOP_K), _F32),
            jax.ShapeDtypeStruct((1, N_EXPERTS), _F32),
        ],
        compiler_params=pltpu.CompilerParams(
            dimension_semantics=("arbitrary",), vmem_limit_bytes=VMEM_LIMIT),
        name="route",
    )(x1, mod, norm_g.reshape(1, d), router_w, router_b.reshape(1, N_EXPERTS))


def _routing_tables(idx, rank, counts):
    counts = counts.reshape(N_EXPERTS).astype(jnp.int32)
    padded = (counts + EXPERT_BM - 1) // EXPERT_BM * EXPERT_BM
    pad_ends = jnp.cumsum(padded)
    pad_starts = pad_ends - padded
    hot = idx[..., None] == jnp.arange(N_EXPERTS, dtype=jnp.int32)
    dest = jnp.sum(jnp.where(hot, pad_starts, 0), axis=-1) + rank
    block_start = jnp.arange(N_BLOCKS, dtype=jnp.int32) * EXPERT_BM
    block_e = jnp.sum((block_start[:, None] >= pad_ends[None, :]).astype(jnp.int32), axis=1)
    block_e = jnp.minimum(block_e, N_EXPERTS - 1).astype(jnp.int32)
    n_used = (pad_ends[-1:] // EXPERT_BM).astype(jnp.int32)
    return dest.reshape(TOKENS * TOP_K).astype(jnp.int32), block_e, n_used


def _row_copy(src, src_row, dst, dst_row, sem):
    return pltpu.make_async_copy(
        src.at[pl.ds(pl.multiple_of(src_row * ROW_CHUNKS, ROW_CHUNKS), ROW_CHUNKS)],
        dst.at[pl.ds(pl.multiple_of(dst_row * ROW_CHUNKS, ROW_CHUNKS), ROW_CHUNKS)],
        sem)


def _dispatch_kernel(dest_ref, h_ref, xs_in_ref, xs_ref, sem):
    del xs_in_ref
    tm = DISP_TM
    base = pl.program_id(0) * tm

    def issue(t, carry):
        for k in range(TOP_K):
            _row_copy(h_ref, t, xs_ref, dest_ref[(base + t) * TOP_K + k], sem).start()
        return carry

    lax.fori_loop(0, tm, issue, 0)
    for _ in range(TOP_K):
        pltpu.make_async_copy(h_ref, xs_ref.at[pl.ds(0, tm * ROW_CHUNKS)], sem).wait()


def _dispatch(dest, h_rows):
    tm = DISP_TM
    xs_init = jnp.zeros((N_ROWS * ROW_CHUNKS, LANES), _F32)
    return pl.pallas_call(
        _dispatch_kernel,
        grid_spec=pltpu.PrefetchScalarGridSpec(
            num_scalar_prefetch=1,
            grid=(TOKENS // tm,),
            in_specs=[
                pl.BlockSpec((tm * ROW_CHUNKS, LANES), lambda i, dest: (i, 0)),
                pl.BlockSpec(memory_space=pl.ANY),
            ],
            out_specs=pl.BlockSpec(memory_space=pl.ANY),
            scratch_shapes=[pltpu.SemaphoreType.DMA],
        ),
        out_shape=jax.ShapeDtypeStruct((N_ROWS * ROW_CHUNKS, LANES), _F32),
        input_output_aliases={2: 0},
        compiler_params=pltpu.CompilerParams(
            dimension_semantics=("arbitrary",), vmem_limit_bytes=VMEM_LIMIT),
        name="dispatch",
    )(dest, h_rows, xs_init)


def _expert_kernel(be_ref, nu_ref, xs_ref, w1_ref, b1_ref, w2_ref, b2_ref, ys_ref, w1s, w2s):
    bm = EXPERT_BM
    b = pl.program_id(0)
    e = be_ref[b]
    half = MXU_COLS // 2

    @pl.when(b < nu_ref[0])
    def _():
        @pl.when(jnp.logical_or(b == 0, be_ref[jnp.maximum(b - 1, 0)] != e))
        def _():
            src = lax.broadcasted_iota(jnp.int32, (MXU_COLS, MXU_COLS), 0)
            dst = lax.broadcasted_iota(jnp.int32, (MXU_COLS, MXU_COLS), 1)
            want = jnp.where(dst < half, 2 * dst, 2 * (dst - half) + 1)
            perm = (src == want).astype(_BF16)
            for blk in range(2 * D_EXPERT // MXU_COLS):
                cs = slice(blk * MXU_COLS, (blk + 1) * MXU_COLS)
                w1s[:, cs] = jnp.dot(w1_ref[0, 0, :, cs].astype(_BF16), perm,
                                     preferred_element_type=_F32).astype(_BF16)
            w2s[...] = w2_ref[0, 0].astype(_BF16)

        x = jnp.concatenate(
            [xs_ref[pl.ds(j, bm, stride=ROW_CHUNKS), :] for j in range(ROW_CHUNKS)], axis=1).astype(_BF16)
        u = jnp.dot(x, w1s[...], preferred_element_type=_F32) + b1_ref[0, 0]
        acts = []
        for blk in range(2 * D_EXPERT // MXU_COLS):
            glu = jnp.minimum(u[:, blk * MXU_COLS:blk * MXU_COLS + half], SWIGLU_LIMIT)
            lin = jnp.clip(u[:, blk * MXU_COLS + half:(blk + 1) * MXU_COLS], -SWIGLU_LIMIT, SWIGLU_LIMIT)
            acts.append((glu * _sigmoid(SWIGLU_ALPHA * glu) * (lin + 1.0)).astype(_BF16))
        hidden = jnp.concatenate(acts, axis=1)
        y = jnp.dot(hidden, w2s[...], preferred_element_type=_F32) + b2_ref[0, 0]
        for j in range(ROW_CHUNKS):
            ys_ref[pl.ds(j, bm, stride=ROW_CHUNKS), :] = y[:, j * LANES:(j + 1) * LANES]

    @pl.when(b >= nu_ref[0])
    def _():
        ys_ref[...] = jnp.zeros((bm * ROW_CHUNKS, LANES), _F32)


def _experts(layer, block_e, n_used, xs, w1, b1_grouped, w2, b2):
    bm = EXPERT_BM
    d, f = D_MODEL, D_EXPERT
    return pl.pallas_call(
        _expert_kernel,
        grid_spec=pltpu.PrefetchScalarGridSpec(
            num_scalar_prefetch=2,
            grid=(N_BLOCKS,),
            in_specs=[
                pl.BlockSpec((bm * ROW_CHUNKS, LANES), lambda b, be, nu: (b, 0)),
                pl.BlockSpec((1, 1, d, 2 * f), lambda b, be, nu: (layer, be[b], 0, 0)),
                pl.BlockSpec((1, 1, 1, 2 * f), lambda b, be, nu: (layer, be[b], 0, 0)),
                pl.BlockSpec((1, 1, f, d), lambda b, be, nu: (layer, be[b], 0, 0)),
                pl.BlockSpec((1, 1, 1, d), lambda b, be, nu: (layer, be[b], 0, 0)),
            ],
            out_specs=pl.BlockSpec((bm * ROW_CHUNKS, LANES), lambda b, be, nu: (b, 0)),
            scratch_shapes=[pltpu.VMEM((d, 2 * f), _BF16), pltpu.VMEM((f, d), _BF16)],
        ),
        out_shape=jax.ShapeDtypeStruct((N_ROWS * ROW_CHUNKS, LANES), _F32),
        compiler_params=pltpu.CompilerParams(
            dimension_semantics=("arbitrary",), vmem_limit_bytes=VMEM_LIMIT),
        name="experts",
    )(block_e, n_used, xs, w1, b1_grouped, w2, b2)


def _group_glu_columns(b1):
    half = MXU_COLS // 2
    j = jnp.arange(MXU_COLS)
    want = jnp.where(j < half, 2 * j, 2 * (j - half) + 1)
    cols = (jnp.arange(2 * D_EXPERT // MXU_COLS)[:, None] * MXU_COLS + want[None, :]).reshape(-1)
    return b1[..., cols]


def _combine_kernel(dest_ref, ys_ref, x_ref, gate_ref, mod_ref, fg_ref, o_ref, buf, sem, *, final):
    tm = COMB_TM
    base = pl.program_id(0) * tm

    def issue(t, carry):
        for k in range(TOP_K):
            _row_copy(ys_ref, dest_ref[(base + t) * TOP_K + k], buf, k * tm + t, sem).start()
        return carry

    lax.fori_loop(0, tm, issue, 0)
    for k in range(TOP_K):
        pltpu.make_async_copy(ys_ref.at[pl.ds(0, tm * ROW_CHUNKS)],
                              buf.at[pl.ds(k * tm * ROW_CHUNKS, tm * ROW_CHUNKS)], sem).wait()

    gates = gate_ref[...]
    y = None
    for k in range(TOP_K):
        rows = jnp.concatenate(
            [buf[pl.ds(k * tm * ROW_CHUNKS + j, tm, stride=ROW_CHUNKS), :] for j in range(ROW_CHUNKS)],
            axis=1)
        term = gates[:, k:k + 1] * rows
        y = term if y is None else y + term
    out = x_ref[...] + mod_ref[0][5:6] * y
    if final:
        out = out * lax.rsqrt(jnp.mean(out * out, axis=-1, keepdims=True) + NORM_EPS) * fg_ref[...]
    o_ref[...] = out


def _combine(dest, ys, x1, gates, mod, final_g, final):
    tm = COMB_TM
    d = D_MODEL
    tiles_per_batch = SEQ // tm
    return pl.pallas_call(
        functools.partial(_combine_kernel, final=final),
        grid_spec=pltpu.PrefetchScalarGridSpec(
            num_scalar_prefetch=1,
            grid=(TOKENS // tm,),
            in_specs=[
                pl.BlockSpec(memory_space=pl.ANY),
                pl.BlockSpec((tm, d), lambda i, dest: (i, 0)),
                pl.BlockSpec((tm, TOP_K), lambda i, dest: (i, 0)),
                pl.BlockSpec((1, N_MOD, d), lambda i, dest: (i // tiles_per_batch, 0, 0)),
                pl.BlockSpec((1, d), lambda i, dest: (0, 0)),
            ],
            out_specs=pl.BlockSpec((tm, d), lambda i, dest: (i, 0)),
            scratch_shapes=[pltpu.VMEM((TOP_K * tm * ROW_CHUNKS, LANES), _F32), pltpu.SemaphoreType.DMA],
        ),
        out_shape=jax.ShapeDtypeStruct((TOKENS, d), _F32),
        compiler_params=pltpu.CompilerParams(
            dimension_semantics=("arbitrary",), vmem_limit_bytes=VMEM_LIMIT),
        name="combine_final" if final else "combine",
    )(dest, ys, x1, gates, mod, final_g.reshape(1, d))


def kernel(x, c, mod_w, mod_b, norm_g, conf_w1, conf_b1, conf_dw, conf_dw_b, conf_ln_g, conf_ln_b, conf_w2, conf_b2, sc_w_in, sc_conv, sc_w_out, router_w, router_b, exp_w1, exp_b1, exp_w2, exp_b2, final_g):
    mods = _modulation(c, mod_w, mod_b).reshape(DEPTH, BATCH, N_MOD, D_MODEL)
    b1_grouped = _group_glu_columns(exp_b1).reshape(DEPTH, N_EXPERTS, 1, 2 * D_EXPERT)
    b2_rows = exp_b2.reshape(DEPTH, N_EXPERTS, 1, D_MODEL)
    xcur = x
    for i in range(DEPTH):
        mod = mods[i]
        if i % 2 == 0:
            j = i // 2
            x1 = _conformer_mixer(xcur, mod, norm_g[i, 0], conf_w1[j], conf_b1[j], conf_dw[j], conf_dw_b[j],
                                  conf_ln_g[j], conf_ln_b[j], conf_w2[j], conf_b2[j])
        else:
            j = i // 2
            x1 = _shortconv_mixer(xcur, mod, norm_g[i, 0], sc_w_in[j], sc_conv[j], sc_w_out[j])
        x1 = x1.reshape(TOKENS, D_MODEL)
        h_rows, idx, rank, gates, counts = _route(x1, mod, norm_g[i, 1], router_w[i], router_b[i])
        dest, block_e, n_used = _routing_tables(idx, rank, counts)
        xs = _dispatch(dest, h_rows)
        ys = _experts(i, block_e, n_used, xs, exp_w1, b1_grouped, exp_w2, b2_rows)
        x2 = _combine(dest, ys, x1, gates, mod, final_g, final=(i == DEPTH - 1))
        xcur = x2.reshape(BATCH, SEQ, D_MODEL)
    return xcur
```

```python
import functools

import jax
import jax.numpy as jnp
from jax import lax
from jax.experimental import pallas as pl
from jax.experimental.pallas import tpu as pltpu

D_MODEL = 1024
BATCH = 8
SEQ = 2048
DEPTH = 2
TOKENS = BATCH * SEQ
CONF_KERNEL = 31
SHORT_KERNEL = 3
N_EXPERTS = 32
TOP_K = 4
D_EXPERT = D_MODEL
SWIGLU_ALPHA = 1.702
SWIGLU_LIMIT = 7.0
NORM_EPS = 1e-5
N_MOD = 6

LANES = 128
SUBLANES = 8
MXU_COLS = 256
VMEM_LIMIT = 56 * 1024 * 1024

ROW_CHUNKS = D_MODEL // LANES
MIX_TS = 256
CONF_HALO = 32
SHORT_HALO = 8
CONV_RC = 64
ROUTE_TM = 512
DISP_TM = 512
COMB_TM = 256
EXPERT_BM = 256
N_ROWS = TOKENS * TOP_K + N_EXPERTS * EXPERT_BM
N_BLOCKS = N_ROWS // EXPERT_BM

_F32 = jnp.float32
_BF16 = jnp.bfloat16


def _rms_mod(x, g, scale, shift):
    y = x * lax.rsqrt(jnp.mean(x * x, axis=-1, keepdims=True) + NORM_EPS)
    return (y * g) * (1.0 + scale) + shift


def _sigmoid(x):
    return 1.0 / (1.0 + jnp.exp(-x))


def _mod_kernel(c_ref, w_ref, b_ref, o_ref):
    c = c_ref[...]
    c_act = c * _sigmoid(c)
    o_ref[0] = jnp.dot(c_act, w_ref[0], preferred_element_type=_F32,
                       precision=lax.Precision.HIGHEST) + b_ref[0]


def _modulation(c, mod_w, mod_b):
    tn = 1536
    n = N_MOD * D_MODEL
    return pl.pallas_call(
        _mod_kernel,
        grid=(DEPTH, n // tn),
        in_specs=[
            pl.BlockSpec((BATCH, D_MODEL), lambda i, j: (0, 0)),
            pl.BlockSpec((1, D_MODEL, tn), lambda i, j: (i, 0, j)),
            pl.BlockSpec((1, 1, tn), lambda i, j: (i, 0, j)),
        ],
        out_specs=pl.BlockSpec((1, BATCH, tn), lambda i, j: (i, 0, j)),
        out_shape=jax.ShapeDtypeStruct((DEPTH, BATCH, n), _F32),
        compiler_params=pltpu.CompilerParams(vmem_limit_bytes=VMEM_LIMIT),
        name="modulation",
    )(c, mod_w, mod_b.reshape(DEPTH, 1, n))


def _causal_taps(cbuf, shifted, w_ref, bias_row, out_ref, ntaps, halo, ts):
    off0 = halo - (ntaps - 1)
    for c in range(ROW_CHUNKS):
        cs = slice(c * LANES, (c + 1) * LANES)
        if shifted is not None:
            rows = cbuf.shape[0]
            whole = cbuf[:, cs]
            for r in range(1, SUBLANES):
                shifted[r - 1, :, cs] = pltpu.roll(whole, rows - r, axis=0)
        for rc in range(ts // CONV_RC):
            r0 = rc * CONV_RC
            acc = None
            for k in range(ntaps):
                s = off0 + k
                if shifted is None or s % SUBLANES == 0:
                    window = cbuf[r0 + s:r0 + s + CONV_RC, cs]
                else:
                    a = s - s % SUBLANES
                    window = shifted[s % SUBLANES - 1, r0 + a:r0 + a + CONV_RC, cs]
                term = w_ref[k:k + 1, cs] * window
                acc = term if acc is None else acc + term
            if bias_row is not None:
                acc = acc + bias_row[:, cs]
            out_ref[r0:r0 + CONV_RC, cs] = acc


def _conformer_kernel(x_ref, mod_ref, ng_ref, w1_ref, b1_ref, dw_ref, dwb_ref, lng_ref, lnb_ref,
                      w2_ref, b2_ref, o_ref, cbuf, shifted, vbuf):
    ts = MIX_TS

    @pl.when(pl.program_id(1) == 0)
    def _():
        cbuf[0:CONF_HALO, :] = jnp.zeros((CONF_HALO, D_MODEL), _F32)
        cbuf[CONF_HALO + ts:, :] = jnp.zeros((SUBLANES, D_MODEL), _F32)

    x = x_ref[0]
    m = mod_ref[0]
    h = _rms_mod(x, ng_ref[...], m[1:2], m[0:1])
    u = jnp.dot(h.astype(_BF16), w1_ref[...], preferred_element_type=_F32) + b1_ref[...]
    cbuf[CONF_HALO:CONF_HALO + ts, :] = u[:, :D_MODEL] * _sigmoid(u[:, D_MODEL:])
    _causal_taps(cbuf, shifted, dw_ref, dwb_ref[...], vbuf, CONF_KERNEL, CONF_HALO, ts)
    cbuf[0:CONF_HALO, :] = cbuf[ts:ts + CONF_HALO, :]
    v = vbuf[...]
    mu = jnp.mean(v, axis=-1, keepdims=True)
    vc = v - mu
    var = jnp.mean(vc * vc, axis=-1, keepdims=True)
    y = vc * lax.rsqrt(var + NORM_EPS) * lng_ref[...] + lnb_ref[...]
    y = y * _sigmoid(y)
    mix = jnp.dot(y.astype(_BF16), w2_ref[...], preferred_element_type=_F32) + b2_ref[...]
    o_ref[0] = x + m[2:3] * mix


def _conformer_mixer(x, mod, norm_g, w1, b1, dw, dw_b, ln_g, ln_b, w2, b2):
    d = D_MODEL
    row = lambda a: a.reshape(1, -1)
    const = lambda shape: pl.BlockSpec(shape, lambda b, s: (0,) * len(shape))
    return pl.pallas_call(
        _conformer_kernel,
        grid=(BATCH, SEQ // MIX_TS),
        in_specs=[
            pl.BlockSpec((1, MIX_TS, d), lambda b, s: (b, s, 0)),
            pl.BlockSpec((1, N_MOD, d), lambda b, s: (b, 0, 0)),
            const((1, d)), const((d, 2 * d)), const((1, 2 * d)), const((CONF_KERNEL, d)),
            const((1, d)), const((1, d)), const((1, d)), const((d, d)), const((1, d)),
        ],
        out_specs=pl.BlockSpec((1, MIX_TS, d), lambda b, s: (b, s, 0)),
        out_shape=jax.ShapeDtypeStruct((BATCH, SEQ, d), _F32),
        scratch_shapes=[pltpu.VMEM((CONF_HALO + MIX_TS + SUBLANES, d), _F32),
                        pltpu.VMEM((SUBLANES - 1, CONF_HALO + MIX_TS + SUBLANES, d), _F32),
                        pltpu.VMEM((MIX_TS, d), _F32)],
        compiler_params=pltpu.CompilerParams(
            dimension_semantics=("arbitrary", "arbitrary"), vmem_limit_bytes=VMEM_LIMIT),
        name="conformer_mixer",
    )(x, mod, row(norm_g), w1.astype(_BF16), row(b1), dw, row(dw_b), row(ln_g), row(ln_b),
      w2.astype(_BF16), row(b2))


def _shortconv_kernel(x_ref, mod_ref, ng_ref, win_ref, cw_ref, wout_ref, o_ref, cbuf, vbuf):
    ts = MIX_TS
    d = D_MODEL

    @pl.when(pl.program_id(1) == 0)
    def _():
        cbuf[0:SHORT_HALO, :] = jnp.zeros((SHORT_HALO, d), _F32)

    x = x_ref[0]
    m = mod_ref[0]
    h = _rms_mod(x, ng_ref[...], m[1:2], m[0:1])
    z = jnp.dot(h.astype(_BF16), win_ref[...], preferred_element_type=_F32)
    cbuf[SHORT_HALO:SHORT_HALO + ts, :] = z[:, d:2 * d] * z[:, 2 * d:]
    _causal_taps(cbuf, None, cw_ref, None, vbuf, SHORT_KERNEL, SHORT_HALO, ts)
    cbuf[0:SHORT_HALO, :] = cbuf[ts:ts + SHORT_HALO, :]
    y = z[:, :d] * vbuf[...]
    mix = jnp.dot(y.astype(_BF16), wout_ref[...], preferred_element_type=_F32)
    o_ref[0] = x + m[2:3] * mix


def _shortconv_mixer(x, mod, norm_g, w_in, conv_w, w_out):
    d = D_MODEL
    const = lambda shape: pl.BlockSpec(shape, lambda b, s: (0,) * len(shape))
    return pl.pallas_call(
        _shortconv_kernel,
        grid=(BATCH, SEQ // MIX_TS),
        in_specs=[
            pl.BlockSpec((1, MIX_TS, d), lambda b, s: (b, s, 0)),
            pl.BlockSpec((1, N_MOD, d), lambda b, s: (b, 0, 0)),
            const((1, d)), const((d, 3 * d)), const((SHORT_KERNEL, d)), const((d, d)),
        ],
        out_specs=pl.BlockSpec((1, MIX_TS, d), lambda b, s: (b, s, 0)),
        out_shape=jax.ShapeDtypeStruct((BATCH, SEQ, d), _F32),
        scratch_shapes=[pltpu.VMEM((SHORT_HALO + MIX_TS, d), _F32), pltpu.VMEM((MIX_TS, d), _F32)],
        compiler_params=pltpu.CompilerParams(
            dimension_semantics=("arbitrary", "arbitrary"), vmem_limit_bytes=VMEM_LIMIT),
        name="shortconv_mixer",
    )(x, mod, norm_g.reshape(1, d), w_in.astype(_BF16), conv_w, w_out.astype(_BF16))


def _route_kernel(x_ref, mod_ref, ng_ref, rw_ref, rb_ref, h_ref, idx_ref, rank_ref, gate_ref, cnt_ref):
    tm = ROUTE_TM

    @pl.when(pl.program_id(0) == 0)
    def _():
        cnt_ref[...] = jnp.zeros((1, N_EXPERTS), _F32)

    x = x_ref[...]
    m = mod_ref[0]
    h = _rms_mod(x, ng_ref[...], m[4:5], m[3:4])
    for j in range(ROW_CHUNKS):
        h_ref[pl.ds(j, tm, stride=ROW_CHUNKS), :] = h[:, j * LANES:(j + 1) * LANES]

    logits = jnp.dot(h, rw_ref[...], preferred_element_type=_F32,
                     precision=lax.Precision.HIGHEST) + rb_ref[...]
    lane = lax.broadcasted_iota(jnp.int32, (tm, N_EXPERTS), 1).astype(_F32)
    work = logits
    vals, idxs, hots = [], [], []
    for _ in range(TOP_K):
        top = jnp.max(work, axis=1, keepdims=True)
        pick = jnp.min(jnp.where(work == top, lane, float(N_EXPERTS)), axis=1, keepdims=True)
        hot = lane == pick
        vals.append(top)
        idxs.append(pick)
        hots.append(hot)
        work = jnp.where(hot, -jnp.inf, work)
    exps = [jnp.exp(v - vals[0]) for v in vals]
    denom = exps[0] + exps[1] + exps[2] + exps[3]

    multi = (hots[0] | hots[1] | hots[2] | hots[3]).astype(_BF16)
    r_i = lax.broadcasted_iota(jnp.int32, (tm, tm), 0)
    c_i = lax.broadcasted_iota(jnp.int32, (tm, tm), 1)
    before = (c_i < r_i).astype(_BF16)
    pos = cnt_ref[...] + jnp.dot(before, multi, preferred_element_type=_F32)
    cnt_ref[...] = cnt_ref[...] + jnp.sum(multi.astype(_F32), axis=0, keepdims=True)

    col = lax.broadcasted_iota(jnp.int32, (tm, TOP_K), 1)
    idx_out = jnp.zeros((tm, TOP_K), _F32)
    rank_out = jnp.zeros((tm, TOP_K), _F32)
    gate_out = jnp.zeros((tm, TOP_K), _F32)
    for k in range(TOP_K):
        rank_k = jnp.sum(jnp.where(hots[k], pos, 0.0), axis=1, keepdims=True)
        idx_out = jnp.where(col == k, idxs[k], idx_out)
        rank_out = jnp.where(col == k, rank_k, rank_out)
        gate_out = jnp.where(col == k, exps[k] / denom, gate_out)
    idx_ref[...] = idx_out.astype(jnp.int32)
    rank_ref[...] = rank_out.astype(jnp.int32)
    gate_ref[...] = gate_out


def _route(x1, mod, norm_g, router_w, router_b):
    d = D_MODEL
    tm = ROUTE_TM
    tiles_per_batch = SEQ // tm
    return pl.pallas_call(
        _route_kernel,
        grid=(TOKENS // tm,),
        in_specs=[
            pl.BlockSpec((tm, d), lambda i: (i, 0)),
            pl.BlockSpec((1, N_MOD, d), lambda i: (i // tiles_per_batch, 0, 0)),
            pl.BlockSpec((1, d), lambda i: (0, 0)),
            pl.BlockSpec((d, N_EXPERTS), lambda i: (0, 0)),
            pl.BlockSpec((1, N_EXPERTS), lambda i: (0, 0)),
        ],
        out_specs=[
            pl.BlockSpec((tm * ROW_CHUNKS, LANES), lambda i: (i, 0)),
            pl.BlockSpec((tm, TOP_K), lambda i: (i, 0)),
            pl.BlockSpec((tm, TOP_K), lambda i: (i, 0)),
            pl.BlockSpec((tm, TOP_K), lambda i: (i, 0)),
            pl.BlockSpec((1, N_EXPERTS), lambda i: (0, 0)),
        ],
        out_shape=[
            jax.ShapeDtypeStruct((TOKENS * ROW_CHUNKS, LANES), _F32),
            jax.ShapeDtypeStruct((TOKENS, TOP_K), jnp.int32),
            jax.ShapeDtypeStruct((TOKENS, TOP_K), jnp.int32),
            jax.ShapeDtypeStruct((TOKENS, TOP_K), _F32),
            jax.ShapeDtypeStruct((1, N_EXPERTS), _F32),
        ],
        compiler_params=pltpu.CompilerParams(
            dimension_semantics=("arbitrary",), vmem_limit_bytes=VMEM_LIMIT),
        name="route",
    )(x1, mod, norm_g.reshape(1, d), router_w, router_b.reshape(1, N_EXPERTS))


def _routing_tables(idx, rank, counts):
    counts = counts.reshape(N_EXPERTS).astype(jnp.int32)
    padded = (counts + EXPERT_BM - 1) // EXPERT_BM * EXPERT_BM
    pad_ends = jnp.cumsum(padded)
    pad_starts = pad_ends - padded
    hot = idx[..., None] == jnp.arange(N_EXPERTS, dtype=jnp.int32)
    dest = jnp.sum(jnp.where(hot, pad_starts, 0), axis=-1) + rank
    block_start = jnp.arange(N_BLOCKS, dtype=jnp.int32) * EXPERT_BM
    block_e = jnp.sum((block_start[:, None] >= pad_ends[None, :]).astype(jnp.int32), axis=1)
    block_e = jnp.minimum(block_e, N_EXPERTS - 1).astype(jnp.int32)
    n_used = (pad_ends[-1:] // EXPERT_BM).astype(jnp.int32)
    ids = jnp.arange(N_EXPERTS, dtype=jnp.int32)
    later_used = (ids[None, :] > ids[:, None]) & (counts[None, :] > 0)
    next_used = jnp.min(jnp.where(later_used, ids[None, :], N_EXPERTS), axis=1).astype(jnp.int32)
    pad_lo = (pad_starts + counts).astype(jnp.int32)
    return (dest.reshape(TOKENS * TOP_K).astype(jnp.int32), block_e, n_used, next_used,
            pad_lo, pad_ends.astype(jnp.int32))


def _row_copy(src, src_row, dst, dst_row, sem):
    return pltpu.make_async_copy(
        src.at[pl.ds(pl.multiple_of(src_row * ROW_CHUNKS, ROW_CHUNKS), ROW_CHUNKS)],
        dst.at[pl.ds(pl.multiple_of(dst_row * ROW_CHUNKS, ROW_CHUNKS), ROW_CHUNKS)],
        sem)


def _dispatch_kernel(dest_ref, lo_ref, hi_ref, h_ref, xs_ref, zblk, sem, zsem):
    tm = DISP_TM
    base = pl.program_id(0) * tm

    def issue(t, carry):
        for k in range(TOP_K):
            _row_copy(h_ref, t, xs_ref, dest_ref[(base + t) * TOP_K + k], sem).start(priority=k % 2)
        return carry

    lax.fori_loop(0, tm, issue, 0)

    @pl.when(pl.program_id(0) == pl.num_programs(0) - 1)
    def _():
        zblk[...] = jnp.zeros((EXPERT_BM * ROW_CHUNKS, LANES), _F32)

        def pad_copy(r):
            return _row_copy(zblk, 0, xs_ref, r, zsem)

        def tail_copy(b):
            rows = EXPERT_BM * ROW_CHUNKS
            return pltpu.make_async_copy(zblk, xs_ref.at[pl.ds(pl.multiple_of(b * rows, rows), rows)], zsem)

        first_unused = hi_ref[N_EXPERTS - 1] // EXPERT_BM
        for e in range(N_EXPERTS):
            lax.fori_loop(lo_ref[e], hi_ref[e], lambda r, c: (pad_copy(r).start(), c)[1], 0)
        lax.fori_loop(first_unused, N_BLOCKS, lambda b, c: (tail_copy(b).start(), c)[1], 0)
        for e in range(N_EXPERTS):
            lax.fori_loop(lo_ref[e], hi_ref[e], lambda r, c: (pad_copy(r).wait(), c)[1], 0)
        lax.fori_loop(first_unused, N_BLOCKS, lambda b, c: (tail_copy(b).wait(), c)[1], 0)

    for _ in range(TOP_K):
        pltpu.make_async_copy(h_ref, xs_ref.at[pl.ds(0, tm * ROW_CHUNKS)], sem).wait()


def _dispatch(dest, pad_lo, pad_hi, h_rows):
    tm = DISP_TM
    return pl.pallas_call(
        _dispatch_kernel,
        grid_spec=pltpu.PrefetchScalarGridSpec(
            num_scalar_prefetch=3,
            grid=(TOKENS // tm,),
            in_specs=[pl.BlockSpec((tm * ROW_CHUNKS, LANES), lambda i, *_: (i, 0))],
            out_specs=pl.BlockSpec(memory_space=pl.ANY),
            scratch_shapes=[pltpu.VMEM((EXPERT_BM * ROW_CHUNKS, LANES), _F32),
                            pltpu.SemaphoreType.DMA, pltpu.SemaphoreType.DMA],
        ),
        out_shape=jax.ShapeDtypeStruct((N_ROWS * ROW_CHUNKS, LANES), _F32),
        compiler_params=pltpu.CompilerParams(
            dimension_semantics=("arbitrary",), vmem_limit_bytes=VMEM_LIMIT),
        name="dispatch",
    )(dest, pad_lo, pad_hi, h_rows)


def _expert_kernel(be_ref, nu_ref, nx_ref, xs_ref, w1_hbm, b1_ref, w2_hbm, b2_ref, ys_ref,
                   w1f, w2f, w1s, w2s, wsem, nsw, *, layer):
    bm = EXPERT_BM
    b = pl.program_id(0)
    e = be_ref[b]
    half = MXU_COLS // 2

    def fetch(expert, slot):
        return (pltpu.make_async_copy(w1_hbm.at[layer, expert], w1f.at[slot], wsem.at[0, slot]),
                pltpu.make_async_copy(w2_hbm.at[layer, expert], w2f.at[slot], wsem.at[1, slot]))

    @pl.when(b == 0)
    def _():
        nsw[0] = 0
        for copy in fetch(e, 0):
            copy.start()

    @pl.when(b < nu_ref[0])
    def _():
        @pl.when(jnp.logical_or(b == 0, be_ref[jnp.maximum(b - 1, 0)] != e))
        def _():
            slot = nsw[0] % 2
            for copy in fetch(e, slot):
                copy.wait()
            src = lax.broadcasted_iota(jnp.int32, (MXU_COLS, MXU_COLS), 0)
            dst = lax.broadcasted_iota(jnp.int32, (MXU_COLS, MXU_COLS), 1)
            want = jnp.where(dst < half, 2 * dst, 2 * (dst - half) + 1)
            perm = (src == want).astype(_BF16)
            for blk in range(2 * D_EXPERT // MXU_COLS):
                cs = slice(blk * MXU_COLS, (blk + 1) * MXU_COLS)
                w1s[:, cs] = jnp.dot(w1f[slot, :, cs].astype(_BF16), perm,
                                     preferred_element_type=_F32).astype(_BF16)
            w2s[...] = w2f[slot].astype(_BF16)
            nsw[0] = nsw[0] + 1
            nxt = nx_ref[e]

            @pl.when(nxt < N_EXPERTS)
            def _():
                for copy in fetch(nxt, 1 - slot):
                    copy.start()

        x = jnp.concatenate(
            [xs_ref[pl.ds(j, bm, stride=ROW_CHUNKS), :] for j in range(ROW_CHUNKS)], axis=1).astype(_BF16)
        u = jnp.dot(x, w1s[...], preferred_element_type=_F32) + b1_ref[0, 0]
        acts = []
        for blk in range(2 * D_EXPERT // MXU_COLS):
            glu = jnp.minimum(u[:, blk * MXU_COLS:blk * MXU_COLS + half], SWIGLU_LIMIT)
            lin = jnp.clip(u[:, blk * MXU_COLS + half:(blk + 1) * MXU_COLS], -SWIGLU_LIMIT, SWIGLU_LIMIT)
            acts.append((glu * _sigmoid(SWIGLU_ALPHA * glu) * (lin + 1.0)).astype(_BF16))
        hidden = jnp.concatenate(acts, axis=1)
        y = jnp.dot(hidden, w2s[...], preferred_element_type=_F32) + b2_ref[0, 0]
        for j in range(ROW_CHUNKS):
            ys_ref[pl.ds(j, bm, stride=ROW_CHUNKS), :] = y[:, j * LANES:(j + 1) * LANES]

    @pl.when(b >= nu_ref[0])
    def _():
        ys_ref[...] = jnp.zeros((bm * ROW_CHUNKS, LANES), _F32)


def _experts(layer, block_e, n_used, next_used, xs, w1, b1_grouped, w2, b2):
    bm = EXPERT_BM
    d, f = D_MODEL, D_EXPERT
    blk = lambda b, be, nu, nx: jnp.minimum(b, nu[0] - 1)
    return pl.pallas_call(
        functools.partial(_expert_kernel, layer=layer),
        grid_spec=pltpu.PrefetchScalarGridSpec(
            num_scalar_prefetch=3,
            grid=(N_BLOCKS,),
            in_specs=[
                pl.BlockSpec((bm * ROW_CHUNKS, LANES), lambda *a: (blk(*a), 0)),
                pl.BlockSpec(memory_space=pl.ANY),
                pl.BlockSpec((1, 1, 1, 2 * f), lambda *a: (layer, a[1][blk(*a)], 0, 0)),
                pl.BlockSpec(memory_space=pl.ANY),
                pl.BlockSpec((1, 1, 1, d), lambda *a: (layer, a[1][blk(*a)], 0, 0)),
            ],
            out_specs=pl.BlockSpec((bm * ROW_CHUNKS, LANES), lambda b, *_: (b, 0)),
            scratch_shapes=[
                pltpu.VMEM((2, d, 2 * f), _F32), pltpu.VMEM((2, f, d), _F32),
                pltpu.VMEM((d, 2 * f), _BF16), pltpu.VMEM((f, d), _BF16),
                pltpu.SemaphoreType.DMA((2, 2)), pltpu.SMEM((1,), jnp.int32),
            ],
        ),
        out_shape=jax.ShapeDtypeStruct((N_ROWS * ROW_CHUNKS, LANES), _F32),
        compiler_params=pltpu.CompilerParams(
            dimension_semantics=("arbitrary",), vmem_limit_bytes=VMEM_LIMIT),
        name="experts",
    )(block_e, n_used, next_used, xs, w1, b1_grouped, w2, b2)


def _group_glu_columns(b1):
    half = MXU_COLS // 2
    j = jnp.arange(MXU_COLS)
    want = jnp.where(j < half, 2 * j, 2 * (j - half) + 1)
    cols = (jnp.arange(2 * D_EXPERT // MXU_COLS)[:, None] * MXU_COLS + want[None, :]).reshape(-1)
    return b1[..., cols]


def _combine_kernel(dest_ref, ys_ref, x_ref, gate_ref, mod_ref, fg_ref, o_ref, buf, sem, *, final):
    tm = COMB_TM
    base = pl.program_id(0) * tm

    def issue(t, carry):
        for k in range(TOP_K):
            _row_copy(ys_ref, dest_ref[(base + t) * TOP_K + k], buf, k * tm + t, sem).start(priority=k % 2)
        return carry

    lax.fori_loop(0, tm, issue, 0)
    for k in range(TOP_K):
        pltpu.make_async_copy(ys_ref.at[pl.ds(0, tm * ROW_CHUNKS)],
                              buf.at[pl.ds(k * tm * ROW_CHUNKS, tm * ROW_CHUNKS)], sem).wait()

    gates = gate_ref[...]
    y = None
    for k in range(TOP_K):
        rows = jnp.concatenate(
            [buf[pl.ds(k * tm * ROW_CHUNKS + j, tm, stride=ROW_CHUNKS), :] for j in range(ROW_CHUNKS)],
            axis=1)
        term = gates[:, k:k + 1] * rows
        y = term if y is None else y + term
    out = x_ref[...] + mod_ref[0][5:6] * y
    if final:
        out = out * lax.rsqrt(jnp.mean(out * out, axis=-1, keepdims=True) + NORM_EPS) * fg_ref[...]
    o_ref[...] = out


def _combine(dest, ys, x1, gates, mod, final_g, final):
    tm = COMB_TM
    d = D_MODEL
    tiles_per_batch = SEQ // tm
    return pl.pallas_call(
        functools.partial(_combine_kernel, final=final),
        grid_spec=pltpu.PrefetchScalarGridSpec(
            num_scalar_prefetch=1,
            grid=(TOKENS // tm,),
            in_specs=[
                pl.BlockSpec(memory_space=pl.ANY),
                pl.BlockSpec((tm, d), lambda i, dest: (i, 0)),
                pl.BlockSpec((tm, TOP_K), lambda i, dest: (i, 0)),
                pl.BlockSpec((1, N_MOD, d), lambda i, dest: (i // tiles_per_batch, 0, 0)),
                pl.BlockSpec((1, d), lambda i, dest: (0, 0)),
            ],
            out_specs=pl.BlockSpec((tm, d), lambda i, dest: (i, 0)),
            scratch_shapes=[pltpu.VMEM((TOP_K * tm * ROW_CHUNKS, LANES), _F32), pltpu.SemaphoreType.DMA],
        ),
        out_shape=jax.ShapeDtypeStruct((TOKENS, d), _F32),
        compiler_params=pltpu.CompilerParams(
            dimension_semantics=("arbitrary",), vmem_limit_bytes=VMEM_LIMIT),
        name="combine_final" if final else "combine",
    )(dest, ys, x1, gates, mod, final_g.reshape(1, d))


def kernel(x, c, mod_w, mod_b, norm_g, conf_w1, conf_b1, conf_dw, conf_dw_b, conf_ln_g, conf_ln_b, conf_w2, conf_b2, sc_w_in, sc_conv, sc_w_out, router_w, router_b, exp_w1, exp_b1, exp_w2, exp_b2, final_g):
    mods = _modulation(c, mod_w, mod_b).reshape(DEPTH, BATCH, N_MOD, D_MODEL)
    b1_grouped = _group_glu_columns(exp_b1).reshape(DEPTH, N_EXPERTS, 1, 2 * D_EXPERT)
    b2_rows = exp_b2.reshape(DEPTH, N_EXPERTS, 1, D_MODEL)
    xcur = x
    for i in range(DEPTH):
        mod = mods[i]
        if i % 2 == 0:
            j = i // 2
            x1 = _conformer_mixer(xcur, mod, norm_g[i, 0], conf_w1[j], conf_b1[j], conf_dw[j], conf_dw_b[j],
                                  conf_ln_g[j], conf_ln_b[j], conf_w2[j], conf_b2[j])
        else:
            j = i // 2
            x1 = _shortconv_mixer(xcur, mod, norm_g[i, 0], sc_w_in[j], sc_conv[j], sc_w_out[j])
        x1 = x1.reshape(TOKENS, D_MODEL)
        h_rows, idx, rank, gates, counts = _route(x1, mod, norm_g[i, 1], router_w[i], router_b[i])
        dest, block_e, n_used, next_used, pad_lo, pad_hi = _routing_tables(idx, rank, counts)
        xs = _dispatch(dest, pad_lo, pad_hi, h_rows)
        ys = _experts(i, block_e, n_used, next_used, xs, exp_w1, b1_grouped, exp_w2, b2_rows)
        x2 = _combine(dest, ys, x1, gates, mod, final_g, final=(i == DEPTH - 1))
        xcur = x2.reshape(BATCH, SEQ, D_MODEL)
    return xcur
```

```python
import functools

import jax
import jax.numpy as jnp
from jax import lax
from jax.experimental import pallas as pl
from jax.experimental.pallas import tpu as pltpu

D_MODEL = 1024
BATCH = 8
SEQ = 2048
DEPTH = 2
TOKENS = BATCH * SEQ
CONF_KERNEL = 31
SHORT_KERNEL = 3
N_EXPERTS = 32
TOP_K = 4
D_EXPERT = D_MODEL
SWIGLU_ALPHA = 1.702
SWIGLU_LIMIT = 7.0
NORM_EPS = 1e-5
N_MOD = 6

LANES = 128
SUBLANES = 8
MXU_COLS = 256
VMEM_LIMIT = 56 * 1024 * 1024

ROW_CHUNKS = D_MODEL // LANES
MIX_TS = 256
CONF_HALO = 32
SHORT_HALO = 8
CONV_RC = 64
ROUTE_TM = 512
DISP_TM = 512
COMB_TM = 256
EXPERT_BM = 256
N_ROWS = TOKENS * TOP_K + N_EXPERTS * EXPERT_BM
N_BLOCKS = N_ROWS // EXPERT_BM
OUT_SLOTS = TOKENS * TOP_K + 2 * EXPERT_BM

_F32 = jnp.float32
_BF16 = jnp.bfloat16


def _rms_mod(x, g, scale, shift):
    y = x * lax.rsqrt(jnp.mean(x * x, axis=-1, keepdims=True) + NORM_EPS)
    return (y * g) * (1.0 + scale) + shift


def _sigmoid(x):
    return 1.0 / (1.0 + jnp.exp(-x))


def _mod_kernel(c_ref, w_ref, b_ref, o_ref):
    c = c_ref[...]
    c_act = c * _sigmoid(c)
    o_ref[0] = jnp.dot(c_act, w_ref[0], preferred_element_type=_F32,
                       precision=lax.Precision.HIGHEST) + b_ref[0]


def _modulation(c, mod_w, mod_b):
    tn = 1536
    n = N_MOD * D_MODEL
    return pl.pallas_call(
        _mod_kernel,
        grid=(DEPTH, n // tn),
        in_specs=[
            pl.BlockSpec((BATCH, D_MODEL), lambda i, j: (0, 0)),
            pl.BlockSpec((1, D_MODEL, tn), lambda i, j: (i, 0, j)),
            pl.BlockSpec((1, 1, tn), lambda i, j: (i, 0, j)),
        ],
        out_specs=pl.BlockSpec((1, BATCH, tn), lambda i, j: (i, 0, j)),
        out_shape=jax.ShapeDtypeStruct((DEPTH, BATCH, n), _F32),
        compiler_params=pltpu.CompilerParams(vmem_limit_bytes=VMEM_LIMIT),
        name="modulation",
    )(c, mod_w, mod_b.reshape(DEPTH, 1, n))


def _causal_taps(cbuf, shifted, w_ref, bias_row, out_ref, ntaps, halo, ts):
    off0 = halo - (ntaps - 1)
    for c in range(ROW_CHUNKS):
        cs = slice(c * LANES, (c + 1) * LANES)
        if shifted is not None:
            rows = cbuf.shape[0]
            whole = cbuf[:, cs]
            for r in range(1, SUBLANES):
                shifted[r - 1, :, cs] = pltpu.roll(whole, rows - r, axis=0)
        for rc in range(ts // CONV_RC):
            r0 = rc * CONV_RC
            acc = None
            for k in range(ntaps):
                s = off0 + k
                if shifted is None or s % SUBLANES == 0:
                    window = cbuf[r0 + s:r0 + s + CONV_RC, cs]
                else:
                    a = s - s % SUBLANES
                    window = shifted[s % SUBLANES - 1, r0 + a:r0 + a + CONV_RC, cs]
                term = w_ref[k:k + 1, cs] * window
                acc = term if acc is None else acc + term
            if bias_row is not None:
                acc = acc + bias_row[:, cs]
            out_ref[r0:r0 + CONV_RC, cs] = acc


def _conformer_kernel(x_ref, mod_ref, ng_ref, w1_ref, b1_ref, dw_ref, dwb_ref, lng_ref, lnb_ref,
                      w2_ref, b2_ref, o_ref, cbuf, shifted, vbuf):
    ts = MIX_TS

    @pl.when(pl.program_id(1) == 0)
    def _():
        cbuf[0:CONF_HALO, :] = jnp.zeros((CONF_HALO, D_MODEL), _F32)
        cbuf[CONF_HALO + ts:, :] = jnp.zeros((SUBLANES, D_MODEL), _F32)

    x = x_ref[0]
    m = mod_ref[0]
    h = _rms_mod(x, ng_ref[...], m[1:2], m[0:1])
    u = jnp.dot(h.astype(_BF16), w1_ref[...], preferred_element_type=_F32) + b1_ref[...]
    cbuf[CONF_HALO:CONF_HALO + ts, :] = u[:, :D_MODEL] * _sigmoid(u[:, D_MODEL:])
    _causal_taps(cbuf, shifted, dw_ref, dwb_ref[...], vbuf, CONF_KERNEL, CONF_HALO, ts)
    cbuf[0:CONF_HALO, :] = cbuf[ts:ts + CONF_HALO, :]
    v = vbuf[...]
    mu = jnp.mean(v, axis=-1, keepdims=True)
    vc = v - mu
    var = jnp.mean(vc * vc, axis=-1, keepdims=True)
    y = vc * lax.rsqrt(var + NORM_EPS) * lng_ref[...] + lnb_ref[...]
    y = y * _sigmoid(y)
    mix = jnp.dot(y.astype(_BF16), w2_ref[...], preferred_element_type=_F32) + b2_ref[...]
    o_ref[0] = x + m[2:3] * mix


def _conformer_mixer(x, mod, norm_g, w1, b1, dw, dw_b, ln_g, ln_b, w2, b2):
    d = D_MODEL
    row = lambda a: a.reshape(1, -1)
    const = lambda shape: pl.BlockSpec(shape, lambda b, s: (0,) * len(shape))
    return pl.pallas_call(
        _conformer_kernel,
        grid=(BATCH, SEQ // MIX_TS),
        in_specs=[
            pl.BlockSpec((1, MIX_TS, d), lambda b, s: (b, s, 0)),
            pl.BlockSpec((1, N_MOD, d), lambda b, s: (b, 0, 0)),
            const((1, d)), const((d, 2 * d)), const((1, 2 * d)), const((CONF_KERNEL, d)),
            const((1, d)), const((1, d)), const((1, d)), const((d, d)), const((1, d)),
        ],
        out_specs=pl.BlockSpec((1, MIX_TS, d), lambda b, s: (b, s, 0)),
        out_shape=jax.ShapeDtypeStruct((BATCH, SEQ, d), _F32),
        scratch_shapes=[pltpu.VMEM((CONF_HALO + MIX_TS + SUBLANES, d), _F32),
                        pltpu.VMEM((SUBLANES - 1, CONF_HALO + MIX_TS + SUBLANES, d), _F32),
                        pltpu.VMEM((MIX_TS, d), _F32)],
        compiler_params=pltpu.CompilerParams(
            dimension_semantics=("arbitrary", "arbitrary"), vmem_limit_bytes=VMEM_LIMIT),
        name="conformer_mixer",
    )(x, mod, row(norm_g), w1.astype(_BF16), row(b1), dw, row(dw_b), row(ln_g), row(ln_b),
      w2.astype(_BF16), row(b2))


def _shortconv_kernel(x_ref, mod_ref, ng_ref, win_ref, cw_ref, wout_ref, o_ref, cbuf, vbuf):
    ts = MIX_TS
    d = D_MODEL

    @pl.when(pl.program_id(1) == 0)
    def _():
        cbuf[0:SHORT_HALO, :] = jnp.zeros((SHORT_HALO, d), _F32)

    x = x_ref[0]
    m = mod_ref[0]
    h = _rms_mod(x, ng_ref[...], m[1:2], m[0:1])
    z = jnp.dot(h.astype(_BF16), win_ref[...], preferred_element_type=_F32)
    cbuf[SHORT_HALO:SHORT_HALO + ts, :] = z[:, d:2 * d] * z[:, 2 * d:]
    _causal_taps(cbuf, None, cw_ref, None, vbuf, SHORT_KERNEL, SHORT_HALO, ts)
    cbuf[0:SHORT_HALO, :] = cbuf[ts:ts + SHORT_HALO, :]
    y = z[:, :d] * vbuf[...]
    mix = jnp.dot(y.astype(_BF16), wout_ref[...], preferred_element_type=_F32)
    o_ref[0] = x + m[2:3] * mix


def _shortconv_mixer(x, mod, norm_g, w_in, conv_w, w_out):
    d = D_MODEL
    const = lambda shape: pl.BlockSpec(shape, lambda b, s: (0,) * len(shape))
    return pl.pallas_call(
        _shortconv_kernel,
        grid=(BATCH, SEQ // MIX_TS),
        in_specs=[
            pl.BlockSpec((1, MIX_TS, d), lambda b, s: (b, s, 0)),
            pl.BlockSpec((1, N_MOD, d), lambda b, s: (b, 0, 0)),
            const((1, d)), const((d, 3 * d)), const((SHORT_KERNEL, d)), const((d, d)),
        ],
        out_specs=pl.BlockSpec((1, MIX_TS, d), lambda b, s: (b, s, 0)),
        out_shape=jax.ShapeDtypeStruct((BATCH, SEQ, d), _F32),
        scratch_shapes=[pltpu.VMEM((SHORT_HALO + MIX_TS, d), _F32), pltpu.VMEM((MIX_TS, d), _F32)],
        compiler_params=pltpu.CompilerParams(
            dimension_semantics=("arbitrary", "arbitrary"), vmem_limit_bytes=VMEM_LIMIT),
        name="shortconv_mixer",
    )(x, mod, norm_g.reshape(1, d), w_in.astype(_BF16), conv_w, w_out.astype(_BF16))


def _route_kernel(x_ref, mod_ref, ng_ref, rw_ref, rb_ref, h_ref, idx_ref, rank_ref, gate_ref, cnt_ref):
    tm = ROUTE_TM

    @pl.when(pl.program_id(0) == 0)
    def _():
        cnt_ref[...] = jnp.zeros((1, N_EXPERTS), _F32)

    x = x_ref[...]
    m = mod_ref[0]
    h = _rms_mod(x, ng_ref[...], m[4:5], m[3:4])
    for j in range(ROW_CHUNKS):
        h_ref[pl.ds(j, tm, stride=ROW_CHUNKS), :] = h[:, j * LANES:(j + 1) * LANES]

    logits = jnp.dot(h, rw_ref[...], preferred_element_type=_F32,
                     precision=lax.Precision.HIGHEST) + rb_ref[...]
    lane = lax.broadcasted_iota(jnp.int32, (tm, N_EXPERTS), 1).astype(_F32)
    work = logits
    vals, idxs, hots = [], [], []
    for _ in range(TOP_K):
        top = jnp.max(work, axis=1, keepdims=True)
        pick = jnp.min(jnp.where(work == top, lane, float(N_EXPERTS)), axis=1, keepdims=True)
        hot = lane == pick
        vals.append(top)
        idxs.append(pick)
        hots.append(hot)
        work = jnp.where(hot, -jnp.inf, work)
    exps = [jnp.exp(v - vals[0]) for v in vals]
    denom = exps[0] + exps[1] + exps[2] + exps[3]

    multi = (hots[0] | hots[1] | hots[2] | hots[3]).astype(_BF16)
    r_i = lax.broadcasted_iota(jnp.int32, (tm, tm), 0)
    c_i = lax.broadcasted_iota(jnp.int32, (tm, tm), 1)
    before = (c_i < r_i).astype(_BF16)
    pos = cnt_ref[...] + jnp.dot(before, multi, preferred_element_type=_F32)
    cnt_ref[...] = cnt_ref[...] + jnp.sum(multi.astype(_F32), axis=0, keepdims=True)

    col = lax.broadcasted_iota(jnp.int32, (tm, TOP_K), 1)
    idx_out = jnp.zeros((tm, TOP_K), _F32)
    rank_out = jnp.zeros((tm, TOP_K), _F32)
    gate_out = jnp.zeros((tm, TOP_K), _F32)
    for k in range(TOP_K):
        rank_k = jnp.sum(jnp.where(hots[k], pos, 0.0), axis=1, keepdims=True)
        idx_out = jnp.where(col == k, idxs[k], idx_out)
        rank_out = jnp.where(col == k, rank_k, rank_out)
        gate_out = jnp.where(col == k, exps[k] / denom, gate_out)
    idx_ref[...] = idx_out.astype(jnp.int32)
    rank_ref[...] = rank_out.astype(jnp.int32)
    gate_ref[...] = gate_out


def _route(x1, mod, norm_g, router_w, router_b):
    d = D_MODEL
    tm = ROUTE_TM
    tiles_per_batch = SEQ // tm
    return pl.pallas_call(
        _route_kernel,
        grid=(TOKENS // tm,),
        in_specs=[
            pl.BlockSpec((tm, d), lambda i: (i, 0)),
            pl.BlockSpec((1, N_MOD, d), lambda i: (i // tiles_per_batch, 0, 0)),
            pl.BlockSpec((1, d), lambda i: (0, 0)),
            pl.BlockSpec((d, N_EXPERTS), lambda i: (0, 0)),
            pl.BlockSpec((1, N_EXPERTS), lambda i: (0, 0)),
        ],
        out_specs=[
            pl.BlockSpec((tm * ROW_CHUNKS, LANES), lambda i: (i, 0)),
            pl.BlockSpec((tm, TOP_K), lambda i: (i, 0)),
            pl.BlockSpec((tm, TOP_K), lambda i: (i, 0)),
            pl.BlockSpec((tm, TOP_K), lambda i: (i, 0)),
            pl.BlockSpec((1, N_EXPERTS), lambda i: (0, 0)),
        ],
        out_shape=[
            jax.ShapeDtypeStruct((TOKENS * ROW_CHUNKS, LANES), _F32),
            jax.ShapeDtypeStruct((TOKENS, TOP_K), jnp.int32),
            jax.ShapeDtypeStruct((TOKENS, TOP_K), jnp.int32),
            jax.ShapeDtypeStruct((TOKENS, TOP_K), _F32),
            jax.ShapeDtypeStruct((1, N_EXPERTS), _F32),
        ],
        compiler_params=pltpu.CompilerParams(
            dimension_semantics=("arbitrary",), vmem_limit_bytes=VMEM_LIMIT),
        name="route",
    )(x1, mod, norm_g.reshape(1, d), router_w, router_b.reshape(1, N_EXPERTS))


def _routing_tables(idx, rank, counts):
    counts = counts.reshape(N_EXPERTS).astype(jnp.int32)
    padded = (counts + EXPERT_BM - 1) // EXPERT_BM * EXPERT_BM
    pad_ends = jnp.cumsum(padded)
    pad_starts = pad_ends - padded
    hot = idx[..., None] == jnp.arange(N_EXPERTS, dtype=jnp.int32)
    dest = jnp.sum(jnp.where(hot, pad_starts, 0), axis=-1) + rank
    block_start = jnp.arange(N_BLOCKS, dtype=jnp.int32) * EXPERT_BM
    block_e = jnp.sum((block_start[:, None] >= pad_ends[None, :]).astype(jnp.int32), axis=1)
    block_e = jnp.minimum(block_e, N_EXPERTS - 1).astype(jnp.int32)
    n_used = (pad_ends[-1:] // EXPERT_BM).astype(jnp.int32)
    ids = jnp.arange(N_EXPERTS, dtype=jnp.int32)
    later_used = (ids[None, :] > ids[:, None]) & (counts[None, :] > 0)
    next_used = jnp.min(jnp.where(later_used, ids[None, :], N_EXPERTS), axis=1).astype(jnp.int32)
    pad_lo = (pad_starts + counts).astype(jnp.int32)
    dest = dest.reshape(TOKENS * TOP_K).astype(jnp.int32)
    slot = (jnp.arange(TOP_K, dtype=jnp.int32)[None, :] * TOKENS
            + jnp.arange(TOKENS, dtype=jnp.int32)[:, None]).reshape(TOKENS * TOP_K)
    dump = TOKENS * TOP_K + jnp.arange(N_ROWS + EXPERT_BM, dtype=jnp.int32) % EXPERT_BM
    row_slot = dump.at[dest + EXPERT_BM].set(slot, unique_indices=True, indices_are_sorted=False)
    return dest, block_e, n_used, next_used, pad_lo, pad_ends.astype(jnp.int32), row_slot


def _row_copy(src, src_row, dst, dst_row, sem):
    def tile(row):
        start = row * ROW_CHUNKS
        return pl.ds(start if isinstance(row, int) else pl.multiple_of(start, ROW_CHUNKS), ROW_CHUNKS)

    return pltpu.make_async_copy(src.at[tile(src_row)], dst.at[tile(dst_row)], sem)


def _dispatch_kernel(dest_ref, lo_ref, hi_ref, h_ref, xs_ref, zblk, sem, zsem):
    tm = DISP_TM
    base = pl.program_id(0) * tm

    def issue(t, carry):
        for k in range(TOP_K):
            _row_copy(h_ref, t, xs_ref, dest_ref[(base + t) * TOP_K + k], sem).start(priority=k % 2)
        return carry

    lax.fori_loop(0, tm, issue, 0)

    @pl.when(pl.program_id(0) == pl.num_programs(0) - 1)
    def _():
        zblk[...] = jnp.zeros((EXPERT_BM * ROW_CHUNKS, LANES), _F32)

        def pad_copy(r):
            return _row_copy(zblk, 0, xs_ref, r, zsem)

        def tail_copy(b):
            rows = EXPERT_BM * ROW_CHUNKS
            return pltpu.make_async_copy(zblk, xs_ref.at[pl.ds(pl.multiple_of(b * rows, rows), rows)], zsem)

        first_unused = hi_ref[N_EXPERTS - 1] // EXPERT_BM
        for e in range(N_EXPERTS):
            lax.fori_loop(lo_ref[e], hi_ref[e], lambda r, c: (pad_copy(r).start(), c)[1], 0)
        lax.fori_loop(first_unused, N_BLOCKS, lambda b, c: (tail_copy(b).start(), c)[1], 0)
        for e in range(N_EXPERTS):
            lax.fori_loop(lo_ref[e], hi_ref[e], lambda r, c: (pad_copy(r).wait(), c)[1], 0)
        lax.fori_loop(first_unused, N_BLOCKS, lambda b, c: (tail_copy(b).wait(), c)[1], 0)

    for _ in range(TOP_K):
        pltpu.make_async_copy(h_ref, xs_ref.at[pl.ds(0, tm * ROW_CHUNKS)], sem).wait()


def _dispatch(dest, pad_lo, pad_hi, h_rows):
    tm = DISP_TM
    return pl.pallas_call(
        _dispatch_kernel,
        grid_spec=pltpu.PrefetchScalarGridSpec(
            num_scalar_prefetch=3,
            grid=(TOKENS // tm,),
            in_specs=[pl.BlockSpec((tm * ROW_CHUNKS, LANES), lambda i, *_: (i, 0))],
            out_specs=pl.BlockSpec(memory_space=pl.ANY),
            scratch_shapes=[pltpu.VMEM((EXPERT_BM * ROW_CHUNKS, LANES), _F32),
                            pltpu.SemaphoreType.DMA, pltpu.SemaphoreType.DMA],
        ),
        out_shape=jax.ShapeDtypeStruct((N_ROWS * ROW_CHUNKS, LANES), _F32),
        compiler_params=pltpu.CompilerParams(
            dimension_semantics=("arbitrary",), vmem_limit_bytes=VMEM_LIMIT),
        name="dispatch",
    )(dest, pad_lo, pad_hi, h_rows)


def _expert_kernel(be_ref, nu_ref, nx_ref, rs_ref, xs_ref, w1_hbm, b1_ref, w2_hbm, b2_ref, ys_ref,
                   w1f, w2f, w1s, w2s, obuf, wsem, osem, nsw, *, layer):
    bm = EXPERT_BM
    b = pl.program_id(0)
    e = be_ref[b]
    n_used = nu_ref[0]
    half = MXU_COLS // 2

    def fetch(expert, slot):
        return (pltpu.make_async_copy(w1_hbm.at[layer, expert], w1f.at[slot], wsem.at[0, slot]),
                pltpu.make_async_copy(w2_hbm.at[layer, expert], w2f.at[slot], wsem.at[1, slot]))

    def scatter_start(step, slot):
        for i in range(bm):
            _row_copy(obuf.at[slot], i, ys_ref, rs_ref[step * bm + i], osem.at[slot]).start(priority=i % 2)

    def scatter_wait(slot):
        pltpu.make_async_copy(obuf.at[slot], ys_ref.at[pl.ds(0, bm * ROW_CHUNKS)], osem.at[slot]).wait()

    @pl.when(b == 0)
    def _():
        nsw[0] = 0
        obuf[1] = jnp.zeros((bm * ROW_CHUNKS, LANES), _F32)
        pltpu.make_async_copy(
            obuf.at[1], ys_ref.at[pl.ds((OUT_SLOTS - bm) * ROW_CHUNKS, bm * ROW_CHUNKS)], osem.at[0]).start()
        for copy in fetch(e, 0):
            copy.start()

    @pl.when(b == n_used)
    def _():
        scatter_start(b, 1 - b % 2)
        scatter_wait(b % 2)
        scatter_wait(1 - b % 2)

    @pl.when(b < n_used)
    def _():
        out_slot = b % 2

        @pl.when(jnp.logical_or(b == 0, be_ref[jnp.maximum(b - 1, 0)] != e))
        def _():
            slot = nsw[0] % 2
            for copy in fetch(e, slot):
                copy.wait()
            src = lax.broadcasted_iota(jnp.int32, (MXU_COLS, MXU_COLS), 0)
            dst = lax.broadcasted_iota(jnp.int32, (MXU_COLS, MXU_COLS), 1)
            want = jnp.where(dst < half, 2 * dst, 2 * (dst - half) + 1)
            perm = (src == want).astype(_BF16)
            for blk in range(2 * D_EXPERT // MXU_COLS):
                cs = slice(blk * MXU_COLS, (blk + 1) * MXU_COLS)
                w1s[:, cs] = jnp.dot(w1f[slot, :, cs].astype(_BF16), perm,
                                     preferred_element_type=_F32).astype(_BF16)
            w2s[...] = w2f[slot].astype(_BF16)
            nsw[0] = nsw[0] + 1
            nxt = nx_ref[e]

            @pl.when(nxt < N_EXPERTS)
            def _():
                for copy in fetch(nxt, 1 - slot):
                    copy.start()

        scatter_start(b, 1 - out_slot)
        x = jnp.concatenate(
            [xs_ref[pl.ds(j, bm, stride=ROW_CHUNKS), :] for j in range(ROW_CHUNKS)], axis=1).astype(_BF16)
        u = jnp.dot(x, w1s[...], preferred_element_type=_F32) + b1_ref[0, 0]
        acts = []
        for blk in range(2 * D_EXPERT // MXU_COLS):
            glu = jnp.minimum(u[:, blk * MXU_COLS:blk * MXU_COLS + half], SWIGLU_LIMIT)
            lin = jnp.clip(u[:, blk * MXU_COLS + half:(blk + 1) * MXU_COLS], -SWIGLU_LIMIT, SWIGLU_LIMIT)
            acts.append((glu * _sigmoid(SWIGLU_ALPHA * glu) * (lin + 1.0)).astype(_BF16))
        hidden = jnp.concatenate(acts, axis=1)
        y = jnp.dot(hidden, w2s[...], preferred_element_type=_F32) + b2_ref[0, 0]
        scatter_wait(out_slot)
        for j in range(ROW_CHUNKS):
            obuf[out_slot, pl.ds(j, bm, stride=ROW_CHUNKS), :] = y[:, j * LANES:(j + 1) * LANES]


def _experts(layer, block_e, n_used, next_used, row_slot, xs, w1, b1_grouped, w2, b2):
    bm = EXPERT_BM
    d, f = D_MODEL, D_EXPERT
    blk = lambda b, be, nu, nx, rs: jnp.minimum(b, nu[0] - 1)
    return pl.pallas_call(
        functools.partial(_expert_kernel, layer=layer),
        grid_spec=pltpu.PrefetchScalarGridSpec(
            num_scalar_prefetch=4,
            grid=(N_BLOCKS,),
            in_specs=[
                pl.BlockSpec((bm * ROW_CHUNKS, LANES), lambda *a: (blk(*a), 0)),
                pl.BlockSpec(memory_space=pl.ANY),
                pl.BlockSpec((1, 1, 1, 2 * f), lambda *a: (layer, a[1][blk(*a)], 0, 0)),
                pl.BlockSpec(memory_space=pl.ANY),
                pl.BlockSpec((1, 1, 1, d), lambda *a: (layer, a[1][blk(*a)], 0, 0)),
            ],
            out_specs=pl.BlockSpec(memory_space=pl.ANY),
            scratch_shapes=[
                pltpu.VMEM((2, d, 2 * f), _F32), pltpu.VMEM((2, f, d), _F32),
                pltpu.VMEM((d, 2 * f), _BF16), pltpu.VMEM((f, d), _BF16),
                pltpu.VMEM((2, bm * ROW_CHUNKS, LANES), _F32),
                pltpu.SemaphoreType.DMA((2, 2)), pltpu.SemaphoreType.DMA((2,)), pltpu.SMEM((1,), jnp.int32),
            ],
        ),
        out_shape=jax.ShapeDtypeStruct((OUT_SLOTS * ROW_CHUNKS, LANES), _F32),
        compiler_params=pltpu.CompilerParams(
            dimension_semantics=("arbitrary",), vmem_limit_bytes=VMEM_LIMIT),
        name="experts",
    )(block_e, n_used, next_used, row_slot, xs, w1, b1_grouped, w2, b2)


def _group_glu_columns(b1):
    half = MXU_COLS // 2
    j = jnp.arange(MXU_COLS)
    want = jnp.where(j < half, 2 * j, 2 * (j - half) + 1)
    cols = (jnp.arange(2 * D_EXPERT // MXU_COLS)[:, None] * MXU_COLS + want[None, :]).reshape(-1)
    return b1[..., cols]


def _combine_kernel(y0_ref, y1_ref, y2_ref, y3_ref, x_ref, gate_ref, mod_ref, fg_ref, o_ref, *, final):
    tm = COMB_TM
    gates = gate_ref[...]
    y = None
    for k, yk_ref in enumerate((y0_ref, y1_ref, y2_ref, y3_ref)):
        rows = jnp.concatenate(
            [yk_ref[pl.ds(j, tm, stride=ROW_CHUNKS), :] for j in range(ROW_CHUNKS)], axis=1)
        term = gates[:, k:k + 1] * rows
        y = term if y is None else y + term
    out = x_ref[...] + mod_ref[0][5:6] * y
    if final:
        out = out * lax.rsqrt(jnp.mean(out * out, axis=-1, keepdims=True) + NORM_EPS) * fg_ref[...]
    o_ref[...] = out


def _combine(ys, x1, gates, mod, final_g, final):
    tm = COMB_TM
    d = D_MODEL
    tiles = TOKENS // tm
    tiles_per_batch = SEQ // tm
    slot_spec = lambda k: pl.BlockSpec((tm * ROW_CHUNKS, LANES), lambda i: (k * tiles + i, 0))
    return pl.pallas_call(
        functools.partial(_combine_kernel, final=final),
        grid=(tiles,),
        in_specs=[
            slot_spec(0), slot_spec(1), slot_spec(2), slot_spec(3),
            pl.BlockSpec((tm, d), lambda i: (i, 0)),
            pl.BlockSpec((tm, TOP_K), lambda i: (i, 0)),
            pl.BlockSpec((1, N_MOD, d), lambda i: (i // tiles_per_batch, 0, 0)),
            pl.BlockSpec((1, d), lambda i: (0, 0)),
        ],
        out_specs=pl.BlockSpec((tm, d), lambda i: (i, 0)),
        out_shape=jax.ShapeDtypeStruct((TOKENS, d), _F32),
        compiler_params=pltpu.CompilerParams(
            dimension_semantics=("arbitrary",), vmem_limit_bytes=VMEM_LIMIT),
        name="combine_final" if final else "combine",
    )(ys, ys, ys, ys, x1, gates, mod, final_g.reshape(1, d))


def kernel(x, c, mod_w, mod_b, norm_g, conf_w1, conf_b1, conf_dw, conf_dw_b, conf_ln_g, conf_ln_b, conf_w2, conf_b2, sc_w_in, sc_conv, sc_w_out, router_w, router_b, exp_w1, exp_b1, exp_w2, exp_b2, final_g):
    mods = _modulation(c, mod_w, mod_b).reshape(DEPTH, BATCH, N_MOD, D_MODEL)
    b1_grouped = _group_glu_columns(exp_b1).reshape(DEPTH, N_EXPERTS, 1, 2 * D_EXPERT)
    b2_rows = exp_b2.reshape(DEPTH, N_EXPERTS, 1, D_MODEL)
    xcur = x
    for i in range(DEPTH):
        mod = mods[i]
        if i % 2 == 0:
            j = i // 2
            x1 = _conformer_mixer(xcur, mod, norm_g[i, 0], conf_w1[j], conf_b1[j], conf_dw[j], conf_dw_b[j],
                                  conf_ln_g[j], conf_ln_b[j], conf_w2[j], conf_b2[j])
        else:
            j = i // 2
            x1 = _shortconv_mixer(xcur, mod, norm_g[i, 0], sc_w_in[j], sc_conv[j], sc_w_out[j])
        x1 = x1.reshape(TOKENS, D_MODEL)
        h_rows, idx, rank, gates, counts = _route(x1, mod, norm_g[i, 1], router_w[i], router_b[i])
        dest, block_e, n_used, next_used, pad_lo, pad_hi, row_slot = _routing_tables(idx, rank, counts)
        xs = _dispatch(dest, pad_lo, pad_hi, h_rows)
        ys = _experts(i, block_e, n_used, next_used, row_slot, xs, exp_w1, b1_grouped, exp_w2, b2_rows)
        x2 = _combine(ys, x1, gates, mod, final_g, final=(i == DEPTH - 1))
        xcur = x2.reshape(BATCH, SEQ, D_MODEL)
    return xcur
```

```python
import functools

import jax
import jax.numpy as jnp
from jax import lax
from jax.experimental import pallas as pl
from jax.experimental.pallas import tpu as pltpu

D_MODEL = 1024
BATCH = 8
SEQ = 2048
DEPTH = 2
TOKENS = BATCH * SEQ
CONF_KERNEL = 31
SHORT_KERNEL = 3
N_EXPERTS = 32
TOP_K = 4
D_EXPERT = D_MODEL
SWIGLU_ALPHA = 1.702
SWIGLU_LIMIT = 7.0
NORM_EPS = 1e-5
N_MOD = 6

LANES = 128
SUBLANES = 8
MXU_COLS = 256
VMEM_LIMIT = 56 * 1024 * 1024

ROW_CHUNKS = D_MODEL // LANES
MIX_TS = 256
CONF_HALO = 32
SHORT_HALO = 8
CONV_RC = 64
ROUTE_TM = 512
DISP_TM = 512
COMB_TM = 256
EXPERT_BM = 256
N_ROWS = TOKENS * TOP_K + N_EXPERTS * EXPERT_BM
N_BLOCKS = N_ROWS // EXPERT_BM
OUT_SLOTS = TOKENS * TOP_K + 2 * EXPERT_BM

_F32 = jnp.float32
_BF16 = jnp.bfloat16


def _rms_mod(x, g, scale, shift):
    y = x * lax.rsqrt(jnp.mean(x * x, axis=-1, keepdims=True) + NORM_EPS)
    return (y * g) * (1.0 + scale) + shift


def _sigmoid(x):
    return 1.0 / (1.0 + jnp.exp(-x))


def _mod_kernel(c_ref, w_ref, b_ref, o_ref):
    c = c_ref[...]
    c_act = c * _sigmoid(c)
    o_ref[0] = jnp.dot(c_act, w_ref[0], preferred_element_type=_F32,
                       precision=lax.Precision.HIGHEST) + b_ref[0]


def _modulation(c, mod_w, mod_b):
    tn = 1536
    n = N_MOD * D_MODEL
    return pl.pallas_call(
        _mod_kernel,
        grid=(DEPTH, n // tn),
        in_specs=[
            pl.BlockSpec((BATCH, D_MODEL), lambda i, j: (0, 0)),
            pl.BlockSpec((1, D_MODEL, tn), lambda i, j: (i, 0, j)),
            pl.BlockSpec((1, 1, tn), lambda i, j: (i, 0, j)),
        ],
        out_specs=pl.BlockSpec((1, BATCH, tn), lambda i, j: (i, 0, j)),
        out_shape=jax.ShapeDtypeStruct((DEPTH, BATCH, n), _F32),
        compiler_params=pltpu.CompilerParams(vmem_limit_bytes=VMEM_LIMIT),
        name="modulation",
    )(c, mod_w, mod_b.reshape(DEPTH, 1, n))


def _causal_taps(cbuf, shifted, w_ref, bias_row, out_ref, ntaps, halo, ts):
    off0 = halo - (ntaps - 1)
    for c in range(ROW_CHUNKS):
        cs = slice(c * LANES, (c + 1) * LANES)
        if shifted is not None:
            rows = cbuf.shape[0]
            whole = cbuf[:, cs]
            for r in range(1, SUBLANES):
                shifted[r - 1, :, cs] = pltpu.roll(whole, rows - r, axis=0)
        for rc in range(ts // CONV_RC):
            r0 = rc * CONV_RC
            acc = None
            for k in range(ntaps):
                s = off0 + k
                if shifted is None or s % SUBLANES == 0:
                    window = cbuf[r0 + s:r0 + s + CONV_RC, cs]
                else:
                    a = s - s % SUBLANES
                    window = shifted[s % SUBLANES - 1, r0 + a:r0 + a + CONV_RC, cs]
                term = w_ref[k:k + 1, cs] * window
                acc = term if acc is None else acc + term
            if bias_row is not None:
                acc = acc + bias_row[:, cs]
            out_ref[r0:r0 + CONV_RC, cs] = acc


def _conformer_kernel(x_ref, mod_ref, ng_ref, w1_ref, b1_ref, dw_ref, dwb_ref, lng_ref, lnb_ref,
                      w2_ref, b2_ref, o_ref, cbuf, shifted, vbuf):
    ts = MIX_TS

    @pl.when(pl.program_id(1) == 0)
    def _():
        cbuf[0:CONF_HALO, :] = jnp.zeros((CONF_HALO, D_MODEL), _F32)
        cbuf[CONF_HALO + ts:, :] = jnp.zeros((SUBLANES, D_MODEL), _F32)

    x = x_ref[0]
    m = mod_ref[0]
    h = _rms_mod(x, ng_ref[...], m[1:2], m[0:1])
    u = jnp.dot(h.astype(_BF16), w1_ref[...], preferred_element_type=_F32) + b1_ref[...]
    cbuf[CONF_HALO:CONF_HALO + ts, :] = u[:, :D_MODEL] * _sigmoid(u[:, D_MODEL:])
    _causal_taps(cbuf, shifted, dw_ref, dwb_ref[...], vbuf, CONF_KERNEL, CONF_HALO, ts)
    cbuf[0:CONF_HALO, :] = cbuf[ts:ts + CONF_HALO, :]
    v = vbuf[...]
    mu = jnp.mean(v, axis=-1, keepdims=True)
    vc = v - mu
    var = jnp.mean(vc * vc, axis=-1, keepdims=True)
    y = vc * lax.rsqrt(var + NORM_EPS) * lng_ref[...] + lnb_ref[...]
    y = y * _sigmoid(y)
    mix = jnp.dot(y.astype(_BF16), w2_ref[...], preferred_element_type=_F32) + b2_ref[...]
    o_ref[0] = x + m[2:3] * mix


def _conformer_mixer(x, mod, norm_g, w1, b1, dw, dw_b, ln_g, ln_b, w2, b2):
    d = D_MODEL
    row = lambda a: a.reshape(1, -1)
    const = lambda shape: pl.BlockSpec(shape, lambda b, s: (0,) * len(shape))
    return pl.pallas_call(
        _conformer_kernel,
        grid=(BATCH, SEQ // MIX_TS),
        in_specs=[
            pl.BlockSpec((1, MIX_TS, d), lambda b, s: (b, s, 0)),
            pl.BlockSpec((1, N_MOD, d), lambda b, s: (b, 0, 0)),
            const((1, d)), const((d, 2 * d)), const((1, 2 * d)), const((CONF_KERNEL, d)),
            const((1, d)), const((1, d)), const((1, d)), const((d, d)), const((1, d)),
        ],
        out_specs=pl.BlockSpec((1, MIX_TS, d), lambda b, s: (b, s, 0)),
        out_shape=jax.ShapeDtypeStruct((BATCH, SEQ, d), _F32),
        scratch_shapes=[pltpu.VMEM((CONF_HALO + MIX_TS + SUBLANES, d), _F32),
                        pltpu.VMEM((SUBLANES - 1, CONF_HALO + MIX_TS + SUBLANES, d), _F32),
                        pltpu.VMEM((MIX_TS, d), _F32)],
        compiler_params=pltpu.CompilerParams(
            dimension_semantics=("arbitrary", "arbitrary"), vmem_limit_bytes=VMEM_LIMIT),
        name="conformer_mixer",
    )(x, mod, row(norm_g), w1.astype(_BF16), row(b1), dw, row(dw_b), row(ln_g), row(ln_b),
      w2.astype(_BF16), row(b2))


def _shortconv_kernel(x_ref, mod_ref, ng_ref, win_ref, cw_ref, wout_ref, o_ref, cbuf, vbuf):
    ts = MIX_TS
    d = D_MODEL

    @pl.when(pl.program_id(1) == 0)
    def _():
        cbuf[0:SHORT_HALO, :] = jnp.zeros((SHORT_HALO, d), _F32)

    x = x_ref[0]
    m = mod_ref[0]
    h = _rms_mod(x, ng_ref[...], m[1:2], m[0:1])
    z = jnp.dot(h.astype(_BF16), win_ref[...], preferred_element_type=_F32)
    cbuf[SHORT_HALO:SHORT_HALO + ts, :] = z[:, d:2 * d] * z[:, 2 * d:]
    _causal_taps(cbuf, None, cw_ref, None, vbuf, SHORT_KERNEL, SHORT_HALO, ts)
    cbuf[0:SHORT_HALO, :] = cbuf[ts:ts + SHORT_HALO, :]
    y = z[:, :d] * vbuf[...]
    mix = jnp.dot(y.astype(_BF16), wout_ref[...], preferred_element_type=_F32)
    o_ref[0] = x + m[2:3] * mix


def _shortconv_mixer(x, mod, norm_g, w_in, conv_w, w_out):
    d = D_MODEL
    const = lambda shape: pl.BlockSpec(shape, lambda b, s: (0,) * len(shape))
    return pl.pallas_call(
        _shortconv_kernel,
        grid=(BATCH, SEQ // MIX_TS),
        in_specs=[
            pl.BlockSpec((1, MIX_TS, d), lambda b, s: (b, s, 0)),
            pl.BlockSpec((1, N_MOD, d), lambda b, s: (b, 0, 0)),
            const((1, d)), const((d, 3 * d)), const((SHORT_KERNEL, d)), const((d, d)),
        ],
        out_specs=pl.BlockSpec((1, MIX_TS, d), lambda b, s: (b, s, 0)),
        out_shape=jax.ShapeDtypeStruct((BATCH, SEQ, d), _F32),
        scratch_shapes=[pltpu.VMEM((SHORT_HALO + MIX_TS, d), _F32), pltpu.VMEM((MIX_TS, d), _F32)],
        compiler_params=pltpu.CompilerParams(
            dimension_semantics=("arbitrary", "arbitrary"), vmem_limit_bytes=VMEM_LIMIT),
        name="shortconv_mixer",
    )(x, mod, norm_g.reshape(1, d), w_in.astype(_BF16), conv_w, w_out.astype(_BF16))


def _route_kernel(x_ref, mod_ref, ng_ref, rw_ref, rb_ref, h_ref, idx_ref, rank_ref, gate_ref, cnt_ref):
    tm = ROUTE_TM

    @pl.when(pl.program_id(0) == 0)
    def _():
        cnt_ref[...] = jnp.zeros((1, N_EXPERTS), _F32)

    x = x_ref[...]
    m = mod_ref[0]
    h = _rms_mod(x, ng_ref[...], m[4:5], m[3:4])
    for j in range(ROW_CHUNKS):
        h_ref[pl.ds(j, tm, stride=ROW_CHUNKS), :] = h[:, j * LANES:(j + 1) * LANES]

    logits = jnp.dot(h, rw_ref[...], preferred_element_type=_F32,
                     precision=lax.Precision.HIGHEST) + rb_ref[...]
    lane = lax.broadcasted_iota(jnp.int32, (tm, N_EXPERTS), 1).astype(_F32)
    work = logits
    vals, idxs, hots = [], [], []
    for _ in range(TOP_K):
        top = jnp.max(work, axis=1, keepdims=True)
        pick = jnp.min(jnp.where(work == top, lane, float(N_EXPERTS)), axis=1, keepdims=True)
        hot = lane == pick
        vals.append(top)
        idxs.append(pick)
        hots.append(hot)
        work = jnp.where(hot, -jnp.inf, work)
    exps = [jnp.exp(v - vals[0]) for v in vals]
    denom = exps[0] + exps[1] + exps[2] + exps[3]

    multi = (hots[0] | hots[1] | hots[2] | hots[3]).astype(_BF16)
    r_i = lax.broadcasted_iota(jnp.int32, (tm, tm), 0)
    c_i = lax.broadcasted_iota(jnp.int32, (tm, tm), 1)
    before = (c_i < r_i).astype(_BF16)
    pos = cnt_ref[...] + jnp.dot(before, multi, preferred_element_type=_F32)
    cnt_ref[...] = cnt_ref[...] + jnp.sum(multi.astype(_F32), axis=0, keepdims=True)

    col = lax.broadcasted_iota(jnp.int32, (tm, TOP_K), 1)
    idx_out = jnp.zeros((tm, TOP_K), _F32)
    rank_out = jnp.zeros((tm, TOP_K), _F32)
    gate_out = jnp.zeros((tm, TOP_K), _F32)
    for k in range(TOP_K):
        rank_k = jnp.sum(jnp.where(hots[k], pos, 0.0), axis=1, keepdims=True)
        idx_out = jnp.where(col == k, idxs[k], idx_out)
        rank_out = jnp.where(col == k, rank_k, rank_out)
        gate_out = jnp.where(col == k, exps[k] / denom, gate_out)
    idx_ref[...] = idx_out.astype(jnp.int32)
    rank_ref[...] = rank_out.astype(jnp.int32)
    gate_ref[...] = gate_out


def _route(x1, mod, norm_g, router_w, router_b):
    d = D_MODEL
    tm = ROUTE_TM
    tiles_per_batch = SEQ // tm
    return pl.pallas_call(
        _route_kernel,
        grid=(TOKENS // tm,),
        in_specs=[
            pl.BlockSpec((tm, d), lambda i: (i, 0)),
            pl.BlockSpec((1, N_MOD, d), lambda i: (i // tiles_per_batch, 0, 0)),
            pl.BlockSpec((1, d), lambda i: (0, 0)),
            pl.BlockSpec((d, N_EXPERTS), lambda i: (0, 0)),
            pl.BlockSpec((1, N_EXPERTS), lambda i: (0, 0)),
        ],
        out_specs=[
            pl.BlockSpec((tm * ROW_CHUNKS, LANES), lambda i: (i, 0)),
            pl.BlockSpec((tm, TOP_K), lambda i: (i, 0)),
            pl.BlockSpec((tm, TOP_K), lambda i: (i, 0)),
            pl.BlockSpec((tm, TOP_K), lambda i: (i, 0)),
            pl.BlockSpec((1, N_EXPERTS), lambda i: (0, 0)),
        ],
        out_shape=[
            jax.ShapeDtypeStruct((TOKENS * ROW_CHUNKS, LANES), _F32),
            jax.ShapeDtypeStruct((TOKENS, TOP_K), jnp.int32),
            jax.ShapeDtypeStruct((TOKENS, TOP_K), jnp.int32),
            jax.ShapeDtypeStruct((TOKENS, TOP_K), _F32),
            jax.ShapeDtypeStruct((1, N_EXPERTS), _F32),
        ],
        compiler_params=pltpu.CompilerParams(
            dimension_semantics=("arbitrary",), vmem_limit_bytes=VMEM_LIMIT),
        name="route",
    )(x1, mod, norm_g.reshape(1, d), router_w, router_b.reshape(1, N_EXPERTS))


def _routing_tables(idx, rank, counts):
    counts = counts.reshape(N_EXPERTS).astype(jnp.int32)
    padded = (counts + EXPERT_BM - 1) // EXPERT_BM * EXPERT_BM
    pad_ends = jnp.cumsum(padded)
    pad_starts = pad_ends - padded
    hot = idx[..., None] == jnp.arange(N_EXPERTS, dtype=jnp.int32)
    dest = jnp.sum(jnp.where(hot, pad_starts, 0), axis=-1) + rank
    block_start = jnp.arange(N_BLOCKS, dtype=jnp.int32) * EXPERT_BM
    block_e = jnp.sum((block_start[:, None] >= pad_ends[None, :]).astype(jnp.int32), axis=1)
    block_e = jnp.minimum(block_e, N_EXPERTS - 1).astype(jnp.int32)
    n_used = (pad_ends[-1:] // EXPERT_BM).astype(jnp.int32)
    ids = jnp.arange(N_EXPERTS, dtype=jnp.int32)
    later_used = (ids[None, :] > ids[:, None]) & (counts[None, :] > 0)
    next_used = jnp.min(jnp.where(later_used, ids[None, :], N_EXPERTS), axis=1).astype(jnp.int32)
    pad_lo = (pad_starts + counts).astype(jnp.int32)
    dest = dest.reshape(TOKENS * TOP_K).astype(jnp.int32)
    return dest, block_e, n_used, next_used, pad_lo, pad_ends.astype(jnp.int32)


def _row_copy(src, src_row, dst, dst_row, sem):
    def tile(row):
        start = row * ROW_CHUNKS
        return pl.ds(start if isinstance(row, int) else pl.multiple_of(start, ROW_CHUNKS), ROW_CHUNKS)

    return pltpu.make_async_copy(src.at[tile(src_row)], dst.at[tile(dst_row)], sem)


def _dispatch_kernel(dest_ref, lo_ref, hi_ref, h_ref, xs_ref, rs_ref, zblk, sem, zsem):
    tm = DISP_TM
    base = pl.program_id(0) * tm
    dump0 = TOKENS * TOP_K

    def set_dump(r, carry):
        rs_ref[r] = dump0 + r % EXPERT_BM
        return carry

    @pl.when(pl.program_id(0) == 0)
    def _():
        lax.fori_loop(0, EXPERT_BM, set_dump, 0)

    def issue(t, carry):
        for k in range(TOP_K):
            d = dest_ref[(base + t) * TOP_K + k]
            _row_copy(h_ref, t, xs_ref, d, sem).start(priority=k % 2)
            rs_ref[d + EXPERT_BM] = k * TOKENS + base + t
        return carry

    lax.fori_loop(0, tm, issue, 0)

    @pl.when(pl.program_id(0) == pl.num_programs(0) - 1)
    def _():
        zblk[...] = jnp.zeros((EXPERT_BM * ROW_CHUNKS, LANES), _F32)

        def pad_copy(r):
            return _row_copy(zblk, 0, xs_ref, r, zsem)

        def tail_copy(b):
            rows = EXPERT_BM * ROW_CHUNKS
            return pltpu.make_async_copy(zblk, xs_ref.at[pl.ds(pl.multiple_of(b * rows, rows), rows)], zsem)

        first_unused = hi_ref[N_EXPERTS - 1] // EXPERT_BM
        for e in range(N_EXPERTS):
            lax.fori_loop(lo_ref[e] + EXPERT_BM, hi_ref[e] + EXPERT_BM, set_dump, 0)
        lax.fori_loop((first_unused + 1) * EXPERT_BM, N_ROWS + EXPERT_BM, set_dump, 0)
        for e in range(N_EXPERTS):
            lax.fori_loop(lo_ref[e], hi_ref[e], lambda r, c: (pad_copy(r).start(), c)[1], 0)
        lax.fori_loop(first_unused, N_BLOCKS, lambda b, c: (tail_copy(b).start(), c)[1], 0)
        for e in range(N_EXPERTS):
            lax.fori_loop(lo_ref[e], hi_ref[e], lambda r, c: (pad_copy(r).wait(), c)[1], 0)
        lax.fori_loop(first_unused, N_BLOCKS, lambda b, c: (tail_copy(b).wait(), c)[1], 0)

    for _ in range(TOP_K):
        pltpu.make_async_copy(h_ref, xs_ref.at[pl.ds(0, tm * ROW_CHUNKS)], sem).wait()


def _dispatch(dest, pad_lo, pad_hi, h_rows):
    tm = DISP_TM
    return pl.pallas_call(
        _dispatch_kernel,
        grid_spec=pltpu.PrefetchScalarGridSpec(
            num_scalar_prefetch=3,
            grid=(TOKENS // tm,),
            in_specs=[pl.BlockSpec((tm * ROW_CHUNKS, LANES), lambda i, *_: (i, 0))],
            out_specs=[pl.BlockSpec(memory_space=pl.ANY), pl.BlockSpec(memory_space=pltpu.SMEM)],
            scratch_shapes=[pltpu.VMEM((EXPERT_BM * ROW_CHUNKS, LANES), _F32),
                            pltpu.SemaphoreType.DMA, pltpu.SemaphoreType.DMA],
        ),
        out_shape=[jax.ShapeDtypeStruct((N_ROWS * ROW_CHUNKS, LANES), _F32),
                   jax.ShapeDtypeStruct((N_ROWS + EXPERT_BM,), jnp.int32)],
        compiler_params=pltpu.CompilerParams(
            dimension_semantics=("arbitrary",), vmem_limit_bytes=VMEM_LIMIT),
        name="dispatch",
    )(dest, pad_lo, pad_hi, h_rows)


def _expert_kernel(be_ref, nu_ref, nx_ref, rs_ref, xs_ref, w1_hbm, b1_ref, w2_hbm, b2_ref, ys_ref,
                   w1f, w2f, w1s, w2s, obuf, wsem, osem, nsw, *, layer):
    bm = EXPERT_BM
    b = pl.program_id(0)
    e = be_ref[b]
    n_used = nu_ref[0]
    half = MXU_COLS // 2

    def fetch(expert, slot):
        return (pltpu.make_async_copy(w1_hbm.at[layer, expert], w1f.at[slot], wsem.at[0, slot]),
                pltpu.make_async_copy(w2_hbm.at[layer, expert], w2f.at[slot], wsem.at[1, slot]))

    def scatter_start(step, slot):
        for i in range(bm):
            _row_copy(obuf.at[slot], i, ys_ref, rs_ref[step * bm + i], osem.at[slot]).start(priority=i % 2)

    def scatter_wait(slot):
        pltpu.make_async_copy(obuf.at[slot], ys_ref.at[pl.ds(0, bm * ROW_CHUNKS)], osem.at[slot]).wait()

    @pl.when(b == 0)
    def _():
        nsw[0] = 0
        obuf[1] = jnp.zeros((bm * ROW_CHUNKS, LANES), _F32)
        pltpu.make_async_copy(
            obuf.at[1], ys_ref.at[pl.ds((OUT_SLOTS - bm) * ROW_CHUNKS, bm * ROW_CHUNKS)], osem.at[0]).start()
        for copy in fetch(e, 0):
            copy.start()

    @pl.when(b == n_used)
    def _():
        scatter_start(b, 1 - b % 2)
        scatter_wait(b % 2)
        scatter_wait(1 - b % 2)

    @pl.when(b < n_used)
    def _():
        out_slot = b % 2

        @pl.when(jnp.logical_or(b == 0, be_ref[jnp.maximum(b - 1, 0)] != e))
        def _():
            slot = nsw[0] % 2
            for copy in fetch(e, slot):
                copy.wait()
            src = lax.broadcasted_iota(jnp.int32, (MXU_COLS, MXU_COLS), 0)
            dst = lax.broadcasted_iota(jnp.int32, (MXU_COLS, MXU_COLS), 1)
            want = jnp.where(dst < half, 2 * dst, 2 * (dst - half) + 1)
            perm = (src == want).astype(_BF16)
            for blk in range(2 * D_EXPERT // MXU_COLS):
                cs = slice(blk * MXU_COLS, (blk + 1) * MXU_COLS)
                w1s[:, cs] = jnp.dot(w1f[slot, :, cs].astype(_BF16), perm,
                                     preferred_element_type=_F32).astype(_BF16)
            w2s[...] = w2f[slot].astype(_BF16)
            nsw[0] = nsw[0] + 1
            nxt = nx_ref[e]

            @pl.when(nxt < N_EXPERTS)
            def _():
                for copy in fetch(nxt, 1 - slot):
                    copy.start()

        scatter_start(b, 1 - out_slot)
        x = jnp.concatenate(
            [xs_ref[pl.ds(j, bm, stride=ROW_CHUNKS), :] for j in range(ROW_CHUNKS)], axis=1).astype(_BF16)
        u = jnp.dot(x, w1s[...], preferred_element_type=_F32) + b1_ref[0, 0]
        acts = []
        for blk in range(2 * D_EXPERT // MXU_COLS):
            glu = jnp.minimum(u[:, blk * MXU_COLS:blk * MXU_COLS + half], SWIGLU_LIMIT)
            lin = jnp.clip(u[:, blk * MXU_COLS + half:(blk + 1) * MXU_COLS], -SWIGLU_LIMIT, SWIGLU_LIMIT)
            acts.append((glu * _sigmoid(SWIGLU_ALPHA * glu) * (lin + 1.0)).astype(_BF16))
        hidden = jnp.concatenate(acts, axis=1)
        y = jnp.dot(hidden, w2s[...], preferred_element_type=_F32) + b2_ref[0, 0]
        scatter_wait(out_slot)
        for j in range(ROW_CHUNKS):
            obuf[out_slot, pl.ds(j, bm, stride=ROW_CHUNKS), :] = y[:, j * LANES:(j + 1) * LANES]


def _experts(layer, block_e, n_used, next_used, row_slot, xs, w1, b1_grouped, w2, b2):
    bm = EXPERT_BM
    d, f = D_MODEL, D_EXPERT
    blk = lambda b, be, nu, nx, rs: jnp.minimum(b, nu[0] - 1)
    return pl.pallas_call(
        functools.partial(_expert_kernel, layer=layer),
        grid_spec=pltpu.PrefetchScalarGridSpec(
            num_scalar_prefetch=4,
            grid=(N_BLOCKS,),
            in_specs=[
                pl.BlockSpec((bm * ROW_CHUNKS, LANES), lambda *a: (blk(*a), 0)),
                pl.BlockSpec(memory_space=pl.ANY),
                pl.BlockSpec((1, 1, 1, 2 * f), lambda *a: (layer, a[1][blk(*a)], 0, 0)),
                pl.BlockSpec(memory_space=pl.ANY),
                pl.BlockSpec((1, 1, 1, d), lambda *a: (layer, a[1][blk(*a)], 0, 0)),
            ],
            out_specs=pl.BlockSpec(memory_space=pl.ANY),
            scratch_shapes=[
                pltpu.VMEM((2, d, 2 * f), _F32), pltpu.VMEM((2, f, d), _F32),
                pltpu.VMEM((d, 2 * f), _BF16), pltpu.VMEM((f, d), _BF16),
                pltpu.VMEM((2, bm * ROW_CHUNKS, LANES), _F32),
                pltpu.SemaphoreType.DMA((2, 2)), pltpu.SemaphoreType.DMA((2,)), pltpu.SMEM((1,), jnp.int32),
            ],
        ),
        out_shape=jax.ShapeDtypeStruct((OUT_SLOTS * ROW_CHUNKS, LANES), _F32),
        compiler_params=pltpu.CompilerParams(
            dimension_semantics=("arbitrary",), vmem_limit_bytes=VMEM_LIMIT),
        name="experts",
    )(block_e, n_used, next_used, row_slot, xs, w1, b1_grouped, w2, b2)


def _group_glu_columns(b1):
    half = MXU_COLS // 2
    j = jnp.arange(MXU_COLS)
    want = jnp.where(j < half, 2 * j, 2 * (j - half) + 1)
    cols = (jnp.arange(2 * D_EXPERT // MXU_COLS)[:, None] * MXU_COLS + want[None, :]).reshape(-1)
    return b1[..., cols]


def _combine_kernel(y0_ref, y1_ref, y2_ref, y3_ref, x_ref, gate_ref, mod_ref, fg_ref, o_ref, *, final):
    tm = COMB_TM
    gates = gate_ref[...]
    y = None
    for k, yk_ref in enumerate((y0_ref, y1_ref, y2_ref, y3_ref)):
        rows = jnp.concatenate(
            [yk_ref[pl.ds(j, tm, stride=ROW_CHUNKS), :] for j in range(ROW_CHUNKS)], axis=1)
        term = gates[:, k:k + 1] * rows
        y = term if y is None else y + term
    out = x_ref[...] + mod_ref[0][5:6] * y
    if final:
        out = out * lax.rsqrt(jnp.mean(out * out, axis=-1, keepdims=True) + NORM_EPS) * fg_ref[...]
    o_ref[...] = out


def _combine(ys, x1, gates, mod, final_g, final):
    tm = COMB_TM
    d = D_MODEL
    tiles = TOKENS // tm
    tiles_per_batch = SEQ // tm
    slot_spec = lambda k: pl.BlockSpec((tm * ROW_CHUNKS, LANES), lambda i: (k * tiles + i, 0))
    return pl.pallas_call(
        functools.partial(_combine_kernel, final=final),
        grid=(tiles,),
        in_specs=[
            slot_spec(0), slot_spec(1), slot_spec(2), slot_spec(3),
            pl.BlockSpec((tm, d), lambda i: (i, 0)),
            pl.BlockSpec((tm, TOP_K), lambda i: (i, 0)),
            pl.BlockSpec((1, N_MOD, d), lambda i: (i // tiles_per_batch, 0, 0)),
            pl.BlockSpec((1, d), lambda i: (0, 0)),
        ],
        out_specs=pl.BlockSpec((tm, d), lambda i: (i, 0)),
        out_shape=jax.ShapeDtypeStruct((TOKENS, d), _F32),
        compiler_params=pltpu.CompilerParams(
            dimension_semantics=("arbitrary",), vmem_limit_bytes=VMEM_LIMIT),
        name="combine_final" if final else "combine",
    )(ys, ys, ys, ys, x1, gates, mod, final_g.reshape(1, d))


def kernel(x, c, mod_w, mod_b, norm_g, conf_w1, conf_b1, conf_dw, conf_dw_b, conf_ln_g, conf_ln_b, conf_w2, conf_b2, sc_w_in, sc_conv, sc_w_out, router_w, router_b, exp_w1, exp_b1, exp_w2, exp_b2, final_g):
    mods = _modulation(c, mod_w, mod_b).reshape(DEPTH, BATCH, N_MOD, D_MODEL)
    b1_grouped = _group_glu_columns(exp_b1).reshape(DEPTH, N_EXPERTS, 1, 2 * D_EXPERT)
    b2_rows = exp_b2.reshape(DEPTH, N_EXPERTS, 1, D_MODEL)
    xcur = x
    for i in range(DEPTH):
        mod = mods[i]
        if i % 2 == 0:
            j = i // 2
            x1 = _conformer_mixer(xcur, mod, norm_g[i, 0], conf_w1[j], conf_b1[j], conf_dw[j], conf_dw_b[j],
                                  conf_ln_g[j], conf_ln_b[j], conf_w2[j], conf_b2[j])
        else:
            j = i // 2
            x1 = _shortconv_mixer(xcur, mod, norm_g[i, 0], sc_w_in[j], sc_conv[j], sc_w_out[j])
        x1 = x1.reshape(TOKENS, D_MODEL)
        h_rows, idx, rank, gates, counts = _route(x1, mod, norm_g[i, 1], router_w[i], router_b[i])
        dest, block_e, n_used, next_used, pad_lo, pad_hi = _routing_tables(idx, rank, counts)
        xs, row_slot = _dispatch(dest, pad_lo, pad_hi, h_rows)
        ys = _experts(i, block_e, n_used, next_used, row_slot, xs, exp_w1, b1_grouped, exp_w2, b2_rows)
        x2 = _combine(ys, x1, gates, mod, final_g, final=(i == DEPTH - 1))
        xcur = x2.reshape(BATCH, SEQ, D_MODEL)
    return xcur
```

```python
import functools

import jax
import jax.numpy as jnp
from jax import lax
from jax.experimental import pallas as pl
from jax.experimental.pallas import tpu as pltpu

D_MODEL = 1024
BATCH = 8
SEQ = 2048
DEPTH = 2
TOKENS = BATCH * SEQ
CONF_KERNEL = 31
SHORT_KERNEL = 3
N_EXPERTS = 32
TOP_K = 4
D_EXPERT = D_MODEL
SWIGLU_ALPHA = 1.702
SWIGLU_LIMIT = 7.0
NORM_EPS = 1e-5
N_MOD = 6

LANES = 128
SUBLANES = 8
MXU_COLS = 256
VMEM_LIMIT = 56 * 1024 * 1024

ROW_CHUNKS = D_MODEL // LANES
MIX_TS = 256
CONF_HALO = 32
SHORT_HALO = 8
CONV_RC = 64
ROUTE_TM = 512
DISP_TM = 512
DISP_UNROLL = 4
COMB_TM = 256
EXPERT_BM = 256
N_ROWS = TOKENS * TOP_K + N_EXPERTS * EXPERT_BM
N_BLOCKS = N_ROWS // EXPERT_BM
OUT_SLOTS = TOKENS * TOP_K + 3 * EXPERT_BM

_F32 = jnp.float32
_BF16 = jnp.bfloat16


def _rms_mod(x, g, scale, shift):
    y = x * lax.rsqrt(jnp.mean(x * x, axis=-1, keepdims=True) + NORM_EPS)
    return (y * g) * (1.0 + scale) + shift


def _sigmoid(x):
    return 1.0 / (1.0 + jnp.exp(-x))


def _mod_kernel(c_ref, w_ref, b_ref, o_ref):
    c = c_ref[...]
    c_act = c * _sigmoid(c)
    o_ref[0] = jnp.dot(c_act, w_ref[0], preferred_element_type=_F32,
                       precision=lax.Precision.HIGHEST) + b_ref[0]


def _modulation(c, mod_w, mod_b):
    tn = 1536
    n = N_MOD * D_MODEL
    return pl.pallas_call(
        _mod_kernel,
        grid=(DEPTH, n // tn),
        in_specs=[
            pl.BlockSpec((BATCH, D_MODEL), lambda i, j: (0, 0)),
            pl.BlockSpec((1, D_MODEL, tn), lambda i, j: (i, 0, j)),
            pl.BlockSpec((1, 1, tn), lambda i, j: (i, 0, j)),
        ],
        out_specs=pl.BlockSpec((1, BATCH, tn), lambda i, j: (i, 0, j)),
        out_shape=jax.ShapeDtypeStruct((DEPTH, BATCH, n), _F32),
        compiler_params=pltpu.CompilerParams(vmem_limit_bytes=VMEM_LIMIT),
        name="modulation",
    )(c, mod_w, mod_b.reshape(DEPTH, 1, n))


def _causal_taps(cbuf, shifted, w_ref, bias_row, out_ref, ntaps, halo, ts):
    off0 = halo - (ntaps - 1)
    for c in range(ROW_CHUNKS):
        cs = slice(c * LANES, (c + 1) * LANES)
        if shifted is not None:
            rows = cbuf.shape[0]
            whole = cbuf[:, cs]
            for r in range(1, SUBLANES):
                shifted[r - 1, :, cs] = pltpu.roll(whole, rows - r, axis=0)
        for rc in range(ts // CONV_RC):
            r0 = rc * CONV_RC
            acc = None
            for k in range(ntaps):
                s = off0 + k
                if shifted is None or s % SUBLANES == 0:
                    window = cbuf[r0 + s:r0 + s + CONV_RC, cs]
                else:
                    a = s - s % SUBLANES
                    window = shifted[s % SUBLANES - 1, r0 + a:r0 + a + CONV_RC, cs]
                term = w_ref[k:k + 1, cs] * window
                acc = term if acc is None else acc + term
            if bias_row is not None:
                acc = acc + bias_row[:, cs]
            out_ref[r0:r0 + CONV_RC, cs] = acc


def _conformer_kernel(x_ref, mod_ref, ng_ref, w1_ref, b1_ref, dw_ref, dwb_ref, lng_ref, lnb_ref,
                      w2_ref, b2_ref, o_ref, cbuf, shifted, vbuf):
    ts = MIX_TS

    @pl.when(pl.program_id(1) == 0)
    def _():
        cbuf[0:CONF_HALO, :] = jnp.zeros((CONF_HALO, D_MODEL), _F32)
        cbuf[CONF_HALO + ts:, :] = jnp.zeros((SUBLANES, D_MODEL), _F32)

    x = x_ref[0]
    m = mod_ref[0]
    h = _rms_mod(x, ng_ref[...], m[1:2], m[0:1])
    u = jnp.dot(h.astype(_BF16), w1_ref[...], preferred_element_type=_F32) + b1_ref[...]
    cbuf[CONF_HALO:CONF_HALO + ts, :] = u[:, :D_MODEL] * _sigmoid(u[:, D_MODEL:])
    _causal_taps(cbuf, shifted, dw_ref, dwb_ref[...], vbuf, CONF_KERNEL, CONF_HALO, ts)
    cbuf[0:CONF_HALO, :] = cbuf[ts:ts + CONF_HALO, :]
    v = vbuf[...]
    mu = jnp.mean(v, axis=-1, keepdims=True)
    vc = v - mu
    var = jnp.mean(vc * vc, axis=-1, keepdims=True)
    y = vc * lax.rsqrt(var + NORM_EPS) * lng_ref[...] + lnb_ref[...]
    y = y * _sigmoid(y)
    mix = jnp.dot(y.astype(_BF16), w2_ref[...], preferred_element_type=_F32) + b2_ref[...]
    o_ref[0] = x + m[2:3] * mix


def _conformer_mixer(x, mod, norm_g, w1, b1, dw, dw_b, ln_g, ln_b, w2, b2):
    d = D_MODEL
    row = lambda a: a.reshape(1, -1)
    const = lambda shape: pl.BlockSpec(shape, lambda b, s: (0,) * len(shape))
    return pl.pallas_call(
        _conformer_kernel,
        grid=(BATCH, SEQ // MIX_TS),
        in_specs=[
            pl.BlockSpec((1, MIX_TS, d), lambda b, s: (b, s, 0)),
            pl.BlockSpec((1, N_MOD, d), lambda b, s: (b, 0, 0)),
            const((1, d)), const((d, 2 * d)), const((1, 2 * d)), const((CONF_KERNEL, d)),
            const((1, d)), const((1, d)), const((1, d)), const((d, d)), const((1, d)),
        ],
        out_specs=pl.BlockSpec((1, MIX_TS, d), lambda b, s: (b, s, 0)),
        out_shape=jax.ShapeDtypeStruct((BATCH, SEQ, d), _F32),
        scratch_shapes=[pltpu.VMEM((CONF_HALO + MIX_TS + SUBLANES, d), _F32),
                        pltpu.VMEM((SUBLANES - 1, CONF_HALO + MIX_TS + SUBLANES, d), _F32),
                        pltpu.VMEM((MIX_TS, d), _F32)],
        compiler_params=pltpu.CompilerParams(
            dimension_semantics=("arbitrary", "arbitrary"), vmem_limit_bytes=VMEM_LIMIT),
        name="conformer_mixer",
    )(x, mod, row(norm_g), w1.astype(_BF16), row(b1), dw, row(dw_b), row(ln_g), row(ln_b),
      w2.astype(_BF16), row(b2))


def _shortconv_kernel(x_ref, mod_ref, ng_ref, win_ref, cw_ref, wout_ref, o_ref, cbuf, vbuf):
    ts = MIX_TS
    d = D_MODEL

    @pl.when(pl.program_id(1) == 0)
    def _():
        cbuf[0:SHORT_HALO, :] = jnp.zeros((SHORT_HALO, d), _F32)

    x = x_ref[0]
    m = mod_ref[0]
    h = _rms_mod(x, ng_ref[...], m[1:2], m[0:1])
    z = jnp.dot(h.astype(_BF16), win_ref[...], preferred_element_type=_F32)
    cbuf[SHORT_HALO:SHORT_HALO + ts, :] = z[:, d:2 * d] * z[:, 2 * d:]
    _causal_taps(cbuf, None, cw_ref, None, vbuf, SHORT_KERNEL, SHORT_HALO, ts)
    cbuf[0:SHORT_HALO, :] = cbuf[ts:ts + SHORT_HALO, :]
    y = z[:, :d] * vbuf[...]
    mix = jnp.dot(y.astype(_BF16), wout_ref[...], preferred_element_type=_F32)
    o_ref[0] = x + m[2:3] * mix


def _shortconv_mixer(x, mod, norm_g, w_in, conv_w, w_out):
    d = D_MODEL
    const = lambda shape: pl.BlockSpec(shape, lambda b, s: (0,) * len(shape))
    return pl.pallas_call(
        _shortconv_kernel,
        grid=(BATCH, SEQ // MIX_TS),
        in_specs=[
            pl.BlockSpec((1, MIX_TS, d), lambda b, s: (b, s, 0)),
            pl.BlockSpec((1, N_MOD, d), lambda b, s: (b, 0, 0)),
            const((1, d)), const((d, 3 * d)), const((SHORT_KERNEL, d)), const((d, d)),
        ],
        out_specs=pl.BlockSpec((1, MIX_TS, d), lambda b, s: (b, s, 0)),
        out_shape=jax.ShapeDtypeStruct((BATCH, SEQ, d), _F32),
        scratch_shapes=[pltpu.VMEM((SHORT_HALO + MIX_TS, d), _F32), pltpu.VMEM((MIX_TS, d), _F32)],
        compiler_params=pltpu.CompilerParams(
            dimension_semantics=("arbitrary", "arbitrary"), vmem_limit_bytes=VMEM_LIMIT),
        name="shortconv_mixer",
    )(x, mod, norm_g.reshape(1, d), w_in.astype(_BF16), conv_w, w_out.astype(_BF16))


def _route_kernel(x_ref, mod_ref, ng_ref, rw_ref, rb_ref, h_ref, idx_ref, rank_ref, gate_ref, cnt_ref):
    tm = ROUTE_TM

    @pl.when(pl.program_id(0) == 0)
    def _():
        cnt_ref[...] = jnp.zeros((1, N_EXPERTS), _F32)

    x = x_ref[...]
    m = mod_ref[0]
    h = _rms_mod(x, ng_ref[...], m[4:5], m[3:4])
    for j in range(ROW_CHUNKS):
        h_ref[pl.ds(j, tm, stride=ROW_CHUNKS), :] = h[:, j * LANES:(j + 1) * LANES]

    logits = jnp.dot(h, rw_ref[...], preferred_element_type=_F32,
                     precision=lax.Precision.HIGHEST) + rb_ref[...]
    lane = lax.broadcasted_iota(jnp.int32, (tm, N_EXPERTS), 1).astype(_F32)
    work = logits
    vals, idxs, hots = [], [], []
    for _ in range(TOP_K):
        top = jnp.max(work, axis=1, keepdims=True)
        pick = jnp.min(jnp.where(work == top, lane, float(N_EXPERTS)), axis=1, keepdims=True)
        hot = lane == pick
        vals.append(top)
        idxs.append(pick)
        hots.append(hot)
        work = jnp.where(hot, -jnp.inf, work)
    exps = [jnp.exp(v - vals[0]) for v in vals]
    denom = exps[0] + exps[1] + exps[2] + exps[3]

    multi = (hots[0] | hots[1] | hots[2] | hots[3]).astype(_BF16)
    r_i = lax.broadcasted_iota(jnp.int32, (tm, tm), 0)
    c_i = lax.broadcasted_iota(jnp.int32, (tm, tm), 1)
    before = (c_i < r_i).astype(_BF16)
    pos = cnt_ref[...] + jnp.dot(before, multi, preferred_element_type=_F32)
    cnt_ref[...] = cnt_ref[...] + jnp.sum(multi.astype(_F32), axis=0, keepdims=True)

    col = lax.broadcasted_iota(jnp.int32, (tm, TOP_K), 1)
    idx_out = jnp.zeros((tm, TOP_K), _F32)
    rank_out = jnp.zeros((tm, TOP_K), _F32)
    gate_out = jnp.zeros((tm, TOP_K), _F32)
    for k in range(TOP_K):
        rank_k = jnp.sum(jnp.where(hots[k], pos, 0.0), axis=1, keepdims=True)
        idx_out = jnp.where(col == k, idxs[k], idx_out)
        rank_out = jnp.where(col == k, rank_k, rank_out)
        gate_out = jnp.where(col == k, exps[k] / denom, gate_out)
    idx_ref[...] = idx_out.astype(jnp.int32)
    rank_ref[...] = rank_out.astype(jnp.int32)
    gate_ref[...] = gate_out


def _route(x1, mod, norm_g, router_w, router_b):
    d = D_MODEL
    tm = ROUTE_TM
    tiles_per_batch = SEQ // tm
    return pl.pallas_call(
        _route_kernel,
        grid=(TOKENS // tm,),
        in_specs=[
            pl.BlockSpec((tm, d), lambda i: (i, 0)),
            pl.BlockSpec((1, N_MOD, d), lambda i: (i // tiles_per_batch, 0, 0)),
            pl.BlockSpec((1, d), lambda i: (0, 0)),
            pl.BlockSpec((d, N_EXPERTS), lambda i: (0, 0)),
            pl.BlockSpec((1, N_EXPERTS), lambda i: (0, 0)),
        ],
        out_specs=[
            pl.BlockSpec((tm * ROW_CHUNKS, LANES), lambda i: (i, 0)),
            pl.BlockSpec((tm, TOP_K), lambda i: (i, 0)),
            pl.BlockSpec((tm, TOP_K), lambda i: (i, 0)),
            pl.BlockSpec((tm, TOP_K), lambda i: (i, 0)),
            pl.BlockSpec((1, N_EXPERTS), lambda i: (0, 0)),
        ],
        out_shape=[
            jax.ShapeDtypeStruct((TOKENS * ROW_CHUNKS, LANES), _F32),
            jax.ShapeDtypeStruct((TOKENS, TOP_K), jnp.int32),
            jax.ShapeDtypeStruct((TOKENS, TOP_K), jnp.int32),
            jax.ShapeDtypeStruct((TOKENS, TOP_K), _F32),
            jax.ShapeDtypeStruct((1, N_EXPERTS), _F32),
        ],
        compiler_params=pltpu.CompilerParams(
            dimension_semantics=("arbitrary",), vmem_limit_bytes=VMEM_LIMIT),
        name="route",
    )(x1, mod, norm_g.reshape(1, d), router_w, router_b.reshape(1, N_EXPERTS))


def _routing_tables(idx, rank, counts):
    counts = counts.reshape(N_EXPERTS).astype(jnp.int32)
    padded = (counts + EXPERT_BM - 1) // EXPERT_BM * EXPERT_BM
    pad_ends = jnp.cumsum(padded)
    pad_starts = pad_ends - padded
    hot = idx[..., None] == jnp.arange(N_EXPERTS, dtype=jnp.int32)
    dest = jnp.sum(jnp.where(hot, pad_starts, 0), axis=-1) + rank
    block_start = jnp.arange(N_BLOCKS, dtype=jnp.int32) * EXPERT_BM
    block_e = jnp.sum((block_start[:, None] >= pad_ends[None, :]).astype(jnp.int32), axis=1)
    block_e = jnp.minimum(block_e, N_EXPERTS - 1).astype(jnp.int32)
    n_used = (pad_ends[-1:] // EXPERT_BM).astype(jnp.int32)
    ids = jnp.arange(N_EXPERTS, dtype=jnp.int32)
    later_used = (ids[None, :] > ids[:, None]) & (counts[None, :] > 0)
    next_used = jnp.min(jnp.where(later_used, ids[None, :], N_EXPERTS), axis=1).astype(jnp.int32)
    pad_lo = (pad_starts + counts).astype(jnp.int32)
    dest = dest.reshape(TOKENS * TOP_K).astype(jnp.int32)
    return dest, block_e, n_used, next_used, pad_lo, pad_ends.astype(jnp.int32)


def _row_copy(src, src_row, dst, dst_row, sem):
    def tile(row):
        start = row * ROW_CHUNKS
        return pl.ds(start if isinstance(row, int) else pl.multiple_of(start, ROW_CHUNKS), ROW_CHUNKS)

    return pltpu.make_async_copy(src.at[tile(src_row)], dst.at[tile(dst_row)], sem)


def _dispatch_kernel(dest_ref, lo_ref, hi_ref, h_ref, xs_ref, rs_ref, zblk, sem, zsem):
    tm = DISP_TM
    base = pl.program_id(0) * tm
    dump0 = TOKENS * TOP_K

    def set_dump(r, carry):
        rs_ref[r] = dump0 + r % EXPERT_BM
        return carry

    @pl.when(pl.program_id(0) == 0)
    def _():
        lax.fori_loop(0, EXPERT_BM, set_dump, 0)

    def issue(g, carry):
        toks = [g * DISP_UNROLL + u for u in range(DISP_UNROLL)]
        dests = [[dest_ref[(base + t) * TOP_K + k] for k in range(TOP_K)] for t in toks]
        for t, row in zip(toks, dests):
            for k, d in enumerate(row):
                _row_copy(h_ref, t, xs_ref, d, sem).start(priority=k % 2)
        for t, row in zip(toks, dests):
            for k, d in enumerate(row):
                rs_ref[d + EXPERT_BM] = k * TOKENS + base + t
        return carry

    lax.fori_loop(0, tm // DISP_UNROLL, issue, 0)

    @pl.when(pl.program_id(0) == pl.num_programs(0) - 1)
    def _():
        zblk[...] = jnp.zeros((EXPERT_BM * ROW_CHUNKS, LANES), _F32)

        def pad_copy(r):
            return _row_copy(zblk, 0, xs_ref, r, zsem)

        def tail_copy(b):
            rows = EXPERT_BM * ROW_CHUNKS
            return pltpu.make_async_copy(zblk, xs_ref.at[pl.ds(pl.multiple_of(b * rows, rows), rows)], zsem)

        first_unused = hi_ref[N_EXPERTS - 1] // EXPERT_BM
        for e in range(N_EXPERTS):
            lax.fori_loop(lo_ref[e] + EXPERT_BM, hi_ref[e] + EXPERT_BM, set_dump, 0)
        lax.fori_loop((first_unused + 1) * EXPERT_BM, N_ROWS + EXPERT_BM, set_dump, 0)
        for e in range(N_EXPERTS):
            lax.fori_loop(lo_ref[e], hi_ref[e], lambda r, c: (pad_copy(r).start(), c)[1], 0)
        lax.fori_loop(first_unused, N_BLOCKS, lambda b, c: (tail_copy(b).start(), c)[1], 0)
        for e in range(N_EXPERTS):
            lax.fori_loop(lo_ref[e], hi_ref[e], lambda r, c: (pad_copy(r).wait(), c)[1], 0)
        lax.fori_loop(first_unused, N_BLOCKS, lambda b, c: (tail_copy(b).wait(), c)[1], 0)

    for _ in range(TOP_K):
        pltpu.make_async_copy(h_ref, xs_ref.at[pl.ds(0, tm * ROW_CHUNKS)], sem).wait()


def _dispatch(dest, pad_lo, pad_hi, h_rows):
    tm = DISP_TM
    return pl.pallas_call(
        _dispatch_kernel,
        grid_spec=pltpu.PrefetchScalarGridSpec(
            num_scalar_prefetch=3,
            grid=(TOKENS // tm,),
            in_specs=[pl.BlockSpec((tm * ROW_CHUNKS, LANES), lambda i, *_: (i, 0))],
            out_specs=[pl.BlockSpec(memory_space=pl.ANY), pl.BlockSpec(memory_space=pltpu.SMEM)],
            scratch_shapes=[pltpu.VMEM((EXPERT_BM * ROW_CHUNKS, LANES), _F32),
                            pltpu.SemaphoreType.DMA, pltpu.SemaphoreType.DMA],
        ),
        out_shape=[jax.ShapeDtypeStruct((N_ROWS * ROW_CHUNKS, LANES), _F32),
                   jax.ShapeDtypeStruct((N_ROWS + EXPERT_BM,), jnp.int32)],
        compiler_params=pltpu.CompilerParams(
            dimension_semantics=("arbitrary",), vmem_limit_bytes=VMEM_LIMIT),
        name="dispatch",
    )(dest, pad_lo, pad_hi, h_rows)


def _expert_kernel(be_ref, nu_ref, nx_ref, rs_ref, xs_ref, w1_hbm, b1_ref, w2_hbm, b2_ref, ys_ref,
                   w1f, w2f, w1s, w2s, obuf, wsem, osem, nsw, *, layer):
    bm = EXPERT_BM
    b = pl.program_id(0)
    e = be_ref[b]
    n_used = nu_ref[0]
    half = MXU_COLS // 2

    def fetch(expert, slot):
        return (pltpu.make_async_copy(w1_hbm.at[layer, expert], w1f.at[slot], wsem.at[0, slot]),
                pltpu.make_async_copy(w2_hbm.at[layer, expert], w2f.at[slot], wsem.at[1, slot]))

    def scatter_start(step, slot):
        for i in range(bm):
            _row_copy(obuf.at[slot], i, ys_ref, rs_ref[step * bm + i], osem.at[slot]).start(priority=i % 2)

    def scatter_wait(slot):
        pltpu.make_async_copy(obuf.at[slot], ys_ref.at[pl.ds(0, bm * ROW_CHUNKS)], osem.at[slot]).wait()

    @pl.when(b == 0)
    def _():
        nsw[0] = 0
        obuf[2] = jnp.zeros((bm * ROW_CHUNKS, LANES), _F32)
        for virtual in (2, 3):
            pltpu.make_async_copy(
                obuf.at[2],
                ys_ref.at[pl.ds((TOKENS * TOP_K + (virtual - 1) * bm) * ROW_CHUNKS, bm * ROW_CHUNKS)],
                osem.at[3 - virtual]).start()
        for copy in fetch(e, 0):
            copy.start()

    @pl.when(b == n_used)
    def _():
        scatter_start(b, (b + 2) % 3)
        for slot in range(3):
            scatter_wait(slot)

    @pl.when(b < n_used)
    def _():
        out_slot = b % 3
        scatter_wait(out_slot)

        @pl.when(jnp.logical_or(b == 0, be_ref[jnp.maximum(b - 1, 0)] != e))
        def _():
            slot = nsw[0] % 2
            for copy in fetch(e, slot):
                copy.wait()
            src = lax.broadcasted_iota(jnp.int32, (MXU_COLS, MXU_COLS), 0)
            dst = lax.broadcasted_iota(jnp.int32, (MXU_COLS, MXU_COLS), 1)
            want = jnp.where(dst < half, 2 * dst, 2 * (dst - half) + 1)
            perm = (src == want).astype(_BF16)
            for blk in range(2 * D_EXPERT // MXU_COLS):
                cs = slice(blk * MXU_COLS, (blk + 1) * MXU_COLS)
                w1s[:, cs] = jnp.dot(w1f[slot, :, cs].astype(_BF16), perm,
                                     preferred_element_type=_F32).astype(_BF16)
            w2s[...] = w2f[slot].astype(_BF16)
            nsw[0] = nsw[0] + 1
            nxt = nx_ref[e]

            @pl.when(nxt < N_EXPERTS)
            def _():
                for copy in fetch(nxt, 1 - slot):
                    copy.start()

        scatter_start(b, (b + 2) % 3)
        x = jnp.concatenate(
            [xs_ref[pl.ds(j, bm, stride=ROW_CHUNKS), :] for j in range(ROW_CHUNKS)], axis=1).astype(_BF16)
        u = jnp.dot(x, w1s[...], preferred_element_type=_F32) + b1_ref[0, 0]
        acts = []
        for blk in range(2 * D_EXPERT // MXU_COLS):
            glu = jnp.minimum(u[:, blk * MXU_COLS:blk * MXU_COLS + half], SWIGLU_LIMIT)
            lin = jnp.clip(u[:, blk * MXU_COLS + half:(blk + 1) * MXU_COLS], -SWIGLU_LIMIT, SWIGLU_LIMIT)
            acts.append((glu * _sigmoid(SWIGLU_ALPHA * glu) * (lin + 1.0)).astype(_BF16))
        hidden = jnp.concatenate(acts, axis=1)
        y = jnp.dot(hidden, w2s[...], preferred_element_type=_F32) + b2_ref[0, 0]
        for j in range(ROW_CHUNKS):
            obuf[out_slot, pl.ds(j, bm, stride=ROW_CHUNKS), :] = y[:, j * LANES:(j + 1) * LANES]


def _experts(layer, block_e, n_used, next_used, row_slot, xs, w1, b1_grouped, w2, b2):
    bm = EXPERT_BM
    d, f = D_MODEL, D_EXPERT
    blk = lambda b, be, nu, nx, rs: jnp.minimum(b, nu[0] - 1)
    return pl.pallas_call(
        functools.partial(_expert_kernel, layer=layer),
        grid_spec=pltpu.PrefetchScalarGridSpec(
            num_scalar_prefetch=4,
            grid=(N_BLOCKS,),
            in_specs=[
                pl.BlockSpec((bm * ROW_CHUNKS, LANES), lambda *a: (blk(*a), 0)),
                pl.BlockSpec(memory_space=pl.ANY),
                pl.BlockSpec((1, 1, 1, 2 * f), lambda *a: (layer, a[1][blk(*a)], 0, 0)),
                pl.BlockSpec(memory_space=pl.ANY),
                pl.BlockSpec((1, 1, 1, d), lambda *a: (layer, a[1][blk(*a)], 0, 0)),
            ],
            out_specs=pl.BlockSpec(memory_space=pl.ANY),
            scratch_shapes=[
                pltpu.VMEM((2, d, 2 * f), _F32), pltpu.VMEM((2, f, d), _F32),
                pltpu.VMEM((d, 2 * f), _BF16), pltpu.VMEM((f, d), _BF16),
                pltpu.VMEM((3, bm * ROW_CHUNKS, LANES), _F32),
                pltpu.SemaphoreType.DMA((2, 2)), pltpu.SemaphoreType.DMA((3,)), pltpu.SMEM((1,), jnp.int32),
            ],
        ),
        out_shape=jax.ShapeDtypeStruct((OUT_SLOTS * ROW_CHUNKS, LANES), _F32),
        compiler_params=pltpu.CompilerParams(
            dimension_semantics=("arbitrary",), vmem_limit_bytes=VMEM_LIMIT),
        name="experts",
    )(block_e, n_used, next_used, row_slot, xs, w1, b1_grouped, w2, b2)


def _group_glu_columns(b1):
    half = MXU_COLS // 2
    j = jnp.arange(MXU_COLS)
    want = jnp.where(j < half, 2 * j, 2 * (j - half) + 1)
    cols = (jnp.arange(2 * D_EXPERT // MXU_COLS)[:, None] * MXU_COLS + want[None, :]).reshape(-1)
    return b1[..., cols]


def _combine_kernel(y0_ref, y1_ref, y2_ref, y3_ref, x_ref, gate_ref, mod_ref, fg_ref, o_ref, *, final):
    tm = COMB_TM
    gates = gate_ref[...]
    y = None
    for k, yk_ref in enumerate((y0_ref, y1_ref, y2_ref, y3_ref)):
        rows = jnp.concatenate(
            [yk_ref[pl.ds(j, tm, stride=ROW_CHUNKS), :] for j in range(ROW_CHUNKS)], axis=1)
        term = gates[:, k:k + 1] * rows
        y = term if y is None else y + term
    out = x_ref[...] + mod_ref[0][5:6] * y
    if final:
        out = out * lax.rsqrt(jnp.mean(out * out, axis=-1, keepdims=True) + NORM_EPS) * fg_ref[...]
    o_ref[...] = out


def _combine(ys, x1, gates, mod, final_g, final):
    tm = COMB_TM
    d = D_MODEL
    tiles = TOKENS // tm
    tiles_per_batch = SEQ // tm
    slot_spec = lambda k: pl.BlockSpec((tm * ROW_CHUNKS, LANES), lambda i: (k * tiles + i, 0))
    return pl.pallas_call(
        functools.partial(_combine_kernel, final=final),
        grid=(tiles,),
        in_specs=[
            slot_spec(0), slot_spec(1), slot_spec(2), slot_spec(3),
            pl.BlockSpec((tm, d), lambda i: (i, 0)),
            pl.BlockSpec((tm, TOP_K), lambda i: (i, 0)),
            pl.BlockSpec((1, N_MOD, d), lambda i: (i // tiles_per_batch, 0, 0)),
            pl.BlockSpec((1, d), lambda i: (0, 0)),
        ],
        out_specs=pl.BlockSpec((tm, d), lambda i: (i, 0)),
        out_shape=jax.ShapeDtypeStruct((TOKENS, d), _F32),
        compiler_params=pltpu.CompilerParams(
            dimension_semantics=("arbitrary",), vmem_limit_bytes=VMEM_LIMIT),
        name="combine_final" if final else "combine",
    )(ys, ys, ys, ys, x1, gates, mod, final_g.reshape(1, d))


def kernel(x, c, mod_w, mod_b, norm_g, conf_w1, conf_b1, conf_dw, conf_dw_b, conf_ln_g, conf_ln_b, conf_w2, conf_b2, sc_w_in, sc_conv, sc_w_out, router_w, router_b, exp_w1, exp_b1, exp_w2, exp_b2, final_g):
    mods = _modulation(c, mod_w, mod_b).reshape(DEPTH, BATCH, N_MOD, D_MODEL)
    b1_grouped = _group_glu_columns(exp_b1).reshape(DEPTH, N_EXPERTS, 1, 2 * D_EXPERT)
    b2_rows = exp_b2.reshape(DEPTH, N_EXPERTS, 1, D_MODEL)
    xcur = x
    for i in range(DEPTH):
        mod = mods[i]
        if i % 2 == 0:
            j = i // 2
            x1 = _conformer_mixer(xcur, mod, norm_g[i, 0], conf_w1[j], conf_b1[j], conf_dw[j], conf_dw_b[j],
                                  conf_ln_g[j], conf_ln_b[j], conf_w2[j], conf_b2[j])
        else:
            j = i // 2
            x1 = _shortconv_mixer(xcur, mod, norm_g[i, 0], sc_w_in[j], sc_conv[j], sc_w_out[j])
        x1 = x1.reshape(TOKENS, D_MODEL)
        h_rows, idx, rank, gates, counts = _route(x1, mod, norm_g[i, 1], router_w[i], router_b[i])
        dest, block_e, n_used, next_used, pad_lo, pad_hi = _routing_tables(idx, rank, counts)
        xs, row_slot = _dispatch(dest, pad_lo, pad_hi, h_rows)
        ys = _experts(i, block_e, n_used, next_used, row_slot, xs, exp_w1, b1_grouped, exp_w2, b2_rows)
        x2 = _combine(ys, x1, gates, mod, final_g, final=(i == DEPTH - 1))
        xcur = x2.reshape(BATCH, SEQ, D_MODEL)
    return xcur
```

```python
import functools

import jax
import jax.numpy as jnp
from jax import lax
from jax.experimental import pallas as pl
from jax.experimental.pallas import tpu as pltpu

D_MODEL = 1024
BATCH = 8
SEQ = 2048
DEPTH = 2
TOKENS = BATCH * SEQ
CONF_KERNEL = 31
SHORT_KERNEL = 3
N_EXPERTS = 32
TOP_K = 4
D_EXPERT = D_MODEL
SWIGLU_ALPHA = 1.702
SWIGLU_LIMIT = 7.0
NORM_EPS = 1e-5
N_MOD = 6

LANES = 128
SUBLANES = 8
MXU_COLS = 256
VMEM_LIMIT = 56 * 1024 * 1024

ROW_CHUNKS = D_MODEL // LANES
MIX_TS = 256
CONF_HALO = 32
SHORT_HALO = 8
CONV_RC = 64
ROUTE_TM = 512
COMB_TM = 256
EXPERT_BM = 256
N_ROWS = TOKENS * TOP_K + N_EXPERTS * EXPERT_BM
N_BLOCKS = N_ROWS // EXPERT_BM
INV_UNROLL = 16
RING = 3
PAD_DUMP0 = TOKENS * TOP_K + RING * EXPERT_BM
PAD_DUMP_BLOCKS = 4
PAD_DUMP_SLOTS = PAD_DUMP_BLOCKS * EXPERT_BM
OUT_SLOTS = PAD_DUMP0 + PAD_DUMP_SLOTS

_F32 = jnp.float32
_BF16 = jnp.bfloat16


def _rms_mod(x, g, scale, shift):
    y = x * lax.rsqrt(jnp.mean(x * x, axis=-1, keepdims=True) + NORM_EPS)
    return (y * g) * (1.0 + scale) + shift


def _sigmoid(x):
    return 1.0 / (1.0 + jnp.exp(-x))


def _mod_kernel(c_ref, w_ref, b_ref, o_ref):
    c = c_ref[...]
    c_act = c * _sigmoid(c)
    o_ref[0] = jnp.dot(c_act, w_ref[0], preferred_element_type=_F32,
                       precision=lax.Precision.HIGHEST) + b_ref[0]


def _modulation(c, mod_w, mod_b):
    tn = 1536
    n = N_MOD * D_MODEL
    return pl.pallas_call(
        _mod_kernel,
        grid=(DEPTH, n // tn),
        in_specs=[
            pl.BlockSpec((BATCH, D_MODEL), lambda i, j: (0, 0)),
            pl.BlockSpec((1, D_MODEL, tn), lambda i, j: (i, 0, j)),
            pl.BlockSpec((1, 1, tn), lambda i, j: (i, 0, j)),
        ],
        out_specs=pl.BlockSpec((1, BATCH, tn), lambda i, j: (i, 0, j)),
        out_shape=jax.ShapeDtypeStruct((DEPTH, BATCH, n), _F32),
        compiler_params=pltpu.CompilerParams(vmem_limit_bytes=VMEM_LIMIT),
        name="modulation",
    )(c, mod_w, mod_b.reshape(DEPTH, 1, n))


def _causal_taps(cbuf, shifted, w_ref, bias_row, out_ref, ntaps, halo, ts):
    off0 = halo - (ntaps - 1)
    for c in range(ROW_CHUNKS):
        cs = slice(c * LANES, (c + 1) * LANES)
        if shifted is not None:
            rows = cbuf.shape[0]
            whole = cbuf[:, cs]
            for r in range(1, SUBLANES):
                shifted[r - 1, :, cs] = pltpu.roll(whole, rows - r, axis=0)
        for rc in range(ts // CONV_RC):
            r0 = rc * CONV_RC
            acc = None
            for k in range(ntaps):
                s = off0 + k
                if shifted is None or s % SUBLANES == 0:
                    window = cbuf[r0 + s:r0 + s + CONV_RC, cs]
                else:
                    a = s - s % SUBLANES
                    window = shifted[s % SUBLANES - 1, r0 + a:r0 + a + CONV_RC, cs]
                term = w_ref[k:k + 1, cs] * window
                acc = term if acc is None else acc + term
            if bias_row is not None:
                acc = acc + bias_row[:, cs]
            out_ref[r0:r0 + CONV_RC, cs] = acc


def _conformer_kernel(x_ref, mod_ref, ng_ref, w1_ref, b1_ref, dw_ref, dwb_ref, lng_ref, lnb_ref,
                      w2_ref, b2_ref, o_ref, cbuf, shifted, vbuf):
    ts = MIX_TS

    @pl.when(pl.program_id(1) == 0)
    def _():
        cbuf[0:CONF_HALO, :] = jnp.zeros((CONF_HALO, D_MODEL), _F32)
        cbuf[CONF_HALO + ts:, :] = jnp.zeros((SUBLANES, D_MODEL), _F32)

    x = x_ref[0]
    m = mod_ref[0]
    h = _rms_mod(x, ng_ref[...], m[1:2], m[0:1])
    u = jnp.dot(h.astype(_BF16), w1_ref[...], preferred_element_type=_F32) + b1_ref[...]
    cbuf[CONF_HALO:CONF_HALO + ts, :] = u[:, :D_MODEL] * _sigmoid(u[:, D_MODEL:])
    _causal_taps(cbuf, shifted, dw_ref, dwb_ref[...], vbuf, CONF_KERNEL, CONF_HALO, ts)
    cbuf[0:CONF_HALO, :] = cbuf[ts:ts + CONF_HALO, :]
    v = vbuf[...]
    mu = jnp.mean(v, axis=-1, keepdims=True)
    vc = v - mu
    var = jnp.mean(vc * vc, axis=-1, keepdims=True)
    y = vc * lax.rsqrt(var + NORM_EPS) * lng_ref[...] + lnb_ref[...]
    y = y * _sigmoid(y)
    mix = jnp.dot(y.astype(_BF16), w2_ref[...], preferred_element_type=_F32) + b2_ref[...]
    o_ref[0] = x + m[2:3] * mix


def _conformer_mixer(x, mod, norm_g, w1, b1, dw, dw_b, ln_g, ln_b, w2, b2):
    d = D_MODEL
    row = lambda a: a.reshape(1, -1)
    const = lambda shape: pl.BlockSpec(shape, lambda b, s: (0,) * len(shape))
    return pl.pallas_call(
        _conformer_kernel,
        grid=(BATCH, SEQ // MIX_TS),
        in_specs=[
            pl.BlockSpec((1, MIX_TS, d), lambda b, s: (b, s, 0)),
            pl.BlockSpec((1, N_MOD, d), lambda b, s: (b, 0, 0)),
            const((1, d)), const((d, 2 * d)), const((1, 2 * d)), const((CONF_KERNEL, d)),
            const((1, d)), const((1, d)), const((1, d)), const((d, d)), const((1, d)),
        ],
        out_specs=pl.BlockSpec((1, MIX_TS, d), lambda b, s: (b, s, 0)),
        out_shape=jax.ShapeDtypeStruct((BATCH, SEQ, d), _F32),
        scratch_shapes=[pltpu.VMEM((CONF_HALO + MIX_TS + SUBLANES, d), _F32),
                        pltpu.VMEM((SUBLANES - 1, CONF_HALO + MIX_TS + SUBLANES, d), _F32),
                        pltpu.VMEM((MIX_TS, d), _F32)],
        compiler_params=pltpu.CompilerParams(
            dimension_semantics=("arbitrary", "arbitrary"), vmem_limit_bytes=VMEM_LIMIT),
        name="conformer_mixer",
    )(x, mod, row(norm_g), w1.astype(_BF16), row(b1), dw, row(dw_b), row(ln_g), row(ln_b),
      w2.astype(_BF16), row(b2))


def _shortconv_kernel(x_ref, mod_ref, ng_ref, win_ref, cw_ref, wout_ref, o_ref, cbuf, vbuf):
    ts = MIX_TS
    d = D_MODEL

    @pl.when(pl.program_id(1) == 0)
    def _():
        cbuf[0:SHORT_HALO, :] = jnp.zeros((SHORT_HALO, d), _F32)

    x = x_ref[0]
    m = mod_ref[0]
    h = _rms_mod(x, ng_ref[...], m[1:2], m[0:1])
    z = jnp.dot(h.astype(_BF16), win_ref[...], preferred_element_type=_F32)
    cbuf[SHORT_HALO:SHORT_HALO + ts, :] = z[:, d:2 * d] * z[:, 2 * d:]
    _causal_taps(cbuf, None, cw_ref, None, vbuf, SHORT_KERNEL, SHORT_HALO, ts)
    cbuf[0:SHORT_HALO, :] = cbuf[ts:ts + SHORT_HALO, :]
    y = z[:, :d] * vbuf[...]
    mix = jnp.dot(y.astype(_BF16), wout_ref[...], preferred_element_type=_F32)
    o_ref[0] = x + m[2:3] * mix


def _shortconv_mixer(x, mod, norm_g, w_in, conv_w, w_out):
    d = D_MODEL
    const = lambda shape: pl.BlockSpec(shape, lambda b, s: (0,) * len(shape))
    return pl.pallas_call(
        _shortconv_kernel,
        grid=(BATCH, SEQ // MIX_TS),
        in_specs=[
            pl.BlockSpec((1, MIX_TS, d), lambda b, s: (b, s, 0)),
            pl.BlockSpec((1, N_MOD, d), lambda b, s: (b, 0, 0)),
            const((1, d)), const((d, 3 * d)), const((SHORT_KERNEL, d)), const((d, d)),
        ],
        out_specs=pl.BlockSpec((1, MIX_TS, d), lambda b, s: (b, s, 0)),
        out_shape=jax.ShapeDtypeStruct((BATCH, SEQ, d), _F32),
        scratch_shapes=[pltpu.VMEM((SHORT_HALO + MIX_TS, d), _F32), pltpu.VMEM((MIX_TS, d), _F32)],
        compiler_params=pltpu.CompilerParams(
            dimension_semantics=("arbitrary", "arbitrary"), vmem_limit_bytes=VMEM_LIMIT),
        name="shortconv_mixer",
    )(x, mod, norm_g.reshape(1, d), w_in.astype(_BF16), conv_w, w_out.astype(_BF16))


def _route_kernel(x_ref, mod_ref, ng_ref, rw_ref, rb_ref, h_ref, idx_ref, rank_ref, gate_ref, cnt_ref):
    tm = ROUTE_TM

    @pl.when(pl.program_id(0) == 0)
    def _():
        cnt_ref[...] = jnp.zeros((1, N_EXPERTS), _F32)

    x = x_ref[...]
    m = mod_ref[0]
    h = _rms_mod(x, ng_ref[...], m[4:5], m[3:4])
    for j in range(ROW_CHUNKS):
        h_ref[pl.ds(j, tm, stride=ROW_CHUNKS), :] = h[:, j * LANES:(j + 1) * LANES]

    logits = jnp.dot(h, rw_ref[...], preferred_element_type=_F32,
                     precision=lax.Precision.HIGHEST) + rb_ref[...]
    lane = lax.broadcasted_iota(jnp.int32, (tm, N_EXPERTS), 1).astype(_F32)
    work = logits
    vals, idxs, hots = [], [], []
    for _ in range(TOP_K):
        top = jnp.max(work, axis=1, keepdims=True)
        pick = jnp.min(jnp.where(work == top, lane, float(N_EXPERTS)), axis=1, keepdims=True)
        hot = lane == pick
        vals.append(top)
        idxs.append(pick)
        hots.append(hot)
        work = jnp.where(hot, -jnp.inf, work)
    exps = [jnp.exp(v - vals[0]) for v in vals]
    denom = exps[0] + exps[1] + exps[2] + exps[3]

    multi = (hots[0] | hots[1] | hots[2] | hots[3]).astype(_BF16)
    r_i = lax.broadcasted_iota(jnp.int32, (tm, tm), 0)
    c_i = lax.broadcasted_iota(jnp.int32, (tm, tm), 1)
    before = (c_i < r_i).astype(_BF16)
    pos = cnt_ref[...] + jnp.dot(before, multi, preferred_element_type=_F32)
    cnt_ref[...] = cnt_ref[...] + jnp.sum(multi.astype(_F32), axis=0, keepdims=True)

    col = lax.broadcasted_iota(jnp.int32, (tm, TOP_K), 1)
    idx_out = jnp.zeros((tm, TOP_K), _F32)
    rank_out = jnp.zeros((tm, TOP_K), _F32)
    gate_out = jnp.zeros((tm, TOP_K), _F32)
    for k in range(TOP_K):
        rank_k = jnp.sum(jnp.where(hots[k], pos, 0.0), axis=1, keepdims=True)
        idx_out = jnp.where(col == k, idxs[k], idx_out)
        rank_out = jnp.where(col == k, rank_k, rank_out)
        gate_out = jnp.where(col == k, exps[k] / denom, gate_out)
    idx_ref[...] = idx_out.astype(jnp.int32)
    rank_ref[...] = rank_out.astype(jnp.int32)
    gate_ref[...] = gate_out


def _route(x1, mod, norm_g, router_w, router_b):
    d = D_MODEL
    tm = ROUTE_TM
    tiles_per_batch = SEQ // tm
    return pl.pallas_call(
        _route_kernel,
        grid=(TOKENS // tm,),
        in_specs=[
            pl.BlockSpec((tm, d), lambda i: (i, 0)),
            pl.BlockSpec((1, N_MOD, d), lambda i: (i // tiles_per_batch, 0, 0)),
            pl.BlockSpec((1, d), lambda i: (0, 0)),
            pl.BlockSpec((d, N_EXPERTS), lambda i: (0, 0)),
            pl.BlockSpec((1, N_EXPERTS), lambda i: (0, 0)),
        ],
        out_specs=[
            pl.BlockSpec((tm * ROW_CHUNKS, LANES), lambda i: (i, 0)),
            pl.BlockSpec((tm, TOP_K), lambda i: (i, 0)),
            pl.BlockSpec((tm, TOP_K), lambda i: (i, 0)),
            pl.BlockSpec((tm, TOP_K), lambda i: (i, 0)),
            pl.BlockSpec((1, N_EXPERTS), lambda i: (0, 0)),
        ],
        out_shape=[
            jax.ShapeDtypeStruct((TOKENS * ROW_CHUNKS, LANES), _F32),
            jax.ShapeDtypeStruct((TOKENS, TOP_K), jnp.int32),
            jax.ShapeDtypeStruct((TOKENS, TOP_K), jnp.int32),
            jax.ShapeDtypeStruct((TOKENS, TOP_K), _F32),
            jax.ShapeDtypeStruct((1, N_EXPERTS), _F32),
        ],
        compiler_params=pltpu.CompilerParams(
            dimension_semantics=("arbitrary",), vmem_limit_bytes=VMEM_LIMIT),
        name="route",
    )(x1, mod, norm_g.reshape(1, d), router_w, router_b.reshape(1, N_EXPERTS))


def _routing_tables(idx, rank, counts):
    counts = counts.reshape(N_EXPERTS).astype(jnp.int32)
    padded = (counts + EXPERT_BM - 1) // EXPERT_BM * EXPERT_BM
    pad_ends = jnp.cumsum(padded)
    pad_starts = pad_ends - padded
    hot = idx[..., None] == jnp.arange(N_EXPERTS, dtype=jnp.int32)
    dest = jnp.sum(jnp.where(hot, pad_starts, 0), axis=-1) + rank
    block_start = jnp.arange(N_BLOCKS, dtype=jnp.int32) * EXPERT_BM
    block_e = jnp.sum((block_start[:, None] >= pad_ends[None, :]).astype(jnp.int32), axis=1)
    block_e = jnp.minimum(block_e, N_EXPERTS - 1).astype(jnp.int32)
    n_used = (pad_ends[-1:] // EXPERT_BM).astype(jnp.int32)
    ids = jnp.arange(N_EXPERTS, dtype=jnp.int32)
    later_used = (ids[None, :] > ids[:, None]) & (counts[None, :] > 0)
    next_used = jnp.min(jnp.where(later_used, ids[None, :], N_EXPERTS), axis=1).astype(jnp.int32)
    dest = dest.reshape(TOKENS * TOP_K).astype(jnp.int32)
    pad_lo = (pad_starts + counts).astype(jnp.int32)
    return dest, block_e, n_used, next_used, pad_lo, pad_ends.astype(jnp.int32)


def _invert_kernel(dest_ref, lo_ref, hi_ref, rs_ref):
    def set_virtual(r, carry):
        rs_ref[r] = TOKENS * TOP_K + r
        return carry

    def set_dump(r, carry):
        rs_ref[r + EXPERT_BM] = PAD_DUMP0 + (r & (PAD_DUMP_SLOTS - 1))
        return carry

    lax.fori_loop(0, EXPERT_BM, set_virtual, 0)
    for e in range(N_EXPERTS):
        lax.fori_loop(lo_ref[e], hi_ref[e], set_dump, 0)
    lax.fori_loop(hi_ref[N_EXPERTS - 1], N_ROWS, set_dump, 0)

    def fill(g, carry):
        rows = [dest_ref[g * INV_UNROLL + u] for u in range(INV_UNROLL)]
        for u, row in enumerate(rows):
            rs_ref[row + EXPERT_BM] = (u % TOP_K) * TOKENS + g * (INV_UNROLL // TOP_K) + u // TOP_K
        return carry

    lax.fori_loop(0, TOKENS * TOP_K // INV_UNROLL, fill, 0)


def _invert(dest, pad_lo, pad_hi):
    return pl.pallas_call(
        _invert_kernel,
        grid_spec=pltpu.PrefetchScalarGridSpec(
            num_scalar_prefetch=3,
            grid=(1,),
            in_specs=[],
            out_specs=pl.BlockSpec(memory_space=pltpu.SMEM),
        ),
        out_shape=jax.ShapeDtypeStruct((N_ROWS + EXPERT_BM,), jnp.int32),
        name="invert",
    )(dest, pad_lo, pad_hi)


def _row_copy(src, src_row, dst, dst_row, sem):
    def tile(row):
        start = row * ROW_CHUNKS
        return pl.ds(start if isinstance(row, int) else pl.multiple_of(start, ROW_CHUNKS), ROW_CHUNKS)

    return pltpu.make_async_copy(src.at[tile(src_row)], dst.at[tile(dst_row)], sem)


def _expert_kernel(be_ref, nu_ref, nx_ref, rs_ref, h_hbm, w1_hbm, b1_ref, w2_hbm, b2_ref, ys_ref,
                   w1f, w2f, w1s, w2s, xbuf, obuf, wsem, xsem, osem, isem, nsw, *, layer):
    bm = EXPERT_BM
    rows = bm * ROW_CHUNKS
    b = pl.program_id(0)
    e = be_ref[b]
    n_used = nu_ref[0]
    half = MXU_COLS // 2

    def fetch(expert, slot):
        return (pltpu.make_async_copy(w1_hbm.at[layer, expert], w1f.at[slot], wsem.at[0, slot]),
                pltpu.make_async_copy(w2_hbm.at[layer, expert], w2f.at[slot], wsem.at[1, slot]))

    def for_rows(body, unrolled):
        if unrolled:
            for i in range(bm):
                body(i, i % 2)
        else:
            def pair(p, carry):
                body(2 * p, 0)
                body(2 * p + 1, 1)
                return carry
            lax.fori_loop(0, bm // 2, pair, 0)

    def gather_start(block, slot, unrolled=True):
        def one(i, priority):
            token = rs_ref[(block + 1) * bm + i] & (TOKENS - 1)
            _row_copy(h_hbm, token, xbuf.at[slot], i, xsem.at[slot]).start(priority=priority)
        for_rows(one, unrolled)

    def gather_wait(slot):
        pltpu.make_async_copy(h_hbm.at[pl.ds(0, rows)], xbuf.at[slot], xsem.at[slot]).wait()

    def scatter_start(block, slot, unrolled=True):
        def one(i, priority):
            _row_copy(obuf.at[slot], i, ys_ref, rs_ref[(block + 1) * bm + i], osem.at[slot]).start(priority=priority)
        for_rows(one, unrolled)

    def scatter_wait(slot):
        pltpu.make_async_copy(obuf.at[slot], ys_ref.at[pl.ds(0, rows)], osem.at[slot]).wait()

    def dump_block(which):
        return ys_ref.at[pl.ds((TOKENS * TOP_K + which * bm) * ROW_CHUNKS, rows)]

    @pl.when(b == 0)
    def _():
        nsw[0] = 0
        obuf[RING - 1] = jnp.zeros((rows, LANES), _F32)
        for which in range(RING, RING + PAD_DUMP_BLOCKS):
            pltpu.make_async_copy(obuf.at[RING - 1], dump_block(which), isem).start()
        for which in range(RING, RING + PAD_DUMP_BLOCKS):
            pltpu.make_async_copy(obuf.at[RING - 1], dump_block(which), isem).wait()
        for virtual in range(2, RING + 1):
            pltpu.make_async_copy(obuf.at[RING - 1], dump_block(virtual - 1), osem.at[RING - virtual]).start()
        for copy in fetch(e, 0):
            copy.start()
        gather_start(0, 0, unrolled=False)

    @pl.when(b == n_used)
    def _():
        scatter_start(b - 1, (b + RING - 1) % RING, unrolled=False)
        for slot in range(RING):
            scatter_wait(slot)
        gather_wait(b % 2)

    @pl.when(b < n_used)
    def _():
        out_slot = b % RING
        gather_wait(b % 2)

        @pl.when(jnp.logical_or(b == 0, be_ref[jnp.maximum(b - 1, 0)] != e))
        def _():
            slot = nsw[0] % 2
            for copy in fetch(e, slot):
                copy.wait()
            src = lax.broadcasted_iota(jnp.int32, (MXU_COLS, MXU_COLS), 0)
            dst = lax.broadcasted_iota(jnp.int32, (MXU_COLS, MXU_COLS), 1)
            want = jnp.where(dst < half, 2 * dst, 2 * (dst - half) + 1)
            perm = (src == want).astype(_BF16)
            for blk in range(2 * D_EXPERT // MXU_COLS):
                cs = slice(blk * MXU_COLS, (blk + 1) * MXU_COLS)
                w1s[:, cs] = jnp.dot(w1f[slot, :, cs].astype(_BF16), perm,
                                     preferred_element_type=_F32).astype(_BF16)
            w2s[...] = w2f[slot].astype(_BF16)
            nsw[0] = nsw[0] + 1
            nxt = nx_ref[e]

            @pl.when(nxt < N_EXPERTS)
            def _():
                for copy in fetch(nxt, 1 - slot):
                    copy.start()

        x = jnp.concatenate(
            [xbuf[b % 2, pl.ds(j, bm, stride=ROW_CHUNKS), :] for j in range(ROW_CHUNKS)], axis=1).astype(_BF16)
        gather_start(b + 1, (b + 1) % 2)
        u = jnp.dot(x, w1s[...], preferred_element_type=_F32) + b1_ref[0, 0]
        acts = []
        for blk in range(2 * D_EXPERT // MXU_COLS):
            glu = jnp.minimum(u[:, blk * MXU_COLS:blk * MXU_COLS + half], SWIGLU_LIMIT)
            lin = jnp.clip(u[:, blk * MXU_COLS + half:(blk + 1) * MXU_COLS], -SWIGLU_LIMIT, SWIGLU_LIMIT)
            acts.append((glu * _sigmoid(SWIGLU_ALPHA * glu) * (lin + 1.0)).astype(_BF16))
        hidden = jnp.concatenate(acts, axis=1)
        scatter_wait(out_slot)
        scatter_start(b - 1, (b + RING - 1) % RING)
        y = jnp.dot(hidden, w2s[...], preferred_element_type=_F32) + b2_ref[0, 0]
        for j in range(ROW_CHUNKS):
            obuf[out_slot, pl.ds(j, bm, stride=ROW_CHUNKS), :] = y[:, j * LANES:(j + 1) * LANES]


def _experts(layer, block_e, n_used, next_used, row_slot, h_rows, w1, b1_grouped, w2, b2):
    bm = EXPERT_BM
    d, f = D_MODEL, D_EXPERT
    assert TOKENS & (TOKENS - 1) == 0, "token of a slot is taken with a bit mask"
    blk = lambda b, be, nu, nx, rs: jnp.minimum(b, nu[0] - 1)
    return pl.pallas_call(
        functools.partial(_expert_kernel, layer=layer),
        grid_spec=pltpu.PrefetchScalarGridSpec(
            num_scalar_prefetch=4,
            grid=(N_BLOCKS,),
            in_specs=[
                pl.BlockSpec(memory_space=pl.ANY),
                pl.BlockSpec(memory_space=pl.ANY),
                pl.BlockSpec((1, 1, 1, 2 * f), lambda *a: (layer, a[1][blk(*a)], 0, 0)),
                pl.BlockSpec(memory_space=pl.ANY),
                pl.BlockSpec((1, 1, 1, d), lambda *a: (layer, a[1][blk(*a)], 0, 0)),
            ],
            out_specs=pl.BlockSpec(memory_space=pl.ANY),
            scratch_shapes=[
                pltpu.VMEM((2, d, 2 * f), _F32), pltpu.VMEM((2, f, d), _F32),
                pltpu.VMEM((d, 2 * f), _BF16), pltpu.VMEM((f, d), _BF16),
                pltpu.VMEM((2, bm * ROW_CHUNKS, LANES), _F32),
                pltpu.VMEM((RING, bm * ROW_CHUNKS, LANES), _F32),
                pltpu.SemaphoreType.DMA((2, 2)), pltpu.SemaphoreType.DMA((2,)),
                pltpu.SemaphoreType.DMA((RING,)), pltpu.SemaphoreType.DMA, pltpu.SMEM((1,), jnp.int32),
            ],
        ),
        out_shape=jax.ShapeDtypeStruct((OUT_SLOTS * ROW_CHUNKS, LANES), _F32),
        compiler_params=pltpu.CompilerParams(
            dimension_semantics=("arbitrary",), vmem_limit_bytes=VMEM_LIMIT),
        name="experts",
    )(block_e, n_used, next_used, row_slot, h_rows, w1, b1_grouped, w2, b2)


def _group_glu_columns(b1):
    half = MXU_COLS // 2
    j = jnp.arange(MXU_COLS)
    want = jnp.where(j < half, 2 * j, 2 * (j - half) + 1)
    cols = (jnp.arange(2 * D_EXPERT // MXU_COLS)[:, None] * MXU_COLS + want[None, :]).reshape(-1)
    return b1[..., cols]


def _combine_kernel(y0_ref, y1_ref, y2_ref, y3_ref, x_ref, gate_ref, mod_ref, fg_ref, o_ref, *, final):
    tm = COMB_TM
    gates = gate_ref[...]
    y = None
    for k, yk_ref in enumerate((y0_ref, y1_ref, y2_ref, y3_ref)):
        rows = jnp.concatenate(
            [yk_ref[pl.ds(j, tm, stride=ROW_CHUNKS), :] for j in range(ROW_CHUNKS)], axis=1)
        term = gates[:, k:k + 1] * rows
        y = term if y is None else y + term
    out = x_ref[...] + mod_ref[0][5:6] * y
    if final:
        out = out * lax.rsqrt(jnp.mean(out * out, axis=-1, keepdims=True) + NORM_EPS) * fg_ref[...]
    o_ref[...] = out


def _combine(ys, x1, gates, mod, final_g, final):
    tm = COMB_TM
    d = D_MODEL
    tiles = TOKENS // tm
    tiles_per_batch = SEQ // tm
    slot_spec = lambda k: pl.BlockSpec((tm * ROW_CHUNKS, LANES), lambda i: (k * tiles + i, 0))
    return pl.pallas_call(
        functools.partial(_combine_kernel, final=final),
        grid=(tiles,),
        in_specs=[
            slot_spec(0), slot_spec(1), slot_spec(2), slot_spec(3),
            pl.BlockSpec((tm, d), lambda i: (i, 0)),
            pl.BlockSpec((tm, TOP_K), lambda i: (i, 0)),
            pl.BlockSpec((1, N_MOD, d), lambda i: (i // tiles_per_batch, 0, 0)),
            pl.BlockSpec((1, d), lambda i: (0, 0)),
        ],
        out_specs=pl.BlockSpec((tm, d), lambda i: (i, 0)),
        out_shape=jax.ShapeDtypeStruct((TOKENS, d), _F32),
        compiler_params=pltpu.CompilerParams(
            dimension_semantics=("arbitrary",), vmem_limit_bytes=VMEM_LIMIT),
        name="combine_final" if final else "combine",
    )(ys, ys, ys, ys, x1, gates, mod, final_g.reshape(1, d))


def kernel(x, c, mod_w, mod_b, norm_g, conf_w1, conf_b1, conf_dw, conf_dw_b, conf_ln_g, conf_ln_b, conf_w2, conf_b2, sc_w_in, sc_conv, sc_w_out, router_w, router_b, exp_w1, exp_b1, exp_w2, exp_b2, final_g):
    mods = _modulation(c, mod_w, mod_b).reshape(DEPTH, BATCH, N_MOD, D_MODEL)
    b1_grouped = _group_glu_columns(exp_b1).reshape(DEPTH, N_EXPERTS, 1, 2 * D_EXPERT)
    b2_rows = exp_b2.reshape(DEPTH, N_EXPERTS, 1, D_MODEL)
    xcur = x
    for i in range(DEPTH):
        mod = mods[i]
        if i % 2 == 0:
            j = i // 2
            x1 = _conformer_mixer(xcur, mod, norm_g[i, 0], conf_w1[j], conf_b1[j], conf_dw[j], conf_dw_b[j],
                                  conf_ln_g[j], conf_ln_b[j], conf_w2[j], conf_b2[j])
        else:
            j = i // 2
            x1 = _shortconv_mixer(xcur, mod, norm_g[i, 0], sc_w_in[j], sc_conv[j], sc_w_out[j])
        x1 = x1.reshape(TOKENS, D_MODEL)
        h_rows, idx, rank, gates, counts = _route(x1, mod, norm_g[i, 1], router_w[i], router_b[i])
        dest, block_e, n_used, next_used, pad_lo, pad_hi = _routing_tables(idx, rank, counts)
        row_slot = _invert(dest, pad_lo, pad_hi)
        ys = _experts(i, block_e, n_used, next_used, row_slot, h_rows, exp_w1, b1_grouped, exp_w2, b2_rows)
        x2 = _combine(ys, x1, gates, mod, final_g, final=(i == DEPTH - 1))
        xcur = x2.reshape(BATCH, SEQ, D_MODEL)
    return xcur
```

```python
import functools

import jax
import jax.numpy as jnp
from jax import lax
from jax.experimental import pallas as pl
from jax.experimental.pallas import tpu as pltpu

D_MODEL = 1024
BATCH = 8
SEQ = 2048
DEPTH = 2
TOKENS = BATCH * SEQ
CONF_KERNEL = 31
SHORT_KERNEL = 3
N_EXPERTS = 32
TOP_K = 4
D_EXPERT = D_MODEL
SWIGLU_ALPHA = 1.702
SWIGLU_LIMIT = 7.0
NORM_EPS = 1e-5
N_MOD = 6

LANES = 128
SUBLANES = 8
MXU_COLS = 256
VMEM_LIMIT = 56 * 1024 * 1024

HALF_D = D_MODEL // 2
ROW_WORDS = HALF_D // LANES
MIX_TS = 256
CONF_HALO = 32
SHORT_HALO = 8
CONV_RC = 64
ROUTE_TM = 512
DISP_TM = 512
DISP_UNROLL = 4
COMB_TM = 256
EXPERT_BM = 256
N_ROWS = TOKENS * TOP_K + N_EXPERTS * EXPERT_BM
N_BLOCKS = N_ROWS // EXPERT_BM
RING = 3
PAD_DUMP0 = TOKENS * TOP_K + RING * EXPERT_BM
PAD_DUMP_BLOCKS = 4
OUT_SLOTS = PAD_DUMP0 + PAD_DUMP_BLOCKS * EXPERT_BM

_F32 = jnp.float32
_BF16 = jnp.bfloat16
_U32 = jnp.uint32
_HI_MASK = 0xFFFF0000


def _rms_mod(x, g, scale, shift):
    y = x * lax.rsqrt(jnp.mean(x * x, axis=-1, keepdims=True) + NORM_EPS)
    return (y * g) * (1.0 + scale) + shift


def _sigmoid(x):
    return 1.0 / (1.0 + jnp.exp(-x))


def _pack_rows(v):
    bits = lambda a: lax.bitcast_convert_type(a.astype(_BF16).astype(_F32), _U32)
    return (bits(v[:, HALF_D:]) & _U32(_HI_MASK)) | (bits(v[:, :HALF_D]) >> 16)


def _store_packed(ref, lead, v):
    rows = v.shape[0]
    words = _pack_rows(v)
    for j in range(ROW_WORDS):
        ref[lead + (pl.ds(j, rows, stride=ROW_WORDS), slice(None))] = words[:, j * LANES:(j + 1) * LANES]


def _load_packed(ref, lead, rows, dtype):
    lo, hi = [], []
    for j in range(ROW_WORDS):
        w = ref[lead + (pl.ds(j, rows, stride=ROW_WORDS), slice(None))]
        lo.append(lax.bitcast_convert_type(w << 16, _F32).astype(dtype))
        hi.append(lax.bitcast_convert_type(w & _U32(_HI_MASK), _F32).astype(dtype))
    return jnp.concatenate(lo + hi, axis=1)


def _mod_kernel(c_ref, w_ref, b_ref, o_ref):
    c = c_ref[...]
    c_act = c * _sigmoid(c)
    o_ref[0] = jnp.dot(c_act, w_ref[0], preferred_element_type=_F32,
                       precision=lax.Precision.HIGHEST) + b_ref[0]


def _modulation(c, mod_w, mod_b):
    tn = 1536
    n = N_MOD * D_MODEL
    return pl.pallas_call(
        _mod_kernel,
        grid=(DEPTH, n // tn),
        in_specs=[
            pl.BlockSpec((BATCH, D_MODEL), lambda i, j: (0, 0)),
            pl.BlockSpec((1, D_MODEL, tn), lambda i, j: (i, 0, j)),
            pl.BlockSpec((1, 1, tn), lambda i, j: (i, 0, j)),
        ],
        out_specs=pl.BlockSpec((1, BATCH, tn), lambda i, j: (i, 0, j)),
        out_shape=jax.ShapeDtypeStruct((DEPTH, BATCH, n), _F32),
        compiler_params=pltpu.CompilerParams(vmem_limit_bytes=VMEM_LIMIT),
        name="modulation",
    )(c, mod_w, mod_b.reshape(DEPTH, 1, n))


def _causal_taps(cbuf, shifted, w_ref, bias_row, out_ref, ntaps, halo, ts):
    off0 = halo - (ntaps - 1)
    for c in range(D_MODEL // LANES):
        cs = slice(c * LANES, (c + 1) * LANES)
        if shifted is not None:
            rows = cbuf.shape[0]
            whole = cbuf[:, cs]
            for r in range(1, SUBLANES):
                shifted[r - 1, :, cs] = pltpu.roll(whole, rows - r, axis=0)
        for rc in range(ts // CONV_RC):
            r0 = rc * CONV_RC
            acc = None
            for k in range(ntaps):
                s = off0 + k
                if shifted is None or s % SUBLANES == 0:
                    window = cbuf[r0 + s:r0 + s + CONV_RC, cs]
                else:
                    a = s - s % SUBLANES
                    window = shifted[s % SUBLANES - 1, r0 + a:r0 + a + CONV_RC, cs]
                term = w_ref[k:k + 1, cs] * window
                acc = term if acc is None else acc + term
            if bias_row is not None:
                acc = acc + bias_row[:, cs]
            out_ref[r0:r0 + CONV_RC, cs] = acc


def _conformer_kernel(x_ref, mod_ref, ng_ref, w1_ref, b1_ref, dw_ref, dwb_ref, lng_ref, lnb_ref,
                      w2_ref, b2_ref, o_ref, cbuf, shifted, vbuf):
    ts = MIX_TS

    @pl.when(pl.program_id(1) == 0)
    def _():
        cbuf[0:CONF_HALO, :] = jnp.zeros((CONF_HALO, D_MODEL), _F32)
        cbuf[CONF_HALO + ts:, :] = jnp.zeros((SUBLANES, D_MODEL), _F32)

    x = x_ref[0]
    m = mod_ref[0]
    h = _rms_mod(x, ng_ref[...], m[1:2], m[0:1])
    u = jnp.dot(h.astype(_BF16), w1_ref[...], preferred_element_type=_F32) + b1_ref[...]
    cbuf[CONF_HALO:CONF_HALO + ts, :] = u[:, :D_MODEL] * _sigmoid(u[:, D_MODEL:])
    _causal_taps(cbuf, shifted, dw_ref, dwb_ref[...], vbuf, CONF_KERNEL, CONF_HALO, ts)
    cbuf[0:CONF_HALO, :] = cbuf[ts:ts + CONF_HALO, :]
    v = vbuf[...]
    mu = jnp.mean(v, axis=-1, keepdims=True)
    vc = v - mu
    var = jnp.mean(vc * vc, axis=-1, keepdims=True)
    y = vc * lax.rsqrt(var + NORM_EPS) * lng_ref[...] + lnb_ref[...]
    y = y * _sigmoid(y)
    mix = jnp.dot(y.astype(_BF16), w2_ref[...], preferred_element_type=_F32) + b2_ref[...]
    o_ref[0] = x + m[2:3] * mix


def _conformer_mixer(x, mod, norm_g, w1, b1, dw, dw_b, ln_g, ln_b, w2, b2):
    d = D_MODEL
    row = lambda a: a.reshape(1, -1)
    const = lambda shape: pl.BlockSpec(shape, lambda b, s: (0,) * len(shape))
    return pl.pallas_call(
        _conformer_kernel,
        grid=(BATCH, SEQ // MIX_TS),
        in_specs=[
            pl.BlockSpec((1, MIX_TS, d), lambda b, s: (b, s, 0)),
            pl.BlockSpec((1, N_MOD, d), lambda b, s: (b, 0, 0)),
            const((1, d)), const((d, 2 * d)), const((1, 2 * d)), const((CONF_KERNEL, d)),
            const((1, d)), const((1, d)), const((1, d)), const((d, d)), const((1, d)),
        ],
        out_specs=pl.BlockSpec((1, MIX_TS, d), lambda b, s: (b, s, 0)),
        out_shape=jax.ShapeDtypeStruct((BATCH, SEQ, d), _F32),
        scratch_shapes=[pltpu.VMEM((CONF_HALO + MIX_TS + SUBLANES, d), _F32),
                        pltpu.VMEM((SUBLANES - 1, CONF_HALO + MIX_TS + SUBLANES, d), _F32),
                        pltpu.VMEM((MIX_TS, d), _F32)],
        compiler_params=pltpu.CompilerParams(
            dimension_semantics=("arbitrary", "arbitrary"), vmem_limit_bytes=VMEM_LIMIT),
        name="conformer_mixer",
    )(x, mod, row(norm_g), w1.astype(_BF16), row(b1), dw, row(dw_b), row(ln_g), row(ln_b),
      w2.astype(_BF16), row(b2))


def _shortconv_kernel(x_ref, mod_ref, ng_ref, win_ref, cw_ref, wout_ref, o_ref, cbuf, vbuf):
    ts = MIX_TS
    d = D_MODEL

    @pl.when(pl.program_id(1) == 0)
    def _():
        cbuf[0:SHORT_HALO, :] = jnp.zeros((SHORT_HALO, d), _F32)

    x = x_ref[0]
    m = mod_ref[0]
    h = _rms_mod(x, ng_ref[...], m[1:2], m[0:1])
    z = jnp.dot(h.astype(_BF16), win_ref[...], preferred_element_type=_F32)
    cbuf[SHORT_HALO:SHORT_HALO + ts, :] = z[:, d:2 * d] * z[:, 2 * d:]
    _causal_taps(cbuf, None, cw_ref, None, vbuf, SHORT_KERNEL, SHORT_HALO, ts)
    cbuf[0:SHORT_HALO, :] = cbuf[ts:ts + SHORT_HALO, :]
    y = z[:, :d] * vbuf[...]
    mix = jnp.dot(y.astype(_BF16), wout_ref[...], preferred_element_type=_F32)
    o_ref[0] = x + m[2:3] * mix


def _shortconv_mixer(x, mod, norm_g, w_in, conv_w, w_out):
    d = D_MODEL
    const = lambda shape: pl.BlockSpec(shape, lambda b, s: (0,) * len(shape))
    return pl.pallas_call(
        _shortconv_kernel,
        grid=(BATCH, SEQ // MIX_TS),
        in_specs=[
            pl.BlockSpec((1, MIX_TS, d), lambda b, s: (b, s, 0)),
            pl.BlockSpec((1, N_MOD, d), lambda b, s: (b, 0, 0)),
            const((1, d)), const((d, 3 * d)), const((SHORT_KERNEL, d)), const((d, d)),
        ],
        out_specs=pl.BlockSpec((1, MIX_TS, d), lambda b, s: (b, s, 0)),
        out_shape=jax.ShapeDtypeStruct((BATCH, SEQ, d), _F32),
        scratch_shapes=[pltpu.VMEM((SHORT_HALO + MIX_TS, d), _F32), pltpu.VMEM((MIX_TS, d), _F32)],
        compiler_params=pltpu.CompilerParams(
            dimension_semantics=("arbitrary", "arbitrary"), vmem_limit_bytes=VMEM_LIMIT),
        name="shortconv_mixer",
    )(x, mod, norm_g.reshape(1, d), w_in.astype(_BF16), conv_w, w_out.astype(_BF16))


def _route_kernel(x_ref, mod_ref, ng_ref, rw_ref, rb_ref, h_ref, idx_ref, rank_ref, gate_ref, cnt_ref):
    tm = ROUTE_TM

    @pl.when(pl.program_id(0) == 0)
    def _():
        cnt_ref[...] = jnp.zeros((1, N_EXPERTS), _F32)

    x = x_ref[...]
    m = mod_ref[0]
    h = _rms_mod(x, ng_ref[...], m[4:5], m[3:4])
    _store_packed(h_ref, (), h)

    logits = jnp.dot(h, rw_ref[...], preferred_element_type=_F32,
                     precision=lax.Precision.HIGHEST) + rb_ref[...]
    lane = lax.broadcasted_iota(jnp.int32, (tm, N_EXPERTS), 1).astype(_F32)
    work = logits
    vals, idxs, hots = [], [], []
    for _ in range(TOP_K):
        top = jnp.max(work, axis=1, keepdims=True)
        pick = jnp.min(jnp.where(work == top, lane, float(N_EXPERTS)), axis=1, keepdims=True)
        hot = lane == pick
        vals.append(top)
        idxs.append(pick)
        hots.append(hot)
        work = jnp.where(hot, -jnp.inf, work)
    exps = [jnp.exp(v - vals[0]) for v in vals]
    denom = exps[0] + exps[1] + exps[2] + exps[3]

    multi = (hots[0] | hots[1] | hots[2] | hots[3]).astype(_BF16)
    r_i = lax.broadcasted_iota(jnp.int32, (tm, tm), 0)
    c_i = lax.broadcasted_iota(jnp.int32, (tm, tm), 1)
    before = (c_i < r_i).astype(_BF16)
    pos = cnt_ref[...] + jnp.dot(before, multi, preferred_element_type=_F32)
    cnt_ref[...] = cnt_ref[...] + jnp.sum(multi.astype(_F32), axis=0, keepdims=True)

    col = lax.broadcasted_iota(jnp.int32, (tm, TOP_K), 1)
    idx_out = jnp.zeros((tm, TOP_K), _F32)
    rank_out = jnp.zeros((tm, TOP_K), _F32)
    gate_out = jnp.zeros((tm, TOP_K), _F32)
    for k in range(TOP_K):
        rank_k = jnp.sum(jnp.where(hots[k], pos, 0.0), axis=1, keepdims=True)
        idx_out = jnp.where(col == k, idxs[k], idx_out)
        rank_out = jnp.where(col == k, rank_k, rank_out)
        gate_out = jnp.where(col == k, exps[k] / denom, gate_out)
    idx_ref[...] = idx_out.astype(jnp.int32)
    rank_ref[...] = rank_out.astype(jnp.int32)
    gate_ref[...] = gate_out


def _route(x1, mod, norm_g, router_w, router_b):
    d = D_MODEL
    tm = ROUTE_TM
    tiles_per_batch = SEQ // tm
    return pl.pallas_call(
        _route_kernel,
        grid=(TOKENS // tm,),
        in_specs=[
            pl.BlockSpec((tm, d), lambda i: (i, 0)),
            pl.BlockSpec((1, N_MOD, d), lambda i: (i // tiles_per_batch, 0, 0)),
            pl.BlockSpec((1, d), lambda i: (0, 0)),
            pl.BlockSpec((d, N_EXPERTS), lambda i: (0, 0)),
            pl.BlockSpec((1, N_EXPERTS), lambda i: (0, 0)),
        ],
        out_specs=[
            pl.BlockSpec((tm * ROW_WORDS, LANES), lambda i: (i, 0)),
            pl.BlockSpec((tm, TOP_K), lambda i: (i, 0)),
            pl.BlockSpec((tm, TOP_K), lambda i: (i, 0)),
            pl.BlockSpec((tm, TOP_K), lambda i: (i, 0)),
            pl.BlockSpec((1, N_EXPERTS), lambda i: (0, 0)),
        ],
        out_shape=[
            jax.ShapeDtypeStruct((TOKENS * ROW_WORDS, LANES), _U32),
            jax.ShapeDtypeStruct((TOKENS, TOP_K), jnp.int32),
            jax.ShapeDtypeStruct((TOKENS, TOP_K), jnp.int32),
            jax.ShapeDtypeStruct((TOKENS, TOP_K), _F32),
            jax.ShapeDtypeStruct((1, N_EXPERTS), _F32),
        ],
        compiler_params=pltpu.CompilerParams(
            dimension_semantics=("arbitrary",), vmem_limit_bytes=VMEM_LIMIT),
        name="route",
    )(x1, mod, norm_g.reshape(1, d), router_w, router_b.reshape(1, N_EXPERTS))


def _routing_tables(idx, rank, counts):
    counts = counts.reshape(N_EXPERTS).astype(jnp.int32)
    padded = (counts + EXPERT_BM - 1) // EXPERT_BM * EXPERT_BM
    pad_ends = jnp.cumsum(padded)
    pad_starts = pad_ends - padded
    hot = idx[..., None] == jnp.arange(N_EXPERTS, dtype=jnp.int32)
    dest = jnp.sum(jnp.where(hot, pad_starts, 0), axis=-1) + rank
    block_start = jnp.arange(N_BLOCKS, dtype=jnp.int32) * EXPERT_BM
    block_e = jnp.sum((block_start[:, None] >= pad_ends[None, :]).astype(jnp.int32), axis=1)
    block_e = jnp.minimum(block_e, N_EXPERTS - 1).astype(jnp.int32)
    n_used = (pad_ends[-1:] // EXPERT_BM).astype(jnp.int32)
    ids = jnp.arange(N_EXPERTS, dtype=jnp.int32)
    later_used = (ids[None, :] > ids[:, None]) & (counts[None, :] > 0)
    next_used = jnp.min(jnp.where(later_used, ids[None, :], N_EXPERTS), axis=1).astype(jnp.int32)
    pad_lo = (pad_starts + counts).astype(jnp.int32)
    dest = dest.reshape(TOKENS * TOP_K).astype(jnp.int32)
    return dest, block_e, n_used, next_used, pad_lo, pad_ends.astype(jnp.int32)


def _vmem_row(buf, row):
    start = row * ROW_WORDS
    return buf.at[pl.ds(start if isinstance(row, int) else pl.multiple_of(start, ROW_WORDS), ROW_WORDS)]


def _dispatch_kernel(dest_ref, lo_ref, hi_ref, h_ref, xs_ref, rs_ref, zblk, sem, zsem):
    tm = DISP_TM
    base = pl.program_id(0) * tm

    def set_virtual(r, carry):
        rs_ref[r] = TOKENS * TOP_K + r
        return carry

    def set_dump(r, carry):
        rs_ref[r] = PAD_DUMP0 + (r & (PAD_DUMP_BLOCKS * EXPERT_BM - 1))
        return carry

    @pl.when(pl.program_id(0) == 0)
    def _():
        lax.fori_loop(0, EXPERT_BM, set_virtual, 0)

    def issue(g, carry):
        toks = [g * DISP_UNROLL + u for u in range(DISP_UNROLL)]
        dests = [[dest_ref[(base + t) * TOP_K + k] for k in range(TOP_K)] for t in toks]
        for t, row in zip(toks, dests):
            for k, d in enumerate(row):
                pltpu.make_async_copy(_vmem_row(h_ref, t), xs_ref.at[d], sem).start(priority=k % 2)
        for t, row in zip(toks, dests):
            for k, d in enumerate(row):
                rs_ref[d + EXPERT_BM] = k * TOKENS + base + t
        return carry

    lax.fori_loop(0, tm // DISP_UNROLL, issue, 0)

    @pl.when(pl.program_id(0) == pl.num_programs(0) - 1)
    def _():
        zblk[...] = jnp.zeros((EXPERT_BM * ROW_WORDS, LANES), _U32)

        def pad_copy(r):
            return pltpu.make_async_copy(_vmem_row(zblk, 0), xs_ref.at[r], zsem)

        def tail_copy(b):
            return pltpu.make_async_copy(zblk.reshape(EXPERT_BM, ROW_WORDS, LANES),
                                         xs_ref.at[pl.ds(pl.multiple_of(b * EXPERT_BM, EXPERT_BM), EXPERT_BM)], zsem)

        first_unused = hi_ref[N_EXPERTS - 1] // EXPERT_BM
        for e in range(N_EXPERTS):
            lax.fori_loop(lo_ref[e] + EXPERT_BM, hi_ref[e] + EXPERT_BM, set_dump, 0)
        lax.fori_loop((first_unused + 1) * EXPERT_BM, N_ROWS + EXPERT_BM, set_dump, 0)
        for e in range(N_EXPERTS):
            lax.fori_loop(lo_ref[e], hi_ref[e], lambda r, c: (pad_copy(r).start(), c)[1], 0)
        lax.fori_loop(first_unused, N_BLOCKS, lambda b, c: (tail_copy(b).start(), c)[1], 0)
        for e in range(N_EXPERTS):
            lax.fori_loop(lo_ref[e], hi_ref[e], lambda r, c: (pad_copy(r).wait(), c)[1], 0)
        lax.fori_loop(first_unused, N_BLOCKS, lambda b, c: (tail_copy(b).wait(), c)[1], 0)

    for _ in range(TOP_K):
        pltpu.make_async_copy(h_ref.reshape(tm, ROW_WORDS, LANES), xs_ref.at[pl.ds(0, tm)], sem).wait()


def _dispatch(dest, pad_lo, pad_hi, h_rows):
    tm = DISP_TM
    return pl.pallas_call(
        _dispatch_kernel,
        grid_spec=pltpu.PrefetchScalarGridSpec(
            num_scalar_prefetch=3,
            grid=(TOKENS // tm,),
            in_specs=[pl.BlockSpec((tm * ROW_WORDS, LANES), lambda i, *_: (i, 0))],
            out_specs=[pl.BlockSpec(memory_space=pl.ANY), pl.BlockSpec(memory_space=pltpu.SMEM)],
            scratch_shapes=[pltpu.VMEM((EXPERT_BM * ROW_WORDS, LANES), _U32),
                            pltpu.SemaphoreType.DMA, pltpu.SemaphoreType.DMA],
        ),
        out_shape=[jax.ShapeDtypeStruct((N_ROWS, ROW_WORDS, LANES), _U32),
                   jax.ShapeDtypeStruct((N_ROWS + EXPERT_BM,), jnp.int32)],
        compiler_params=pltpu.CompilerParams(
            dimension_semantics=("arbitrary",), vmem_limit_bytes=VMEM_LIMIT),
        name="dispatch",
    )(dest, pad_lo, pad_hi, h_rows)


def _expert_kernel(be_ref, nu_ref, nx_ref, rs_ref, xs_ref, w1_hbm, b1_ref, w2_hbm, b2_ref, ys_ref,
                   w1f, w2f, w1s, w2s, obuf, wsem, osem, isem, nsw, *, layer):
    bm = EXPERT_BM
    b = pl.program_id(0)
    e = be_ref[b]
    n_used = nu_ref[0]
    half = MXU_COLS // 2

    def fetch(expert, slot):
        return (pltpu.make_async_copy(w1_hbm.at[layer, expert], w1f.at[slot], wsem.at[0, slot]),
                pltpu.make_async_copy(w2_hbm.at[layer, expert], w2f.at[slot], wsem.at[1, slot]))

    def scatter_start(step, slot):
        for i in range(bm):
            pltpu.make_async_copy(_vmem_row(obuf.at[slot], i), ys_ref.at[rs_ref[step * bm + i]],
                                  osem.at[slot]).start(priority=i % 2)

    def block_view(slot):
        return obuf.at[slot].reshape(bm, ROW_WORDS, LANES)

    def scatter_wait(slot):
        pltpu.make_async_copy(block_view(slot), ys_ref.at[pl.ds(0, bm)], osem.at[slot]).wait()

    @pl.when(b == 0)
    def _():
        nsw[0] = 0
        obuf[RING - 1] = jnp.zeros((bm * ROW_WORDS, LANES), _U32)
        for blk in range(PAD_DUMP_BLOCKS):
            pltpu.make_async_copy(block_view(RING - 1), ys_ref.at[pl.ds(PAD_DUMP0 + blk * bm, bm)], isem).start()
        for blk in range(PAD_DUMP_BLOCKS):
            pltpu.make_async_copy(block_view(RING - 1), ys_ref.at[pl.ds(PAD_DUMP0 + blk * bm, bm)], isem).wait()
        for virtual in range(2, RING + 1):
            pltpu.make_async_copy(
                block_view(RING - 1), ys_ref.at[pl.ds(TOKENS * TOP_K + (virtual - 1) * bm, bm)],
                osem.at[RING - virtual]).start()
        for copy in fetch(e, 0):
            copy.start()

    @pl.when(b == n_used)
    def _():
        scatter_start(b, (b + RING - 1) % RING)
        for slot in range(RING):
            scatter_wait(slot)

    @pl.when(b < n_used)
    def _():
        out_slot = b % RING
        scatter_wait(out_slot)

        @pl.when(jnp.logical_or(b == 0, be_ref[jnp.maximum(b - 1, 0)] != e))
        def _():
            slot = nsw[0] % 2
            for copy in fetch(e, slot):
                copy.wait()
            src = lax.broadcasted_iota(jnp.int32, (MXU_COLS, MXU_COLS), 0)
            dst = lax.broadcasted_iota(jnp.int32, (MXU_COLS, MXU_COLS), 1)
            want = jnp.where(dst < half, 2 * dst, 2 * (dst - half) + 1)
            perm = (src == want).astype(_BF16)
            for blk in range(2 * D_EXPERT // MXU_COLS):
                cs = slice(blk * MXU_COLS, (blk + 1) * MXU_COLS)
                w1s[:, cs] = jnp.dot(w1f[slot, :, cs].astype(_BF16), perm,
                                     preferred_element_type=_F32).astype(_BF16)
            w2s[...] = w2f[slot].astype(_BF16)
            nsw[0] = nsw[0] + 1
            nxt = nx_ref[e]

            @pl.when(nxt < N_EXPERTS)
            def _():
                for copy in fetch(nxt, 1 - slot):
                    copy.start()

        scatter_start(b, (b + RING - 1) % RING)
        x = _load_packed(xs_ref, (), bm, _BF16)
        u = jnp.dot(x, w1s[...], preferred_element_type=_F32) + b1_ref[0, 0]
        acts = []
        for blk in range(2 * D_EXPERT // MXU_COLS):
            glu = jnp.minimum(u[:, blk * MXU_COLS:blk * MXU_COLS + half], SWIGLU_LIMIT)
            lin = jnp.clip(u[:, blk * MXU_COLS + half:(blk + 1) * MXU_COLS], -SWIGLU_LIMIT, SWIGLU_LIMIT)
            acts.append((glu * _sigmoid(SWIGLU_ALPHA * glu) * (lin + 1.0)).astype(_BF16))
        hidden = jnp.concatenate(acts, axis=1)
        y = jnp.dot(hidden, w2s[...], preferred_element_type=_F32) + b2_ref[0, 0]
        _store_packed(obuf, (out_slot,), y)


def _experts(layer, block_e, n_used, next_used, row_slot, xs, w1, b1_grouped, w2, b2):
    bm = EXPERT_BM
    d, f = D_MODEL, D_EXPERT
    blk = lambda b, be, nu, nx, rs: jnp.minimum(b, nu[0] - 1)
    return pl.pallas_call(
        functools.partial(_expert_kernel, layer=layer),
        grid_spec=pltpu.PrefetchScalarGridSpec(
            num_scalar_prefetch=4,
            grid=(N_BLOCKS,),
            in_specs=[
                pl.BlockSpec((bm * ROW_WORDS, LANES), lambda *a: (blk(*a), 0)),
                pl.BlockSpec(memory_space=pl.ANY),
                pl.BlockSpec((1, 1, 1, 2 * f), lambda *a: (layer, a[1][blk(*a)], 0, 0)),
                pl.BlockSpec(memory_space=pl.ANY),
                pl.BlockSpec((1, 1, 1, d), lambda *a: (layer, a[1][blk(*a)], 0, 0)),
            ],
            out_specs=pl.BlockSpec(memory_space=pl.ANY),
            scratch_shapes=[
                pltpu.VMEM((2, d, 2 * f), _F32), pltpu.VMEM((2, f, d), _F32),
                pltpu.VMEM((d, 2 * f), _BF16), pltpu.VMEM((f, d), _BF16),
                pltpu.VMEM((RING, bm * ROW_WORDS, LANES), _U32),
                pltpu.SemaphoreType.DMA((2, 2)), pltpu.SemaphoreType.DMA((RING,)), pltpu.SemaphoreType.DMA,
                pltpu.SMEM((1,), jnp.int32),
            ],
        ),
        out_shape=jax.ShapeDtypeStruct((OUT_SLOTS, ROW_WORDS, LANES), _U32),
        compiler_params=pltpu.CompilerParams(
            dimension_semantics=("arbitrary",), vmem_limit_bytes=VMEM_LIMIT),
        name="experts",
    )(block_e, n_used, next_used, row_slot, xs.reshape(N_ROWS * ROW_WORDS, LANES), w1, b1_grouped, w2, b2)


def _group_glu_columns(b1):
    half = MXU_COLS // 2
    j = jnp.arange(MXU_COLS)
    want = jnp.where(j < half, 2 * j, 2 * (j - half) + 1)
    cols = (jnp.arange(2 * D_EXPERT // MXU_COLS)[:, None] * MXU_COLS + want[None, :]).reshape(-1)
    return b1[..., cols]


def _combine_kernel(y0_ref, y1_ref, y2_ref, y3_ref, x_ref, gate_ref, mod_ref, fg_ref, o_ref, *, final):
    tm = COMB_TM
    gates = gate_ref[...]
    y = None
    for k, yk_ref in enumerate((y0_ref, y1_ref, y2_ref, y3_ref)):
        term = gates[:, k:k + 1] * _load_packed(yk_ref, (), tm, _F32)
        y = term if y is None else y + term
    out = x_ref[...] + mod_ref[0][5:6] * y
    if final:
        out = out * lax.rsqrt(jnp.mean(out * out, axis=-1, keepdims=True) + NORM_EPS) * fg_ref[...]
    o_ref[...] = out


def _combine(ys, x1, gates, mod, final_g, final):
    tm = COMB_TM
    d = D_MODEL
    tiles = TOKENS // tm
    tiles_per_batch = SEQ // tm
    ys2 = ys.reshape(OUT_SLOTS * ROW_WORDS, LANES)
    slot_spec = lambda k: pl.BlockSpec((tm * ROW_WORDS, LANES), lambda i: (k * tiles + i, 0))
    return pl.pallas_call(
        functools.partial(_combine_kernel, final=final),
        grid=(tiles,),
        in_specs=[
            slot_spec(0), slot_spec(1), slot_spec(2), slot_spec(3),
            pl.BlockSpec((tm, d), lambda i: (i, 0)),
            pl.BlockSpec((tm, TOP_K), lambda i: (i, 0)),
            pl.BlockSpec((1, N_MOD, d), lambda i: (i // tiles_per_batch, 0, 0)),
            pl.BlockSpec((1, d), lambda i: (0, 0)),
        ],
        out_specs=pl.BlockSpec((tm, d), lambda i: (i, 0)),
        out_shape=jax.ShapeDtypeStruct((TOKENS, d), _F32),
        compiler_params=pltpu.CompilerParams(
            dimension_semantics=("arbitrary",), vmem_limit_bytes=VMEM_LIMIT),
        name="combine_final" if final else "combine",
    )(ys2, ys2, ys2, ys2, x1, gates, mod, final_g.reshape(1, d))


def kernel(x, c, mod_w, mod_b, norm_g, conf_w1, conf_b1, conf_dw, conf_dw_b, conf_ln_g, conf_ln_b, conf_w2, conf_b2, sc_w_in, sc_conv, sc_w_out, router_w, router_b, exp_w1, exp_b1, exp_w2, exp_b2, final_g):
    mods = _modulation(c, mod_w, mod_b).reshape(DEPTH, BATCH, N_MOD, D_MODEL)
    b1_grouped = _group_glu_columns(exp_b1).reshape(DEPTH, N_EXPERTS, 1, 2 * D_EXPERT)
    b2_rows = exp_b2.reshape(DEPTH, N_EXPERTS, 1, D_MODEL)
    xcur = x
    for i in range(DEPTH):
        mod = mods[i]
        if i % 2 == 0:
            j = i // 2
            x1 = _conformer_mixer(xcur, mod, norm_g[i, 0], conf_w1[j], conf_b1[j], conf_dw[j], conf_dw_b[j],
                                  conf_ln_g[j], conf_ln_b[j], conf_w2[j], conf_b2[j])
        else:
            j = i // 2
            x1 = _shortconv_mixer(xcur, mod, norm_g[i, 0], sc_w_in[j], sc_conv[j], sc_w_out[j])
        x1 = x1.reshape(TOKENS, D_MODEL)
        h_rows, idx, rank, gates, counts = _route(x1, mod, norm_g[i, 1], router_w[i], router_b[i])
        dest, block_e, n_used, next_used, pad_lo, pad_hi = _routing_tables(idx, rank, counts)
        xs, row_slot = _dispatch(dest, pad_lo, pad_hi, h_rows)
        ys = _experts(i, block_e, n_used, next_used, row_slot, xs, exp_w1, b1_grouped, exp_w2, b2_rows)
        x2 = _combine(ys, x1, gates, mod, final_g, final=(i == DEPTH - 1))
        xcur = x2.reshape(BATCH, SEQ, D_MODEL)
    return xcur
```

```python
import functools

import jax
import jax.numpy as jnp
from jax import lax
from jax.experimental import pallas as pl
from jax.experimental.pallas import tpu as pltpu

D_MODEL = 1024
BATCH = 8
SEQ = 2048
DEPTH = 2
TOKENS = BATCH * SEQ
CONF_KERNEL = 31
SHORT_KERNEL = 3
N_EXPERTS = 32
TOP_K = 4
D_EXPERT = D_MODEL
SWIGLU_ALPHA = 1.702
SWIGLU_LIMIT = 7.0
NORM_EPS = 1e-5
N_MOD = 6

LANES = 128
SUBLANES = 8
MXU_COLS = 256
VMEM_LIMIT = 56 * 1024 * 1024

HALF_D = D_MODEL // 2
ROW_WORDS = HALF_D // LANES
MIX_TS = 256
CONF_HALO = 32
SHORT_HALO = 8
CONV_RC = 64
ROUTE_TM = 512
DISP_TM = 512
DISP_UNROLL = 4
INV_UNROLL = 16
COMB_TM = 256
EXPERT_BM = 256
N_ROWS = TOKENS * TOP_K + N_EXPERTS * EXPERT_BM
N_BLOCKS = N_ROWS // EXPERT_BM
RING = 3
PAD_DUMP0 = TOKENS * TOP_K + RING * EXPERT_BM
PAD_DUMP_BLOCKS = 4
OUT_SLOTS = PAD_DUMP0 + PAD_DUMP_BLOCKS * EXPERT_BM

_F32 = jnp.float32
_BF16 = jnp.bfloat16
_U32 = jnp.uint32
_HI_MASK = 0xFFFF0000


def _rms_mod(x, g, scale, shift):
    y = x * lax.rsqrt(jnp.mean(x * x, axis=-1, keepdims=True) + NORM_EPS)
    return (y * g) * (1.0 + scale) + shift


def _sigmoid(x):
    return 1.0 / (1.0 + jnp.exp(-x))


def _pack_rows(v):
    bits = lambda a: lax.bitcast_convert_type(a.astype(_BF16).astype(_F32), _U32)
    return (bits(v[:, HALF_D:]) & _U32(_HI_MASK)) | (bits(v[:, :HALF_D]) >> 16)


def _store_packed(ref, lead, v):
    rows = v.shape[0]
    words = _pack_rows(v)
    for j in range(ROW_WORDS):
        ref[lead + (pl.ds(j, rows, stride=ROW_WORDS), slice(None))] = words[:, j * LANES:(j + 1) * LANES]


def _load_packed(ref, lead, rows, dtype):
    lo, hi = [], []
    for j in range(ROW_WORDS):
        w = ref[lead + (pl.ds(j, rows, stride=ROW_WORDS), slice(None))]
        lo.append(lax.bitcast_convert_type(w << 16, _F32).astype(dtype))
        hi.append(lax.bitcast_convert_type(w & _U32(_HI_MASK), _F32).astype(dtype))
    return jnp.concatenate(lo + hi, axis=1)


def _mod_kernel(c_ref, w_ref, b_ref, o_ref):
    c = c_ref[...]
    c_act = c * _sigmoid(c)
    o_ref[0] = jnp.dot(c_act, w_ref[0], preferred_element_type=_F32,
                       precision=lax.Precision.HIGHEST) + b_ref[0]


def _modulation(c, mod_w, mod_b):
    tn = 1536
    n = N_MOD * D_MODEL
    return pl.pallas_call(
        _mod_kernel,
        grid=(DEPTH, n // tn),
        in_specs=[
            pl.BlockSpec((BATCH, D_MODEL), lambda i, j: (0, 0)),
            pl.BlockSpec((1, D_MODEL, tn), lambda i, j: (i, 0, j)),
            pl.BlockSpec((1, 1, tn), lambda i, j: (i, 0, j)),
        ],
        out_specs=pl.BlockSpec((1, BATCH, tn), lambda i, j: (i, 0, j)),
        out_shape=jax.ShapeDtypeStruct((DEPTH, BATCH, n), _F32),
        compiler_params=pltpu.CompilerParams(vmem_limit_bytes=VMEM_LIMIT),
        name="modulation",
    )(c, mod_w, mod_b.reshape(DEPTH, 1, n))


def _causal_taps(cbuf, shifted, w_ref, bias_row, out_ref, ntaps, halo, ts):
    off0 = halo - (ntaps - 1)
    for c in range(D_MODEL // LANES):
        cs = slice(c * LANES, (c + 1) * LANES)
        if shifted is not None:
            rows = cbuf.shape[0]
            whole = cbuf[:, cs]
            for r in range(1, SUBLANES):
                shifted[r - 1, :, cs] = pltpu.roll(whole, rows - r, axis=0)
        for rc in range(ts // CONV_RC):
            r0 = rc * CONV_RC
            acc = None
            for k in range(ntaps):
                s = off0 + k
                if shifted is None or s % SUBLANES == 0:
                    window = cbuf[r0 + s:r0 + s + CONV_RC, cs]
                else:
                    a = s - s % SUBLANES
                    window = shifted[s % SUBLANES - 1, r0 + a:r0 + a + CONV_RC, cs]
                term = w_ref[k:k + 1, cs] * window
                acc = term if acc is None else acc + term
            if bias_row is not None:
                acc = acc + bias_row[:, cs]
            out_ref[r0:r0 + CONV_RC, cs] = acc


def _conformer_kernel(x_ref, mod_ref, ng_ref, w1_ref, b1_ref, dw_ref, dwb_ref, lng_ref, lnb_ref,
                      w2_ref, b2_ref, o_ref, cbuf, shifted, vbuf):
    ts = MIX_TS

    @pl.when(pl.program_id(1) == 0)
    def _():
        cbuf[0:CONF_HALO, :] = jnp.zeros((CONF_HALO, D_MODEL), _F32)
        cbuf[CONF_HALO + ts:, :] = jnp.zeros((SUBLANES, D_MODEL), _F32)

    x = x_ref[0]
    m = mod_ref[0]
    h = _rms_mod(x, ng_ref[...], m[1:2], m[0:1])
    u = jnp.dot(h.astype(_BF16), w1_ref[...], preferred_element_type=_F32) + b1_ref[...]
    cbuf[CONF_HALO:CONF_HALO + ts, :] = u[:, :D_MODEL] * _sigmoid(u[:, D_MODEL:])
    _causal_taps(cbuf, shifted, dw_ref, dwb_ref[...], vbuf, CONF_KERNEL, CONF_HALO, ts)
    cbuf[0:CONF_HALO, :] = cbuf[ts:ts + CONF_HALO, :]
    v = vbuf[...]
    mu = jnp.mean(v, axis=-1, keepdims=True)
    vc = v - mu
    var = jnp.mean(vc * vc, axis=-1, keepdims=True)
    y = vc * lax.rsqrt(var + NORM_EPS) * lng_ref[...] + lnb_ref[...]
    y = y * _sigmoid(y)
    mix = jnp.dot(y.astype(_BF16), w2_ref[...], preferred_element_type=_F32) + b2_ref[...]
    o_ref[0] = x + m[2:3] * mix


def _conformer_mixer(x, mod, norm_g, w1, b1, dw, dw_b, ln_g, ln_b, w2, b2):
    d = D_MODEL
    row = lambda a: a.reshape(1, -1)
    const = lambda shape: pl.BlockSpec(shape, lambda b, s: (0,) * len(shape))
    return pl.pallas_call(
        _conformer_kernel,
        grid=(BATCH, SEQ // MIX_TS),
        in_specs=[
            pl.BlockSpec((1, MIX_TS, d), lambda b, s: (b, s, 0)),
            pl.BlockSpec((1, N_MOD, d), lambda b, s: (b, 0, 0)),
            const((1, d)), const((d, 2 * d)), const((1, 2 * d)), const((CONF_KERNEL, d)),
            const((1, d)), const((1, d)), const((1, d)), const((d, d)), const((1, d)),
        ],
        out_specs=pl.BlockSpec((1, MIX_TS, d), lambda b, s: (b, s, 0)),
        out_shape=jax.ShapeDtypeStruct((BATCH, SEQ, d), _F32),
        scratch_shapes=[pltpu.VMEM((CONF_HALO + MIX_TS + SUBLANES, d), _F32),
                        pltpu.VMEM((SUBLANES - 1, CONF_HALO + MIX_TS + SUBLANES, d), _F32),
                        pltpu.VMEM((MIX_TS, d), _F32)],
        compiler_params=pltpu.CompilerParams(
            dimension_semantics=("arbitrary", "arbitrary"), vmem_limit_bytes=VMEM_LIMIT),
        name="conformer_mixer",
    )(x, mod, row(norm_g), w1.astype(_BF16), row(b1), dw, row(dw_b), row(ln_g), row(ln_b),
      w2.astype(_BF16), row(b2))


def _shortconv_kernel(x_ref, mod_ref, ng_ref, win_ref, cw_ref, wout_ref, o_ref, cbuf, vbuf):
    ts = MIX_TS
    d = D_MODEL

    @pl.when(pl.program_id(1) == 0)
    def _():
        cbuf[0:SHORT_HALO, :] = jnp.zeros((SHORT_HALO, d), _F32)

    x = x_ref[0]
    m = mod_ref[0]
    h = _rms_mod(x, ng_ref[...], m[1:2], m[0:1])
    z = jnp.dot(h.astype(_BF16), win_ref[...], preferred_element_type=_F32)
    cbuf[SHORT_HALO:SHORT_HALO + ts, :] = z[:, d:2 * d] * z[:, 2 * d:]
    _causal_taps(cbuf, None, cw_ref, None, vbuf, SHORT_KERNEL, SHORT_HALO, ts)
    cbuf[0:SHORT_HALO, :] = cbuf[ts:ts + SHORT_HALO, :]
    y = z[:, :d] * vbuf[...]
    mix = jnp.dot(y.astype(_BF16), wout_ref[...], preferred_element_type=_F32)
    o_ref[0] = x + m[2:3] * mix


def _shortconv_mixer(x, mod, norm_g, w_in, conv_w, w_out):
    d = D_MODEL
    const = lambda shape: pl.BlockSpec(shape, lambda b, s: (0,) * len(shape))
    return pl.pallas_call(
        _shortconv_kernel,
        grid=(BATCH, SEQ // MIX_TS),
        in_specs=[
            pl.BlockSpec((1, MIX_TS, d), lambda b, s: (b, s, 0)),
            pl.BlockSpec((1, N_MOD, d), lambda b, s: (b, 0, 0)),
            const((1, d)), const((d, 3 * d)), const((SHORT_KERNEL, d)), const((d, d)),
        ],
        out_specs=pl.BlockSpec((1, MIX_TS, d), lambda b, s: (b, s, 0)),
        out_shape=jax.ShapeDtypeStruct((BATCH, SEQ, d), _F32),
        scratch_shapes=[pltpu.VMEM((SHORT_HALO + MIX_TS, d), _F32), pltpu.VMEM((MIX_TS, d), _F32)],
        compiler_params=pltpu.CompilerParams(
            dimension_semantics=("arbitrary", "arbitrary"), vmem_limit_bytes=VMEM_LIMIT),
        name="shortconv_mixer",
    )(x, mod, norm_g.reshape(1, d), w_in.astype(_BF16), conv_w, w_out.astype(_BF16))


def _route_kernel(x_ref, mod_ref, ng_ref, rwh_ref, rwl_ref, rb_ref, h_ref, idx_ref, rank_ref, gate_ref, cnt_ref):
    tm = ROUTE_TM

    @pl.when(pl.program_id(0) == 0)
    def _():
        cnt_ref[...] = jnp.zeros((1, N_EXPERTS), _F32)

    x = x_ref[...]
    m = mod_ref[0]
    h = _rms_mod(x, ng_ref[...], m[4:5], m[3:4])
    _store_packed(h_ref, (), h)

    h_hi = h.astype(_BF16)
    h_lo = (h - h_hi.astype(_F32)).astype(_BF16)
    logits = (jnp.dot(h_hi, rwh_ref[...], preferred_element_type=_F32)
              + jnp.dot(h_lo, rwh_ref[...], preferred_element_type=_F32)
              + jnp.dot(h_hi, rwl_ref[...], preferred_element_type=_F32)) + rb_ref[...]
    lane = lax.broadcasted_iota(jnp.int32, (tm, N_EXPERTS), 1).astype(_F32)
    work = logits
    vals, idxs, hots = [], [], []
    for _ in range(TOP_K):
        top = jnp.max(work, axis=1, keepdims=True)
        pick = jnp.min(jnp.where(work == top, lane, float(N_EXPERTS)), axis=1, keepdims=True)
        hot = lane == pick
        vals.append(top)
        idxs.append(pick)
        hots.append(hot)
        work = jnp.where(hot, -jnp.inf, work)
    exps = [jnp.exp(v - vals[0]) for v in vals]
    denom = exps[0] + exps[1] + exps[2] + exps[3]

    multi = (hots[0] | hots[1] | hots[2] | hots[3]).astype(_BF16)
    r_i = lax.broadcasted_iota(jnp.int32, (tm, tm), 0)
    c_i = lax.broadcasted_iota(jnp.int32, (tm, tm), 1)
    before = (c_i < r_i).astype(_BF16)
    pos = cnt_ref[...] + jnp.dot(before, multi, preferred_element_type=_F32)
    cnt_ref[...] = cnt_ref[...] + jnp.sum(multi.astype(_F32), axis=0, keepdims=True)

    col = lax.broadcasted_iota(jnp.int32, (tm, TOP_K), 1)
    idx_out = jnp.zeros((tm, TOP_K), _F32)
    rank_out = jnp.zeros((tm, TOP_K), _F32)
    gate_out = jnp.zeros((tm, TOP_K), _F32)
    for k in range(TOP_K):
        rank_k = jnp.sum(jnp.where(hots[k], pos, 0.0), axis=1, keepdims=True)
        idx_out = jnp.where(col == k, idxs[k], idx_out)
        rank_out = jnp.where(col == k, rank_k, rank_out)
        gate_out = jnp.where(col == k, exps[k] / denom, gate_out)
    idx_ref[...] = idx_out.astype(jnp.int32)
    rank_ref[...] = rank_out.astype(jnp.int32)
    gate_ref[...] = gate_out


def _route(x1, mod, norm_g, router_w, router_b):
    d = D_MODEL
    tm = ROUTE_TM
    tiles_per_batch = SEQ // tm
    rw_hi = router_w.astype(_BF16)
    rw_lo = (router_w - rw_hi.astype(_F32)).astype(_BF16)
    return pl.pallas_call(
        _route_kernel,
        grid=(TOKENS // tm,),
        in_specs=[
            pl.BlockSpec((tm, d), lambda i: (i, 0)),
            pl.BlockSpec((1, N_MOD, d), lambda i: (i // tiles_per_batch, 0, 0)),
            pl.BlockSpec((1, d), lambda i: (0, 0)),
            pl.BlockSpec((d, N_EXPERTS), lambda i: (0, 0)),
            pl.BlockSpec((d, N_EXPERTS), lambda i: (0, 0)),
            pl.BlockSpec((1, N_EXPERTS), lambda i: (0, 0)),
        ],
        out_specs=[
            pl.BlockSpec((tm * ROW_WORDS, LANES), lambda i: (i, 0)),
            pl.BlockSpec((tm, TOP_K), lambda i: (i, 0)),
            pl.BlockSpec((tm, TOP_K), lambda i: (i, 0)),
            pl.BlockSpec((tm, TOP_K), lambda i: (i, 0)),
            pl.BlockSpec((1, N_EXPERTS), lambda i: (0, 0)),
        ],
        out_shape=[
            jax.ShapeDtypeStruct((TOKENS * ROW_WORDS, LANES), _U32),
            jax.ShapeDtypeStruct((TOKENS, TOP_K), jnp.int32),
            jax.ShapeDtypeStruct((TOKENS, TOP_K), jnp.int32),
            jax.ShapeDtypeStruct((TOKENS, TOP_K), _F32),
            jax.ShapeDtypeStruct((1, N_EXPERTS), _F32),
        ],
        compiler_params=pltpu.CompilerParams(
            dimension_semantics=("arbitrary",), vmem_limit_bytes=VMEM_LIMIT),
        name="route",
    )(x1, mod, norm_g.reshape(1, d), rw_hi, rw_lo, router_b.reshape(1, N_EXPERTS))


def _routing_tables(idx, rank, counts):
    counts = counts.reshape(N_EXPERTS).astype(jnp.int32)
    padded = (counts + EXPERT_BM - 1) // EXPERT_BM * EXPERT_BM
    pad_ends = jnp.cumsum(padded)
    pad_starts = pad_ends - padded
    hot = idx[..., None] == jnp.arange(N_EXPERTS, dtype=jnp.int32)
    dest = jnp.sum(jnp.where(hot, pad_starts, 0), axis=-1) + rank
    block_start = jnp.arange(N_BLOCKS, dtype=jnp.int32) * EXPERT_BM
    block_e = jnp.sum((block_start[:, None] >= pad_ends[None, :]).astype(jnp.int32), axis=1)
    block_e = jnp.minimum(block_e, N_EXPERTS - 1).astype(jnp.int32)
    n_used = (pad_ends[-1:] // EXPERT_BM).astype(jnp.int32)
    ids = jnp.arange(N_EXPERTS, dtype=jnp.int32)
    later_used = (ids[None, :] > ids[:, None]) & (counts[None, :] > 0)
    next_used = jnp.min(jnp.where(later_used, ids[None, :], N_EXPERTS), axis=1).astype(jnp.int32)
    pad_lo = (pad_starts + counts).astype(jnp.int32)
    dest = dest.reshape(TOKENS * TOP_K).astype(jnp.int32)
    return dest, block_e, n_used, next_used, pad_lo, pad_ends.astype(jnp.int32)


def _vmem_row(buf, row):
    start = row * ROW_WORDS
    return buf.at[pl.ds(start if isinstance(row, int) else pl.multiple_of(start, ROW_WORDS), ROW_WORDS)]


def _dispatch_kernel(dest_ref, lo_ref, hi_ref, h_ref, xs_ref, rs_ref, zblk, sem, zsem):
    tm = DISP_TM
    base = pl.program_id(0) * tm

    def set_virtual(r, carry):
        rs_ref[r] = TOKENS * TOP_K + r
        return carry

    def set_dump(r, carry):
        rs_ref[r] = PAD_DUMP0 + (r & (PAD_DUMP_BLOCKS * EXPERT_BM - 1))
        return carry

    @pl.when(pl.program_id(0) == 0)
    def _():
        lax.fori_loop(0, EXPERT_BM, set_virtual, 0)

    def issue(g, carry):
        for u in range(DISP_UNROLL):
            t = g * DISP_UNROLL + u
            for k in range(TOP_K):
                pltpu.make_async_copy(_vmem_row(h_ref, t), xs_ref.at[dest_ref[(base + t) * TOP_K + k]],
                                      sem).start(priority=k % 2)
        return carry

    def record(g, carry):
        for u in range(INV_UNROLL):
            token = base + g * (INV_UNROLL // TOP_K) + u // TOP_K
            rs_ref[dest_ref[(base + g * (INV_UNROLL // TOP_K)) * TOP_K + u] + EXPERT_BM] = (u % TOP_K) * TOKENS + token
        return carry

    lax.fori_loop(0, tm // DISP_UNROLL, issue, 0)
    lax.fori_loop(0, tm * TOP_K // INV_UNROLL, record, 0)

    @pl.when(pl.program_id(0) == pl.num_programs(0) - 1)
    def _():
        zblk[...] = jnp.zeros((EXPERT_BM * ROW_WORDS, LANES), _U32)

        def pad_copy(r):
            return pltpu.make_async_copy(_vmem_row(zblk, 0), xs_ref.at[r], zsem)

        def tail_copy(b):
            return pltpu.make_async_copy(zblk.reshape(EXPERT_BM, ROW_WORDS, LANES),
                                         xs_ref.at[pl.ds(pl.multiple_of(b * EXPERT_BM, EXPERT_BM), EXPERT_BM)], zsem)

        first_unused = hi_ref[N_EXPERTS - 1] // EXPERT_BM
        for e in range(N_EXPERTS):
            lax.fori_loop(lo_ref[e] + EXPERT_BM, hi_ref[e] + EXPERT_BM, set_dump, 0)
        lax.fori_loop((first_unused + 1) * EXPERT_BM, N_ROWS + EXPERT_BM, set_dump, 0)
        for e in range(N_EXPERTS):
            lax.fori_loop(lo_ref[e], hi_ref[e], lambda r, c: (pad_copy(r).start(), c)[1], 0)
        lax.fori_loop(first_unused, N_BLOCKS, lambda b, c: (tail_copy(b).start(), c)[1], 0)
        for e in range(N_EXPERTS):
            lax.fori_loop(lo_ref[e], hi_ref[e], lambda r, c: (pad_copy(r).wait(), c)[1], 0)
        lax.fori_loop(first_unused, N_BLOCKS, lambda b, c: (tail_copy(b).wait(), c)[1], 0)

    for _ in range(TOP_K):
        pltpu.make_async_copy(h_ref.reshape(tm, ROW_WORDS, LANES), xs_ref.at[pl.ds(0, tm)], sem).wait()


def _dispatch(dest, pad_lo, pad_hi, h_rows):
    tm = DISP_TM
    return pl.pallas_call(
        _dispatch_kernel,
        grid_spec=pltpu.PrefetchScalarGridSpec(
            num_scalar_prefetch=3,
            grid=(TOKENS // tm,),
            in_specs=[pl.BlockSpec((tm * ROW_WORDS, LANES), lambda i, *_: (i, 0))],
            out_specs=[pl.BlockSpec(memory_space=pl.ANY), pl.BlockSpec(memory_space=pltpu.SMEM)],
            scratch_shapes=[pltpu.VMEM((EXPERT_BM * ROW_WORDS, LANES), _U32),
                            pltpu.SemaphoreType.DMA, pltpu.SemaphoreType.DMA],
        ),
        out_shape=[jax.ShapeDtypeStruct((N_ROWS, ROW_WORDS, LANES), _U32),
                   jax.ShapeDtypeStruct((N_ROWS + EXPERT_BM,), jnp.int32)],
        compiler_params=pltpu.CompilerParams(
            dimension_semantics=("arbitrary",), vmem_limit_bytes=VMEM_LIMIT),
        name="dispatch",
    )(dest, pad_lo, pad_hi, h_rows)


def _expert_kernel(be_ref, nu_ref, nx_ref, rs_ref, xs_ref, w1_hbm, b1_ref, w2_hbm, b2_ref, ys_ref,
                   w1f, w2f, w1s, w2s, obuf, wsem, osem, isem, nsw, *, layer):
    bm = EXPERT_BM
    b = pl.program_id(0)
    e = be_ref[b]
    n_used = nu_ref[0]
    half = MXU_COLS // 2

    def fetch(expert, slot):
        return (pltpu.make_async_copy(w1_hbm.at[layer, expert], w1f.at[slot], wsem.at[0, slot]),
                pltpu.make_async_copy(w2_hbm.at[layer, expert], w2f.at[slot], wsem.at[1, slot]))

    def scatter_start(step, slot):
        for i in range(bm):
            pltpu.make_async_copy(_vmem_row(obuf.at[slot], i), ys_ref.at[rs_ref[step * bm + i]],
                                  osem.at[slot]).start(priority=i % 2)

    def block_view(slot):
        return obuf.at[slot].reshape(bm, ROW_WORDS, LANES)

    def scatter_wait(slot):
        pltpu.make_async_copy(block_view(slot), ys_ref.at[pl.ds(0, bm)], osem.at[slot]).wait()

    @pl.when(b == 0)
    def _():
        nsw[0] = 0
        obuf[RING - 1] = jnp.zeros((bm * ROW_WORDS, LANES), _U32)
        for blk in range(PAD_DUMP_BLOCKS):
            pltpu.make_async_copy(block_view(RING - 1), ys_ref.at[pl.ds(PAD_DUMP0 + blk * bm, bm)], isem).start()
        for blk in range(PAD_DUMP_BLOCKS):
            pltpu.make_async_copy(block_view(RING - 1), ys_ref.at[pl.ds(PAD_DUMP0 + blk * bm, bm)], isem).wait()
        for virtual in range(2, RING + 1):
            pltpu.make_async_copy(
                block_view(RING - 1), ys_ref.at[pl.ds(TOKENS * TOP_K + (virtual - 1) * bm, bm)],
                osem.at[RING - virtual]).start()
        for copy in fetch(e, 0):
            copy.start()

    @pl.when(b == n_used)
    def _():
        scatter_start(b, (b + RING - 1) % RING)
        for slot in range(RING):
            scatter_wait(slot)

    @pl.when(b < n_used)
    def _():
        out_slot = b % RING
        scatter_wait(out_slot)

        @pl.when(jnp.logical_or(b == 0, be_ref[jnp.maximum(b - 1, 0)] != e))
        def _():
            slot = nsw[0] % 2
            for copy in fetch(e, slot):
                copy.wait()
            src = lax.broadcasted_iota(jnp.int32, (MXU_COLS, MXU_COLS), 0)
            dst = lax.broadcasted_iota(jnp.int32, (MXU_COLS, MXU_COLS), 1)
            want = jnp.where(dst < half, 2 * dst, 2 * (dst - half) + 1)
            perm = (src == want).astype(_BF16)
            for blk in range(2 * D_EXPERT // MXU_COLS):
                cs = slice(blk * MXU_COLS, (blk + 1) * MXU_COLS)
                w1s[:, cs] = jnp.dot(w1f[slot, :, cs].astype(_BF16), perm,
                                     preferred_element_type=_F32).astype(_BF16)
            w2s[...] = w2f[slot].astype(_BF16)
            nsw[0] = nsw[0] + 1
            nxt = nx_ref[e]

            @pl.when(nxt < N_EXPERTS)
            def _():
                for copy in fetch(nxt, 1 - slot):
                    copy.start()

        scatter_start(b, (b + RING - 1) % RING)
        x = _load_packed(xs_ref, (), bm, _BF16)
        u = jnp.dot(x, w1s[...], preferred_element_type=_F32) + b1_ref[0, 0]
        acts = []
        for blk in range(2 * D_EXPERT // MXU_COLS):
            glu = jnp.minimum(u[:, blk * MXU_COLS:blk * MXU_COLS + half], SWIGLU_LIMIT)
            lin = jnp.clip(u[:, blk * MXU_COLS + half:(blk + 1) * MXU_COLS], -SWIGLU_LIMIT, SWIGLU_LIMIT)
            acts.append((glu * _sigmoid(SWIGLU_ALPHA * glu) * (lin + 1.0)).astype(_BF16))
        hidden = jnp.concatenate(acts, axis=1)
        y = jnp.dot(hidden, w2s[...], preferred_element_type=_F32) + b2_ref[0, 0]
        _store_packed(obuf, (out_slot,), y)


def _experts(layer, block_e, n_used, next_used, row_slot, xs, w1, b1_grouped, w2, b2):
    bm = EXPERT_BM
    d, f = D_MODEL, D_EXPERT
    blk = lambda b, be, nu, nx, rs: jnp.minimum(b, nu[0] - 1)
    return pl.pallas_call(
        functools.partial(_expert_kernel, layer=layer),
        grid_spec=pltpu.PrefetchScalarGridSpec(
            num_scalar_prefetch=4,
            grid=(N_BLOCKS,),
            in_specs=[
                pl.BlockSpec((bm * ROW_WORDS, LANES), lambda *a: (blk(*a), 0)),
                pl.BlockSpec(memory_space=pl.ANY),
                pl.BlockSpec((1, 1, 1, 2 * f), lambda *a: (layer, a[1][blk(*a)], 0, 0)),
                pl.BlockSpec(memory_space=pl.ANY),
                pl.BlockSpec((1, 1, 1, d), lambda *a: (layer, a[1][blk(*a)], 0, 0)),
            ],
            out_specs=pl.BlockSpec(memory_space=pl.ANY),
            scratch_shapes=[
                pltpu.VMEM((2, d, 2 * f), _F32), pltpu.VMEM((2, f, d), _F32),
                pltpu.VMEM((d, 2 * f), _BF16), pltpu.VMEM((f, d), _BF16),
                pltpu.VMEM((RING, bm * ROW_WORDS, LANES), _U32),
                pltpu.SemaphoreType.DMA((2, 2)), pltpu.SemaphoreType.DMA((RING,)), pltpu.SemaphoreType.DMA,
                pltpu.SMEM((1,), jnp.int32),
            ],
        ),
        out_shape=jax.ShapeDtypeStruct((OUT_SLOTS, ROW_WORDS, LANES), _U32),
        compiler_params=pltpu.CompilerParams(
            dimension_semantics=("arbitrary",), vmem_limit_bytes=VMEM_LIMIT),
        name="experts",
    )(block_e, n_used, next_used, row_slot, xs.reshape(N_ROWS * ROW_WORDS, LANES), w1, b1_grouped, w2, b2)


def _group_glu_columns(b1):
    half = MXU_COLS // 2
    j = jnp.arange(MXU_COLS)
    want = jnp.where(j < half, 2 * j, 2 * (j - half) + 1)
    cols = (jnp.arange(2 * D_EXPERT // MXU_COLS)[:, None] * MXU_COLS + want[None, :]).reshape(-1)
    return b1[..., cols]


def _combine_kernel(y0_ref, y1_ref, y2_ref, y3_ref, x_ref, gate_ref, mod_ref, fg_ref, o_ref, *, final):
    tm = COMB_TM
    gates = gate_ref[...]
    y = None
    for k, yk_ref in enumerate((y0_ref, y1_ref, y2_ref, y3_ref)):
        term = gates[:, k:k + 1] * _load_packed(yk_ref, (), tm, _F32)
        y = term if y is None else y + term
    out = x_ref[...] + mod_ref[0][5:6] * y
    if final:
        out = out * lax.rsqrt(jnp.mean(out * out, axis=-1, keepdims=True) + NORM_EPS) * fg_ref[...]
    o_ref[...] = out


def _combine(ys, x1, gates, mod, final_g, final):
    tm = COMB_TM
    d = D_MODEL
    tiles = TOKENS // tm
    tiles_per_batch = SEQ // tm
    ys2 = ys.reshape(OUT_SLOTS * ROW_WORDS, LANES)
    slot_spec = lambda k: pl.BlockSpec((tm * ROW_WORDS, LANES), lambda i: (k * tiles + i, 0))
    return pl.pallas_call(
        functools.partial(_combine_kernel, final=final),
        grid=(tiles,),
        in_specs=[
            slot_spec(0), slot_spec(1), slot_spec(2), slot_spec(3),
            pl.BlockSpec((tm, d), lambda i: (i, 0)),
            pl.BlockSpec((tm, TOP_K), lambda i: (i, 0)),
            pl.BlockSpec((1, N_MOD, d), lambda i: (i // tiles_per_batch, 0, 0)),
            pl.BlockSpec((1, d), lambda i: (0, 0)),
        ],
        out_specs=pl.BlockSpec((tm, d), lambda i: (i, 0)),
        out_shape=jax.ShapeDtypeStruct((TOKENS, d), _F32),
        compiler_params=pltpu.CompilerParams(
            dimension_semantics=("arbitrary",), vmem_limit_bytes=VMEM_LIMIT),
        name="combine_final" if final else "combine",
    )(ys2, ys2, ys2, ys2, x1, gates, mod, final_g.reshape(1, d))


def kernel(x, c, mod_w, mod_b, norm_g, conf_w1, conf_b1, conf_dw, conf_dw_b, conf_ln_g, conf_ln_b, conf_w2, conf_b2, sc_w_in, sc_conv, sc_w_out, router_w, router_b, exp_w1, exp_b1, exp_w2, exp_b2, final_g):
    mods = _modulation(c, mod_w, mod_b).reshape(DEPTH, BATCH, N_MOD, D_MODEL)
    b1_grouped = _group_glu_columns(exp_b1).reshape(DEPTH, N_EXPERTS, 1, 2 * D_EXPERT)
    b2_rows = exp_b2.reshape(DEPTH, N_EXPERTS, 1, D_MODEL)
    xcur = x
    for i in range(DEPTH):
        mod = mods[i]
        if i % 2 == 0:
            j = i // 2
            x1 = _conformer_mixer(xcur, mod, norm_g[i, 0], conf_w1[j], conf_b1[j], conf_dw[j], conf_dw_b[j],
                                  conf_ln_g[j], conf_ln_b[j], conf_w2[j], conf_b2[j])
        else:
            j = i // 2
            x1 = _shortconv_mixer(xcur, mod, norm_g[i, 0], sc_w_in[j], sc_conv[j], sc_w_out[j])
        x1 = x1.reshape(TOKENS, D_MODEL)
        h_rows, idx, rank, gates, counts = _route(x1, mod, norm_g[i, 1], router_w[i], router_b[i])
        dest, block_e, n_used, next_used, pad_lo, pad_hi = _routing_tables(idx, rank, counts)
        xs, row_slot = _dispatch(dest, pad_lo, pad_hi, h_rows)
        ys = _experts(i, block_e, n_used, next_used, row_slot, xs, exp_w1, b1_grouped, exp_w2, b2_rows)
        x2 = _combine(ys, x1, gates, mod, final_g, final=(i == DEPTH - 1))
        xcur = x2.reshape(BATCH, SEQ, D_MODEL)
    return xcur
```

```python
import functools

import jax
import jax.numpy as jnp
from jax import lax
from jax.experimental import pallas as pl
from jax.experimental.pallas import tpu as pltpu

D_MODEL = 1024
BATCH = 8
SEQ = 2048
DEPTH = 2
TOKENS = BATCH * SEQ
CONF_KERNEL = 31
SHORT_KERNEL = 3
N_EXPERTS = 32
TOP_K = 4
D_EXPERT = D_MODEL
SWIGLU_ALPHA = 1.702
SWIGLU_LIMIT = 7.0
NORM_EPS = 1e-5
N_MOD = 6

LANES = 128
SUBLANES = 8
MXU_COLS = 256
VMEM_LIMIT = 56 * 1024 * 1024

HALF_D = D_MODEL // 2
ROW_WORDS = HALF_D // LANES
MIX_TS = 256
CONF_HALO = 32
SHORT_HALO = 8
CONV_RC = 64
ROUTE_TM = 512
DISP_TM = 512
DISP_UNROLL = 4
INV_UNROLL = 16
COMB_TM = 256
EXPERT_BM = 256
N_ROWS = TOKENS * TOP_K + N_EXPERTS * EXPERT_BM
N_BLOCKS = N_ROWS // EXPERT_BM
RING = 3
PAD_DUMP0 = TOKENS * TOP_K + RING * EXPERT_BM
PAD_DUMP_BLOCKS = 4
OUT_SLOTS = PAD_DUMP0 + PAD_DUMP_BLOCKS * EXPERT_BM

_F32 = jnp.float32
_BF16 = jnp.bfloat16
_U32 = jnp.uint32
_HI_MASK = 0xFFFF0000


def _rms_mod(x, g, scale, shift):
    y = x * lax.rsqrt(jnp.mean(x * x, axis=-1, keepdims=True) + NORM_EPS)
    return (y * g) * (1.0 + scale) + shift


def _sigmoid(x):
    return 1.0 / (1.0 + jnp.exp(-x))


def _pack_rows(v):
    bits = lambda a: lax.bitcast_convert_type(a.astype(_BF16).astype(_F32), _U32)
    return (bits(v[:, HALF_D:]) & _U32(_HI_MASK)) | (bits(v[:, :HALF_D]) >> 16)


def _store_packed(ref, lead, v):
    rows = v.shape[0]
    words = _pack_rows(v)
    for j in range(ROW_WORDS):
        ref[lead + (pl.ds(j, rows, stride=ROW_WORDS), slice(None))] = words[:, j * LANES:(j + 1) * LANES]


def _load_packed(ref, lead, rows, dtype):
    lo, hi = [], []
    for j in range(ROW_WORDS):
        w = ref[lead + (pl.ds(j, rows, stride=ROW_WORDS), slice(None))]
        lo.append(lax.bitcast_convert_type(w << 16, _F32).astype(dtype))
        hi.append(lax.bitcast_convert_type(w & _U32(_HI_MASK), _F32).astype(dtype))
    return jnp.concatenate(lo + hi, axis=1)


def _mod_kernel(c_ref, w_ref, b_ref, o_ref):
    c = c_ref[...]
    c_act = c * _sigmoid(c)
    o_ref[0] = jnp.dot(c_act, w_ref[0], preferred_element_type=_F32,
                       precision=lax.Precision.HIGHEST) + b_ref[0]


def _modulation(c, mod_w, mod_b):
    tn = 1536
    n = N_MOD * D_MODEL
    return pl.pallas_call(
        _mod_kernel,
        grid=(DEPTH, n // tn),
        in_specs=[
            pl.BlockSpec((BATCH, D_MODEL), lambda i, j: (0, 0)),
            pl.BlockSpec((1, D_MODEL, tn), lambda i, j: (i, 0, j)),
            pl.BlockSpec((1, 1, tn), lambda i, j: (i, 0, j)),
        ],
        out_specs=pl.BlockSpec((1, BATCH, tn), lambda i, j: (i, 0, j)),
        out_shape=jax.ShapeDtypeStruct((DEPTH, BATCH, n), _F32),
        compiler_params=pltpu.CompilerParams(vmem_limit_bytes=VMEM_LIMIT),
        name="modulation",
    )(c, mod_w, mod_b.reshape(DEPTH, 1, n))


def _causal_taps(cbuf, shifted, w_ref, bias_row, out_ref, ntaps, halo, ts):
    off0 = halo - (ntaps - 1)
    for c in range(D_MODEL // LANES):
        cs = slice(c * LANES, (c + 1) * LANES)
        if shifted is not None:
            rows = cbuf.shape[0]
            whole = cbuf[:, cs]
            for r in range(1, SUBLANES):
                shifted[r - 1, :, cs] = pltpu.roll(whole, rows - r, axis=0)
        for rc in range(ts // CONV_RC):
            r0 = rc * CONV_RC
            acc = None
            for k in range(ntaps):
                s = off0 + k
                if shifted is None or s % SUBLANES == 0:
                    window = cbuf[r0 + s:r0 + s + CONV_RC, cs]
                else:
                    a = s - s % SUBLANES
                    window = shifted[s % SUBLANES - 1, r0 + a:r0 + a + CONV_RC, cs]
                term = w_ref[k:k + 1, cs] * window
                acc = term if acc is None else acc + term
            if bias_row is not None:
                acc = acc + bias_row[:, cs]
            out_ref[r0:r0 + CONV_RC, cs] = acc


def _conformer_kernel(x_ref, mod_ref, ng_ref, w1_ref, b1_ref, dw_ref, dwb_ref, lng_ref, lnb_ref,
                      w2_ref, b2_ref, o_ref, cbuf, shifted, vbuf):
    ts = MIX_TS

    @pl.when(pl.program_id(1) == 0)
    def _():
        cbuf[0:CONF_HALO, :] = jnp.zeros((CONF_HALO, D_MODEL), _F32)
        cbuf[CONF_HALO + ts:, :] = jnp.zeros((SUBLANES, D_MODEL), _F32)

    x = x_ref[0]
    m = mod_ref[0]
    h = _rms_mod(x, ng_ref[...], m[1:2], m[0:1])
    u = jnp.dot(h.astype(_BF16), w1_ref[...], preferred_element_type=_F32) + b1_ref[...]
    cbuf[CONF_HALO:CONF_HALO + ts, :] = u[:, :D_MODEL] * _sigmoid(u[:, D_MODEL:])
    _causal_taps(cbuf, shifted, dw_ref, dwb_ref[...], vbuf, CONF_KERNEL, CONF_HALO, ts)
    cbuf[0:CONF_HALO, :] = cbuf[ts:ts + CONF_HALO, :]
    v = vbuf[...]
    mu = jnp.mean(v, axis=-1, keepdims=True)
    vc = v - mu
    var = jnp.mean(vc * vc, axis=-1, keepdims=True)
    y = vc * lax.rsqrt(var + NORM_EPS) * lng_ref[...] + lnb_ref[...]
    y = y * _sigmoid(y)
    mix = jnp.dot(y.astype(_BF16), w2_ref[...], preferred_element_type=_F32) + b2_ref[...]
    o_ref[0] = x + m[2:3] * mix


def _conformer_mixer(x, mod, norm_g, w1, b1, dw, dw_b, ln_g, ln_b, w2, b2):
    d = D_MODEL
    row = lambda a: a.reshape(1, -1)
    const = lambda shape: pl.BlockSpec(shape, lambda b, s: (0,) * len(shape))
    return pl.pallas_call(
        _conformer_kernel,
        grid=(BATCH, SEQ // MIX_TS),
        in_specs=[
            pl.BlockSpec((1, MIX_TS, d), lambda b, s: (b, s, 0)),
            pl.BlockSpec((1, N_MOD, d), lambda b, s: (b, 0, 0)),
            const((1, d)), const((d, 2 * d)), const((1, 2 * d)), const((CONF_KERNEL, d)),
            const((1, d)), const((1, d)), const((1, d)), const((d, d)), const((1, d)),
        ],
        out_specs=pl.BlockSpec((1, MIX_TS, d), lambda b, s: (b, s, 0)),
        out_shape=jax.ShapeDtypeStruct((BATCH, SEQ, d), _F32),
        scratch_shapes=[pltpu.VMEM((CONF_HALO + MIX_TS + SUBLANES, d), _F32),
                        pltpu.VMEM((SUBLANES - 1, CONF_HALO + MIX_TS + SUBLANES, d), _F32),
                        pltpu.VMEM((MIX_TS, d), _F32)],
        compiler_params=pltpu.CompilerParams(
            dimension_semantics=("arbitrary", "arbitrary"), vmem_limit_bytes=VMEM_LIMIT),
        name="conformer_mixer",
    )(x, mod, row(norm_g), w1.astype(_BF16), row(b1), dw, row(dw_b), row(ln_g), row(ln_b),
      w2.astype(_BF16), row(b2))


def _moe_residual(y_refs, gate_ref, x1, gate2, rows):
    gates = gate_ref[...]
    y = None
    for k, yk_ref in enumerate(y_refs):
        term = gates[:, k:k + 1] * _load_packed(yk_ref, (), rows, _F32)
        y = term if y is None else y + term
    return x1 + gate2 * y


def _shortconv_kernel(y0_ref, y1_ref, y2_ref, y3_ref, gate_ref, pmod_ref, x_ref, mod_ref, ng_ref, win_ref, cw_ref,
                      wout_ref, o_ref, cbuf, vbuf):
    ts = MIX_TS
    d = D_MODEL

    @pl.when(pl.program_id(1) == 0)
    def _():
        cbuf[0:SHORT_HALO, :] = jnp.zeros((SHORT_HALO, d), _F32)

    x = _moe_residual((y0_ref, y1_ref, y2_ref, y3_ref), gate_ref, x_ref[0], pmod_ref[0][5:6], ts)
    m = mod_ref[0]
    h = _rms_mod(x, ng_ref[...], m[1:2], m[0:1])
    z = jnp.dot(h.astype(_BF16), win_ref[...], preferred_element_type=_F32)
    cbuf[SHORT_HALO:SHORT_HALO + ts, :] = z[:, d:2 * d] * z[:, 2 * d:]
    _causal_taps(cbuf, None, cw_ref, None, vbuf, SHORT_KERNEL, SHORT_HALO, ts)
    cbuf[0:SHORT_HALO, :] = cbuf[ts:ts + SHORT_HALO, :]
    y = z[:, :d] * vbuf[...]
    mix = jnp.dot(y.astype(_BF16), wout_ref[...], preferred_element_type=_F32)
    o_ref[0] = x + m[2:3] * mix


def _shortconv_mixer(pending, mod, norm_g, w_in, conv_w, w_out):
    ys, x_prev, gates, prev_mod = pending
    d = D_MODEL
    tiles_per_batch = SEQ // MIX_TS
    tiles = TOKENS // MIX_TS
    ys2 = ys.reshape(OUT_SLOTS * ROW_WORDS, LANES)
    const = lambda shape: pl.BlockSpec(shape, lambda b, s: (0,) * len(shape))
    slot_spec = lambda k: pl.BlockSpec((MIX_TS * ROW_WORDS, LANES),
                                       lambda b, s: (k * tiles + b * tiles_per_batch + s, 0))
    return pl.pallas_call(
        _shortconv_kernel,
        grid=(BATCH, SEQ // MIX_TS),
        in_specs=[
            slot_spec(0), slot_spec(1), slot_spec(2), slot_spec(3),
            pl.BlockSpec((MIX_TS, TOP_K), lambda b, s: (b * tiles_per_batch + s, 0)),
            pl.BlockSpec((1, N_MOD, d), lambda b, s: (b, 0, 0)),
            pl.BlockSpec((1, MIX_TS, d), lambda b, s: (b, s, 0)),
            pl.BlockSpec((1, N_MOD, d), lambda b, s: (b, 0, 0)),
            const((1, d)), const((d, 3 * d)), const((SHORT_KERNEL, d)), const((d, d)),
        ],
        out_specs=pl.BlockSpec((1, MIX_TS, d), lambda b, s: (b, s, 0)),
        out_shape=jax.ShapeDtypeStruct((BATCH, SEQ, d), _F32),
        scratch_shapes=[pltpu.VMEM((SHORT_HALO + MIX_TS, d), _F32), pltpu.VMEM((MIX_TS, d), _F32)],
        compiler_params=pltpu.CompilerParams(
            dimension_semantics=("arbitrary", "arbitrary"), vmem_limit_bytes=VMEM_LIMIT),
        name="shortconv_mixer",
    )(ys2, ys2, ys2, ys2, gates, prev_mod, x_prev.reshape(BATCH, SEQ, d), mod, norm_g.reshape(1, d),
      w_in.astype(_BF16), conv_w, w_out.astype(_BF16))


def _route_kernel(x_ref, mod_ref, ng_ref, rwh_ref, rwl_ref, rb_ref, h_ref, idx_ref, rank_ref, gate_ref, cnt_ref):
    tm = ROUTE_TM

    @pl.when(pl.program_id(0) == 0)
    def _():
        cnt_ref[...] = jnp.zeros((1, N_EXPERTS), _F32)

    x = x_ref[...]
    m = mod_ref[0]
    h = _rms_mod(x, ng_ref[...], m[4:5], m[3:4])
    _store_packed(h_ref, (), h)

    h_hi = h.astype(_BF16)
    h_lo = (h - h_hi.astype(_F32)).astype(_BF16)
    logits = (jnp.dot(h_hi, rwh_ref[...], preferred_element_type=_F32)
              + jnp.dot(h_lo, rwh_ref[...], preferred_element_type=_F32)
              + jnp.dot(h_hi, rwl_ref[...], preferred_element_type=_F32)) + rb_ref[...]
    lane = lax.broadcasted_iota(jnp.int32, (tm, N_EXPERTS), 1).astype(_F32)
    work = logits
    vals, idxs, hots = [], [], []
    for _ in range(TOP_K):
        top = jnp.max(work, axis=1, keepdims=True)
        pick = jnp.min(jnp.where(work == top, lane, float(N_EXPERTS)), axis=1, keepdims=True)
        hot = lane == pick
        vals.append(top)
        idxs.append(pick)
        hots.append(hot)
        work = jnp.where(hot, -jnp.inf, work)
    exps = [jnp.exp(v - vals[0]) for v in vals]
    denom = exps[0] + exps[1] + exps[2] + exps[3]

    multi = (hots[0] | hots[1] | hots[2] | hots[3]).astype(_BF16)
    r_i = lax.broadcasted_iota(jnp.int32, (tm, tm), 0)
    c_i = lax.broadcasted_iota(jnp.int32, (tm, tm), 1)
    before = (c_i < r_i).astype(_BF16)
    pos = cnt_ref[...] + jnp.dot(before, multi, preferred_element_type=_F32)
    cnt_ref[...] = cnt_ref[...] + jnp.sum(multi.astype(_F32), axis=0, keepdims=True)

    col = lax.broadcasted_iota(jnp.int32, (tm, TOP_K), 1)
    idx_out = jnp.zeros((tm, TOP_K), _F32)
    rank_out = jnp.zeros((tm, TOP_K), _F32)
    gate_out = jnp.zeros((tm, TOP_K), _F32)
    for k in range(TOP_K):
        rank_k = jnp.sum(jnp.where(hots[k], pos, 0.0), axis=1, keepdims=True)
        idx_out = jnp.where(col == k, idxs[k], idx_out)
        rank_out = jnp.where(col == k, rank_k, rank_out)
        gate_out = jnp.where(col == k, exps[k] / denom, gate_out)
    idx_ref[...] = idx_out.astype(jnp.int32)
    rank_ref[...] = rank_out.astype(jnp.int32)
    gate_ref[...] = gate_out


def _route(x1, mod, norm_g, router_w, router_b):
    d = D_MODEL
    tm = ROUTE_TM
    tiles_per_batch = SEQ // tm
    rw_hi = router_w.astype(_BF16)
    rw_lo = (router_w - rw_hi.astype(_F32)).astype(_BF16)
    return pl.pallas_call(
        _route_kernel,
        grid=(TOKENS // tm,),
        in_specs=[
            pl.BlockSpec((tm, d), lambda i: (i, 0)),
            pl.BlockSpec((1, N_MOD, d), lambda i: (i // tiles_per_batch, 0, 0)),
            pl.BlockSpec((1, d), lambda i: (0, 0)),
            pl.BlockSpec((d, N_EXPERTS), lambda i: (0, 0)),
            pl.BlockSpec((d, N_EXPERTS), lambda i: (0, 0)),
            pl.BlockSpec((1, N_EXPERTS), lambda i: (0, 0)),
        ],
        out_specs=[
            pl.BlockSpec((tm * ROW_WORDS, LANES), lambda i: (i, 0)),
            pl.BlockSpec((tm, TOP_K), lambda i: (i, 0)),
            pl.BlockSpec((tm, TOP_K), lambda i: (i, 0)),
            pl.BlockSpec((tm, TOP_K), lambda i: (i, 0)),
            pl.BlockSpec((1, N_EXPERTS), lambda i: (0, 0)),
        ],
        out_shape=[
            jax.ShapeDtypeStruct((TOKENS * ROW_WORDS, LANES), _U32),
            jax.ShapeDtypeStruct((TOKENS, TOP_K), jnp.int32),
            jax.ShapeDtypeStruct((TOKENS, TOP_K), jnp.int32),
            jax.ShapeDtypeStruct((TOKENS, TOP_K), _F32),
            jax.ShapeDtypeStruct((1, N_EXPERTS), _F32),
        ],
        compiler_params=pltpu.CompilerParams(
            dimension_semantics=("arbitrary",), vmem_limit_bytes=VMEM_LIMIT),
        name="route",
    )(x1, mod, norm_g.reshape(1, d), rw_hi, rw_lo, router_b.reshape(1, N_EXPERTS))


def _routing_tables(idx, rank, counts):
    counts = counts.reshape(N_EXPERTS).astype(jnp.int32)
    padded = (counts + EXPERT_BM - 1) // EXPERT_BM * EXPERT_BM
    pad_ends = jnp.cumsum(padded)
    pad_starts = pad_ends - padded
    hot = idx[..., None] == jnp.arange(N_EXPERTS, dtype=jnp.int32)
    dest = jnp.sum(jnp.where(hot, pad_starts, 0), axis=-1) + rank
    block_start = jnp.arange(N_BLOCKS, dtype=jnp.int32) * EXPERT_BM
    block_e = jnp.sum((block_start[:, None] >= pad_ends[None, :]).astype(jnp.int32), axis=1)
    block_e = jnp.minimum(block_e, N_EXPERTS - 1).astype(jnp.int32)
    n_used = (pad_ends[-1:] // EXPERT_BM).astype(jnp.int32)
    ids = jnp.arange(N_EXPERTS, dtype=jnp.int32)
    later_used = (ids[None, :] > ids[:, None]) & (counts[None, :] > 0)
    next_used = jnp.min(jnp.where(later_used, ids[None, :], N_EXPERTS), axis=1).astype(jnp.int32)
    pad_lo = (pad_starts + counts).astype(jnp.int32)
    dest = dest.reshape(TOKENS * TOP_K).astype(jnp.int32)
    return dest, block_e, n_used, next_used, pad_lo, pad_ends.astype(jnp.int32)


def _vmem_row(buf, row):
    start = row * ROW_WORDS
    return buf.at[pl.ds(start if isinstance(row, int) else pl.multiple_of(start, ROW_WORDS), ROW_WORDS)]


def _dispatch_kernel(dest_ref, lo_ref, hi_ref, h_ref, xs_ref, rs_ref, zblk, sem, zsem):
    tm = DISP_TM
    base = pl.program_id(0) * tm

    def set_virtual(r, carry):
        rs_ref[r] = TOKENS * TOP_K + r
        return carry

    def set_dump(r, carry):
        rs_ref[r] = PAD_DUMP0 + (r & (PAD_DUMP_BLOCKS * EXPERT_BM - 1))
        return carry

    @pl.when(pl.program_id(0) == 0)
    def _():
        lax.fori_loop(0, EXPERT_BM, set_virtual, 0)

    def issue(g, carry):
        for u in range(DISP_UNROLL):
            t = g * DISP_UNROLL + u
            for k in range(TOP_K):
                pltpu.make_async_copy(_vmem_row(h_ref, t), xs_ref.at[dest_ref[(base + t) * TOP_K + k]],
                                      sem).start(priority=k % 2)
        return carry

    def record(g, carry):
        for u in range(INV_UNROLL):
            token = base + g * (INV_UNROLL // TOP_K) + u // TOP_K
            rs_ref[dest_ref[(base + g * (INV_UNROLL // TOP_K)) * TOP_K + u] + EXPERT_BM] = (u % TOP_K) * TOKENS + token
        return carry

    lax.fori_loop(0, tm // DISP_UNROLL, issue, 0)
    lax.fori_loop(0, tm * TOP_K // INV_UNROLL, record, 0)

    @pl.when(pl.program_id(0) == pl.num_programs(0) - 1)
    def _():
        zblk[...] = jnp.zeros((EXPERT_BM * ROW_WORDS, LANES), _U32)

        def pad_copy(r):
            return pltpu.make_async_copy(_vmem_row(zblk, 0), xs_ref.at[r], zsem)

        def tail_copy(b):
            return pltpu.make_async_copy(zblk.reshape(EXPERT_BM, ROW_WORDS, LANES),
                                         xs_ref.at[pl.ds(pl.multiple_of(b * EXPERT_BM, EXPERT_BM), EXPERT_BM)], zsem)

        first_unused = hi_ref[N_EXPERTS - 1] // EXPERT_BM
        for e in range(N_EXPERTS):
            lax.fori_loop(lo_ref[e] + EXPERT_BM, hi_ref[e] + EXPERT_BM, set_dump, 0)
        lax.fori_loop((first_unused + 1) * EXPERT_BM, N_ROWS + EXPERT_BM, set_dump, 0)
        for e in range(N_EXPERTS):
            lax.fori_loop(lo_ref[e], hi_ref[e], lambda r, c: (pad_copy(r).start(), c)[1], 0)
        lax.fori_loop(first_unused, N_BLOCKS, lambda b, c: (tail_copy(b).start(), c)[1], 0)
        for e in range(N_EXPERTS):
            lax.fori_loop(lo_ref[e], hi_ref[e], lambda r, c: (pad_copy(r).wait(), c)[1], 0)
        lax.fori_loop(first_unused, N_BLOCKS, lambda b, c: (tail_copy(b).wait(), c)[1], 0)

    for _ in range(TOP_K):
        pltpu.make_async_copy(h_ref.reshape(tm, ROW_WORDS, LANES), xs_ref.at[pl.ds(0, tm)], sem).wait()


def _dispatch(dest, pad_lo, pad_hi, h_rows):
    tm = DISP_TM
    return pl.pallas_call(
        _dispatch_kernel,
        grid_spec=pltpu.PrefetchScalarGridSpec(
            num_scalar_prefetch=3,
            grid=(TOKENS // tm,),
            in_specs=[pl.BlockSpec((tm * ROW_WORDS, LANES), lambda i, *_: (i, 0))],
            out_specs=[pl.BlockSpec(memory_space=pl.ANY), pl.BlockSpec(memory_space=pltpu.SMEM)],
            scratch_shapes=[pltpu.VMEM((EXPERT_BM * ROW_WORDS, LANES), _U32),
                            pltpu.SemaphoreType.DMA, pltpu.SemaphoreType.DMA],
        ),
        out_shape=[jax.ShapeDtypeStruct((N_ROWS, ROW_WORDS, LANES), _U32),
                   jax.ShapeDtypeStruct((N_ROWS + EXPERT_BM,), jnp.int32)],
        compiler_params=pltpu.CompilerParams(
            dimension_semantics=("arbitrary",), vmem_limit_bytes=VMEM_LIMIT),
        name="dispatch",
    )(dest, pad_lo, pad_hi, h_rows)


def _expert_kernel(be_ref, nu_ref, nx_ref, rs_ref, xs_ref, w1_hbm, b1_ref, w2_hbm, b2_ref, ys_ref,
                   w1f, w2f, w1s, w2s, obuf, wsem, osem, isem, nsw, *, layer):
    bm = EXPERT_BM
    b = pl.program_id(0)
    e = be_ref[b]
    n_used = nu_ref[0]
    half = MXU_COLS // 2

    def fetch(expert, slot):
        return (pltpu.make_async_copy(w1_hbm.at[layer, expert], w1f.at[slot], wsem.at[0, slot]),
                pltpu.make_async_copy(w2_hbm.at[layer, expert], w2f.at[slot], wsem.at[1, slot]))

    def scatter_start(step, slot):
        for i in range(bm):
            pltpu.make_async_copy(_vmem_row(obuf.at[slot], i), ys_ref.at[rs_ref[step * bm + i]],
                                  osem.at[slot]).start(priority=i % 2)

    def block_view(slot):
        return obuf.at[slot].reshape(bm, ROW_WORDS, LANES)

    def scatter_wait(slot):
        pltpu.make_async_copy(block_view(slot), ys_ref.at[pl.ds(0, bm)], osem.at[slot]).wait()

    @pl.when(b == 0)
    def _():
        nsw[0] = 0
        obuf[RING - 1] = jnp.zeros((bm * ROW_WORDS, LANES), _U32)
        for blk in range(PAD_DUMP_BLOCKS):
            pltpu.make_async_copy(block_view(RING - 1), ys_ref.at[pl.ds(PAD_DUMP0 + blk * bm, bm)], isem).start()
        for blk in range(PAD_DUMP_BLOCKS):
            pltpu.make_async_copy(block_view(RING - 1), ys_ref.at[pl.ds(PAD_DUMP0 + blk * bm, bm)], isem).wait()
        for virtual in range(2, RING + 1):
            pltpu.make_async_copy(
                block_view(RING - 1), ys_ref.at[pl.ds(TOKENS * TOP_K + (virtual - 1) * bm, bm)],
                osem.at[RING - virtual]).start()
        for copy in fetch(e, 0):
            copy.start()

    @pl.when(b == n_used)
    def _():
        scatter_start(b, (b + RING - 1) % RING)
        for slot in range(RING):
            scatter_wait(slot)

    @pl.when(b < n_used)
    def _():
        out_slot = b % RING
        scatter_wait(out_slot)

        @pl.when(jnp.logical_or(b == 0, be_ref[jnp.maximum(b - 1, 0)] != e))
        def _():
            slot = nsw[0] % 2
            for copy in fetch(e, slot):
                copy.wait()
            src = lax.broadcasted_iota(jnp.int32, (MXU_COLS, MXU_COLS), 0)
            dst = lax.broadcasted_iota(jnp.int32, (MXU_COLS, MXU_COLS), 1)
            want = jnp.where(dst < half, 2 * dst, 2 * (dst - half) + 1)
            perm = (src == want).astype(_BF16)
            for blk in range(2 * D_EXPERT // MXU_COLS):
                cs = slice(blk * MXU_COLS, (blk + 1) * MXU_COLS)
                w1s[:, cs] = jnp.dot(w1f[slot, :, cs].astype(_BF16), perm,
                                     preferred_element_type=_F32).astype(_BF16)
            w2s[...] = w2f[slot].astype(_BF16)
            nsw[0] = nsw[0] + 1
            nxt = nx_ref[e]

            @pl.when(nxt < N_EXPERTS)
            def _():
                for copy in fetch(nxt, 1 - slot):
                    copy.start()

        scatter_start(b, (b + RING - 1) % RING)
        x = _load_packed(xs_ref, (), bm, _BF16)
        u = jnp.dot(x, w1s[...], preferred_element_type=_F32) + b1_ref[0, 0]
        acts = []
        for blk in range(2 * D_EXPERT // MXU_COLS):
            glu = jnp.minimum(u[:, blk * MXU_COLS:blk * MXU_COLS + half], SWIGLU_LIMIT)
            lin = jnp.clip(u[:, blk * MXU_COLS + half:(blk + 1) * MXU_COLS], -SWIGLU_LIMIT, SWIGLU_LIMIT)
            acts.append((glu * _sigmoid(SWIGLU_ALPHA * glu) * (lin + 1.0)).astype(_BF16))
        hidden = jnp.concatenate(acts, axis=1)
        y = jnp.dot(hidden, w2s[...], preferred_element_type=_F32) + b2_ref[0, 0]
        _store_packed(obuf, (out_slot,), y)


def _experts(layer, block_e, n_used, next_used, row_slot, xs, w1, b1_grouped, w2, b2):
    bm = EXPERT_BM
    d, f = D_MODEL, D_EXPERT
    blk = lambda b, be, nu, nx, rs: jnp.minimum(b, nu[0] - 1)
    return pl.pallas_call(
        functools.partial(_expert_kernel, layer=layer),
        grid_spec=pltpu.PrefetchScalarGridSpec(
            num_scalar_prefetch=4,
            grid=(N_BLOCKS,),
            in_specs=[
                pl.BlockSpec((bm * ROW_WORDS, LANES), lambda *a: (blk(*a), 0)),
                pl.BlockSpec(memory_space=pl.ANY),
                pl.BlockSpec((1, 1, 1, 2 * f), lambda *a: (layer, a[1][blk(*a)], 0, 0)),
                pl.BlockSpec(memory_space=pl.ANY),
                pl.BlockSpec((1, 1, 1, d), lambda *a: (layer, a[1][blk(*a)], 0, 0)),
            ],
            out_specs=pl.BlockSpec(memory_space=pl.ANY),
            scratch_shapes=[
                pltpu.VMEM((2, d, 2 * f), _F32), pltpu.VMEM((2, f, d), _F32),
                pltpu.VMEM((d, 2 * f), _BF16), pltpu.VMEM((f, d), _BF16),
                pltpu.VMEM((RING, bm * ROW_WORDS, LANES), _U32),
                pltpu.SemaphoreType.DMA((2, 2)), pltpu.SemaphoreType.DMA((RING,)), pltpu.SemaphoreType.DMA,
                pltpu.SMEM((1,), jnp.int32),
            ],
        ),
        out_shape=jax.ShapeDtypeStruct((OUT_SLOTS, ROW_WORDS, LANES), _U32),
        compiler_params=pltpu.CompilerParams(
            dimension_semantics=("arbitrary",), vmem_limit_bytes=VMEM_LIMIT),
        name="experts",
    )(block_e, n_used, next_used, row_slot, xs.reshape(N_ROWS * ROW_WORDS, LANES), w1, b1_grouped, w2, b2)


def _group_glu_columns(b1):
    half = MXU_COLS // 2
    j = jnp.arange(MXU_COLS)
    want = jnp.where(j < half, 2 * j, 2 * (j - half) + 1)
    cols = (jnp.arange(2 * D_EXPERT // MXU_COLS)[:, None] * MXU_COLS + want[None, :]).reshape(-1)
    return b1[..., cols]


def _combine_kernel(y0_ref, y1_ref, y2_ref, y3_ref, x_ref, gate_ref, mod_ref, fg_ref, o_ref):
    out = _moe_residual((y0_ref, y1_ref, y2_ref, y3_ref), gate_ref, x_ref[...], mod_ref[0][5:6], COMB_TM)
    o_ref[...] = out * lax.rsqrt(jnp.mean(out * out, axis=-1, keepdims=True) + NORM_EPS) * fg_ref[...]


def _combine_final(ys, x1, gates, mod, final_g):
    tm = COMB_TM
    d = D_MODEL
    tiles = TOKENS // tm
    tiles_per_batch = SEQ // tm
    ys2 = ys.reshape(OUT_SLOTS * ROW_WORDS, LANES)
    slot_spec = lambda k: pl.BlockSpec((tm * ROW_WORDS, LANES), lambda i: (k * tiles + i, 0))
    return pl.pallas_call(
        _combine_kernel,
        grid=(tiles,),
        in_specs=[
            slot_spec(0), slot_spec(1), slot_spec(2), slot_spec(3),
            pl.BlockSpec((tm, d), lambda i: (i, 0)),
            pl.BlockSpec((tm, TOP_K), lambda i: (i, 0)),
            pl.BlockSpec((1, N_MOD, d), lambda i: (i // tiles_per_batch, 0, 0)),
            pl.BlockSpec((1, d), lambda i: (0, 0)),
        ],
        out_specs=pl.BlockSpec((tm, d), lambda i: (i, 0)),
        out_shape=jax.ShapeDtypeStruct((TOKENS, d), _F32),
        compiler_params=pltpu.CompilerParams(
            dimension_semantics=("arbitrary",), vmem_limit_bytes=VMEM_LIMIT),
        name="combine_final",
    )(ys2, ys2, ys2, ys2, x1, gates, mod, final_g.reshape(1, d))


def kernel(x, c, mod_w, mod_b, norm_g, conf_w1, conf_b1, conf_dw, conf_dw_b, conf_ln_g, conf_ln_b, conf_w2, conf_b2, sc_w_in, sc_conv, sc_w_out, router_w, router_b, exp_w1, exp_b1, exp_w2, exp_b2, final_g):
    mods = _modulation(c, mod_w, mod_b).reshape(DEPTH, BATCH, N_MOD, D_MODEL)
    b1_grouped = _group_glu_columns(exp_b1).reshape(DEPTH, N_EXPERTS, 1, 2 * D_EXPERT)
    b2_rows = exp_b2.reshape(DEPTH, N_EXPERTS, 1, D_MODEL)
    assert DEPTH == 2, "layer 0's MoE residual is applied inside layer 1's mixer kernel"
    pending = None
    for i in range(DEPTH):
        mod = mods[i]
        j = i // 2
        if i % 2 == 0:
            x1 = _conformer_mixer(x, mod, norm_g[i, 0], conf_w1[j], conf_b1[j], conf_dw[j], conf_dw_b[j],
                                  conf_ln_g[j], conf_ln_b[j], conf_w2[j], conf_b2[j])
        else:
            x1 = _shortconv_mixer(pending, mod, norm_g[i, 0], sc_w_in[j], sc_conv[j], sc_w_out[j])
        x1 = x1.reshape(TOKENS, D_MODEL)
        h_rows, idx, rank, gates, counts = _route(x1, mod, norm_g[i, 1], router_w[i], router_b[i])
        dest, block_e, n_used, next_used, pad_lo, pad_hi = _routing_tables(idx, rank, counts)
        xs, row_slot = _dispatch(dest, pad_lo, pad_hi, h_rows)
        ys = _experts(i, block_e, n_used, next_used, row_slot, xs, exp_w1, b1_grouped, exp_w2, b2_rows)
        pending = (ys, x1, gates, mod)
    out = _combine_final(*pending, final_g)
    return out.reshape(BATCH, SEQ, D_MODEL)
```

```python
import functools

import jax
import jax.numpy as jnp
from jax import lax
from jax.experimental import pallas as pl
from jax.experimental.pallas import tpu as pltpu

D_MODEL = 1024
BATCH = 8
SEQ = 2048
DEPTH = 2
TOKENS = BATCH * SEQ
CONF_KERNEL = 31
SHORT_KERNEL = 3
N_EXPERTS = 32
TOP_K = 4
D_EXPERT = D_MODEL
SWIGLU_ALPHA = 1.702
SWIGLU_LIMIT = 7.0
NORM_EPS = 1e-5
N_MOD = 6

LANES = 128
SUBLANES = 8
MXU_COLS = 256
VMEM_LIMIT = 56 * 1024 * 1024

HALF_D = D_MODEL // 2
ROW_WORDS = HALF_D // LANES
MIX_TS = 256
CONF_HALO = 32
SHORT_HALO = 8
CONV_RC = 64
ROUTE_TM = 512
DISP_TM = 2048
DISP_UNROLL = 4
COMB_TM = 256
EXPERT_BM = 256
N_ROWS = TOKENS * TOP_K + N_EXPERTS * EXPERT_BM
N_BLOCKS = N_ROWS // EXPERT_BM
RING = 3
PAD_DUMP0 = TOKENS * TOP_K + RING * EXPERT_BM
PAD_DUMP_BLOCKS = 4
OUT_SLOTS = PAD_DUMP0 + PAD_DUMP_BLOCKS * EXPERT_BM

_F32 = jnp.float32
_BF16 = jnp.bfloat16
_U32 = jnp.uint32
_HI_MASK = 0xFFFF0000


def _rms_mod(x, g, scale, shift):
    y = x * lax.rsqrt(jnp.mean(x * x, axis=-1, keepdims=True) + NORM_EPS)
    return (y * g) * (1.0 + scale) + shift


def _sigmoid(x):
    return 1.0 / (1.0 + jnp.exp(-x))


def _pack_rows(v):
    bits = lambda a: lax.bitcast_convert_type(a.astype(_BF16).astype(_F32), _U32)
    return (bits(v[:, HALF_D:]) & _U32(_HI_MASK)) | (bits(v[:, :HALF_D]) >> 16)


def _store_packed(ref, lead, v):
    rows = v.shape[0]
    words = _pack_rows(v)
    for j in range(ROW_WORDS):
        ref[lead + (pl.ds(j, rows, stride=ROW_WORDS), slice(None))] = words[:, j * LANES:(j + 1) * LANES]


def _load_packed(ref, lead, rows, dtype):
    lo, hi = [], []
    for j in range(ROW_WORDS):
        w = ref[lead + (pl.ds(j, rows, stride=ROW_WORDS), slice(None))]
        lo.append(lax.bitcast_convert_type(w << 16, _F32).astype(dtype))
        hi.append(lax.bitcast_convert_type(w & _U32(_HI_MASK), _F32).astype(dtype))
    return jnp.concatenate(lo + hi, axis=1)


def _mod_kernel(c_ref, w_ref, b_ref, o_ref):
    c = c_ref[...]
    c_act = c * _sigmoid(c)
    o_ref[0] = jnp.dot(c_act, w_ref[0], preferred_element_type=_F32,
                       precision=lax.Precision.HIGHEST) + b_ref[0]


def _modulation(c, mod_w, mod_b):
    tn = 1536
    n = N_MOD * D_MODEL
    return pl.pallas_call(
        _mod_kernel,
        grid=(DEPTH, n // tn),
        in_specs=[
            pl.BlockSpec((BATCH, D_MODEL), lambda i, j: (0, 0)),
            pl.BlockSpec((1, D_MODEL, tn), lambda i, j: (i, 0, j)),
            pl.BlockSpec((1, 1, tn), lambda i, j: (i, 0, j)),
        ],
        out_specs=pl.BlockSpec((1, BATCH, tn), lambda i, j: (i, 0, j)),
        out_shape=jax.ShapeDtypeStruct((DEPTH, BATCH, n), _F32),
        compiler_params=pltpu.CompilerParams(vmem_limit_bytes=VMEM_LIMIT),
        name="modulation",
    )(c, mod_w, mod_b.reshape(DEPTH, 1, n))


def _causal_taps(cbuf, shifted, w_ref, bias_row, out_ref, ntaps, halo, ts):
    off0 = halo - (ntaps - 1)
    for c in range(D_MODEL // LANES):
        cs = slice(c * LANES, (c + 1) * LANES)
        if shifted is not None:
            rows = cbuf.shape[0]
            whole = cbuf[:, cs]
            for r in range(1, SUBLANES):
                shifted[r - 1, :, cs] = pltpu.roll(whole, rows - r, axis=0)
        for rc in range(ts // CONV_RC):
            r0 = rc * CONV_RC
            acc = None
            for k in range(ntaps):
                s = off0 + k
                if shifted is None or s % SUBLANES == 0:
                    window = cbuf[r0 + s:r0 + s + CONV_RC, cs]
                else:
                    a = s - s % SUBLANES
                    window = shifted[s % SUBLANES - 1, r0 + a:r0 + a + CONV_RC, cs]
                term = w_ref[k:k + 1, cs] * window
                acc = term if acc is None else acc + term
            if bias_row is not None:
                acc = acc + bias_row[:, cs]
            out_ref[r0:r0 + CONV_RC, cs] = acc


def _conformer_kernel(x_ref, mod_ref, ng_ref, w1_ref, b1_ref, dw_ref, dwb_ref, lng_ref, lnb_ref,
                      w2_ref, b2_ref, o_ref, cbuf, shifted, vbuf):
    ts = MIX_TS

    @pl.when(pl.program_id(1) == 0)
    def _():
        cbuf[0:CONF_HALO, :] = jnp.zeros((CONF_HALO, D_MODEL), _F32)
        cbuf[CONF_HALO + ts:, :] = jnp.zeros((SUBLANES, D_MODEL), _F32)

    x = x_ref[0]
    m = mod_ref[0]
    h = _rms_mod(x, ng_ref[...], m[1:2], m[0:1])
    u = jnp.dot(h.astype(_BF16), w1_ref[...], preferred_element_type=_F32) + b1_ref[...]
    cbuf[CONF_HALO:CONF_HALO + ts, :] = u[:, :D_MODEL] * _sigmoid(u[:, D_MODEL:])
    _causal_taps(cbuf, shifted, dw_ref, dwb_ref[...], vbuf, CONF_KERNEL, CONF_HALO, ts)
    cbuf[0:CONF_HALO, :] = cbuf[ts:ts + CONF_HALO, :]
    v = vbuf[...]
    mu = jnp.mean(v, axis=-1, keepdims=True)
    vc = v - mu
    var = jnp.mean(vc * vc, axis=-1, keepdims=True)
    y = vc * lax.rsqrt(var + NORM_EPS) * lng_ref[...] + lnb_ref[...]
    y = y * _sigmoid(y)
    mix = jnp.dot(y.astype(_BF16), w2_ref[...], preferred_element_type=_F32) + b2_ref[...]
    o_ref[0] = x + m[2:3] * mix


def _conformer_mixer(x, mod, norm_g, w1, b1, dw, dw_b, ln_g, ln_b, w2, b2):
    d = D_MODEL
    row = lambda a: a.reshape(1, -1)
    const = lambda shape: pl.BlockSpec(shape, lambda b, s: (0,) * len(shape))
    return pl.pallas_call(
        _conformer_kernel,
        grid=(BATCH, SEQ // MIX_TS),
        in_specs=[
            pl.BlockSpec((1, MIX_TS, d), lambda b, s: (b, s, 0)),
            pl.BlockSpec((1, N_MOD, d), lambda b, s: (b, 0, 0)),
            const((1, d)), const((d, 2 * d)), const((1, 2 * d)), const((CONF_KERNEL, d)),
            const((1, d)), const((1, d)), const((1, d)), const((d, d)), const((1, d)),
        ],
        out_specs=pl.BlockSpec((1, MIX_TS, d), lambda b, s: (b, s, 0)),
        out_shape=jax.ShapeDtypeStruct((BATCH, SEQ, d), _F32),
        scratch_shapes=[pltpu.VMEM((CONF_HALO + MIX_TS + SUBLANES, d), _F32),
                        pltpu.VMEM((SUBLANES - 1, CONF_HALO + MIX_TS + SUBLANES, d), _F32),
                        pltpu.VMEM((MIX_TS, d), _F32)],
        compiler_params=pltpu.CompilerParams(
            dimension_semantics=("arbitrary", "arbitrary"), vmem_limit_bytes=VMEM_LIMIT),
        name="conformer_mixer",
    )(x, mod, row(norm_g), w1.astype(_BF16), row(b1), dw, row(dw_b), row(ln_g), row(ln_b),
      w2.astype(_BF16), row(b2))


def _moe_residual(y_refs, gate_ref, x1, gate2, rows):
    gates = gate_ref[...]
    y = None
    for k, yk_ref in enumerate(y_refs):
        term = gates[:, k:k + 1] * _load_packed(yk_ref, (), rows, _F32)
        y = term if y is None else y + term
    return x1 + gate2 * y


def _shortconv_kernel(y0_ref, y1_ref, y2_ref, y3_ref, gate_ref, pmod_ref, x_ref, mod_ref, ng_ref, win_ref, cw_ref,
                      wout_ref, o_ref, cbuf, vbuf):
    ts = MIX_TS
    d = D_MODEL

    @pl.when(pl.program_id(1) == 0)
    def _():
        cbuf[0:SHORT_HALO, :] = jnp.zeros((SHORT_HALO, d), _F32)

    x = _moe_residual((y0_ref, y1_ref, y2_ref, y3_ref), gate_ref, x_ref[0], pmod_ref[0][5:6], ts)
    m = mod_ref[0]
    h = _rms_mod(x, ng_ref[...], m[1:2], m[0:1])
    z = jnp.dot(h.astype(_BF16), win_ref[...], preferred_element_type=_F32)
    cbuf[SHORT_HALO:SHORT_HALO + ts, :] = z[:, d:2 * d] * z[:, 2 * d:]
    _causal_taps(cbuf, None, cw_ref, None, vbuf, SHORT_KERNEL, SHORT_HALO, ts)
    cbuf[0:SHORT_HALO, :] = cbuf[ts:ts + SHORT_HALO, :]
    y = z[:, :d] * vbuf[...]
    mix = jnp.dot(y.astype(_BF16), wout_ref[...], preferred_element_type=_F32)
    o_ref[0] = x + m[2:3] * mix


def _shortconv_mixer(pending, mod, norm_g, w_in, conv_w, w_out):
    ys, x_prev, gates, prev_mod = pending
    d = D_MODEL
    tiles_per_batch = SEQ // MIX_TS
    tiles = TOKENS // MIX_TS
    ys2 = ys.reshape(OUT_SLOTS * ROW_WORDS, LANES)
    const = lambda shape: pl.BlockSpec(shape, lambda b, s: (0,) * len(shape))
    slot_spec = lambda k: pl.BlockSpec((MIX_TS * ROW_WORDS, LANES),
                                       lambda b, s: (k * tiles + b * tiles_per_batch + s, 0))
    return pl.pallas_call(
        _shortconv_kernel,
        grid=(BATCH, SEQ // MIX_TS),
        in_specs=[
            slot_spec(0), slot_spec(1), slot_spec(2), slot_spec(3),
            pl.BlockSpec((MIX_TS, TOP_K), lambda b, s: (b * tiles_per_batch + s, 0)),
            pl.BlockSpec((1, N_MOD, d), lambda b, s: (b, 0, 0)),
            pl.BlockSpec((1, MIX_TS, d), lambda b, s: (b, s, 0)),
            pl.BlockSpec((1, N_MOD, d), lambda b, s: (b, 0, 0)),
            const((1, d)), const((d, 3 * d)), const((SHORT_KERNEL, d)), const((d, d)),
        ],
        out_specs=pl.BlockSpec((1, MIX_TS, d), lambda b, s: (b, s, 0)),
        out_shape=jax.ShapeDtypeStruct((BATCH, SEQ, d), _F32),
        scratch_shapes=[pltpu.VMEM((SHORT_HALO + MIX_TS, d), _F32), pltpu.VMEM((MIX_TS, d), _F32)],
        compiler_params=pltpu.CompilerParams(
            dimension_semantics=("arbitrary", "arbitrary"), vmem_limit_bytes=VMEM_LIMIT),
        name="shortconv_mixer",
    )(ys2, ys2, ys2, ys2, gates, prev_mod, x_prev.reshape(BATCH, SEQ, d), mod, norm_g.reshape(1, d),
      w_in.astype(_BF16), conv_w, w_out.astype(_BF16))


def _route_kernel(x_ref, mod_ref, ng_ref, rwh_ref, rwl_ref, rb_ref, h_ref, ir_ref, gate_ref, cnt_ref):
    tm = ROUTE_TM

    @pl.when(pl.program_id(0) == 0)
    def _():
        cnt_ref[...] = jnp.zeros((1, N_EXPERTS), _F32)

    x = x_ref[...]
    m = mod_ref[0]
    h = _rms_mod(x, ng_ref[...], m[4:5], m[3:4])
    _store_packed(h_ref, (), h)

    h_hi = h.astype(_BF16)
    h_lo = (h - h_hi.astype(_F32)).astype(_BF16)
    logits = (jnp.dot(h_hi, rwh_ref[...], preferred_element_type=_F32)
              + jnp.dot(h_lo, rwh_ref[...], preferred_element_type=_F32)
              + jnp.dot(h_hi, rwl_ref[...], preferred_element_type=_F32)) + rb_ref[...]
    lane = lax.broadcasted_iota(jnp.int32, (tm, N_EXPERTS), 1).astype(_F32)
    work = logits
    vals, idxs, hots = [], [], []
    for _ in range(TOP_K):
        top = jnp.max(work, axis=1, keepdims=True)
        pick = jnp.min(jnp.where(work == top, lane, float(N_EXPERTS)), axis=1, keepdims=True)
        hot = lane == pick
        vals.append(top)
        idxs.append(pick)
        hots.append(hot)
        work = jnp.where(hot, -jnp.inf, work)
    exps = [jnp.exp(v - vals[0]) for v in vals]
    denom = exps[0] + exps[1] + exps[2] + exps[3]

    multi = (hots[0] | hots[1] | hots[2] | hots[3]).astype(_BF16)
    r_i = lax.broadcasted_iota(jnp.int32, (tm, tm), 0)
    c_i = lax.broadcasted_iota(jnp.int32, (tm, tm), 1)
    before = (c_i < r_i).astype(_BF16)
    pos = cnt_ref[...] + jnp.dot(before, multi, preferred_element_type=_F32)
    cnt_ref[...] = cnt_ref[...] + jnp.sum(multi.astype(_F32), axis=0, keepdims=True)

    col = lax.broadcasted_iota(jnp.int32, (tm, LANES), 1)
    gate_col = lax.broadcasted_iota(jnp.int32, (tm, TOP_K), 1)
    ints = jnp.zeros((tm, LANES), _F32)
    gate_out = jnp.zeros((tm, TOP_K), _F32)
    for k in range(TOP_K):
        rank_k = jnp.sum(jnp.where(hots[k], pos, 0.0), axis=1, keepdims=True)
        ints = jnp.where(col == k, idxs[k], ints)
        ints = jnp.where(col == TOP_K + k, rank_k, ints)
        gate_out = jnp.where(gate_col == k, exps[k] / denom, gate_out)
    sel = (lax.broadcasted_iota(jnp.int32, (2 * TOP_K, LANES), 0)
           == lax.broadcasted_iota(jnp.int32, (2 * TOP_K, LANES), 1)).astype(_BF16)
    high = jnp.floor(ints * (1.0 / LANES))
    low = ints - high * LANES
    nt = (((1,), (1,)), ((), ()))
    rows = (lax.dot_general(sel, high.astype(_BF16), nt, preferred_element_type=_F32) * LANES
            + lax.dot_general(sel, low.astype(_BF16), nt, preferred_element_type=_F32))
    ir_ref[...] = rows.astype(jnp.int32)
    gate_ref[...] = gate_out


def _route(x1, mod, norm_g, router_w, router_b):
    d = D_MODEL
    tm = ROUTE_TM
    tiles_per_batch = SEQ // tm
    rw_hi = router_w.astype(_BF16)
    rw_lo = (router_w - rw_hi.astype(_F32)).astype(_BF16)
    return pl.pallas_call(
        _route_kernel,
        grid=(TOKENS // tm,),
        in_specs=[
            pl.BlockSpec((tm, d), lambda i: (i, 0)),
            pl.BlockSpec((1, N_MOD, d), lambda i: (i // tiles_per_batch, 0, 0)),
            pl.BlockSpec((1, d), lambda i: (0, 0)),
            pl.BlockSpec((d, N_EXPERTS), lambda i: (0, 0)),
            pl.BlockSpec((d, N_EXPERTS), lambda i: (0, 0)),
            pl.BlockSpec((1, N_EXPERTS), lambda i: (0, 0)),
        ],
        out_specs=[
            pl.BlockSpec((tm * ROW_WORDS, LANES), lambda i: (i, 0)),
            pl.BlockSpec((2 * TOP_K, tm), lambda i: (0, i)),
            pl.BlockSpec((tm, TOP_K), lambda i: (i, 0)),
            pl.BlockSpec((1, N_EXPERTS), lambda i: (0, 0)),
        ],
        out_shape=[
            jax.ShapeDtypeStruct((TOKENS * ROW_WORDS, LANES), _U32),
            jax.ShapeDtypeStruct((2 * TOP_K, TOKENS), jnp.int32),
            jax.ShapeDtypeStruct((TOKENS, TOP_K), _F32),
            jax.ShapeDtypeStruct((1, N_EXPERTS), _F32),
        ],
        compiler_params=pltpu.CompilerParams(
            dimension_semantics=("arbitrary",), vmem_limit_bytes=VMEM_LIMIT),
        name="route",
    )(x1, mod, norm_g.reshape(1, d), rw_hi, rw_lo, router_b.reshape(1, N_EXPERTS))


def _routing_tables(idx, rank, counts):
    counts = counts.reshape(N_EXPERTS).astype(jnp.int32)
    padded = (counts + EXPERT_BM - 1) // EXPERT_BM * EXPERT_BM
    pad_ends = jnp.cumsum(padded)
    pad_starts = pad_ends - padded
    dest = rank
    for e in range(N_EXPERTS):
        dest = dest + jnp.where(idx == e, pad_starts[e], 0)
    block_start = jnp.arange(N_BLOCKS, dtype=jnp.int32) * EXPERT_BM
    block_e = jnp.sum((block_start[:, None] >= pad_ends[None, :]).astype(jnp.int32), axis=1)
    block_e = jnp.minimum(block_e, N_EXPERTS - 1).astype(jnp.int32)
    n_used = (pad_ends[-1:] // EXPERT_BM).astype(jnp.int32)
    ids = jnp.arange(N_EXPERTS, dtype=jnp.int32)
    later_used = (ids[None, :] > ids[:, None]) & (counts[None, :] > 0)
    next_used = jnp.min(jnp.where(later_used, ids[None, :], N_EXPERTS), axis=1).astype(jnp.int32)
    pad_lo = (pad_starts + counts).astype(jnp.int32)
    dest = dest.reshape(TOKENS * TOP_K).astype(jnp.int32)
    return dest, block_e, n_used, next_used, pad_lo, pad_ends.astype(jnp.int32)


def _vmem_row(buf, row):
    start = row * ROW_WORDS
    return buf.at[pl.ds(start if isinstance(row, int) else pl.multiple_of(start, ROW_WORDS), ROW_WORDS)]


def _dispatch_kernel(dest_ref, lo_ref, hi_ref, h_ref, xs_ref, rs_ref, zblk, sem, zsem):
    tm = DISP_TM
    base = pl.program_id(0) * tm

    def set_virtual(r, carry):
        rs_ref[r] = TOKENS * TOP_K + r
        return carry

    def set_dump(r, carry):
        rs_ref[r] = PAD_DUMP0 + (r & (PAD_DUMP_BLOCKS * EXPERT_BM - 1))
        return carry

    @pl.when(pl.program_id(0) == 0)
    def _():
        lax.fori_loop(0, EXPERT_BM, set_virtual, 0)

    def issue(g, carry):
        toks = [g * DISP_UNROLL + u for u in range(DISP_UNROLL)]
        dests = [[dest_ref[k * TOKENS + base + t] for k in range(TOP_K)] for t in toks]
        for t, row in zip(toks, dests):
            for k, d in enumerate(row):
                pltpu.make_async_copy(_vmem_row(h_ref, t), xs_ref.at[d], sem).start(priority=k % 2)
            for k, d in enumerate(row):
                rs_ref[d + EXPERT_BM] = k * TOKENS + base + t
        return carry

    lax.fori_loop(0, tm // DISP_UNROLL, issue, 0)

    @pl.when(pl.program_id(0) == pl.num_programs(0) - 1)
    def _():
        zblk[...] = jnp.zeros((EXPERT_BM * ROW_WORDS, LANES), _U32)

        def pad_copy(r):
            return pltpu.make_async_copy(_vmem_row(zblk, 0), xs_ref.at[r], zsem)

        def tail_copy(b):
            return pltpu.make_async_copy(zblk.reshape(EXPERT_BM, ROW_WORDS, LANES),
                                         xs_ref.at[pl.ds(pl.multiple_of(b * EXPERT_BM, EXPERT_BM), EXPERT_BM)], zsem)

        first_unused = hi_ref[N_EXPERTS - 1] // EXPERT_BM
        for e in range(N_EXPERTS):
            lax.fori_loop(lo_ref[e] + EXPERT_BM, hi_ref[e] + EXPERT_BM, set_dump, 0)
        lax.fori_loop((first_unused + 1) * EXPERT_BM, N_ROWS + EXPERT_BM, set_dump, 0)
        for e in range(N_EXPERTS):
            lax.fori_loop(lo_ref[e], hi_ref[e], lambda r, c: (pad_copy(r).start(), c)[1], 0)
        lax.fori_loop(first_unused, N_BLOCKS, lambda b, c: (tail_copy(b).start(), c)[1], 0)
        for e in range(N_EXPERTS):
            lax.fori_loop(lo_ref[e], hi_ref[e], lambda r, c: (pad_copy(r).wait(), c)[1], 0)
        lax.fori_loop(first_unused, N_BLOCKS, lambda b, c: (tail_copy(b).wait(), c)[1], 0)

    for _ in range(TOP_K):
        pltpu.make_async_copy(h_ref.reshape(tm, ROW_WORDS, LANES), xs_ref.at[pl.ds(0, tm)], sem).wait()


def _dispatch(dest, pad_lo, pad_hi, h_rows):
    tm = DISP_TM
    return pl.pallas_call(
        _dispatch_kernel,
        grid_spec=pltpu.PrefetchScalarGridSpec(
            num_scalar_prefetch=3,
            grid=(TOKENS // tm,),
            in_specs=[pl.BlockSpec((tm * ROW_WORDS, LANES), lambda i, *_: (i, 0))],
            out_specs=[pl.BlockSpec(memory_space=pl.ANY), pl.BlockSpec(memory_space=pltpu.SMEM)],
            scratch_shapes=[pltpu.VMEM((EXPERT_BM * ROW_WORDS, LANES), _U32),
                            pltpu.SemaphoreType.DMA, pltpu.SemaphoreType.DMA],
        ),
        out_shape=[jax.ShapeDtypeStruct((N_ROWS, ROW_WORDS, LANES), _U32),
                   jax.ShapeDtypeStruct((N_ROWS + EXPERT_BM,), jnp.int32)],
        compiler_params=pltpu.CompilerParams(
            dimension_semantics=("arbitrary",), vmem_limit_bytes=VMEM_LIMIT),
        name="dispatch",
    )(dest, pad_lo, pad_hi, h_rows)


def _expert_kernel(be_ref, nu_ref, nx_ref, rs_ref, xs_ref, w1_hbm, b1_ref, w2_hbm, b2_ref, ys_ref,
                   w1f, w2f, w1s, w2s, obuf, wsem, osem, isem, nsw, *, layer):
    bm = EXPERT_BM
    b = pl.program_id(0)
    e = be_ref[b]
    n_used = nu_ref[0]
    half = MXU_COLS // 2

    def fetch(expert, slot):
        return (pltpu.make_async_copy(w1_hbm.at[layer, expert], w1f.at[slot], wsem.at[0, slot]),
                pltpu.make_async_copy(w2_hbm.at[layer, expert], w2f.at[slot], wsem.at[1, slot]))

    def scatter_start(step, slot):
        for i in range(bm):
            pltpu.make_async_copy(_vmem_row(obuf.at[slot], i), ys_ref.at[rs_ref[step * bm + i]],
                                  osem.at[slot]).start(priority=i % 2)

    def block_view(slot):
        return obuf.at[slot].reshape(bm, ROW_WORDS, LANES)

    def scatter_wait(slot):
        pltpu.make_async_copy(block_view(slot), ys_ref.at[pl.ds(0, bm)], osem.at[slot]).wait()

    @pl.when(b == 0)
    def _():
        nsw[0] = 0
        obuf[RING - 1] = jnp.zeros((bm * ROW_WORDS, LANES), _U32)
        for blk in range(PAD_DUMP_BLOCKS):
            pltpu.make_async_copy(block_view(RING - 1), ys_ref.at[pl.ds(PAD_DUMP0 + blk * bm, bm)], isem).start()
        for blk in range(PAD_DUMP_BLOCKS):
            pltpu.make_async_copy(block_view(RING - 1), ys_ref.at[pl.ds(PAD_DUMP0 + blk * bm, bm)], isem).wait()
        for virtual in range(2, RING + 1):
            pltpu.make_async_copy(
                block_view(RING - 1), ys_ref.at[pl.ds(TOKENS * TOP_K + (virtual - 1) * bm, bm)],
                osem.at[RING - virtual]).start()
        for copy in fetch(e, 0):
            copy.start()

    @pl.when(b == n_used)
    def _():
        scatter_start(b, (b + RING - 1) % RING)
        for slot in range(RING):
            scatter_wait(slot)

    @pl.when(b < n_used)
    def _():
        out_slot = b % RING
        scatter_wait(out_slot)

        @pl.when(jnp.logical_or(b == 0, be_ref[jnp.maximum(b - 1, 0)] != e))
        def _():
            slot = nsw[0] % 2
            for copy in fetch(e, slot):
                copy.wait()
            src = lax.broadcasted_iota(jnp.int32, (MXU_COLS, MXU_COLS), 0)
            dst = lax.broadcasted_iota(jnp.int32, (MXU_COLS, MXU_COLS), 1)
            want = jnp.where(dst < half, 2 * dst, 2 * (dst - half) + 1)
            perm = (src == want).astype(_BF16)
            for blk in range(2 * D_EXPERT // MXU_COLS):
                cs = slice(blk * MXU_COLS, (blk + 1) * MXU_COLS)
                w1s[:, cs] = jnp.dot(w1f[slot, :, cs].astype(_BF16), perm,
                                     preferred_element_type=_F32).astype(_BF16)
            w2s[...] = w2f[slot].astype(_BF16)
            nsw[0] = nsw[0] + 1
            nxt = nx_ref[e]

            @pl.when(nxt < N_EXPERTS)
            def _():
                for copy in fetch(nxt, 1 - slot):
                    copy.start()

        scatter_start(b, (b + RING - 1) % RING)
        x = _load_packed(xs_ref, (), bm, _BF16)
        u = jnp.dot(x, w1s[...], preferred_element_type=_F32) + b1_ref[0, 0]
        acts = []
        for blk in range(2 * D_EXPERT // MXU_COLS):
            glu = jnp.minimum(u[:, blk * MXU_COLS:blk * MXU_COLS + half], SWIGLU_LIMIT)
            lin = jnp.clip(u[:, blk * MXU_COLS + half:(blk + 1) * MXU_COLS], -SWIGLU_LIMIT, SWIGLU_LIMIT)
            acts.append((glu * _sigmoid(SWIGLU_ALPHA * glu) * (lin + 1.0)).astype(_BF16))
        hidden = jnp.concatenate(acts, axis=1)
        y = jnp.dot(hidden, w2s[...], preferred_element_type=_F32) + b2_ref[0, 0]
        _store_packed(obuf, (out_slot,), y)


def _experts(layer, block_e, n_used, next_used, row_slot, xs, w1, b1_grouped, w2, b2):
    bm = EXPERT_BM
    d, f = D_MODEL, D_EXPERT
    blk = lambda b, be, nu, nx, rs: jnp.minimum(b, nu[0] - 1)
    return pl.pallas_call(
        functools.partial(_expert_kernel, layer=layer),
        grid_spec=pltpu.PrefetchScalarGridSpec(
            num_scalar_prefetch=4,
            grid=(N_BLOCKS,),
            in_specs=[
                pl.BlockSpec((bm * ROW_WORDS, LANES), lambda *a: (blk(*a), 0)),
                pl.BlockSpec(memory_space=pl.ANY),
                pl.BlockSpec((1, 1, 1, 2 * f), lambda *a: (layer, a[1][blk(*a)], 0, 0)),
                pl.BlockSpec(memory_space=pl.ANY),
                pl.BlockSpec((1, 1, 1, d), lambda *a: (layer, a[1][blk(*a)], 0, 0)),
            ],
            out_specs=pl.BlockSpec(memory_space=pl.ANY),
            scratch_shapes=[
                pltpu.VMEM((2, d, 2 * f), _F32), pltpu.VMEM((2, f, d), _F32),
                pltpu.VMEM((d, 2 * f), _BF16), pltpu.VMEM((f, d), _BF16),
                pltpu.VMEM((RING, bm * ROW_WORDS, LANES), _U32),
                pltpu.SemaphoreType.DMA((2, 2)), pltpu.SemaphoreType.DMA((RING,)), pltpu.SemaphoreType.DMA,
                pltpu.SMEM((1,), jnp.int32),
            ],
        ),
        out_shape=jax.ShapeDtypeStruct((OUT_SLOTS, ROW_WORDS, LANES), _U32),
        compiler_params=pltpu.CompilerParams(
            dimension_semantics=("arbitrary",), vmem_limit_bytes=VMEM_LIMIT),
        name="experts",
    )(block_e, n_used, next_used, row_slot, xs.reshape(N_ROWS * ROW_WORDS, LANES), w1, b1_grouped, w2, b2)


def _group_glu_columns(b1):
    half = MXU_COLS // 2
    j = jnp.arange(MXU_COLS)
    want = jnp.where(j < half, 2 * j, 2 * (j - half) + 1)
    cols = (jnp.arange(2 * D_EXPERT // MXU_COLS)[:, None] * MXU_COLS + want[None, :]).reshape(-1)
    return b1[..., cols]


def _combine_kernel(y0_ref, y1_ref, y2_ref, y3_ref, x_ref, gate_ref, mod_ref, fg_ref, o_ref):
    out = _moe_residual((y0_ref, y1_ref, y2_ref, y3_ref), gate_ref, x_ref[...], mod_ref[0][5:6], COMB_TM)
    o_ref[...] = out * lax.rsqrt(jnp.mean(out * out, axis=-1, keepdims=True) + NORM_EPS) * fg_ref[...]


def _combine_final(ys, x1, gates, mod, final_g):
    tm = COMB_TM
    d = D_MODEL
    tiles = TOKENS // tm
    tiles_per_batch = SEQ // tm
    ys2 = ys.reshape(OUT_SLOTS * ROW_WORDS, LANES)
    slot_spec = lambda k: pl.BlockSpec((tm * ROW_WORDS, LANES), lambda i: (k * tiles + i, 0))
    return pl.pallas_call(
        _combine_kernel,
        grid=(tiles,),
        in_specs=[
            slot_spec(0), slot_spec(1), slot_spec(2), slot_spec(3),
            pl.BlockSpec((tm, d), lambda i: (i, 0)),
            pl.BlockSpec((tm, TOP_K), lambda i: (i, 0)),
            pl.BlockSpec((1, N_MOD, d), lambda i: (i // tiles_per_batch, 0, 0)),
            pl.BlockSpec((1, d), lambda i: (0, 0)),
        ],
        out_specs=pl.BlockSpec((tm, d), lambda i: (i, 0)),
        out_shape=jax.ShapeDtypeStruct((TOKENS, d), _F32),
        compiler_params=pltpu.CompilerParams(
            dimension_semantics=("arbitrary",), vmem_limit_bytes=VMEM_LIMIT),
        name="combine_final",
    )(ys2, ys2, ys2, ys2, x1, gates, mod, final_g.reshape(1, d))


def kernel(x, c, mod_w, mod_b, norm_g, conf_w1, conf_b1, conf_dw, conf_dw_b, conf_ln_g, conf_ln_b, conf_w2, conf_b2, sc_w_in, sc_conv, sc_w_out, router_w, router_b, exp_w1, exp_b1, exp_w2, exp_b2, final_g):
    mods = _modulation(c, mod_w, mod_b).reshape(DEPTH, BATCH, N_MOD, D_MODEL)
    b1_grouped = _group_glu_columns(exp_b1).reshape(DEPTH, N_EXPERTS, 1, 2 * D_EXPERT)
    b2_rows = exp_b2.reshape(DEPTH, N_EXPERTS, 1, D_MODEL)
    assert DEPTH == 2, "layer 0's MoE residual is applied inside layer 1's mixer kernel"
    pending = None
    for i in range(DEPTH):
        mod = mods[i]
        j = i // 2
        if i % 2 == 0:
            x1 = _conformer_mixer(x, mod, norm_g[i, 0], conf_w1[j], conf_b1[j], conf_dw[j], conf_dw_b[j],
                                  conf_ln_g[j], conf_ln_b[j], conf_w2[j], conf_b2[j])
        else:
            x1 = _shortconv_mixer(pending, mod, norm_g[i, 0], sc_w_in[j], sc_conv[j], sc_w_out[j])
        x1 = x1.reshape(TOKENS, D_MODEL)
        h_rows, idx_rank, gates, counts = _route(x1, mod, norm_g[i, 1], router_w[i], router_b[i])
        dest, block_e, n_used, next_used, pad_lo, pad_hi = _routing_tables(
            idx_rank[:TOP_K], idx_rank[TOP_K:], counts)
        xs, row_slot = _dispatch(dest, pad_lo, pad_hi, h_rows)
        ys = _experts(i, block_e, n_used, next_used, row_slot, xs, exp_w1, b1_grouped, exp_w2, b2_rows)
        pending = (ys, x1, gates, mod)
    out = _combine_final(*pending, final_g)
    return out.reshape(BATCH, SEQ, D_MODEL)
```

```python
import functools

import jax
import jax.numpy as jnp
from jax import lax
from jax.experimental import pallas as pl
from jax.experimental.pallas import tpu as pltpu

D_MODEL = 1024
BATCH = 8
SEQ = 2048
DEPTH = 2
TOKENS = BATCH * SEQ
CONF_KERNEL = 31
SHORT_KERNEL = 3
N_EXPERTS = 32
TOP_K = 4
D_EXPERT = D_MODEL
SWIGLU_ALPHA = 1.702
SWIGLU_LIMIT = 7.0
NORM_EPS = 1e-5
N_MOD = 6

LANES = 128
SUBLANES = 8
MXU_COLS = 256
VMEM_LIMIT = 56 * 1024 * 1024

HALF_D = D_MODEL // 2
ROW_WORDS = HALF_D // LANES
MIX_TS = 256
CONF_HALO = 32
SHORT_HALO = 8
CONV_RC = 64
ROUTE_TM = 512
DISP_TM = 2048
DISP_UNROLL = 4
COMB_TM = 256
EXPERT_BM = 256
N_ROWS = TOKENS * TOP_K + N_EXPERTS * EXPERT_BM
N_BLOCKS = N_ROWS // EXPERT_BM
RING = 3
PAD_DUMP0 = TOKENS * TOP_K + RING * EXPERT_BM
PAD_DUMP_BLOCKS = 4
OUT_SLOTS = PAD_DUMP0 + PAD_DUMP_BLOCKS * EXPERT_BM

_F32 = jnp.float32
_BF16 = jnp.bfloat16
_U32 = jnp.uint32
_HI_MASK = 0xFFFF0000


def _rms_mod(x, g, scale, shift):
    y = x * lax.rsqrt(jnp.mean(x * x, axis=-1, keepdims=True) + NORM_EPS)
    return (y * g) * (1.0 + scale) + shift


def _sigmoid(x):
    return 1.0 / (1.0 + jnp.exp(-x))


def _pack_rows(v):
    bits = lambda a: lax.bitcast_convert_type(a.astype(_BF16).astype(_F32), _U32)
    return (bits(v[:, HALF_D:]) & _U32(_HI_MASK)) | (bits(v[:, :HALF_D]) >> 16)


def _store_packed(ref, lead, v):
    rows = v.shape[0]
    words = _pack_rows(v)
    for j in range(ROW_WORDS):
        ref[lead + (pl.ds(j, rows, stride=ROW_WORDS), slice(None))] = words[:, j * LANES:(j + 1) * LANES]


def _load_packed(ref, lead, rows, dtype):
    lo, hi = [], []
    for j in range(ROW_WORDS):
        w = ref[lead + (pl.ds(j, rows, stride=ROW_WORDS), slice(None))]
        lo.append(lax.bitcast_convert_type(w << 16, _F32).astype(dtype))
        hi.append(lax.bitcast_convert_type(w & _U32(_HI_MASK), _F32).astype(dtype))
    return jnp.concatenate(lo + hi, axis=1)


def _mod_kernel(c_ref, w_ref, b_ref, o_ref):
    c = c_ref[...]
    c_act = c * _sigmoid(c)
    o_ref[0] = jnp.dot(c_act, w_ref[0], preferred_element_type=_F32,
                       precision=lax.Precision.HIGHEST) + b_ref[0]


def _modulation(c, mod_w, mod_b):
    tn = 1536
    n = N_MOD * D_MODEL
    return pl.pallas_call(
        _mod_kernel,
        grid=(DEPTH, n // tn),
        in_specs=[
            pl.BlockSpec((BATCH, D_MODEL), lambda i, j: (0, 0)),
            pl.BlockSpec((1, D_MODEL, tn), lambda i, j: (i, 0, j)),
            pl.BlockSpec((1, 1, tn), lambda i, j: (i, 0, j)),
        ],
        out_specs=pl.BlockSpec((1, BATCH, tn), lambda i, j: (i, 0, j)),
        out_shape=jax.ShapeDtypeStruct((DEPTH, BATCH, n), _F32),
        compiler_params=pltpu.CompilerParams(vmem_limit_bytes=VMEM_LIMIT),
        name="modulation",
    )(c, mod_w, mod_b.reshape(DEPTH, 1, n))


def _causal_taps(cbuf, shifted, w_ref, bias_row, out_ref, ntaps, halo, ts):
    off0 = halo - (ntaps - 1)
    for c in range(D_MODEL // LANES):
        cs = slice(c * LANES, (c + 1) * LANES)
        if shifted is not None:
            rows = cbuf.shape[0]
            whole = cbuf[:, cs]
            for r in range(1, SUBLANES):
                shifted[r - 1, :, cs] = pltpu.roll(whole, rows - r, axis=0)
        for rc in range(ts // CONV_RC):
            r0 = rc * CONV_RC
            acc = None
            for k in range(ntaps):
                s = off0 + k
                if shifted is None or s % SUBLANES == 0:
                    window = cbuf[r0 + s:r0 + s + CONV_RC, cs]
                else:
                    a = s - s % SUBLANES
                    window = shifted[s % SUBLANES - 1, r0 + a:r0 + a + CONV_RC, cs]
                term = w_ref[k:k + 1, cs] * window
                acc = term if acc is None else acc + term
            if bias_row is not None:
                acc = acc + bias_row[:, cs]
            out_ref[r0:r0 + CONV_RC, cs] = acc


def _conformer_kernel(x_ref, mod_ref, ng_ref, w1_ref, b1_ref, dw_ref, dwb_ref, lng_ref, lnb_ref,
                      w2_ref, b2_ref, o_ref, cbuf, shifted, vbuf):
    ts = MIX_TS

    @pl.when(pl.program_id(1) == 0)
    def _():
        cbuf[0:CONF_HALO, :] = jnp.zeros((CONF_HALO, D_MODEL), _F32)
        cbuf[CONF_HALO + ts:, :] = jnp.zeros((SUBLANES, D_MODEL), _F32)

    x = x_ref[0]
    m = mod_ref[0]
    h = _rms_mod(x, ng_ref[...], m[1:2], m[0:1])
    u = jnp.dot(h.astype(_BF16), w1_ref[...], preferred_element_type=_F32) + b1_ref[...]
    cbuf[CONF_HALO:CONF_HALO + ts, :] = u[:, :D_MODEL] * _sigmoid(u[:, D_MODEL:])
    _causal_taps(cbuf, shifted, dw_ref, dwb_ref[...], vbuf, CONF_KERNEL, CONF_HALO, ts)
    cbuf[0:CONF_HALO, :] = cbuf[ts:ts + CONF_HALO, :]
    v = vbuf[...]
    mu = jnp.mean(v, axis=-1, keepdims=True)
    vc = v - mu
    var = jnp.mean(vc * vc, axis=-1, keepdims=True)
    y = vc * lax.rsqrt(var + NORM_EPS) * lng_ref[...] + lnb_ref[...]
    y = y * _sigmoid(y)
    mix = jnp.dot(y.astype(_BF16), w2_ref[...], preferred_element_type=_F32) + b2_ref[...]
    o_ref[0] = x + m[2:3] * mix


def _conformer_mixer(x, mod, norm_g, w1, b1, dw, dw_b, ln_g, ln_b, w2, b2):
    d = D_MODEL
    row = lambda a: a.reshape(1, -1)
    const = lambda shape: pl.BlockSpec(shape, lambda b, s: (0,) * len(shape))
    return pl.pallas_call(
        _conformer_kernel,
        grid=(BATCH, SEQ // MIX_TS),
        in_specs=[
            pl.BlockSpec((1, MIX_TS, d), lambda b, s: (b, s, 0)),
            pl.BlockSpec((1, N_MOD, d), lambda b, s: (b, 0, 0)),
            const((1, d)), const((d, 2 * d)), const((1, 2 * d)), const((CONF_KERNEL, d)),
            const((1, d)), const((1, d)), const((1, d)), const((d, d)), const((1, d)),
        ],
        out_specs=pl.BlockSpec((1, MIX_TS, d), lambda b, s: (b, s, 0)),
        out_shape=jax.ShapeDtypeStruct((BATCH, SEQ, d), _F32),
        scratch_shapes=[pltpu.VMEM((CONF_HALO + MIX_TS + SUBLANES, d), _F32),
                        pltpu.VMEM((SUBLANES - 1, CONF_HALO + MIX_TS + SUBLANES, d), _F32),
                        pltpu.VMEM((MIX_TS, d), _F32)],
        compiler_params=pltpu.CompilerParams(
            dimension_semantics=("arbitrary", "arbitrary"), vmem_limit_bytes=VMEM_LIMIT),
        name="conformer_mixer",
    )(x, mod, row(norm_g), w1.astype(_BF16), row(b1), dw, row(dw_b), row(ln_g), row(ln_b),
      w2.astype(_BF16), row(b2))


def _moe_residual(y_refs, gate_ref, x1, gate2, rows):
    gates = gate_ref[...]
    y = None
    for k, yk_ref in enumerate(y_refs):
        term = gates[:, k:k + 1] * _load_packed(yk_ref, (), rows, _F32)
        y = term if y is None else y + term
    return x1 + gate2 * y


def _shortconv_kernel(y0_ref, y1_ref, y2_ref, y3_ref, gate_ref, pmod_ref, x_ref, mod_ref, ng_ref, win_ref, cw_ref,
                      wout_ref, o_ref, cbuf, vbuf):
    ts = MIX_TS
    d = D_MODEL

    @pl.when(pl.program_id(1) == 0)
    def _():
        cbuf[0:SHORT_HALO, :] = jnp.zeros((SHORT_HALO, d), _F32)

    x = _moe_residual((y0_ref, y1_ref, y2_ref, y3_ref), gate_ref, x_ref[0], pmod_ref[0][5:6], ts)
    m = mod_ref[0]
    h = _rms_mod(x, ng_ref[...], m[1:2], m[0:1])
    z = jnp.dot(h.astype(_BF16), win_ref[...], preferred_element_type=_F32)
    cbuf[SHORT_HALO:SHORT_HALO + ts, :] = z[:, d:2 * d] * z[:, 2 * d:]
    _causal_taps(cbuf, None, cw_ref, None, vbuf, SHORT_KERNEL, SHORT_HALO, ts)
    cbuf[0:SHORT_HALO, :] = cbuf[ts:ts + SHORT_HALO, :]
    y = z[:, :d] * vbuf[...]
    mix = jnp.dot(y.astype(_BF16), wout_ref[...], preferred_element_type=_F32)
    o_ref[0] = x + m[2:3] * mix


def _shortconv_mixer(pending, mod, norm_g, w_in, conv_w, w_out):
    ys, x_prev, gates, prev_mod = pending
    d = D_MODEL
    tiles_per_batch = SEQ // MIX_TS
    tiles = TOKENS // MIX_TS
    ys2 = ys.reshape(OUT_SLOTS * ROW_WORDS, LANES)
    const = lambda shape: pl.BlockSpec(shape, lambda b, s: (0,) * len(shape))
    slot_spec = lambda k: pl.BlockSpec((MIX_TS * ROW_WORDS, LANES),
                                       lambda b, s: (k * tiles + b * tiles_per_batch + s, 0))
    return pl.pallas_call(
        _shortconv_kernel,
        grid=(BATCH, SEQ // MIX_TS),
        in_specs=[
            slot_spec(0), slot_spec(1), slot_spec(2), slot_spec(3),
            pl.BlockSpec((MIX_TS, TOP_K), lambda b, s: (b * tiles_per_batch + s, 0)),
            pl.BlockSpec((1, N_MOD, d), lambda b, s: (b, 0, 0)),
            pl.BlockSpec((1, MIX_TS, d), lambda b, s: (b, s, 0)),
            pl.BlockSpec((1, N_MOD, d), lambda b, s: (b, 0, 0)),
            const((1, d)), const((d, 3 * d)), const((SHORT_KERNEL, d)), const((d, d)),
        ],
        out_specs=pl.BlockSpec((1, MIX_TS, d), lambda b, s: (b, s, 0)),
        out_shape=jax.ShapeDtypeStruct((BATCH, SEQ, d), _F32),
        scratch_shapes=[pltpu.VMEM((SHORT_HALO + MIX_TS, d), _F32), pltpu.VMEM((MIX_TS, d), _F32)],
        compiler_params=pltpu.CompilerParams(
            dimension_semantics=("arbitrary", "arbitrary"), vmem_limit_bytes=VMEM_LIMIT),
        name="shortconv_mixer",
    )(ys2, ys2, ys2, ys2, gates, prev_mod, x_prev.reshape(BATCH, SEQ, d), mod, norm_g.reshape(1, d),
      w_in.astype(_BF16), conv_w, w_out.astype(_BF16))


def _route_kernel(x_ref, mod_ref, ng_ref, rwh_ref, rwl_ref, rb_ref, h_ref, ir_ref, gate_ref, cnt_ref):
    tm = ROUTE_TM

    @pl.when(pl.program_id(0) == 0)
    def _():
        cnt_ref[...] = jnp.zeros((1, N_EXPERTS), _F32)

    x = x_ref[...]
    m = mod_ref[0]
    h = _rms_mod(x, ng_ref[...], m[4:5], m[3:4])
    _store_packed(h_ref, (), h)

    h_hi = h.astype(_BF16)
    h_lo = (h - h_hi.astype(_F32)).astype(_BF16)
    logits = (jnp.dot(h_hi, rwh_ref[...], preferred_element_type=_F32)
              + jnp.dot(h_lo, rwh_ref[...], preferred_element_type=_F32)
              + jnp.dot(h_hi, rwl_ref[...], preferred_element_type=_F32)) + rb_ref[...]
    lane = lax.broadcasted_iota(jnp.int32, (tm, N_EXPERTS), 1).astype(_F32)
    work = logits
    vals, idxs, hots = [], [], []
    for _ in range(TOP_K):
        top = jnp.max(work, axis=1, keepdims=True)
        pick = jnp.min(jnp.where(work == top, lane, float(N_EXPERTS)), axis=1, keepdims=True)
        hot = lane == pick
        vals.append(top)
        idxs.append(pick)
        hots.append(hot)
        work = jnp.where(hot, -jnp.inf, work)
    exps = [jnp.exp(v - vals[0]) for v in vals]
    denom = exps[0] + exps[1] + exps[2] + exps[3]

    multi = (hots[0] | hots[1] | hots[2] | hots[3]).astype(_BF16)
    r_i = lax.broadcasted_iota(jnp.int32, (tm, tm), 0)
    c_i = lax.broadcasted_iota(jnp.int32, (tm, tm), 1)
    before = (c_i < r_i).astype(_BF16)
    pos = cnt_ref[...] + jnp.dot(before, multi, preferred_element_type=_F32)
    cnt_ref[...] = cnt_ref[...] + jnp.sum(multi.astype(_F32), axis=0, keepdims=True)

    col = lax.broadcasted_iota(jnp.int32, (tm, LANES), 1)
    gate_col = lax.broadcasted_iota(jnp.int32, (tm, TOP_K), 1)
    ints = jnp.zeros((tm, LANES), _F32)
    gate_out = jnp.zeros((tm, TOP_K), _F32)
    for k in range(TOP_K):
        rank_k = jnp.sum(jnp.where(hots[k], pos, 0.0), axis=1, keepdims=True)
        ints = jnp.where(col == k, idxs[k], ints)
        ints = jnp.where(col == TOP_K + k, rank_k, ints)
        gate_out = jnp.where(gate_col == k, exps[k] / denom, gate_out)
    sel = (lax.broadcasted_iota(jnp.int32, (2 * TOP_K, LANES), 0)
           == lax.broadcasted_iota(jnp.int32, (2 * TOP_K, LANES), 1)).astype(_BF16)
    high = jnp.floor(ints * (1.0 / LANES))
    low = ints - high * LANES
    nt = (((1,), (1,)), ((), ()))
    rows = (lax.dot_general(sel, high.astype(_BF16), nt, preferred_element_type=_F32) * LANES
            + lax.dot_general(sel, low.astype(_BF16), nt, preferred_element_type=_F32))
    ir_ref[...] = rows.astype(jnp.int32)
    gate_ref[...] = gate_out


def _route(x1, mod, norm_g, router_w, router_b):
    d = D_MODEL
    tm = ROUTE_TM
    tiles_per_batch = SEQ // tm
    rw_hi = router_w.astype(_BF16)
    rw_lo = (router_w - rw_hi.astype(_F32)).astype(_BF16)
    return pl.pallas_call(
        _route_kernel,
        grid=(TOKENS // tm,),
        in_specs=[
            pl.BlockSpec((tm, d), lambda i: (i, 0)),
            pl.BlockSpec((1, N_MOD, d), lambda i: (i // tiles_per_batch, 0, 0)),
            pl.BlockSpec((1, d), lambda i: (0, 0)),
            pl.BlockSpec((d, N_EXPERTS), lambda i: (0, 0)),
            pl.BlockSpec((d, N_EXPERTS), lambda i: (0, 0)),
            pl.BlockSpec((1, N_EXPERTS), lambda i: (0, 0)),
        ],
        out_specs=[
            pl.BlockSpec((tm * ROW_WORDS, LANES), lambda i: (i, 0)),
            pl.BlockSpec((2 * TOP_K, tm), lambda i: (0, i)),
            pl.BlockSpec((tm, TOP_K), lambda i: (i, 0)),
            pl.BlockSpec((1, N_EXPERTS), lambda i: (0, 0)),
        ],
        out_shape=[
            jax.ShapeDtypeStruct((TOKENS * ROW_WORDS, LANES), _U32),
            jax.ShapeDtypeStruct((2 * TOP_K, TOKENS), jnp.int32),
            jax.ShapeDtypeStruct((TOKENS, TOP_K), _F32),
            jax.ShapeDtypeStruct((1, N_EXPERTS), _F32),
        ],
        compiler_params=pltpu.CompilerParams(
            dimension_semantics=("arbitrary",), vmem_limit_bytes=VMEM_LIMIT),
        name="route",
    )(x1, mod, norm_g.reshape(1, d), rw_hi, rw_lo, router_b.reshape(1, N_EXPERTS))


def _dest_kernel(ps_ref, ir_ref, o_ref):
    idx = ir_ref[0:TOP_K, :]
    dest = ir_ref[TOP_K:2 * TOP_K, :]
    for e in range(N_EXPERTS):
        dest = dest + jnp.where(idx == e, ps_ref[e], 0)
    o_ref[...] = dest


def _dest_rows(pad_starts, idx_rank):
    tn = 4096
    return pl.pallas_call(
        _dest_kernel,
        grid_spec=pltpu.PrefetchScalarGridSpec(
            num_scalar_prefetch=1,
            grid=(TOKENS // tn,),
            in_specs=[pl.BlockSpec((2 * TOP_K, tn), lambda i, ps: (0, i))],
            out_specs=pl.BlockSpec((TOP_K, tn), lambda i, ps: (0, i)),
        ),
        out_shape=jax.ShapeDtypeStruct((TOP_K, TOKENS), jnp.int32),
        name="dest_rows",
    )(pad_starts, idx_rank)


def _routing_tables(idx_rank, counts):
    counts = counts.reshape(N_EXPERTS).astype(jnp.int32)
    padded = (counts + EXPERT_BM - 1) // EXPERT_BM * EXPERT_BM
    pad_ends = jnp.cumsum(padded)
    pad_starts = pad_ends - padded
    dest = _dest_rows(pad_starts.astype(jnp.int32), idx_rank)
    block_start = jnp.arange(N_BLOCKS, dtype=jnp.int32) * EXPERT_BM
    block_e = jnp.sum((block_start[:, None] >= pad_ends[None, :]).astype(jnp.int32), axis=1)
    block_e = jnp.minimum(block_e, N_EXPERTS - 1).astype(jnp.int32)
    n_used = (pad_ends[-1:] // EXPERT_BM).astype(jnp.int32)
    ids = jnp.arange(N_EXPERTS, dtype=jnp.int32)
    later_used = (ids[None, :] > ids[:, None]) & (counts[None, :] > 0)
    next_used = jnp.min(jnp.where(later_used, ids[None, :], N_EXPERTS), axis=1).astype(jnp.int32)
    pad_lo = (pad_starts + counts).astype(jnp.int32)
    dest = dest.reshape(TOKENS * TOP_K).astype(jnp.int32)
    return dest, block_e, n_used, next_used, pad_lo, pad_ends.astype(jnp.int32)


def _vmem_row(buf, row):
    start = row * ROW_WORDS
    return buf.at[pl.ds(start if isinstance(row, int) else pl.multiple_of(start, ROW_WORDS), ROW_WORDS)]


def _dispatch_kernel(dest_ref, lo_ref, hi_ref, h_ref, xs_ref, rs_ref, zblk, sem, zsem):
    tm = DISP_TM
    base = pl.program_id(0) * tm

    def set_virtual(r, carry):
        rs_ref[r] = TOKENS * TOP_K + r
        return carry

    def set_dump(r, carry):
        rs_ref[r] = PAD_DUMP0 + (r & (PAD_DUMP_BLOCKS * EXPERT_BM - 1))
        return carry

    @pl.when(pl.program_id(0) == 0)
    def _():
        lax.fori_loop(0, EXPERT_BM, set_virtual, 0)

    def issue(g, carry):
        toks = [g * DISP_UNROLL + u for u in range(DISP_UNROLL)]
        dests = [[dest_ref[k * TOKENS + base + t] for k in range(TOP_K)] for t in toks]
        for t, row in zip(toks, dests):
            for k, d in enumerate(row):
                pltpu.make_async_copy(_vmem_row(h_ref, t), xs_ref.at[d], sem).start(priority=k % 2)
            for k, d in enumerate(row):
                rs_ref[d + EXPERT_BM] = k * TOKENS + base + t
        return carry

    lax.fori_loop(0, tm // DISP_UNROLL, issue, 0)

    @pl.when(pl.program_id(0) == pl.num_programs(0) - 1)
    def _():
        zblk[...] = jnp.zeros((EXPERT_BM * ROW_WORDS, LANES), _U32)

        def pad_copy(r):
            return pltpu.make_async_copy(_vmem_row(zblk, 0), xs_ref.at[r], zsem)

        def tail_copy(b):
            return pltpu.make_async_copy(zblk.reshape(EXPERT_BM, ROW_WORDS, LANES),
                                         xs_ref.at[pl.ds(pl.multiple_of(b * EXPERT_BM, EXPERT_BM), EXPERT_BM)], zsem)

        first_unused = hi_ref[N_EXPERTS - 1] // EXPERT_BM
        for e in range(N_EXPERTS):
            lax.fori_loop(lo_ref[e] + EXPERT_BM, hi_ref[e] + EXPERT_BM, set_dump, 0)
        lax.fori_loop((first_unused + 1) * EXPERT_BM, N_ROWS + EXPERT_BM, set_dump, 0)
        for e in range(N_EXPERTS):
            lax.fori_loop(lo_ref[e], hi_ref[e], lambda r, c: (pad_copy(r).start(), c)[1], 0)
        lax.fori_loop(first_unused, N_BLOCKS, lambda b, c: (tail_copy(b).start(), c)[1], 0)
        for e in range(N_EXPERTS):
            lax.fori_loop(lo_ref[e], hi_ref[e], lambda r, c: (pad_copy(r).wait(), c)[1], 0)
        lax.fori_loop(first_unused, N_BLOCKS, lambda b, c: (tail_copy(b).wait(), c)[1], 0)

    for _ in range(TOP_K):
        pltpu.make_async_copy(h_ref.reshape(tm, ROW_WORDS, LANES), xs_ref.at[pl.ds(0, tm)], sem).wait()


def _dispatch(dest, pad_lo, pad_hi, h_rows):
    tm = DISP_TM
    return pl.pallas_call(
        _dispatch_kernel,
        grid_spec=pltpu.PrefetchScalarGridSpec(
            num_scalar_prefetch=3,
            grid=(TOKENS // tm,),
            in_specs=[pl.BlockSpec((tm * ROW_WORDS, LANES), lambda i, *_: (i, 0))],
            out_specs=[pl.BlockSpec(memory_space=pl.ANY), pl.BlockSpec(memory_space=pltpu.SMEM)],
            scratch_shapes=[pltpu.VMEM((EXPERT_BM * ROW_WORDS, LANES), _U32),
                            pltpu.SemaphoreType.DMA, pltpu.SemaphoreType.DMA],
        ),
        out_shape=[jax.ShapeDtypeStruct((N_ROWS, ROW_WORDS, LANES), _U32),
                   jax.ShapeDtypeStruct((N_ROWS + EXPERT_BM,), jnp.int32)],
        compiler_params=pltpu.CompilerParams(
            dimension_semantics=("arbitrary",), vmem_limit_bytes=VMEM_LIMIT),
        name="dispatch",
    )(dest, pad_lo, pad_hi, h_rows)


def _expert_kernel(be_ref, nu_ref, nx_ref, rs_ref, xs_ref, w1_hbm, b1_ref, w2_hbm, b2_ref, ys_ref,
                   w1f, w2f, w1s, w2s, obuf, wsem, osem, isem, nsw, *, layer):
    bm = EXPERT_BM
    b = pl.program_id(0)
    e = be_ref[b]
    n_used = nu_ref[0]
    half = MXU_COLS // 2

    def fetch(expert, slot):
        return (pltpu.make_async_copy(w1_hbm.at[layer, expert], w1f.at[slot], wsem.at[0, slot]),
                pltpu.make_async_copy(w2_hbm.at[layer, expert], w2f.at[slot], wsem.at[1, slot]))

    def scatter_start(step, slot):
        for i in range(bm):
            pltpu.make_async_copy(_vmem_row(obuf.at[slot], i), ys_ref.at[rs_ref[step * bm + i]],
                                  osem.at[slot]).start(priority=i % 2)

    def block_view(slot):
        return obuf.at[slot].reshape(bm, ROW_WORDS, LANES)

    def scatter_wait(slot):
        pltpu.make_async_copy(block_view(slot), ys_ref.at[pl.ds(0, bm)], osem.at[slot]).wait()

    @pl.when(b == 0)
    def _():
        nsw[0] = 0
        obuf[RING - 1] = jnp.zeros((bm * ROW_WORDS, LANES), _U32)
        for blk in range(PAD_DUMP_BLOCKS):
            pltpu.make_async_copy(block_view(RING - 1), ys_ref.at[pl.ds(PAD_DUMP0 + blk * bm, bm)], isem).start()
        for blk in range(PAD_DUMP_BLOCKS):
            pltpu.make_async_copy(block_view(RING - 1), ys_ref.at[pl.ds(PAD_DUMP0 + blk * bm, bm)], isem).wait()
        for virtual in range(2, RING + 1):
            pltpu.make_async_copy(
                block_view(RING - 1), ys_ref.at[pl.ds(TOKENS * TOP_K + (virtual - 1) * bm, bm)],
                osem.at[RING - virtual]).start()
        for copy in fetch(e, 0):
            copy.start()

    @pl.when(b == n_used)
    def _():
        scatter_start(b, (b + RING - 1) % RING)
        for slot in range(RING):
            scatter_wait(slot)

    @pl.when(b < n_used)
    def _():
        out_slot = b % RING
        scatter_wait(out_slot)

        @pl.when(jnp.logical_or(b == 0, be_ref[jnp.maximum(b - 1, 0)] != e))
        def _():
            slot = nsw[0] % 2
            for copy in fetch(e, slot):
                copy.wait()
            src = lax.broadcasted_iota(jnp.int32, (MXU_COLS, MXU_COLS), 0)
            dst = lax.broadcasted_iota(jnp.int32, (MXU_COLS, MXU_COLS), 1)
            want = jnp.where(dst < half, 2 * dst, 2 * (dst - half) + 1)
            perm = (src == want).astype(_BF16)
            for blk in range(2 * D_EXPERT // MXU_COLS):
                cs = slice(blk * MXU_COLS, (blk + 1) * MXU_COLS)
                w1s[:, cs] = jnp.dot(w1f[slot, :, cs].astype(_BF16), perm,
                                     preferred_element_type=_F32).astype(_BF16)
            w2s[...] = w2f[slot].astype(_BF16)
            nsw[0] = nsw[0] + 1
            nxt = nx_ref[e]

            @pl.when(nxt < N_EXPERTS)
            def _():
                for copy in fetch(nxt, 1 - slot):
                    copy.start()

        scatter_start(b, (b + RING - 1) % RING)
        x = _load_packed(xs_ref, (), bm, _BF16)
        u = jnp.dot(x, w1s[...], preferred_element_type=_F32) + b1_ref[0, 0]
        acts = []
        for blk in range(2 * D_EXPERT // MXU_COLS):
            glu = jnp.minimum(u[:, blk * MXU_COLS:blk * MXU_COLS + half], SWIGLU_LIMIT)
            lin = jnp.clip(u[:, blk * MXU_COLS + half:(blk + 1) * MXU_COLS], -SWIGLU_LIMIT, SWIGLU_LIMIT)
            acts.append((glu * _sigmoid(SWIGLU_ALPHA * glu) * (lin + 1.0)).astype(_BF16))
        hidden = jnp.concatenate(acts, axis=1)
        y = jnp.dot(hidden, w2s[...], preferred_element_type=_F32) + b2_ref[0, 0]
        _store_packed(obuf, (out_slot,), y)


def _experts(layer, block_e, n_used, next_used, row_slot, xs, w1, b1_grouped, w2, b2):
    bm = EXPERT_BM
    d, f = D_MODEL, D_EXPERT
    blk = lambda b, be, nu, nx, rs: jnp.minimum(b, nu[0] - 1)
    return pl.pallas_call(
        functools.partial(_expert_kernel, layer=layer),
        grid_spec=pltpu.PrefetchScalarGridSpec(
            num_scalar_prefetch=4,
            grid=(N_BLOCKS,),
            in_specs=[
                pl.BlockSpec((bm * ROW_WORDS, LANES), lambda *a: (blk(*a), 0)),
                pl.BlockSpec(memory_space=pl.ANY),
                pl.BlockSpec((1, 1, 1, 2 * f), lambda *a: (layer, a[1][blk(*a)], 0, 0)),
                pl.BlockSpec(memory_space=pl.ANY),
                pl.BlockSpec((1, 1, 1, d), lambda *a: (layer, a[1][blk(*a)], 0, 0)),
            ],
            out_specs=pl.BlockSpec(memory_space=pl.ANY),
            scratch_shapes=[
                pltpu.VMEM((2, d, 2 * f), _F32), pltpu.VMEM((2, f, d), _F32),
                pltpu.VMEM((d, 2 * f), _BF16), pltpu.VMEM((f, d), _BF16),
                pltpu.VMEM((RING, bm * ROW_WORDS, LANES), _U32),
                pltpu.SemaphoreType.DMA((2, 2)), pltpu.SemaphoreType.DMA((RING,)), pltpu.SemaphoreType.DMA,
                pltpu.SMEM((1,), jnp.int32),
            ],
        ),
        out_shape=jax.ShapeDtypeStruct((OUT_SLOTS, ROW_WORDS, LANES), _U32),
        compiler_params=pltpu.CompilerParams(
            dimension_semantics=("arbitrary",), vmem_limit_bytes=VMEM_LIMIT),
        name="experts",
    )(block_e, n_used, next_used, row_slot, xs.reshape(N_ROWS * ROW_WORDS, LANES), w1, b1_grouped, w2, b2)


def _group_glu_columns(b1):
    half = MXU_COLS // 2
    j = jnp.arange(MXU_COLS)
    want = jnp.where(j < half, 2 * j, 2 * (j - half) + 1)
    cols = (jnp.arange(2 * D_EXPERT // MXU_COLS)[:, None] * MXU_COLS + want[None, :]).reshape(-1)
    return b1[..., cols]


def _combine_kernel(y0_ref, y1_ref, y2_ref, y3_ref, x_ref, gate_ref, mod_ref, fg_ref, o_ref):
    out = _moe_residual((y0_ref, y1_ref, y2_ref, y3_ref), gate_ref, x_ref[...], mod_ref[0][5:6], COMB_TM)
    o_ref[...] = out * lax.rsqrt(jnp.mean(out * out, axis=-1, keepdims=True) + NORM_EPS) * fg_ref[...]


def _combine_final(ys, x1, gates, mod, final_g):
    tm = COMB_TM
    d = D_MODEL
    tiles = TOKENS // tm
    tiles_per_batch = SEQ // tm
    ys2 = ys.reshape(OUT_SLOTS * ROW_WORDS, LANES)
    slot_spec = lambda k: pl.BlockSpec((tm * ROW_WORDS, LANES), lambda i: (k * tiles + i, 0))
    return pl.pallas_call(
        _combine_kernel,
        grid=(tiles,),
        in_specs=[
            slot_spec(0), slot_spec(1), slot_spec(2), slot_spec(3),
            pl.BlockSpec((tm, d), lambda i: (i, 0)),
            pl.BlockSpec((tm, TOP_K), lambda i: (i, 0)),
            pl.BlockSpec((1, N_MOD, d), lambda i: (i // tiles_per_batch, 0, 0)),
            pl.BlockSpec((1, d), lambda i: (0, 0)),
        ],
        out_specs=pl.BlockSpec((tm, d), lambda i: (i, 0)),
        out_shape=jax.ShapeDtypeStruct((TOKENS, d), _F32),
        compiler_params=pltpu.CompilerParams(
            dimension_semantics=("arbitrary",), vmem_limit_bytes=VMEM_LIMIT),
        name="combine_final",
    )(ys2, ys2, ys2, ys2, x1, gates, mod, final_g.reshape(1, d))


def kernel(x, c, mod_w, mod_b, norm_g, conf_w1, conf_b1, conf_dw, conf_dw_b, conf_ln_g, conf_ln_b, conf_w2, conf_b2, sc_w_in, sc_conv, sc_w_out, router_w, router_b, exp_w1, exp_b1, exp_w2, exp_b2, final_g):
    mods = _modulation(c, mod_w, mod_b).reshape(DEPTH, BATCH, N_MOD, D_MODEL)
    b1_grouped = _group_glu_columns(exp_b1).reshape(DEPTH, N_EXPERTS, 1, 2 * D_EXPERT)
    b2_rows = exp_b2.reshape(DEPTH, N_EXPERTS, 1, D_MODEL)
    assert DEPTH == 2, "layer 0's MoE residual is applied inside layer 1's mixer kernel"
    pending = None
    for i in range(DEPTH):
        mod = mods[i]
        j = i // 2
        if i % 2 == 0:
            x1 = _conformer_mixer(x, mod, norm_g[i, 0], conf_w1[j], conf_b1[j], conf_dw[j], conf_dw_b[j],
                                  conf_ln_g[j], conf_ln_b[j], conf_w2[j], conf_b2[j])
        else:
            x1 = _shortconv_mixer(pending, mod, norm_g[i, 0], sc_w_in[j], sc_conv[j], sc_w_out[j])
        x1 = x1.reshape(TOKENS, D_MODEL)
        h_rows, idx_rank, gates, counts = _route(x1, mod, norm_g[i, 1], router_w[i], router_b[i])
        dest, block_e, n_used, next_used, pad_lo, pad_hi = _routing_tables(idx_rank, counts)
        xs, row_slot = _dispatch(dest, pad_lo, pad_hi, h_rows)
        ys = _experts(i, block_e, n_used, next_used, row_slot, xs, exp_w1, b1_grouped, exp_w2, b2_rows)
        pending = (ys, x1, gates, mod)
    out = _combine_final(*pending, final_g)
    return out.reshape(BATCH, SEQ, D_MODEL)
```

```python
import functools

import jax
import jax.numpy as jnp
from jax import lax
from jax.experimental import pallas as pl
from jax.experimental.pallas import tpu as pltpu

D_MODEL = 1024
BATCH = 8
SEQ = 2048
DEPTH = 2
TOKENS = BATCH * SEQ
CONF_KERNEL = 31
SHORT_KERNEL = 3
N_EXPERTS = 32
TOP_K = 4
D_EXPERT = D_MODEL
SWIGLU_ALPHA = 1.702
SWIGLU_LIMIT = 7.0
NORM_EPS = 1e-5
N_MOD = 6

LANES = 128
SUBLANES = 8
MXU_COLS = 256
VMEM_LIMIT = 56 * 1024 * 1024

HALF_D = D_MODEL // 2
ROW_WORDS = HALF_D // LANES
MIX_TS = 256
CONF_HALO = 32
SHORT_HALO = 8
CONV_RC = 64
ROUTE_TM = 512
DISP_TM = 2048
DISP_UNROLL = 4
COMB_TM = 256
EXPERT_BM = 256
N_ROWS = TOKENS * TOP_K + N_EXPERTS * EXPERT_BM
N_BLOCKS = N_ROWS // EXPERT_BM
RING = 3
PAD_DUMP0 = TOKENS * TOP_K + RING * EXPERT_BM
PAD_DUMP_BLOCKS = 4
OUT_SLOTS = PAD_DUMP0 + PAD_DUMP_BLOCKS * EXPERT_BM

_F32 = jnp.float32
_BF16 = jnp.bfloat16
_U32 = jnp.uint32
_HI_MASK = 0xFFFF0000


def _rms_mod(x, g, scale, shift):
    y = x * lax.rsqrt(jnp.mean(x * x, axis=-1, keepdims=True) + NORM_EPS)
    return (y * g) * (1.0 + scale) + shift


def _sigmoid(x):
    return 1.0 / (1.0 + jnp.exp(-x))


def _pack_rows(v):
    bits = lambda a: lax.bitcast_convert_type(a.astype(_BF16).astype(_F32), _U32)
    return (bits(v[:, HALF_D:]) & _U32(_HI_MASK)) | (bits(v[:, :HALF_D]) >> 16)


def _store_packed(ref, lead, v):
    rows = v.shape[0]
    words = _pack_rows(v)
    for j in range(ROW_WORDS):
        ref[lead + (pl.ds(j, rows, stride=ROW_WORDS), slice(None))] = words[:, j * LANES:(j + 1) * LANES]


def _load_packed(ref, lead, rows, dtype):
    lo, hi = [], []
    for j in range(ROW_WORDS):
        w = ref[lead + (pl.ds(j, rows, stride=ROW_WORDS), slice(None))]
        lo.append(lax.bitcast_convert_type(w << 16, _F32).astype(dtype))
        hi.append(lax.bitcast_convert_type(w & _U32(_HI_MASK), _F32).astype(dtype))
    return jnp.concatenate(lo + hi, axis=1)


def _mod_kernel(c_ref, w_ref, b_ref, o_ref):
    c = c_ref[...]
    c_act = c * _sigmoid(c)
    o_ref[0] = jnp.dot(c_act, w_ref[0], preferred_element_type=_F32,
                       precision=lax.Precision.HIGHEST) + b_ref[0]


def _modulation(c, mod_w, mod_b):
    tn = 1536
    n = N_MOD * D_MODEL
    return pl.pallas_call(
        _mod_kernel,
        grid=(DEPTH, n // tn),
        in_specs=[
            pl.BlockSpec((BATCH, D_MODEL), lambda i, j: (0, 0)),
            pl.BlockSpec((1, D_MODEL, tn), lambda i, j: (i, 0, j)),
            pl.BlockSpec((1, 1, tn), lambda i, j: (i, 0, j)),
        ],
        out_specs=pl.BlockSpec((1, BATCH, tn), lambda i, j: (i, 0, j)),
        out_shape=jax.ShapeDtypeStruct((DEPTH, BATCH, n), _F32),
        compiler_params=pltpu.CompilerParams(vmem_limit_bytes=VMEM_LIMIT),
        name="modulation",
    )(c, mod_w, mod_b.reshape(DEPTH, 1, n))


def _causal_taps(cbuf, shifted, w_ref, bias_row, out_ref, ntaps, halo, ts):
    off0 = halo - (ntaps - 1)
    for c in range(D_MODEL // LANES):
        cs = slice(c * LANES, (c + 1) * LANES)
        if shifted is not None:
            rows = cbuf.shape[0]
            whole = cbuf[:, cs]
            for r in range(1, SUBLANES):
                shifted[r - 1, :, cs] = pltpu.roll(whole, rows - r, axis=0)
        for rc in range(ts // CONV_RC):
            r0 = rc * CONV_RC
            acc = None
            for k in range(ntaps):
                s = off0 + k
                if shifted is None or s % SUBLANES == 0:
                    window = cbuf[r0 + s:r0 + s + CONV_RC, cs]
                else:
                    a = s - s % SUBLANES
                    window = shifted[s % SUBLANES - 1, r0 + a:r0 + a + CONV_RC, cs]
                term = w_ref[k:k + 1, cs] * window
                acc = term if acc is None else acc + term
            if bias_row is not None:
                acc = acc + bias_row[:, cs]
            out_ref[r0:r0 + CONV_RC, cs] = acc


def _conformer_kernel(x_ref, mod_ref, ng_ref, w1_ref, b1_ref, dw_ref, dwb_ref, lng_ref, lnb_ref,
                      w2_ref, b2_ref, o_ref, cbuf, shifted, vbuf):
    ts = MIX_TS

    @pl.when(pl.program_id(1) == 0)
    def _():
        cbuf[0:CONF_HALO, :] = jnp.zeros((CONF_HALO, D_MODEL), _F32)
        cbuf[CONF_HALO + ts:, :] = jnp.zeros((SUBLANES, D_MODEL), _F32)

    x = x_ref[0]
    m = mod_ref[0]
    h = _rms_mod(x, ng_ref[...], m[1:2], m[0:1])
    u = jnp.dot(h.astype(_BF16), w1_ref[...], preferred_element_type=_F32) + b1_ref[...]
    cbuf[CONF_HALO:CONF_HALO + ts, :] = u[:, :D_MODEL] * _sigmoid(u[:, D_MODEL:])
    _causal_taps(cbuf, shifted, dw_ref, dwb_ref[...], vbuf, CONF_KERNEL, CONF_HALO, ts)
    cbuf[0:CONF_HALO, :] = cbuf[ts:ts + CONF_HALO, :]
    v = vbuf[...]
    mu = jnp.mean(v, axis=-1, keepdims=True)
    vc = v - mu
    var = jnp.mean(vc * vc, axis=-1, keepdims=True)
    y = vc * lax.rsqrt(var + NORM_EPS) * lng_ref[...] + lnb_ref[...]
    y = y * _sigmoid(y)
    mix = jnp.dot(y.astype(_BF16), w2_ref[...], preferred_element_type=_F32) + b2_ref[...]
    o_ref[0] = x + m[2:3] * mix


def _conformer_mixer(x, mod, norm_g, w1, b1, dw, dw_b, ln_g, ln_b, w2, b2):
    d = D_MODEL
    row = lambda a: a.reshape(1, -1)
    const = lambda shape: pl.BlockSpec(shape, lambda b, s: (0,) * len(shape))
    return pl.pallas_call(
        _conformer_kernel,
        grid=(BATCH, SEQ // MIX_TS),
        in_specs=[
            pl.BlockSpec((1, MIX_TS, d), lambda b, s: (b, s, 0)),
            pl.BlockSpec((1, N_MOD, d), lambda b, s: (b, 0, 0)),
            const((1, d)), const((d, 2 * d)), const((1, 2 * d)), const((CONF_KERNEL, d)),
            const((1, d)), const((1, d)), const((1, d)), const((d, d)), const((1, d)),
        ],
        out_specs=pl.BlockSpec((1, MIX_TS, d), lambda b, s: (b, s, 0)),
        out_shape=jax.ShapeDtypeStruct((BATCH, SEQ, d), _F32),
        scratch_shapes=[pltpu.VMEM((CONF_HALO + MIX_TS + SUBLANES, d), _F32),
                        pltpu.VMEM((SUBLANES - 1, CONF_HALO + MIX_TS + SUBLANES, d), _F32),
                        pltpu.VMEM((MIX_TS, d), _F32)],
        compiler_params=pltpu.CompilerParams(
            dimension_semantics=("arbitrary", "arbitrary"), vmem_limit_bytes=VMEM_LIMIT),
        name="conformer_mixer",
    )(x, mod, row(norm_g), w1.astype(_BF16), row(b1), dw, row(dw_b), row(ln_g), row(ln_b),
      w2.astype(_BF16), row(b2))


def _moe_residual(y_refs, gate_ref, x1, gate2, rows):
    gates = gate_ref[...]
    y = None
    for k, yk_ref in enumerate(y_refs):
        term = gates[:, k:k + 1] * _load_packed(yk_ref, (), rows, _F32)
        y = term if y is None else y + term
    return x1 + gate2 * y


def _shortconv_kernel(y0_ref, y1_ref, y2_ref, y3_ref, gate_ref, pmod_ref, x_ref, mod_ref, ng_ref, win_ref, cw_ref,
                      wout_ref, o_ref, cbuf, vbuf):
    ts = MIX_TS
    d = D_MODEL

    @pl.when(pl.program_id(1) == 0)
    def _():
        cbuf[0:SHORT_HALO, :] = jnp.zeros((SHORT_HALO, d), _F32)

    x = _moe_residual((y0_ref, y1_ref, y2_ref, y3_ref), gate_ref, x_ref[0], pmod_ref[0][5:6], ts)
    m = mod_ref[0]
    h = _rms_mod(x, ng_ref[...], m[1:2], m[0:1])
    z = jnp.dot(h.astype(_BF16), win_ref[...], preferred_element_type=_F32)
    cbuf[SHORT_HALO:SHORT_HALO + ts, :] = z[:, d:2 * d] * z[:, 2 * d:]
    _causal_taps(cbuf, None, cw_ref, None, vbuf, SHORT_KERNEL, SHORT_HALO, ts)
    cbuf[0:SHORT_HALO, :] = cbuf[ts:ts + SHORT_HALO, :]
    y = z[:, :d] * vbuf[...]
    mix = jnp.dot(y.astype(_BF16), wout_ref[...], preferred_element_type=_F32)
    o_ref[0] = x + m[2:3] * mix


def _shortconv_mixer(pending, mod, norm_g, w_in, conv_w, w_out):
    ys, x_prev, gates, prev_mod = pending
    d = D_MODEL
    tiles_per_batch = SEQ // MIX_TS
    tiles = TOKENS // MIX_TS
    ys2 = ys.reshape(OUT_SLOTS * ROW_WORDS, LANES)
    const = lambda shape: pl.BlockSpec(shape, lambda b, s: (0,) * len(shape))
    slot_spec = lambda k: pl.BlockSpec((MIX_TS * ROW_WORDS, LANES),
                                       lambda b, s: (k * tiles + b * tiles_per_batch + s, 0))
    return pl.pallas_call(
        _shortconv_kernel,
        grid=(BATCH, SEQ // MIX_TS),
        in_specs=[
            slot_spec(0), slot_spec(1), slot_spec(2), slot_spec(3),
            pl.BlockSpec((MIX_TS, TOP_K), lambda b, s: (b * tiles_per_batch + s, 0)),
            pl.BlockSpec((1, N_MOD, d), lambda b, s: (b, 0, 0)),
            pl.BlockSpec((1, MIX_TS, d), lambda b, s: (b, s, 0)),
            pl.BlockSpec((1, N_MOD, d), lambda b, s: (b, 0, 0)),
            const((1, d)), const((d, 3 * d)), const((SHORT_KERNEL, d)), const((d, d)),
        ],
        out_specs=pl.BlockSpec((1, MIX_TS, d), lambda b, s: (b, s, 0)),
        out_shape=jax.ShapeDtypeStruct((BATCH, SEQ, d), _F32),
        scratch_shapes=[pltpu.VMEM((SHORT_HALO + MIX_TS, d), _F32), pltpu.VMEM((MIX_TS, d), _F32)],
        compiler_params=pltpu.CompilerParams(
            dimension_semantics=("arbitrary", "arbitrary"), vmem_limit_bytes=VMEM_LIMIT),
        name="shortconv_mixer",
    )(ys2, ys2, ys2, ys2, gates, prev_mod, x_prev.reshape(BATCH, SEQ, d), mod, norm_g.reshape(1, d),
      w_in.astype(_BF16), conv_w, w_out.astype(_BF16))


def _route_kernel(x_ref, mod_ref, ng_ref, rwh_ref, rwl_ref, rb_ref, h_ref, ir_ref, gate_ref, cnt_ref):
    tm = ROUTE_TM

    @pl.when(pl.program_id(0) == 0)
    def _():
        cnt_ref[...] = jnp.zeros((1, N_EXPERTS), _F32)

    x = x_ref[...]
    m = mod_ref[0]
    h = _rms_mod(x, ng_ref[...], m[4:5], m[3:4])
    _store_packed(h_ref, (), h)

    h_hi = h.astype(_BF16)
    h_lo = (h - h_hi.astype(_F32)).astype(_BF16)
    logits = (jnp.dot(h_hi, rwh_ref[...], preferred_element_type=_F32)
              + jnp.dot(h_lo, rwh_ref[...], preferred_element_type=_F32)
              + jnp.dot(h_hi, rwl_ref[...], preferred_element_type=_F32)) + rb_ref[...]
    lane = lax.broadcasted_iota(jnp.int32, (tm, N_EXPERTS), 1).astype(_F32)
    work = logits
    vals, idxs, hots = [], [], []
    for _ in range(TOP_K):
        top = jnp.max(work, axis=1, keepdims=True)
        pick = jnp.min(jnp.where(work == top, lane, float(N_EXPERTS)), axis=1, keepdims=True)
        hot = lane == pick
        vals.append(top)
        idxs.append(pick)
        hots.append(hot)
        work = jnp.where(hot, -jnp.inf, work)
    exps = [jnp.exp(v - vals[0]) for v in vals]
    denom = exps[0] + exps[1] + exps[2] + exps[3]

    multi = (hots[0] | hots[1] | hots[2] | hots[3]).astype(_BF16)
    r_i = lax.broadcasted_iota(jnp.int32, (tm, tm), 0)
    c_i = lax.broadcasted_iota(jnp.int32, (tm, tm), 1)
    before = (c_i < r_i).astype(_BF16)
    pos = cnt_ref[...] + jnp.dot(before, multi, preferred_element_type=_F32)
    cnt_ref[...] = cnt_ref[...] + jnp.sum(multi.astype(_F32), axis=0, keepdims=True)

    col = lax.broadcasted_iota(jnp.int32, (tm, LANES), 1)
    gate_col = lax.broadcasted_iota(jnp.int32, (tm, TOP_K), 1)
    ints = jnp.zeros((tm, LANES), _F32)
    gate_out = jnp.zeros((tm, TOP_K), _F32)
    for k in range(TOP_K):
        rank_k = jnp.sum(jnp.where(hots[k], pos, 0.0), axis=1, keepdims=True)
        ints = jnp.where(col == k, idxs[k], ints)
        ints = jnp.where(col == TOP_K + k, rank_k, ints)
        gate_out = jnp.where(gate_col == k, exps[k] / denom, gate_out)
    sel = (lax.broadcasted_iota(jnp.int32, (2 * TOP_K, LANES), 0)
           == lax.broadcasted_iota(jnp.int32, (2 * TOP_K, LANES), 1)).astype(_BF16)
    high = jnp.floor(ints * (1.0 / LANES))
    low = ints - high * LANES
    nt = (((1,), (1,)), ((), ()))
    rows = (lax.dot_general(sel, high.astype(_BF16), nt, preferred_element_type=_F32) * LANES
            + lax.dot_general(sel, low.astype(_BF16), nt, preferred_element_type=_F32))
    ir_ref[...] = rows.astype(jnp.int32)
    gate_ref[...] = gate_out


def _route(x1, mod, norm_g, router_w, router_b):
    d = D_MODEL
    tm = ROUTE_TM
    tiles_per_batch = SEQ // tm
    rw_hi = router_w.astype(_BF16)
    rw_lo = (router_w - rw_hi.astype(_F32)).astype(_BF16)
    return pl.pallas_call(
        _route_kernel,
        grid=(TOKENS // tm,),
        in_specs=[
            pl.BlockSpec((tm, d), lambda i: (i, 0)),
            pl.BlockSpec((1, N_MOD, d), lambda i: (i // tiles_per_batch, 0, 0)),
            pl.BlockSpec((1, d), lambda i: (0, 0)),
            pl.BlockSpec((d, N_EXPERTS), lambda i: (0, 0)),
            pl.BlockSpec((d, N_EXPERTS), lambda i: (0, 0)),
            pl.BlockSpec((1, N_EXPERTS), lambda i: (0, 0)),
        ],
        out_specs=[
            pl.BlockSpec((tm * ROW_WORDS, LANES), lambda i: (i, 0)),
            pl.BlockSpec((2 * TOP_K, tm), lambda i: (0, i)),
            pl.BlockSpec((tm, TOP_K), lambda i: (i, 0)),
            pl.BlockSpec((1, N_EXPERTS), lambda i: (0, 0)),
        ],
        out_shape=[
            jax.ShapeDtypeStruct((TOKENS * ROW_WORDS, LANES), _U32),
            jax.ShapeDtypeStruct((2 * TOP_K, TOKENS), jnp.int32),
            jax.ShapeDtypeStruct((TOKENS, TOP_K), _F32),
            jax.ShapeDtypeStruct((1, N_EXPERTS), _F32),
        ],
        compiler_params=pltpu.CompilerParams(
            dimension_semantics=("arbitrary",), vmem_limit_bytes=VMEM_LIMIT),
        name="route",
    )(x1, mod, norm_g.reshape(1, d), rw_hi, rw_lo, router_b.reshape(1, N_EXPERTS))


def _dest_kernel(ps_ref, ir_ref, o_ref):
    idx = ir_ref[0:TOP_K, :]
    dest = ir_ref[TOP_K:2 * TOP_K, :]
    for e in range(N_EXPERTS):
        dest = dest + jnp.where(idx == e, ps_ref[e], 0)
    o_ref[...] = dest


def _dest_rows(pad_starts, idx_rank):
    tn = 4096
    return pl.pallas_call(
        _dest_kernel,
        grid_spec=pltpu.PrefetchScalarGridSpec(
            num_scalar_prefetch=1,
            grid=(TOKENS // tn,),
            in_specs=[pl.BlockSpec((2 * TOP_K, tn), lambda i, ps: (0, i))],
            out_specs=pl.BlockSpec((TOP_K, tn), lambda i, ps: (0, i)),
        ),
        out_shape=jax.ShapeDtypeStruct((TOP_K, TOKENS), jnp.int32),
        name="dest_rows",
    )(pad_starts, idx_rank)


def _routing_tables(idx_rank, counts):
    counts = counts.reshape(N_EXPERTS).astype(jnp.int32)
    padded = (counts + EXPERT_BM - 1) // EXPERT_BM * EXPERT_BM
    pad_ends = jnp.cumsum(padded)
    pad_starts = pad_ends - padded
    dest = _dest_rows(pad_starts.astype(jnp.int32), idx_rank)
    block_start = jnp.arange(N_BLOCKS, dtype=jnp.int32) * EXPERT_BM
    block_e = jnp.sum((block_start[:, None] >= pad_ends[None, :]).astype(jnp.int32), axis=1)
    block_e = jnp.minimum(block_e, N_EXPERTS - 1).astype(jnp.int32)
    n_used = (pad_ends[-1:] // EXPERT_BM).astype(jnp.int32)
    ids = jnp.arange(N_EXPERTS, dtype=jnp.int32)
    later_used = (ids[None, :] > ids[:, None]) & (counts[None, :] > 0)
    next_used = jnp.min(jnp.where(later_used, ids[None, :], N_EXPERTS), axis=1).astype(jnp.int32)
    pad_lo = (pad_starts + counts).astype(jnp.int32)
    dest = dest.reshape(TOKENS * TOP_K).astype(jnp.int32)
    return dest, block_e, n_used, next_used, pad_lo, pad_ends.astype(jnp.int32)


def _vmem_row(buf, row):
    start = row * ROW_WORDS
    return buf.at[pl.ds(start if isinstance(row, int) else pl.multiple_of(start, ROW_WORDS), ROW_WORDS)]


def _dispatch_kernel(dest_ref, lo_ref, hi_ref, h_ref, xs_ref, rs_ref, zblk, sem, zsem):
    tm = DISP_TM
    base = pl.program_id(0) * tm

    def set_virtual(r, carry):
        rs_ref[r] = TOKENS * TOP_K + r
        return carry

    def set_dump(r, carry):
        rs_ref[r] = PAD_DUMP0 + (r & (PAD_DUMP_BLOCKS * EXPERT_BM - 1))
        return carry

    @pl.when(pl.program_id(0) == 0)
    def _():
        lax.fori_loop(0, EXPERT_BM, set_virtual, 0)

    def issue(g, carry):
        toks = [g * DISP_UNROLL + u for u in range(DISP_UNROLL)]
        dests = [[dest_ref[k * TOKENS + base + t] for k in range(TOP_K)] for t in toks]
        for t, row in zip(toks, dests):
            for k, d in enumerate(row):
                pltpu.make_async_copy(_vmem_row(h_ref, t), xs_ref.at[d], sem).start(priority=k % 2)
            for k, d in enumerate(row):
                rs_ref[d + EXPERT_BM] = k * TOKENS + base + t
        return carry

    lax.fori_loop(0, tm // DISP_UNROLL, issue, 0)

    @pl.when(pl.program_id(0) == pl.num_programs(0) - 1)
    def _():
        zblk[...] = jnp.zeros((EXPERT_BM * ROW_WORDS, LANES), _U32)

        def pad_copy(r):
            return pltpu.make_async_copy(_vmem_row(zblk, 0), xs_ref.at[r], zsem)

        def tail_copy(b):
            return pltpu.make_async_copy(zblk.reshape(EXPERT_BM, ROW_WORDS, LANES),
                                         xs_ref.at[pl.ds(pl.multiple_of(b * EXPERT_BM, EXPERT_BM), EXPERT_BM)], zsem)

        first_unused = hi_ref[N_EXPERTS - 1] // EXPERT_BM
        for e in range(N_EXPERTS):
            lax.fori_loop(lo_ref[e] + EXPERT_BM, hi_ref[e] + EXPERT_BM, set_dump, 0)
        lax.fori_loop((first_unused + 1) * EXPERT_BM, N_ROWS + EXPERT_BM, set_dump, 0)
        for e in range(N_EXPERTS):
            lax.fori_loop(lo_ref[e], hi_ref[e], lambda r, c: (pad_copy(r).start(), c)[1], 0)
        lax.fori_loop(first_unused, N_BLOCKS, lambda b, c: (tail_copy(b).start(), c)[1], 0)
        for e in range(N_EXPERTS):
            lax.fori_loop(lo_ref[e], hi_ref[e], lambda r, c: (pad_copy(r).wait(), c)[1], 0)
        lax.fori_loop(first_unused, N_BLOCKS, lambda b, c: (tail_copy(b).wait(), c)[1], 0)

    for _ in range(TOP_K):
        pltpu.make_async_copy(h_ref.reshape(tm, ROW_WORDS, LANES), xs_ref.at[pl.ds(0, tm)], sem).wait()


def _dispatch(dest, pad_lo, pad_hi, h_rows):
    tm = DISP_TM
    return pl.pallas_call(
        _dispatch_kernel,
        grid_spec=pltpu.PrefetchScalarGridSpec(
            num_scalar_prefetch=3,
            grid=(TOKENS // tm,),
            in_specs=[pl.BlockSpec((tm * ROW_WORDS, LANES), lambda i, *_: (i, 0))],
            out_specs=[pl.BlockSpec(memory_space=pl.ANY), pl.BlockSpec(memory_space=pltpu.SMEM)],
            scratch_shapes=[pltpu.VMEM((EXPERT_BM * ROW_WORDS, LANES), _U32),
                            pltpu.SemaphoreType.DMA, pltpu.SemaphoreType.DMA],
        ),
        out_shape=[jax.ShapeDtypeStruct((N_ROWS, ROW_WORDS, LANES), _U32),
                   jax.ShapeDtypeStruct((N_ROWS + EXPERT_BM,), jnp.int32)],
        compiler_params=pltpu.CompilerParams(
            dimension_semantics=("arbitrary",), vmem_limit_bytes=VMEM_LIMIT),
        name="dispatch",
    )(dest, pad_lo, pad_hi, h_rows)


def _expert_kernel(be_ref, nu_ref, nx_ref, rs_ref, xs_ref, w1_hbm, b1_ref, w2_hbm, b2_ref, ys_ref,
                   w1f, w2f, w1s, w2s, obuf, wsem, osem, isem, nsw, *, layer):
    bm = EXPERT_BM
    b = pl.program_id(0)
    e = be_ref[b]
    n_used = nu_ref[0]
    half = MXU_COLS // 2

    def fetch(expert, slot):
        return (pltpu.make_async_copy(w1_hbm.at[layer, expert], w1f.at[slot], wsem.at[0, slot]),
                pltpu.make_async_copy(w2_hbm.at[layer, expert], w2f.at[slot], wsem.at[1, slot]))

    def scatter_start(step, slot):
        for i in range(bm):
            pltpu.make_async_copy(_vmem_row(obuf.at[slot], i), ys_ref.at[rs_ref[step * bm + i]],
                                  osem.at[slot]).start(priority=i % 2)

    def block_view(slot):
        return obuf.at[slot].reshape(bm, ROW_WORDS, LANES)

    def scatter_wait(slot):
        pltpu.make_async_copy(block_view(slot), ys_ref.at[pl.ds(0, bm)], osem.at[slot]).wait()

    @pl.when(b == 0)
    def _():
        nsw[0] = 0
        obuf[RING - 1] = jnp.zeros((bm * ROW_WORDS, LANES), _U32)
        for blk in range(PAD_DUMP_BLOCKS):
            pltpu.make_async_copy(block_view(RING - 1), ys_ref.at[pl.ds(PAD_DUMP0 + blk * bm, bm)], isem).start()
        for blk in range(PAD_DUMP_BLOCKS):
            pltpu.make_async_copy(block_view(RING - 1), ys_ref.at[pl.ds(PAD_DUMP0 + blk * bm, bm)], isem).wait()
        for virtual in range(2, RING + 1):
            pltpu.make_async_copy(
                block_view(RING - 1), ys_ref.at[pl.ds(TOKENS * TOP_K + (virtual - 1) * bm, bm)],
                osem.at[RING - virtual]).start()
        for copy in fetch(e, 0):
            copy.start()

    @pl.when(b == n_used)
    def _():
        scatter_start(b, (b + RING - 1) % RING)
        for slot in range(RING):
            scatter_wait(slot)

    @pl.when(b < n_used)
    def _():
        out_slot = b % RING
        scatter_wait(out_slot)

        @pl.when(jnp.logical_or(b == 0, be_ref[jnp.maximum(b - 1, 0)] != e))
        def _():
            slot = nsw[0] % 2
            for copy in fetch(e, slot):
                copy.wait()
            src = lax.broadcasted_iota(jnp.int32, (MXU_COLS, MXU_COLS), 0)
            dst = lax.broadcasted_iota(jnp.int32, (MXU_COLS, MXU_COLS), 1)
            want = jnp.where(dst < half, 2 * dst, 2 * (dst - half) + 1)
            perm = (src == want).astype(_BF16)
            for blk in range(2 * D_EXPERT // MXU_COLS):
                cs = slice(blk * MXU_COLS, (blk + 1) * MXU_COLS)
                w1s[:, cs] = jnp.dot(w1f[slot, :, cs].astype(_BF16), perm,
                                     preferred_element_type=_F32).astype(_BF16)
            w2s[...] = w2f[slot].astype(_BF16)
            nsw[0] = nsw[0] + 1
            nxt = nx_ref[e]

            @pl.when(nxt < N_EXPERTS)
            def _():
                for copy in fetch(nxt, 1 - slot):
                    copy.start()

        scatter_start(b, (b + RING - 1) % RING)
        x = _load_packed(xs_ref, (), bm, _BF16)
        u = jnp.dot(x, w1s[...], preferred_element_type=_F32) + b1_ref[0, e]
        acts = []
        for blk in range(2 * D_EXPERT // MXU_COLS):
            glu = jnp.minimum(u[:, blk * MXU_COLS:blk * MXU_COLS + half], SWIGLU_LIMIT)
            lin = jnp.clip(u[:, blk * MXU_COLS + half:(blk + 1) * MXU_COLS], -SWIGLU_LIMIT, SWIGLU_LIMIT)
            acts.append((glu * _sigmoid(SWIGLU_ALPHA * glu) * (lin + 1.0)).astype(_BF16))
        hidden = jnp.concatenate(acts, axis=1)
        y = jnp.dot(hidden, w2s[...], preferred_element_type=_F32) + b2_ref[0, e]
        _store_packed(obuf, (out_slot,), y)


def _experts(layer, block_e, n_used, next_used, row_slot, xs, w1, b1_grouped, w2, b2):
    bm = EXPERT_BM
    d, f = D_MODEL, D_EXPERT
    blk = lambda b, be, nu, nx, rs: jnp.minimum(b, nu[0] - 1)
    return pl.pallas_call(
        functools.partial(_expert_kernel, layer=layer),
        grid_spec=pltpu.PrefetchScalarGridSpec(
            num_scalar_prefetch=4,
            grid=(N_BLOCKS,),
            in_specs=[
                pl.BlockSpec((bm * ROW_WORDS, LANES), lambda *a: (blk(*a), 0)),
                pl.BlockSpec(memory_space=pl.ANY),
                pl.BlockSpec((1, N_EXPERTS, 1, 2 * f), lambda *a: (layer, 0, 0, 0)),
                pl.BlockSpec(memory_space=pl.ANY),
                pl.BlockSpec((1, N_EXPERTS, 1, d), lambda *a: (layer, 0, 0, 0)),
            ],
            out_specs=pl.BlockSpec(memory_space=pl.ANY),
            scratch_shapes=[
                pltpu.VMEM((2, d, 2 * f), _F32), pltpu.VMEM((2, f, d), _F32),
                pltpu.VMEM((d, 2 * f), _BF16), pltpu.VMEM((f, d), _BF16),
                pltpu.VMEM((RING, bm * ROW_WORDS, LANES), _U32),
                pltpu.SemaphoreType.DMA((2, 2)), pltpu.SemaphoreType.DMA((RING,)), pltpu.SemaphoreType.DMA,
                pltpu.SMEM((1,), jnp.int32),
            ],
        ),
        out_shape=jax.ShapeDtypeStruct((OUT_SLOTS, ROW_WORDS, LANES), _U32),
        compiler_params=pltpu.CompilerParams(
            dimension_semantics=("arbitrary",), vmem_limit_bytes=VMEM_LIMIT),
        name="experts",
    )(block_e, n_used, next_used, row_slot, xs.reshape(N_ROWS * ROW_WORDS, LANES), w1, b1_grouped, w2, b2)


def _group_glu_columns(b1):
    half = MXU_COLS // 2
    j = jnp.arange(MXU_COLS)
    want = jnp.where(j < half, 2 * j, 2 * (j - half) + 1)
    cols = (jnp.arange(2 * D_EXPERT // MXU_COLS)[:, None] * MXU_COLS + want[None, :]).reshape(-1)
    return b1[..., cols]


def _combine_kernel(y0_ref, y1_ref, y2_ref, y3_ref, x_ref, gate_ref, mod_ref, fg_ref, o_ref):
    out = _moe_residual((y0_ref, y1_ref, y2_ref, y3_ref), gate_ref, x_ref[...], mod_ref[0][5:6], COMB_TM)
    o_ref[...] = out * lax.rsqrt(jnp.mean(out * out, axis=-1, keepdims=True) + NORM_EPS) * fg_ref[...]


def _combine_final(ys, x1, gates, mod, final_g):
    tm = COMB_TM
    d = D_MODEL
    tiles = TOKENS // tm
    tiles_per_batch = SEQ // tm
    ys2 = ys.reshape(OUT_SLOTS * ROW_WORDS, LANES)
    slot_spec = lambda k: pl.BlockSpec((tm * ROW_WORDS, LANES), lambda i: (k * tiles + i, 0))
    return pl.pallas_call(
        _combine_kernel,
        grid=(tiles,),
        in_specs=[
            slot_spec(0), slot_spec(1), slot_spec(2), slot_spec(3),
            pl.BlockSpec((tm, d), lambda i: (i, 0)),
            pl.BlockSpec((tm, TOP_K), lambda i: (i, 0)),
            pl.BlockSpec((1, N_MOD, d), lambda i: (i // tiles_per_batch, 0, 0)),
            pl.BlockSpec((1, d), lambda i: (0, 0)),
        ],
        out_specs=pl.BlockSpec((tm, d), lambda i: (i, 0)),
        out_shape=jax.ShapeDtypeStruct((TOKENS, d), _F32),
        compiler_params=pltpu.CompilerParams(
            dimension_semantics=("arbitrary",), vmem_limit_bytes=VMEM_LIMIT),
        name="combine_final",
    )(ys2, ys2, ys2, ys2, x1, gates, mod, final_g.reshape(1, d))


def kernel(x, c, mod_w, mod_b, norm_g, conf_w1, conf_b1, conf_dw, conf_dw_b, conf_ln_g, conf_ln_b, conf_w2, conf_b2, sc_w_in, sc_conv, sc_w_out, router_w, router_b, exp_w1, exp_b1, exp_w2, exp_b2, final_g):
    mods = _modulation(c, mod_w, mod_b).reshape(DEPTH, BATCH, N_MOD, D_MODEL)
    b1_grouped = _group_glu_columns(exp_b1).reshape(DEPTH, N_EXPERTS, 1, 2 * D_EXPERT)
    b2_rows = exp_b2.reshape(DEPTH, N_EXPERTS, 1, D_MODEL)
    assert DEPTH == 2, "layer 0's MoE residual is applied inside layer 1's mixer kernel"
    pending = None
    for i in range(DEPTH):
        mod = mods[i]
        j = i // 2
        if i % 2 == 0:
            x1 = _conformer_mixer(x, mod, norm_g[i, 0], conf_w1[j], conf_b1[j], conf_dw[j], conf_dw_b[j],
                                  conf_ln_g[j], conf_ln_b[j], conf_w2[j], conf_b2[j])
        else:
            x1 = _shortconv_mixer(pending, mod, norm_g[i, 0], sc_w_in[j], sc_conv[j], sc_w_out[j])
        x1 = x1.reshape(TOKENS, D_MODEL)
        h_rows, idx_rank, gates, counts = _route(x1, mod, norm_g[i, 1], router_w[i], router_b[i])
        dest, block_e, n_used, next_used, pad_lo, pad_hi = _routing_tables(idx_rank, counts)
        xs, row_slot = _dispatch(dest, pad_lo, pad_hi, h_rows)
        ys = _experts(i, block_e, n_used, next_used, row_slot, xs, exp_w1, b1_grouped, exp_w2, b2_rows)
        pending = (ys, x1, gates, mod)
    out = _combine_final(*pending, final_g)
    return out.reshape(BATCH, SEQ, D_MODEL)
```

```python
import functools

import jax
import jax.numpy as jnp
from jax import lax
from jax.experimental import pallas as pl
from jax.experimental.pallas import tpu as pltpu

D_MODEL = 1024
BATCH = 8
SEQ = 2048
DEPTH = 2
TOKENS = BATCH * SEQ
CONF_KERNEL = 31
SHORT_KERNEL = 3
N_EXPERTS = 32
TOP_K = 4
D_EXPERT = D_MODEL
SWIGLU_ALPHA = 1.702
SWIGLU_LIMIT = 7.0
NORM_EPS = 1e-5
N_MOD = 6

LANES = 128
SUBLANES = 8
MXU_COLS = 256
VMEM_LIMIT = 56 * 1024 * 1024

HALF_D = D_MODEL // 2
ROW_WORDS = HALF_D // LANES
MIX_TS = 256
CONF_HALO = 32
SHORT_HALO = 8
CONV_RC = 64
ROUTE_TM = 512
DISP_TM = 2048
DISP_UNROLL = 8
COMB_TM = 256
EXPERT_BM = 256
N_ROWS = TOKENS * TOP_K + N_EXPERTS * EXPERT_BM
N_BLOCKS = N_ROWS // EXPERT_BM
RING = 3
PAD_DUMP0 = TOKENS * TOP_K + RING * EXPERT_BM
PAD_DUMP_BLOCKS = 4
OUT_SLOTS = PAD_DUMP0 + PAD_DUMP_BLOCKS * EXPERT_BM

_F32 = jnp.float32
_BF16 = jnp.bfloat16
_U32 = jnp.uint32
_HI_MASK = 0xFFFF0000


def _rms_mod(x, g, scale, shift):
    y = x * lax.rsqrt(jnp.mean(x * x, axis=-1, keepdims=True) + NORM_EPS)
    return (y * g) * (1.0 + scale) + shift


def _sigmoid(x):
    return 1.0 / (1.0 + jnp.exp(-x))


def _pack_rows(v):
    bits = lambda a: lax.bitcast_convert_type(a.astype(_BF16).astype(_F32), _U32)
    return (bits(v[:, HALF_D:]) & _U32(_HI_MASK)) | (bits(v[:, :HALF_D]) >> 16)


def _store_packed(ref, lead, v):
    rows = v.shape[0]
    words = _pack_rows(v)
    for j in range(ROW_WORDS):
        ref[lead + (pl.ds(j, rows, stride=ROW_WORDS), slice(None))] = words[:, j * LANES:(j + 1) * LANES]


def _load_packed(ref, lead, rows, dtype):
    lo, hi = [], []
    for j in range(ROW_WORDS):
        w = ref[lead + (pl.ds(j, rows, stride=ROW_WORDS), slice(None))]
        lo.append(lax.bitcast_convert_type(w << 16, _F32).astype(dtype))
        hi.append(lax.bitcast_convert_type(w & _U32(_HI_MASK), _F32).astype(dtype))
    return jnp.concatenate(lo + hi, axis=1)


def _mod_kernel(c_ref, w_ref, b_ref, o_ref):
    c = c_ref[...]
    c_act = c * _sigmoid(c)
    o_ref[0] = jnp.dot(c_act, w_ref[0], preferred_element_type=_F32,
                       precision=lax.Precision.HIGHEST) + b_ref[0]


def _modulation(c, mod_w, mod_b):
    tn = 1536
    n = N_MOD * D_MODEL
    return pl.pallas_call(
        _mod_kernel,
        grid=(DEPTH, n // tn),
        in_specs=[
            pl.BlockSpec((BATCH, D_MODEL), lambda i, j: (0, 0)),
            pl.BlockSpec((1, D_MODEL, tn), lambda i, j: (i, 0, j)),
            pl.BlockSpec((1, 1, tn), lambda i, j: (i, 0, j)),
        ],
        out_specs=pl.BlockSpec((1, BATCH, tn), lambda i, j: (i, 0, j)),
        out_shape=jax.ShapeDtypeStruct((DEPTH, BATCH, n), _F32),
        compiler_params=pltpu.CompilerParams(vmem_limit_bytes=VMEM_LIMIT),
        name="modulation",
    )(c, mod_w, mod_b.reshape(DEPTH, 1, n))


def _causal_taps(cbuf, shifted, w_ref, bias_row, out_ref, ntaps, halo, ts):
    off0 = halo - (ntaps - 1)
    for c in range(D_MODEL // LANES):
        cs = slice(c * LANES, (c + 1) * LANES)
        if shifted is not None:
            rows = cbuf.shape[0]
            whole = cbuf[:, cs]
            for r in range(1, SUBLANES):
                shifted[r - 1, :, cs] = pltpu.roll(whole, rows - r, axis=0)
        for rc in range(ts // CONV_RC):
            r0 = rc * CONV_RC
            acc = None
            for k in range(ntaps):
                s = off0 + k
                if shifted is None or s % SUBLANES == 0:
                    window = cbuf[r0 + s:r0 + s + CONV_RC, cs]
                else:
                    a = s - s % SUBLANES
                    window = shifted[s % SUBLANES - 1, r0 + a:r0 + a + CONV_RC, cs]
                term = w_ref[k:k + 1, cs] * window
                acc = term if acc is None else acc + term
            if bias_row is not None:
                acc = acc + bias_row[:, cs]
            out_ref[r0:r0 + CONV_RC, cs] = acc


def _conformer_kernel(x_ref, mod_ref, ng_ref, w1_ref, b1_ref, dw_ref, dwb_ref, lng_ref, lnb_ref,
                      w2_ref, b2_ref, o_ref, cbuf, shifted, vbuf):
    ts = MIX_TS

    @pl.when(pl.program_id(1) == 0)
    def _():
        cbuf[0:CONF_HALO, :] = jnp.zeros((CONF_HALO, D_MODEL), _F32)
        cbuf[CONF_HALO + ts:, :] = jnp.zeros((SUBLANES, D_MODEL), _F32)

    x = x_ref[0]
    m = mod_ref[0]
    h = _rms_mod(x, ng_ref[...], m[1:2], m[0:1])
    u = jnp.dot(h.astype(_BF16), w1_ref[...], preferred_element_type=_F32) + b1_ref[...]
    cbuf[CONF_HALO:CONF_HALO + ts, :] = u[:, :D_MODEL] * _sigmoid(u[:, D_MODEL:])
    _causal_taps(cbuf, shifted, dw_ref, dwb_ref[...], vbuf, CONF_KERNEL, CONF_HALO, ts)
    cbuf[0:CONF_HALO, :] = cbuf[ts:ts + CONF_HALO, :]
    v = vbuf[...]
    mu = jnp.mean(v, axis=-1, keepdims=True)
    vc = v - mu
    var = jnp.mean(vc * vc, axis=-1, keepdims=True)
    y = vc * lax.rsqrt(var + NORM_EPS) * lng_ref[...] + lnb_ref[...]
    y = y * _sigmoid(y)
    mix = jnp.dot(y.astype(_BF16), w2_ref[...], preferred_element_type=_F32) + b2_ref[...]
    o_ref[0] = x + m[2:3] * mix


def _conformer_mixer(x, mod, norm_g, w1, b1, dw, dw_b, ln_g, ln_b, w2, b2):
    d = D_MODEL
    row = lambda a: a.reshape(1, -1)
    const = lambda shape: pl.BlockSpec(shape, lambda b, s: (0,) * len(shape))
    return pl.pallas_call(
        _conformer_kernel,
        grid=(BATCH, SEQ // MIX_TS),
        in_specs=[
            pl.BlockSpec((1, MIX_TS, d), lambda b, s: (b, s, 0)),
            pl.BlockSpec((1, N_MOD, d), lambda b, s: (b, 0, 0)),
            const((1, d)), const((d, 2 * d)), const((1, 2 * d)), const((CONF_KERNEL, d)),
            const((1, d)), const((1, d)), const((1, d)), const((d, d)), const((1, d)),
        ],
        out_specs=pl.BlockSpec((1, MIX_TS, d), lambda b, s: (b, s, 0)),
        out_shape=jax.ShapeDtypeStruct((BATCH, SEQ, d), _F32),
        scratch_shapes=[pltpu.VMEM((CONF_HALO + MIX_TS + SUBLANES, d), _F32),
                        pltpu.VMEM((SUBLANES - 1, CONF_HALO + MIX_TS + SUBLANES, d), _F32),
                        pltpu.VMEM((MIX_TS, d), _F32)],
        compiler_params=pltpu.CompilerParams(
            dimension_semantics=("arbitrary", "arbitrary"), vmem_limit_bytes=VMEM_LIMIT),
        name="conformer_mixer",
    )(x, mod, row(norm_g), w1.astype(_BF16), row(b1), dw, row(dw_b), row(ln_g), row(ln_b),
      w2.astype(_BF16), row(b2))


def _moe_residual(y_refs, gate_ref, x1, gate2, rows):
    gates = gate_ref[...]
    y = None
    for k, yk_ref in enumerate(y_refs):
        term = gates[:, k:k + 1] * _load_packed(yk_ref, (), rows, _F32)
        y = term if y is None else y + term
    return x1 + gate2 * y


def _shortconv_kernel(y0_ref, y1_ref, y2_ref, y3_ref, gate_ref, pmod_ref, x_ref, mod_ref, ng_ref, win_ref, cw_ref,
                      wout_ref, o_ref, cbuf, vbuf):
    ts = MIX_TS
    d = D_MODEL

    @pl.when(pl.program_id(1) == 0)
    def _():
        cbuf[0:SHORT_HALO, :] = jnp.zeros((SHORT_HALO, d), _F32)

    x = _moe_residual((y0_ref, y1_ref, y2_ref, y3_ref), gate_ref, x_ref[0], pmod_ref[0][5:6], ts)
    m = mod_ref[0]
    h = _rms_mod(x, ng_ref[...], m[1:2], m[0:1])
    z = jnp.dot(h.astype(_BF16), win_ref[...], preferred_element_type=_F32)
    cbuf[SHORT_HALO:SHORT_HALO + ts, :] = z[:, d:2 * d] * z[:, 2 * d:]
    _causal_taps(cbuf, None, cw_ref, None, vbuf, SHORT_KERNEL, SHORT_HALO, ts)
    cbuf[0:SHORT_HALO, :] = cbuf[ts:ts + SHORT_HALO, :]
    y = z[:, :d] * vbuf[...]
    mix = jnp.dot(y.astype(_BF16), wout_ref[...], preferred_element_type=_F32)
    o_ref[0] = x + m[2:3] * mix


def _shortconv_mixer(pending, mod, norm_g, w_in, conv_w, w_out):
    ys, x_prev, gates, prev_mod = pending
    d = D_MODEL
    tiles_per_batch = SEQ // MIX_TS
    tiles = TOKENS // MIX_TS
    ys2 = ys.reshape(OUT_SLOTS * ROW_WORDS, LANES)
    const = lambda shape: pl.BlockSpec(shape, lambda b, s: (0,) * len(shape))
    slot_spec = lambda k: pl.BlockSpec((MIX_TS * ROW_WORDS, LANES),
                                       lambda b, s: (k * tiles + b * tiles_per_batch + s, 0))
    return pl.pallas_call(
        _shortconv_kernel,
        grid=(BATCH, SEQ // MIX_TS),
        in_specs=[
            slot_spec(0), slot_spec(1), slot_spec(2), slot_spec(3),
            pl.BlockSpec((MIX_TS, TOP_K), lambda b, s: (b * tiles_per_batch + s, 0)),
            pl.BlockSpec((1, N_MOD, d), lambda b, s: (b, 0, 0)),
            pl.BlockSpec((1, MIX_TS, d), lambda b, s: (b, s, 0)),
            pl.BlockSpec((1, N_MOD, d), lambda b, s: (b, 0, 0)),
            const((1, d)), const((d, 3 * d)), const((SHORT_KERNEL, d)), const((d, d)),
        ],
        out_specs=pl.BlockSpec((1, MIX_TS, d), lambda b, s: (b, s, 0)),
        out_shape=jax.ShapeDtypeStruct((BATCH, SEQ, d), _F32),
        scratch_shapes=[pltpu.VMEM((SHORT_HALO + MIX_TS, d), _F32), pltpu.VMEM((MIX_TS, d), _F32)],
        compiler_params=pltpu.CompilerParams(
            dimension_semantics=("arbitrary", "arbitrary"), vmem_limit_bytes=VMEM_LIMIT),
        name="shortconv_mixer",
    )(ys2, ys2, ys2, ys2, gates, prev_mod, x_prev.reshape(BATCH, SEQ, d), mod, norm_g.reshape(1, d),
      w_in.astype(_BF16), conv_w, w_out.astype(_BF16))


def _route_kernel(x_ref, mod_ref, ng_ref, rw_ref, rb_ref, h_ref, ir_ref, gate_ref, cnt_ref):
    tm = ROUTE_TM

    @pl.when(pl.program_id(0) == 0)
    def _():
        cnt_ref[...] = jnp.zeros((1, N_EXPERTS), _F32)

    x = x_ref[...]
    m = mod_ref[0]
    h = _rms_mod(x, ng_ref[...], m[4:5], m[3:4])
    _store_packed(h_ref, (), h)

    h_hi = h.astype(_BF16)
    h_lo = (h - h_hi.astype(_F32)).astype(_BF16)
    both = jnp.dot(h_hi, rw_ref[...], preferred_element_type=_F32)
    logits = (both[:, :N_EXPERTS] + both[:, N_EXPERTS:]
              + jnp.dot(h_lo, rw_ref[:, :N_EXPERTS], preferred_element_type=_F32)) + rb_ref[...]
    lane = lax.broadcasted_iota(jnp.int32, (tm, N_EXPERTS), 1).astype(_F32)
    work = logits
    vals, idxs, hots = [], [], []
    for _ in range(TOP_K):
        top = jnp.max(work, axis=1, keepdims=True)
        pick = jnp.min(jnp.where(work == top, lane, float(N_EXPERTS)), axis=1, keepdims=True)
        hot = lane == pick
        vals.append(top)
        idxs.append(pick)
        hots.append(hot)
        work = jnp.where(hot, -jnp.inf, work)
    exps = [jnp.exp(v - vals[0]) for v in vals]
    denom = exps[0] + exps[1] + exps[2] + exps[3]

    multi = (hots[0] | hots[1] | hots[2] | hots[3]).astype(_BF16)
    r_i = lax.broadcasted_iota(jnp.int32, (tm, tm), 0)
    c_i = lax.broadcasted_iota(jnp.int32, (tm, tm), 1)
    before = (c_i < r_i).astype(_BF16)
    pos = cnt_ref[...] + jnp.dot(before, multi, preferred_element_type=_F32)
    cnt_ref[...] = cnt_ref[...] + jnp.sum(multi.astype(_F32), axis=0, keepdims=True)

    col = lax.broadcasted_iota(jnp.int32, (tm, LANES), 1)
    gate_col = lax.broadcasted_iota(jnp.int32, (tm, TOP_K), 1)
    ints = jnp.zeros((tm, LANES), _F32)
    gate_out = jnp.zeros((tm, TOP_K), _F32)
    for k in range(TOP_K):
        rank_k = jnp.sum(jnp.where(hots[k], pos, 0.0), axis=1, keepdims=True)
        ints = jnp.where(col == k, idxs[k], ints)
        ints = jnp.where(col == TOP_K + k, rank_k, ints)
        gate_out = jnp.where(gate_col == k, exps[k] / denom, gate_out)
    sel = (lax.broadcasted_iota(jnp.int32, (2 * TOP_K, LANES), 0)
           == lax.broadcasted_iota(jnp.int32, (2 * TOP_K, LANES), 1)).astype(_BF16)
    high = jnp.floor(ints * (1.0 / LANES))
    low = ints - high * LANES
    nt = (((1,), (1,)), ((), ()))
    rows = (lax.dot_general(sel, high.astype(_BF16), nt, preferred_element_type=_F32) * LANES
            + lax.dot_general(sel, low.astype(_BF16), nt, preferred_element_type=_F32))
    ir_ref[...] = rows.astype(jnp.int32)
    gate_ref[...] = gate_out


def _route(x1, mod, norm_g, router_w, router_b):
    d = D_MODEL
    tm = ROUTE_TM
    tiles_per_batch = SEQ // tm
    rw_hi = router_w.astype(_BF16)
    rw_lo = (router_w - rw_hi.astype(_F32)).astype(_BF16)
    return pl.pallas_call(
        _route_kernel,
        grid=(TOKENS // tm,),
        in_specs=[
            pl.BlockSpec((tm, d), lambda i: (i, 0)),
            pl.BlockSpec((1, N_MOD, d), lambda i: (i // tiles_per_batch, 0, 0)),
            pl.BlockSpec((1, d), lambda i: (0, 0)),
            pl.BlockSpec((d, 2 * N_EXPERTS), lambda i: (0, 0)),
            pl.BlockSpec((1, N_EXPERTS), lambda i: (0, 0)),
        ],
        out_specs=[
            pl.BlockSpec((tm * ROW_WORDS, LANES), lambda i: (i, 0)),
            pl.BlockSpec((2 * TOP_K, tm), lambda i: (0, i)),
            pl.BlockSpec((tm, TOP_K), lambda i: (i, 0)),
            pl.BlockSpec((1, N_EXPERTS), lambda i: (0, 0)),
        ],
        out_shape=[
            jax.ShapeDtypeStruct((TOKENS * ROW_WORDS, LANES), _U32),
            jax.ShapeDtypeStruct((2 * TOP_K, TOKENS), jnp.int32),
            jax.ShapeDtypeStruct((TOKENS, TOP_K), _F32),
            jax.ShapeDtypeStruct((1, N_EXPERTS), _F32),
        ],
        compiler_params=pltpu.CompilerParams(
            dimension_semantics=("arbitrary",), vmem_limit_bytes=VMEM_LIMIT),
        name="route",
    )(x1, mod, norm_g.reshape(1, d), jnp.concatenate([rw_hi, rw_lo], axis=1), router_b.reshape(1, N_EXPERTS))


def _dest_kernel(ps_ref, ir_ref, o_ref):
    idx = ir_ref[0:TOP_K, :]
    dest = ir_ref[TOP_K:2 * TOP_K, :]
    for e in range(N_EXPERTS):
        dest = dest + jnp.where(idx == e, ps_ref[e], 0)
    o_ref[...] = dest


def _dest_rows(pad_starts, idx_rank):
    tn = 4096
    return pl.pallas_call(
        _dest_kernel,
        grid_spec=pltpu.PrefetchScalarGridSpec(
            num_scalar_prefetch=1,
            grid=(TOKENS // tn,),
            in_specs=[pl.BlockSpec((2 * TOP_K, tn), lambda i, ps: (0, i))],
            out_specs=pl.BlockSpec((TOP_K, tn), lambda i, ps: (0, i)),
        ),
        out_shape=jax.ShapeDtypeStruct((TOP_K, TOKENS), jnp.int32),
        name="dest_rows",
    )(pad_starts, idx_rank)


def _routing_tables(idx_rank, counts):
    counts = counts.reshape(N_EXPERTS).astype(jnp.int32)
    padded = (counts + EXPERT_BM - 1) // EXPERT_BM * EXPERT_BM
    pad_ends = jnp.cumsum(padded)
    pad_starts = pad_ends - padded
    dest = _dest_rows(pad_starts.astype(jnp.int32), idx_rank)
    block_start = jnp.arange(N_BLOCKS, dtype=jnp.int32) * EXPERT_BM
    block_e = jnp.sum((block_start[:, None] >= pad_ends[None, :]).astype(jnp.int32), axis=1)
    block_e = jnp.minimum(block_e, N_EXPERTS - 1).astype(jnp.int32)
    n_used = (pad_ends[-1:] // EXPERT_BM).astype(jnp.int32)
    ids = jnp.arange(N_EXPERTS, dtype=jnp.int32)
    later_used = (ids[None, :] > ids[:, None]) & (counts[None, :] > 0)
    next_used = jnp.min(jnp.where(later_used, ids[None, :], N_EXPERTS), axis=1).astype(jnp.int32)
    pad_lo = (pad_starts + counts).astype(jnp.int32)
    dest = dest.reshape(TOKENS * TOP_K).astype(jnp.int32)
    return dest, block_e, n_used, next_used, pad_lo, pad_ends.astype(jnp.int32)


def _vmem_row(buf, row):
    start = row * ROW_WORDS
    return buf.at[pl.ds(start if isinstance(row, int) else pl.multiple_of(start, ROW_WORDS), ROW_WORDS)]


def _dispatch_kernel(dest_ref, lo_ref, hi_ref, h_ref, xs_ref, rs_ref, zblk, sem, zsem):
    tm = DISP_TM
    base = pl.program_id(0) * tm

    def set_virtual(r, carry):
        rs_ref[r] = TOKENS * TOP_K + r
        return carry

    def set_dump(r, carry):
        rs_ref[r] = PAD_DUMP0 + (r & (PAD_DUMP_BLOCKS * EXPERT_BM - 1))
        return carry

    @pl.when(pl.program_id(0) == 0)
    def _():
        lax.fori_loop(0, EXPERT_BM, set_virtual, 0)

    def issue(g, carry):
        toks = [g * DISP_UNROLL + u for u in range(DISP_UNROLL)]
        dests = [[dest_ref[k * TOKENS + base + t] for k in range(TOP_K)] for t in toks]
        for t, row in zip(toks, dests):
            for k, d in enumerate(row):
                pltpu.make_async_copy(_vmem_row(h_ref, t), xs_ref.at[d], sem).start(priority=k % 2)
            for k, d in enumerate(row):
                rs_ref[d + EXPERT_BM] = k * TOKENS + base + t
        return carry

    lax.fori_loop(0, tm // DISP_UNROLL, issue, 0)

    @pl.when(pl.program_id(0) == pl.num_programs(0) - 1)
    def _():
        zblk[...] = jnp.zeros((EXPERT_BM * ROW_WORDS, LANES), _U32)

        def pad_copy(r):
            return pltpu.make_async_copy(_vmem_row(zblk, 0), xs_ref.at[r], zsem)

        def tail_copy(b):
            return pltpu.make_async_copy(zblk.reshape(EXPERT_BM, ROW_WORDS, LANES),
                                         xs_ref.at[pl.ds(pl.multiple_of(b * EXPERT_BM, EXPERT_BM), EXPERT_BM)], zsem)

        first_unused = hi_ref[N_EXPERTS - 1] // EXPERT_BM
        for e in range(N_EXPERTS):
            lax.fori_loop(lo_ref[e] + EXPERT_BM, hi_ref[e] + EXPERT_BM, set_dump, 0)
        lax.fori_loop((first_unused + 1) * EXPERT_BM, N_ROWS + EXPERT_BM, set_dump, 0)
        for e in range(N_EXPERTS):
            lax.fori_loop(lo_ref[e], hi_ref[e], lambda r, c: (pad_copy(r).start(), c)[1], 0)
        lax.fori_loop(first_unused, N_BLOCKS, lambda b, c: (tail_copy(b).start(), c)[1], 0)
        for e in range(N_EXPERTS):
            lax.fori_loop(lo_ref[e], hi_ref[e], lambda r, c: (pad_copy(r).wait(), c)[1], 0)
        lax.fori_loop(first_unused, N_BLOCKS, lambda b, c: (tail_copy(b).wait(), c)[1], 0)

    for _ in range(TOP_K):
        pltpu.make_async_copy(h_ref.reshape(tm, ROW_WORDS, LANES), xs_ref.at[pl.ds(0, tm)], sem).wait()


def _dispatch(dest, pad_lo, pad_hi, h_rows):
    tm = DISP_TM
    return pl.pallas_call(
        _dispatch_kernel,
        grid_spec=pltpu.PrefetchScalarGridSpec(
            num_scalar_prefetch=3,
            grid=(TOKENS // tm,),
            in_specs=[pl.BlockSpec((tm * ROW_WORDS, LANES), lambda i, *_: (i, 0))],
            out_specs=[pl.BlockSpec(memory_space=pl.ANY), pl.BlockSpec(memory_space=pltpu.SMEM)],
            scratch_shapes=[pltpu.VMEM((EXPERT_BM * ROW_WORDS, LANES), _U32),
                            pltpu.SemaphoreType.DMA, pltpu.SemaphoreType.DMA],
        ),
        out_shape=[jax.ShapeDtypeStruct((N_ROWS, ROW_WORDS, LANES), _U32),
                   jax.ShapeDtypeStruct((N_ROWS + EXPERT_BM,), jnp.int32)],
        compiler_params=pltpu.CompilerParams(
            dimension_semantics=("arbitrary",), vmem_limit_bytes=VMEM_LIMIT),
        name="dispatch",
    )(dest, pad_lo, pad_hi, h_rows)


def _expert_kernel(be_ref, nu_ref, nx_ref, rs_ref, xs_ref, w1_hbm, b1_ref, w2_hbm, b2_ref, ys_ref,
                   w1f, w2f, w1s, w2s, obuf, wsem, osem, isem, nsw, *, layer):
    bm = EXPERT_BM
    b = pl.program_id(0)
    e = be_ref[b]
    n_used = nu_ref[0]
    half = MXU_COLS // 2

    def fetch(expert, slot):
        return (pltpu.make_async_copy(w1_hbm.at[layer, expert], w1f.at[slot], wsem.at[0, slot]),
                pltpu.make_async_copy(w2_hbm.at[layer, expert], w2f.at[slot], wsem.at[1, slot]))

    def scatter_start(step, slot):
        for i in range(bm):
            pltpu.make_async_copy(_vmem_row(obuf.at[slot], i), ys_ref.at[rs_ref[step * bm + i]],
                                  osem.at[slot]).start(priority=i % 2)

    def block_view(slot):
        return obuf.at[slot].reshape(bm, ROW_WORDS, LANES)

    def scatter_wait(slot):
        pltpu.make_async_copy(block_view(slot), ys_ref.at[pl.ds(0, bm)], osem.at[slot]).wait()

    @pl.when(b == 0)
    def _():
        nsw[0] = 0
        obuf[RING - 1] = jnp.zeros((bm * ROW_WORDS, LANES), _U32)
        for blk in range(PAD_DUMP_BLOCKS):
            pltpu.make_async_copy(block_view(RING - 1), ys_ref.at[pl.ds(PAD_DUMP0 + blk * bm, bm)], isem).start()
        for blk in range(PAD_DUMP_BLOCKS):
            pltpu.make_async_copy(block_view(RING - 1), ys_ref.at[pl.ds(PAD_DUMP0 + blk * bm, bm)], isem).wait()
        for virtual in range(2, RING + 1):
            pltpu.make_async_copy(
                block_view(RING - 1), ys_ref.at[pl.ds(TOKENS * TOP_K + (virtual - 1) * bm, bm)],
                osem.at[RING - virtual]).start()
        for copy in fetch(e, 0):
            copy.start()

    @pl.when(b == n_used)
    def _():
        scatter_start(b, (b + RING - 1) % RING)
        for slot in range(RING):
            scatter_wait(slot)

    @pl.when(b < n_used)
    def _():
        out_slot = b % RING
        scatter_wait(out_slot)

        @pl.when(jnp.logical_or(b == 0, be_ref[jnp.maximum(b - 1, 0)] != e))
        def _():
            slot = nsw[0] % 2
            for copy in fetch(e, slot):
                copy.wait()
            src = lax.broadcasted_iota(jnp.int32, (MXU_COLS, MXU_COLS), 0)
            dst = lax.broadcasted_iota(jnp.int32, (MXU_COLS, MXU_COLS), 1)
            want = jnp.where(dst < half, 2 * dst, 2 * (dst - half) + 1)
            perm = (src == want).astype(_BF16)
            for blk in range(2 * D_EXPERT // MXU_COLS):
                cs = slice(blk * MXU_COLS, (blk + 1) * MXU_COLS)
                w1s[:, cs] = jnp.dot(w1f[slot, :, cs].astype(_BF16), perm,
                                     preferred_element_type=_F32).astype(_BF16)
            w2s[...] = w2f[slot].astype(_BF16)
            nsw[0] = nsw[0] + 1
            nxt = nx_ref[e]

            @pl.when(nxt < N_EXPERTS)
            def _():
                for copy in fetch(nxt, 1 - slot):
                    copy.start()

        scatter_start(b, (b + RING - 1) % RING)
        x = _load_packed(xs_ref, (), bm, _BF16)
        u = jnp.dot(x, w1s[...], preferred_element_type=_F32) + b1_ref[0, e]
        acts = []
        for blk in range(2 * D_EXPERT // MXU_COLS):
            glu = jnp.minimum(u[:, blk * MXU_COLS:blk * MXU_COLS + half], SWIGLU_LIMIT)
            lin = jnp.clip(u[:, blk * MXU_COLS + half:(blk + 1) * MXU_COLS], -SWIGLU_LIMIT, SWIGLU_LIMIT)
            acts.append((glu * _sigmoid(SWIGLU_ALPHA * glu) * (lin + 1.0)).astype(_BF16))
        hidden = jnp.concatenate(acts, axis=1)
        y = jnp.dot(hidden, w2s[...], preferred_element_type=_F32) + b2_ref[0, e]
        _store_packed(obuf, (out_slot,), y)


def _experts(layer, block_e, n_used, next_used, row_slot, xs, w1, b1_grouped, w2, b2):
    bm = EXPERT_BM
    d, f = D_MODEL, D_EXPERT
    blk = lambda b, be, nu, nx, rs: jnp.minimum(b, nu[0] - 1)
    return pl.pallas_call(
        functools.partial(_expert_kernel, layer=layer),
        grid_spec=pltpu.PrefetchScalarGridSpec(
            num_scalar_prefetch=4,
            grid=(N_BLOCKS,),
            in_specs=[
                pl.BlockSpec((bm * ROW_WORDS, LANES), lambda *a: (blk(*a), 0)),
                pl.BlockSpec(memory_space=pl.ANY),
                pl.BlockSpec((1, N_EXPERTS, 1, 2 * f), lambda *a: (layer, 0, 0, 0)),
                pl.BlockSpec(memory_space=pl.ANY),
                pl.BlockSpec((1, N_EXPERTS, 1, d), lambda *a: (layer, 0, 0, 0)),
            ],
            out_specs=pl.BlockSpec(memory_space=pl.ANY),
            scratch_shapes=[
                pltpu.VMEM((2, d, 2 * f), _F32), pltpu.VMEM((2, f, d), _F32),
                pltpu.VMEM((d, 2 * f), _BF16), pltpu.VMEM((f, d), _BF16),
                pltpu.VMEM((RING, bm * ROW_WORDS, LANES), _U32),
                pltpu.SemaphoreType.DMA((2, 2)), pltpu.SemaphoreType.DMA((RING,)), pltpu.SemaphoreType.DMA,
                pltpu.SMEM((1,), jnp.int32),
            ],
        ),
        out_shape=jax.ShapeDtypeStruct((OUT_SLOTS, ROW_WORDS, LANES), _U32),
        compiler_params=pltpu.CompilerParams(
            dimension_semantics=("arbitrary",), vmem_limit_bytes=VMEM_LIMIT),
        name="experts",
    )(block_e, n_used, next_used, row_slot, xs.reshape(N_ROWS * ROW_WORDS, LANES), w1, b1_grouped, w2, b2)


def _group_glu_columns(b1):
    half = MXU_COLS // 2
    j = jnp.arange(MXU_COLS)
    want = jnp.where(j < half, 2 * j, 2 * (j - half) + 1)
    cols = (jnp.arange(2 * D_EXPERT // MXU_COLS)[:, None] * MXU_COLS + want[None, :]).reshape(-1)
    return b1[..., cols]


def _combine_kernel(y0_ref, y1_ref, y2_ref, y3_ref, x_ref, gate_ref, mod_ref, fg_ref, o_ref):
    out = _moe_residual((y0_ref, y1_ref, y2_ref, y3_ref), gate_ref, x_ref[...], mod_ref[0][5:6], COMB_TM)
    o_ref[...] = out * lax.rsqrt(jnp.mean(out * out, axis=-1, keepdims=True) + NORM_EPS) * fg_ref[...]


def _combine_final(ys, x1, gates, mod, final_g):
    tm = COMB_TM
    d = D_MODEL
    tiles = TOKENS // tm
    tiles_per_batch = SEQ // tm
    ys2 = ys.reshape(OUT_SLOTS * ROW_WORDS, LANES)
    slot_spec = lambda k: pl.BlockSpec((tm * ROW_WORDS, LANES), lambda i: (k * tiles + i, 0))
    return pl.pallas_call(
        _combine_kernel,
        grid=(tiles,),
        in_specs=[
            slot_spec(0), slot_spec(1), slot_spec(2), slot_spec(3),
            pl.BlockSpec((tm, d), lambda i: (i, 0)),
            pl.BlockSpec((tm, TOP_K), lambda i: (i, 0)),
            pl.BlockSpec((1, N_MOD, d), lambda i: (i // tiles_per_batch, 0, 0)),
            pl.BlockSpec((1, d), lambda i: (0, 0)),
        ],
        out_specs=pl.BlockSpec((tm, d), lambda i: (i, 0)),
        out_shape=jax.ShapeDtypeStruct((TOKENS, d), _F32),
        compiler_params=pltpu.CompilerParams(
            dimension_semantics=("arbitrary",), vmem_limit_bytes=VMEM_LIMIT),
        name="combine_final",
    )(ys2, ys2, ys2, ys2, x1, gates, mod, final_g.reshape(1, d))


def kernel(x, c, mod_w, mod_b, norm_g, conf_w1, conf_b1, conf_dw, conf_dw_b, conf_ln_g, conf_ln_b, conf_w2, conf_b2, sc_w_in, sc_conv, sc_w_out, router_w, router_b, exp_w1, exp_b1, exp_w2, exp_b2, final_g):
    mods = _modulation(c, mod_w, mod_b).reshape(DEPTH, BATCH, N_MOD, D_MODEL)
    b1_grouped = _group_glu_columns(exp_b1).reshape(DEPTH, N_EXPERTS, 1, 2 * D_EXPERT)
    b2_rows = exp_b2.reshape(DEPTH, N_EXPERTS, 1, D_MODEL)
    assert DEPTH == 2, "layer 0's MoE residual is applied inside layer 1's mixer kernel"
    pending = None
    for i in range(DEPTH):
        mod = mods[i]
        j = i // 2
        if i % 2 == 0:
            x1 = _conformer_mixer(x, mod, norm_g[i, 0], conf_w1[j], conf_b1[j], conf_dw[j], conf_dw_b[j],
                                  conf_ln_g[j], conf_ln_b[j], conf_w2[j], conf_b2[j])
        else:
            x1 = _shortconv_mixer(pending, mod, norm_g[i, 0], sc_w_in[j], sc_conv[j], sc_w_out[j])
        x1 = x1.reshape(TOKENS, D_MODEL)
        h_rows, idx_rank, gates, counts = _route(x1, mod, norm_g[i, 1], router_w[i], router_b[i])
        dest, block_e, n_used, next_used, pad_lo, pad_hi = _routing_tables(idx_rank, counts)
        xs, row_slot = _dispatch(dest, pad_lo, pad_hi, h_rows)
        ys = _experts(i, block_e, n_used, next_used, row_slot, xs, exp_w1, b1_grouped, exp_w2, b2_rows)
        pending = (ys, x1, gates, mod)
    out = _combine_final(*pending, final_g)
    return out.reshape(BATCH, SEQ, D_MODEL)
```

```python
import functools

import jax
import jax.numpy as jnp
from jax import lax
from jax.experimental import pallas as pl
from jax.experimental.pallas import tpu as pltpu

D_MODEL = 1024
BATCH = 8
SEQ = 2048
DEPTH = 2
TOKENS = BATCH * SEQ
CONF_KERNEL = 31
SHORT_KERNEL = 3
N_EXPERTS = 32
TOP_K = 4
D_EXPERT = D_MODEL
SWIGLU_ALPHA = 1.702
SWIGLU_LIMIT = 7.0
NORM_EPS = 1e-5
N_MOD = 6

LANES = 128
SUBLANES = 8
MXU_COLS = 256
VMEM_LIMIT = 56 * 1024 * 1024

HALF_D = D_MODEL // 2
ROW_WORDS = HALF_D // LANES
MIX_TS = 256
CONF_HALO = 32
SHORT_HALO = 8
CONV_RC = 64
ROUTE_TM = 512
DISP_TM = 2048
DISP_UNROLL = 16
COMB_TM = 256
EXPERT_BM = 256
N_ROWS = TOKENS * TOP_K + N_EXPERTS * EXPERT_BM
N_BLOCKS = N_ROWS // EXPERT_BM
RING = 3
PAD_DUMP0 = TOKENS * TOP_K + RING * EXPERT_BM
PAD_DUMP_BLOCKS = 4
OUT_SLOTS = PAD_DUMP0 + PAD_DUMP_BLOCKS * EXPERT_BM

_F32 = jnp.float32
_BF16 = jnp.bfloat16
_U32 = jnp.uint32
_HI_MASK = 0xFFFF0000


def _rms_mod(x, g, scale, shift):
    y = x * lax.rsqrt(jnp.mean(x * x, axis=-1, keepdims=True) + NORM_EPS)
    return (y * g) * (1.0 + scale) + shift


def _sigmoid(x):
    return 1.0 / (1.0 + jnp.exp(-x))


def _pack_rows(v):
    bits = lambda a: lax.bitcast_convert_type(a.astype(_BF16).astype(_F32), _U32)
    return (bits(v[:, HALF_D:]) & _U32(_HI_MASK)) | (bits(v[:, :HALF_D]) >> 16)


def _store_packed(ref, lead, v):
    rows = v.shape[0]
    words = _pack_rows(v)
    for j in range(ROW_WORDS):
        ref[lead + (pl.ds(j, rows, stride=ROW_WORDS), slice(None))] = words[:, j * LANES:(j + 1) * LANES]


def _load_packed(ref, lead, rows, dtype):
    lo, hi = [], []
    for j in range(ROW_WORDS):
        w = ref[lead + (pl.ds(j, rows, stride=ROW_WORDS), slice(None))]
        lo.append(lax.bitcast_convert_type(w << 16, _F32).astype(dtype))
        hi.append(lax.bitcast_convert_type(w & _U32(_HI_MASK), _F32).astype(dtype))
    return jnp.concatenate(lo + hi, axis=1)


def _mod_kernel(c_ref, w_ref, b_ref, o_ref):
    c = c_ref[...]
    c_act = c * _sigmoid(c)
    o_ref[0] = jnp.dot(c_act, w_ref[0], preferred_element_type=_F32,
                       precision=lax.Precision.HIGHEST) + b_ref[0]


def _modulation(c, mod_w, mod_b):
    tn = 1536
    n = N_MOD * D_MODEL
    return pl.pallas_call(
        _mod_kernel,
        grid=(DEPTH, n // tn),
        in_specs=[
            pl.BlockSpec((BATCH, D_MODEL), lambda i, j: (0, 0)),
            pl.BlockSpec((1, D_MODEL, tn), lambda i, j: (i, 0, j)),
            pl.BlockSpec((1, 1, tn), lambda i, j: (i, 0, j)),
        ],
        out_specs=pl.BlockSpec((1, BATCH, tn), lambda i, j: (i, 0, j)),
        out_shape=jax.ShapeDtypeStruct((DEPTH, BATCH, n), _F32),
        compiler_params=pltpu.CompilerParams(vmem_limit_bytes=VMEM_LIMIT),
        name="modulation",
    )(c, mod_w, mod_b.reshape(DEPTH, 1, n))


def _causal_taps(cbuf, shifted, w_ref, bias_row, out_ref, ntaps, halo, ts):
    off0 = halo - (ntaps - 1)
    for c in range(D_MODEL // LANES):
        cs = slice(c * LANES, (c + 1) * LANES)
        if shifted is not None:
            rows = cbuf.shape[0]
            whole = cbuf[:, cs]
            for r in range(1, SUBLANES):
                shifted[r - 1, :, cs] = pltpu.roll(whole, rows - r, axis=0)
        for rc in range(ts // CONV_RC):
            r0 = rc * CONV_RC
            acc = None
            for k in range(ntaps):
                s = off0 + k
                if shifted is None or s % SUBLANES == 0:
                    window = cbuf[r0 + s:r0 + s + CONV_RC, cs]
                else:
                    a = s - s % SUBLANES
                    window = shifted[s % SUBLANES - 1, r0 + a:r0 + a + CONV_RC, cs]
                term = w_ref[k:k + 1, cs] * window
                acc = term if acc is None else acc + term
            if bias_row is not None:
                acc = acc + bias_row[:, cs]
            out_ref[r0:r0 + CONV_RC, cs] = acc


def _conformer_kernel(x_ref, mod_ref, ng_ref, w1_ref, b1_ref, dw_ref, dwb_ref, lng_ref, lnb_ref,
                      w2_ref, b2_ref, o_ref, cbuf, shifted, vbuf):
    ts = MIX_TS

    @pl.when(pl.program_id(1) == 0)
    def _():
        cbuf[0:CONF_HALO, :] = jnp.zeros((CONF_HALO, D_MODEL), _F32)
        cbuf[CONF_HALO + ts:, :] = jnp.zeros((SUBLANES, D_MODEL), _F32)

    x = x_ref[0]
    m = mod_ref[0]
    h = _rms_mod(x, ng_ref[...], m[1:2], m[0:1])
    u = jnp.dot(h.astype(_BF16), w1_ref[...], preferred_element_type=_F32) + b1_ref[...]
    cbuf[CONF_HALO:CONF_HALO + ts, :] = u[:, :D_MODEL] * _sigmoid(u[:, D_MODEL:])
    _causal_taps(cbuf, shifted, dw_ref, dwb_ref[...], vbuf, CONF_KERNEL, CONF_HALO, ts)
    cbuf[0:CONF_HALO, :] = cbuf[ts:ts + CONF_HALO, :]
    v = vbuf[...]
    mu = jnp.mean(v, axis=-1, keepdims=True)
    vc = v - mu
    var = jnp.mean(vc * vc, axis=-1, keepdims=True)
    y = vc * lax.rsqrt(var + NORM_EPS) * lng_ref[...] + lnb_ref[...]
    y = y * _sigmoid(y)
    mix = jnp.dot(y.astype(_BF16), w2_ref[...], preferred_element_type=_F32) + b2_ref[...]
    o_ref[0] = x + m[2:3] * mix


def _conformer_mixer(x, mod, norm_g, w1, b1, dw, dw_b, ln_g, ln_b, w2, b2):
    d = D_MODEL
    row = lambda a: a.reshape(1, -1)
    const = lambda shape: pl.BlockSpec(shape, lambda b, s: (0,) * len(shape))
    return pl.pallas_call(
        _conformer_kernel,
        grid=(BATCH, SEQ // MIX_TS),
        in_specs=[
            pl.BlockSpec((1, MIX_TS, d), lambda b, s: (b, s, 0)),
            pl.BlockSpec((1, N_MOD, d), lambda b, s: (b, 0, 0)),
            const((1, d)), const((d, 2 * d)), const((1, 2 * d)), const((CONF_KERNEL, d)),
            const((1, d)), const((1, d)), const((1, d)), const((d, d)), const((1, d)),
        ],
        out_specs=pl.BlockSpec((1, MIX_TS, d), lambda b, s: (b, s, 0)),
        out_shape=jax.ShapeDtypeStruct((BATCH, SEQ, d), _F32),
        scratch_shapes=[pltpu.VMEM((CONF_HALO + MIX_TS + SUBLANES, d), _F32),
                        pltpu.VMEM((SUBLANES - 1, CONF_HALO + MIX_TS + SUBLANES, d), _F32),
                        pltpu.VMEM((MIX_TS, d), _F32)],
        compiler_params=pltpu.CompilerParams(
            dimension_semantics=("arbitrary", "arbitrary"), vmem_limit_bytes=VMEM_LIMIT),
        name="conformer_mixer",
    )(x, mod, row(norm_g), w1.astype(_BF16), row(b1), dw, row(dw_b), row(ln_g), row(ln_b),
      w2.astype(_BF16), row(b2))


def _moe_residual(y_refs, gate_ref, x1, gate2, rows):
    gates = gate_ref[...]
    y = None
    for k, yk_ref in enumerate(y_refs):
        term = gates[:, k:k + 1] * _load_packed(yk_ref, (), rows, _F32)
        y = term if y is None else y + term
    return x1 + gate2 * y


def _shortconv_kernel(y0_ref, y1_ref, y2_ref, y3_ref, gate_ref, pmod_ref, x_ref, mod_ref, ng_ref, win_ref, cw_ref,
                      wout_ref, o_ref, cbuf, vbuf):
    ts = MIX_TS
    d = D_MODEL

    @pl.when(pl.program_id(1) == 0)
    def _():
        cbuf[0:SHORT_HALO, :] = jnp.zeros((SHORT_HALO, d), _F32)

    x = _moe_residual((y0_ref, y1_ref, y2_ref, y3_ref), gate_ref, x_ref[0], pmod_ref[0][5:6], ts)
    m = mod_ref[0]
    h = _rms_mod(x, ng_ref[...], m[1:2], m[0:1])
    z = jnp.dot(h.astype(_BF16), win_ref[...], preferred_element_type=_F32)
    cbuf[SHORT_HALO:SHORT_HALO + ts, :] = z[:, d:2 * d] * z[:, 2 * d:]
    _causal_taps(cbuf, None, cw_ref, None, vbuf, SHORT_KERNEL, SHORT_HALO, ts)
    cbuf[0:SHORT_HALO, :] = cbuf[ts:ts + SHORT_HALO, :]
    y = z[:, :d] * vbuf[...]
    mix = jnp.dot(y.astype(_BF16), wout_ref[...], preferred_element_type=_F32)
    o_ref[0] = x + m[2:3] * mix


def _shortconv_mixer(pending, mod, norm_g, w_in, conv_w, w_out):
    ys, x_prev, gates, prev_mod = pending
    d = D_MODEL
    tiles_per_batch = SEQ // MIX_TS
    tiles = TOKENS // MIX_TS
    ys2 = ys.reshape(OUT_SLOTS * ROW_WORDS, LANES)
    const = lambda shape: pl.BlockSpec(shape, lambda b, s: (0,) * len(shape))
    slot_spec = lambda k: pl.BlockSpec((MIX_TS * ROW_WORDS, LANES),
                                       lambda b, s: (k * tiles + b * tiles_per_batch + s, 0))
    return pl.pallas_call(
        _shortconv_kernel,
        grid=(BATCH, SEQ // MIX_TS),
        in_specs=[
            slot_spec(0), slot_spec(1), slot_spec(2), slot_spec(3),
            pl.BlockSpec((MIX_TS, TOP_K), lambda b, s: (b * tiles_per_batch + s, 0)),
            pl.BlockSpec((1, N_MOD, d), lambda b, s: (b, 0, 0)),
            pl.BlockSpec((1, MIX_TS, d), lambda b, s: (b, s, 0)),
            pl.BlockSpec((1, N_MOD, d), lambda b, s: (b, 0, 0)),
            const((1, d)), const((d, 3 * d)), const((SHORT_KERNEL, d)), const((d, d)),
        ],
        out_specs=pl.BlockSpec((1, MIX_TS, d), lambda b, s: (b, s, 0)),
        out_shape=jax.ShapeDtypeStruct((BATCH, SEQ, d), _F32),
        scratch_shapes=[pltpu.VMEM((SHORT_HALO + MIX_TS, d), _F32), pltpu.VMEM((MIX_TS, d), _F32)],
        compiler_params=pltpu.CompilerParams(
            dimension_semantics=("arbitrary", "arbitrary"), vmem_limit_bytes=VMEM_LIMIT),
        name="shortconv_mixer",
    )(ys2, ys2, ys2, ys2, gates, prev_mod, x_prev.reshape(BATCH, SEQ, d), mod, norm_g.reshape(1, d),
      w_in.astype(_BF16), conv_w, w_out.astype(_BF16))


def _route_kernel(x_ref, mod_ref, ng_ref, rw_ref, rb_ref, h_ref, ir_ref, gate_ref, cnt_ref):
    tm = ROUTE_TM

    @pl.when(pl.program_id(0) == 0)
    def _():
        cnt_ref[...] = jnp.zeros((1, N_EXPERTS), _F32)

    x = x_ref[...]
    m = mod_ref[0]
    h = _rms_mod(x, ng_ref[...], m[4:5], m[3:4])
    _store_packed(h_ref, (), h)

    h_hi = h.astype(_BF16)
    h_lo = (h - h_hi.astype(_F32)).astype(_BF16)
    both = jnp.dot(h_hi, rw_ref[...], preferred_element_type=_F32)
    logits = (both[:, :N_EXPERTS] + both[:, N_EXPERTS:]
              + jnp.dot(h_lo, rw_ref[:, :N_EXPERTS], preferred_element_type=_F32)) + rb_ref[...]
    lane = lax.broadcasted_iota(jnp.int32, (tm, N_EXPERTS), 1).astype(_F32)
    work = logits
    vals, idxs, hots = [], [], []
    for _ in range(TOP_K):
        top = jnp.max(work, axis=1, keepdims=True)
        pick = jnp.min(jnp.where(work == top, lane, float(N_EXPERTS)), axis=1, keepdims=True)
        hot = lane == pick
        vals.append(top)
        idxs.append(pick)
        hots.append(hot)
        work = jnp.where(hot, -jnp.inf, work)
    exps = [jnp.exp(v - vals[0]) for v in vals]
    denom = exps[0] + exps[1] + exps[2] + exps[3]

    multi = (hots[0] | hots[1] | hots[2] | hots[3]).astype(_BF16)
    r_i = lax.broadcasted_iota(jnp.int32, (tm, tm), 0)
    c_i = lax.broadcasted_iota(jnp.int32, (tm, tm), 1)
    before = (c_i < r_i).astype(_BF16)
    pos = cnt_ref[...] + jnp.dot(before, multi, preferred_element_type=_F32)
    cnt_ref[...] = cnt_ref[...] + jnp.sum(multi.astype(_F32), axis=0, keepdims=True)

    col = lax.broadcasted_iota(jnp.int32, (tm, LANES), 1)
    gate_col = lax.broadcasted_iota(jnp.int32, (tm, TOP_K), 1)
    ints = jnp.zeros((tm, LANES), _F32)
    gate_out = jnp.zeros((tm, TOP_K), _F32)
    for k in range(TOP_K):
        rank_k = jnp.sum(jnp.where(hots[k], pos, 0.0), axis=1, keepdims=True)
        ints = jnp.where(col == k, idxs[k], ints)
        ints = jnp.where(col == TOP_K + k, rank_k, ints)
        gate_out = jnp.where(gate_col == k, exps[k] / denom, gate_out)
    sel = (lax.broadcasted_iota(jnp.int32, (2 * TOP_K, LANES), 0)
           == lax.broadcasted_iota(jnp.int32, (2 * TOP_K, LANES), 1)).astype(_BF16)
    high = jnp.floor(ints * (1.0 / LANES))
    low = ints - high * LANES
    nt = (((1,), (1,)), ((), ()))
    rows = (lax.dot_general(sel, high.astype(_BF16), nt, preferred_element_type=_F32) * LANES
            + lax.dot_general(sel, low.astype(_BF16), nt, preferred_element_type=_F32))
    ir_ref[...] = rows.astype(jnp.int32)
    gate_ref[...] = gate_out


def _route(x1, mod, norm_g, router_w, router_b):
    d = D_MODEL
    tm = ROUTE_TM
    tiles_per_batch = SEQ // tm
    rw_hi = router_w.astype(_BF16)
    rw_lo = (router_w - rw_hi.astype(_F32)).astype(_BF16)
    return pl.pallas_call(
        _route_kernel,
        grid=(TOKENS // tm,),
        in_specs=[
            pl.BlockSpec((tm, d), lambda i: (i, 0)),
            pl.BlockSpec((1, N_MOD, d), lambda i: (i // tiles_per_batch, 0, 0)),
            pl.BlockSpec((1, d), lambda i: (0, 0)),
            pl.BlockSpec((d, 2 * N_EXPERTS), lambda i: (0, 0)),
            pl.BlockSpec((1, N_EXPERTS), lambda i: (0, 0)),
        ],
        out_specs=[
            pl.BlockSpec((tm * ROW_WORDS, LANES), lambda i: (i, 0)),
            pl.BlockSpec((2 * TOP_K, tm), lambda i: (0, i)),
            pl.BlockSpec((tm, TOP_K), lambda i: (i, 0)),
            pl.BlockSpec((1, N_EXPERTS), lambda i: (0, 0)),
        ],
        out_shape=[
            jax.ShapeDtypeStruct((TOKENS * ROW_WORDS, LANES), _U32),
            jax.ShapeDtypeStruct((2 * TOP_K, TOKENS), jnp.int32),
            jax.ShapeDtypeStruct((TOKENS, TOP_K), _F32),
            jax.ShapeDtypeStruct((1, N_EXPERTS), _F32),
        ],
        compiler_params=pltpu.CompilerParams(
            dimension_semantics=("arbitrary",), vmem_limit_bytes=VMEM_LIMIT),
        name="route",
    )(x1, mod, norm_g.reshape(1, d), jnp.concatenate([rw_hi, rw_lo], axis=1), router_b.reshape(1, N_EXPERTS))


def _dest_kernel(ps_ref, ir_ref, o_ref):
    idx = ir_ref[0:TOP_K, :]
    dest = ir_ref[TOP_K:2 * TOP_K, :]
    for e in range(N_EXPERTS):
        dest = dest + jnp.where(idx == e, ps_ref[e], 0)
    o_ref[...] = dest


def _dest_rows(pad_starts, idx_rank):
    tn = 4096
    return pl.pallas_call(
        _dest_kernel,
        grid_spec=pltpu.PrefetchScalarGridSpec(
            num_scalar_prefetch=1,
            grid=(TOKENS // tn,),
            in_specs=[pl.BlockSpec((2 * TOP_K, tn), lambda i, ps: (0, i))],
            out_specs=pl.BlockSpec((TOP_K, tn), lambda i, ps: (0, i)),
        ),
        out_shape=jax.ShapeDtypeStruct((TOP_K, TOKENS), jnp.int32),
        name="dest_rows",
    )(pad_starts, idx_rank)


def _routing_tables(idx_rank, counts):
    counts = counts.reshape(N_EXPERTS).astype(jnp.int32)
    padded = (counts + EXPERT_BM - 1) // EXPERT_BM * EXPERT_BM
    pad_ends = jnp.cumsum(padded)
    pad_starts = pad_ends - padded
    dest = _dest_rows(pad_starts.astype(jnp.int32), idx_rank)
    block_start = jnp.arange(N_BLOCKS, dtype=jnp.int32) * EXPERT_BM
    block_e = jnp.sum((block_start[:, None] >= pad_ends[None, :]).astype(jnp.int32), axis=1)
    block_e = jnp.minimum(block_e, N_EXPERTS - 1).astype(jnp.int32)
    n_used = (pad_ends[-1:] // EXPERT_BM).astype(jnp.int32)
    ids = jnp.arange(N_EXPERTS, dtype=jnp.int32)
    later_used = (ids[None, :] > ids[:, None]) & (counts[None, :] > 0)
    next_used = jnp.min(jnp.where(later_used, ids[None, :], N_EXPERTS), axis=1).astype(jnp.int32)
    pad_lo = (pad_starts + counts).astype(jnp.int32)
    dest = dest.reshape(TOKENS * TOP_K).astype(jnp.int32)
    return dest, block_e, n_used, next_used, pad_lo, pad_ends.astype(jnp.int32)


def _vmem_row(buf, row):
    start = row * ROW_WORDS
    return buf.at[pl.ds(start if isinstance(row, int) else pl.multiple_of(start, ROW_WORDS), ROW_WORDS)]


def _dispatch_kernel(dest_ref, lo_ref, hi_ref, h_ref, xs_ref, rs_ref, zblk, sem, zsem):
    tm = DISP_TM
    base = pl.program_id(0) * tm

    def set_virtual(r, carry):
        rs_ref[r] = TOKENS * TOP_K + r
        return carry

    def set_dump(r, carry):
        rs_ref[r] = PAD_DUMP0 + (r & (PAD_DUMP_BLOCKS * EXPERT_BM - 1))
        return carry

    @pl.when(pl.program_id(0) == 0)
    def _():
        lax.fori_loop(0, EXPERT_BM, set_virtual, 0)

    def issue(g, carry):
        toks = [g * DISP_UNROLL + u for u in range(DISP_UNROLL)]
        dests = [[dest_ref[k * TOKENS + base + t] for k in range(TOP_K)] for t in toks]
        for t, row in zip(toks, dests):
            for k, d in enumerate(row):
                pltpu.make_async_copy(_vmem_row(h_ref, t), xs_ref.at[d], sem).start(priority=k % 2)
            for k, d in enumerate(row):
                rs_ref[d + EXPERT_BM] = k * TOKENS + base + t
        return carry

    lax.fori_loop(0, tm // DISP_UNROLL, issue, 0)

    @pl.when(pl.program_id(0) == pl.num_programs(0) - 1)
    def _():
        zblk[...] = jnp.zeros((EXPERT_BM * ROW_WORDS, LANES), _U32)

        def pad_copy(r):
            return pltpu.make_async_copy(_vmem_row(zblk, 0), xs_ref.at[r], zsem)

        def tail_copy(b):
            return pltpu.make_async_copy(zblk.reshape(EXPERT_BM, ROW_WORDS, LANES),
                                         xs_ref.at[pl.ds(pl.multiple_of(b * EXPERT_BM, EXPERT_BM), EXPERT_BM)], zsem)

        first_unused = hi_ref[N_EXPERTS - 1] // EXPERT_BM
        for e in range(N_EXPERTS):
            lax.fori_loop(lo_ref[e] + EXPERT_BM, hi_ref[e] + EXPERT_BM, set_dump, 0)
        lax.fori_loop((first_unused + 1) * EXPERT_BM, N_ROWS + EXPERT_BM, set_dump, 0)
        for e in range(N_EXPERTS):
            lax.fori_loop(lo_ref[e], hi_ref[e], lambda r, c: (pad_copy(r).start(), c)[1], 0)
        lax.fori_loop(first_unused, N_BLOCKS, lambda b, c: (tail_copy(b).start(), c)[1], 0)
        for e in range(N_EXPERTS):
            lax.fori_loop(lo_ref[e], hi_ref[e], lambda r, c: (pad_copy(r).wait(), c)[1], 0)
        lax.fori_loop(first_unused, N_BLOCKS, lambda b, c: (tail_copy(b).wait(), c)[1], 0)

    for _ in range(TOP_K):
        pltpu.make_async_copy(h_ref.reshape(tm, ROW_WORDS, LANES), xs_ref.at[pl.ds(0, tm)], sem).wait()


def _dispatch(dest, pad_lo, pad_hi, h_rows):
    tm = DISP_TM
    return pl.pallas_call(
        _dispatch_kernel,
        grid_spec=pltpu.PrefetchScalarGridSpec(
            num_scalar_prefetch=3,
            grid=(TOKENS // tm,),
            in_specs=[pl.BlockSpec((tm * ROW_WORDS, LANES), lambda i, *_: (i, 0))],
            out_specs=[pl.BlockSpec(memory_space=pl.ANY), pl.BlockSpec(memory_space=pltpu.SMEM)],
            scratch_shapes=[pltpu.VMEM((EXPERT_BM * ROW_WORDS, LANES), _U32),
                            pltpu.SemaphoreType.DMA, pltpu.SemaphoreType.DMA],
        ),
        out_shape=[jax.ShapeDtypeStruct((N_ROWS, ROW_WORDS, LANES), _U32),
                   jax.ShapeDtypeStruct((N_ROWS + EXPERT_BM,), jnp.int32)],
        compiler_params=pltpu.CompilerParams(
            dimension_semantics=("arbitrary",), vmem_limit_bytes=VMEM_LIMIT),
        name="dispatch",
    )(dest, pad_lo, pad_hi, h_rows)


def _expert_kernel(be_ref, nu_ref, nx_ref, rs_ref, xs_ref, w1_hbm, b1_ref, w2_hbm, b2_ref, ys_ref,
                   w1f, w2f, w1s, w2s, obuf, wsem, osem, isem, nsw, *, layer):
    bm = EXPERT_BM
    b = pl.program_id(0)
    e = be_ref[b]
    n_used = nu_ref[0]
    half = MXU_COLS // 2

    def fetch(expert, slot):
        return (pltpu.make_async_copy(w1_hbm.at[layer, expert], w1f.at[slot], wsem.at[0, slot]),
                pltpu.make_async_copy(w2_hbm.at[layer, expert], w2f.at[slot], wsem.at[1, slot]))

    def scatter_start(step, slot):
        for i in range(bm):
            pltpu.make_async_copy(_vmem_row(obuf.at[slot], i), ys_ref.at[rs_ref[step * bm + i]],
                                  osem.at[slot]).start(priority=i % 2)

    def block_view(slot):
        return obuf.at[slot].reshape(bm, ROW_WORDS, LANES)

    def scatter_wait(slot):
        pltpu.make_async_copy(block_view(slot), ys_ref.at[pl.ds(0, bm)], osem.at[slot]).wait()

    @pl.when(b == 0)
    def _():
        nsw[0] = 0
        obuf[RING - 1] = jnp.zeros((bm * ROW_WORDS, LANES), _U32)
        for blk in range(PAD_DUMP_BLOCKS):
            pltpu.make_async_copy(block_view(RING - 1), ys_ref.at[pl.ds(PAD_DUMP0 + blk * bm, bm)], isem).start()
        for blk in range(PAD_DUMP_BLOCKS):
            pltpu.make_async_copy(block_view(RING - 1), ys_ref.at[pl.ds(PAD_DUMP0 + blk * bm, bm)], isem).wait()
        for virtual in range(2, RING + 1):
            pltpu.make_async_copy(
                block_view(RING - 1), ys_ref.at[pl.ds(TOKENS * TOP_K + (virtual - 1) * bm, bm)],
                osem.at[RING - virtual]).start()
        for copy in fetch(e, 0):
            copy.start()

    @pl.when(b == n_used)
    def _():
        scatter_start(b, (b + RING - 1) % RING)
        for slot in range(RING):
            scatter_wait(slot)

    @pl.when(b < n_used)
    def _():
        out_slot = b % RING
        scatter_wait(out_slot)

        @pl.when(jnp.logical_or(b == 0, be_ref[jnp.maximum(b - 1, 0)] != e))
        def _():
            slot = nsw[0] % 2
            for copy in fetch(e, slot):
                copy.wait()
            src = lax.broadcasted_iota(jnp.int32, (MXU_COLS, MXU_COLS), 0)
            dst = lax.broadcasted_iota(jnp.int32, (MXU_COLS, MXU_COLS), 1)
            want = jnp.where(dst < half, 2 * dst, 2 * (dst - half) + 1)
            perm = (src == want).astype(_BF16)
            for blk in range(2 * D_EXPERT // MXU_COLS):
                cs = slice(blk * MXU_COLS, (blk + 1) * MXU_COLS)
                w1s[:, cs] = jnp.dot(w1f[slot, :, cs].astype(_BF16), perm,
                                     preferred_element_type=_F32).astype(_BF16)
            w2s[...] = w2f[slot].astype(_BF16)
            nsw[0] = nsw[0] + 1
            nxt = nx_ref[e]

            @pl.when(nxt < N_EXPERTS)
            def _():
                for copy in fetch(nxt, 1 - slot):
                    copy.start()

        scatter_start(b, (b + RING - 1) % RING)
        x = _load_packed(xs_ref, (), bm, _BF16)
        u = jnp.dot(x, w1s[...], preferred_element_type=_F32) + b1_ref[0, e]
        acts = []
        for blk in range(2 * D_EXPERT // MXU_COLS):
            glu = jnp.minimum(u[:, blk * MXU_COLS:blk * MXU_COLS + half], SWIGLU_LIMIT)
            lin = jnp.clip(u[:, blk * MXU_COLS + half:(blk + 1) * MXU_COLS], -SWIGLU_LIMIT, SWIGLU_LIMIT)
            acts.append((glu * _sigmoid(SWIGLU_ALPHA * glu) * (lin + 1.0)).astype(_BF16))
        hidden = jnp.concatenate(acts, axis=1)
        y = jnp.dot(hidden, w2s[...], preferred_element_type=_F32) + b2_ref[0, e]
        _store_packed(obuf, (out_slot,), y)


def _experts(layer, block_e, n_used, next_used, row_slot, xs, w1, b1_grouped, w2, b2):
    bm = EXPERT_BM
    d, f = D_MODEL, D_EXPERT
    blk = lambda b, be, nu, nx, rs: jnp.minimum(b, nu[0] - 1)
    return pl.pallas_call(
        functools.partial(_expert_kernel, layer=layer),
        grid_spec=pltpu.PrefetchScalarGridSpec(
            num_scalar_prefetch=4,
            grid=(N_BLOCKS,),
            in_specs=[
                pl.BlockSpec((bm * ROW_WORDS, LANES), lambda *a: (blk(*a), 0)),
                pl.BlockSpec(memory_space=pl.ANY),
                pl.BlockSpec((1, N_EXPERTS, 1, 2 * f), lambda *a: (layer, 0, 0, 0)),
                pl.BlockSpec(memory_space=pl.ANY),
                pl.BlockSpec((1, N_EXPERTS, 1, d), lambda *a: (layer, 0, 0, 0)),
            ],
            out_specs=pl.BlockSpec(memory_space=pl.ANY),
            scratch_shapes=[
                pltpu.VMEM((2, d, 2 * f), _F32), pltpu.VMEM((2, f, d), _F32),
                pltpu.VMEM((d, 2 * f), _BF16), pltpu.VMEM((f, d), _BF16),
                pltpu.VMEM((RING, bm * ROW_WORDS, LANES), _U32),
                pltpu.SemaphoreType.DMA((2, 2)), pltpu.SemaphoreType.DMA((RING,)), pltpu.SemaphoreType.DMA,
                pltpu.SMEM((1,), jnp.int32),
            ],
        ),
        out_shape=jax.ShapeDtypeStruct((OUT_SLOTS, ROW_WORDS, LANES), _U32),
        compiler_params=pltpu.CompilerParams(
            dimension_semantics=("arbitrary",), vmem_limit_bytes=VMEM_LIMIT),
        name="experts",
    )(block_e, n_used, next_used, row_slot, xs.reshape(N_ROWS * ROW_WORDS, LANES), w1, b1_grouped, w2, b2)


def _group_glu_columns(b1):
    half = MXU_COLS // 2
    j = jnp.arange(MXU_COLS)
    want = jnp.where(j < half, 2 * j, 2 * (j - half) + 1)
    cols = (jnp.arange(2 * D_EXPERT // MXU_COLS)[:, None] * MXU_COLS + want[None, :]).reshape(-1)
    return b1[..., cols]


def _combine_kernel(y0_ref, y1_ref, y2_ref, y3_ref, x_ref, gate_ref, mod_ref, fg_ref, o_ref):
    out = _moe_residual((y0_ref, y1_ref, y2_ref, y3_ref), gate_ref, x_ref[...], mod_ref[0][5:6], COMB_TM)
    o_ref[...] = out * lax.rsqrt(jnp.mean(out * out, axis=-1, keepdims=True) + NORM_EPS) * fg_ref[...]


def _combine_final(ys, x1, gates, mod, final_g):
    tm = COMB_TM
    d = D_MODEL
    tiles = TOKENS // tm
    tiles_per_batch = SEQ // tm
    ys2 = ys.reshape(OUT_SLOTS * ROW_WORDS, LANES)
    slot_spec = lambda k: pl.BlockSpec((tm * ROW_WORDS, LANES), lambda i: (k * tiles + i, 0))
    return pl.pallas_call(
        _combine_kernel,
        grid=(tiles,),
        in_specs=[
            slot_spec(0), slot_spec(1), slot_spec(2), slot_spec(3),
            pl.BlockSpec((tm, d), lambda i: (i, 0)),
            pl.BlockSpec((tm, TOP_K), lambda i: (i, 0)),
            pl.BlockSpec((1, N_MOD, d), lambda i: (i // tiles_per_batch, 0, 0)),
            pl.BlockSpec((1, d), lambda i: (0, 0)),
        ],
        out_specs=pl.BlockSpec((tm, d), lambda i: (i, 0)),
        out_shape=jax.ShapeDtypeStruct((TOKENS, d), _F32),
        compiler_params=pltpu.CompilerParams(
            dimension_semantics=("arbitrary",), vmem_limit_bytes=VMEM_LIMIT),
        name="combine_final",
    )(ys2, ys2, ys2, ys2, x1, gates, mod, final_g.reshape(1, d))


def kernel(x, c, mod_w, mod_b, norm_g, conf_w1, conf_b1, conf_dw, conf_dw_b, conf_ln_g, conf_ln_b, conf_w2, conf_b2, sc_w_in, sc_conv, sc_w_out, router_w, router_b, exp_w1, exp_b1, exp_w2, exp_b2, final_g):
    mods = _modulation(c, mod_w, mod_b).reshape(DEPTH, BATCH, N_MOD, D_MODEL)
    b1_grouped = _group_glu_columns(exp_b1).reshape(DEPTH, N_EXPERTS, 1, 2 * D_EXPERT)
    b2_rows = exp_b2.reshape(DEPTH, N_EXPERTS, 1, D_MODEL)
    assert DEPTH == 2, "layer 0's MoE residual is applied inside layer 1's mixer kernel"
    pending = None
    for i in range(DEPTH):
        mod = mods[i]
        j = i // 2
        if i % 2 == 0:
            x1 = _conformer_mixer(x, mod, norm_g[i, 0], conf_w1[j], conf_b1[j], conf_dw[j], conf_dw_b[j],
                                  conf_ln_g[j], conf_ln_b[j], conf_w2[j], conf_b2[j])
        else:
            x1 = _shortconv_mixer(pending, mod, norm_g[i, 0], sc_w_in[j], sc_conv[j], sc_w_out[j])
        x1 = x1.reshape(TOKENS, D_MODEL)
        h_rows, idx_rank, gates, counts = _route(x1, mod, norm_g[i, 1], router_w[i], router_b[i])
        dest, block_e, n_used, next_used, pad_lo, pad_hi = _routing_tables(idx_rank, counts)
        xs, row_slot = _dispatch(dest, pad_lo, pad_hi, h_rows)
        ys = _experts(i, block_e, n_used, next_used, row_slot, xs, exp_w1, b1_grouped, exp_w2, b2_rows)
        pending = (ys, x1, gates, mod)
    out = _combine_final(*pending, final_g)
    return out.reshape(BATCH, SEQ, D_MODEL)
```

```python
import functools

import jax
import jax.numpy as jnp
from jax import lax
from jax.experimental import pallas as pl
from jax.experimental.pallas import tpu as pltpu

D_MODEL = 1024
BATCH = 8
SEQ = 2048
DEPTH = 2
TOKENS = BATCH * SEQ
CONF_KERNEL = 31
SHORT_KERNEL = 3
N_EXPERTS = 32
TOP_K = 4
D_EXPERT = D_MODEL
SWIGLU_ALPHA = 1.702
SWIGLU_LIMIT = 7.0
NORM_EPS = 1e-5
N_MOD = 6

LANES = 128
SUBLANES = 8
MXU_COLS = 256
VMEM_LIMIT = 56 * 1024 * 1024

HALF_D = D_MODEL // 2
ROW_WORDS = HALF_D // LANES
MIX_TS = 256
CONF_HALO = 32
SHORT_HALO = 8
CONV_RC = 64
ROUTE_TM = 512
DISP_TM = 2048
DISP_UNROLL = 16
COMB_TM = 256
EXPERT_BM = 256
N_ROWS = TOKENS * TOP_K + N_EXPERTS * EXPERT_BM
N_BLOCKS = N_ROWS // EXPERT_BM
RING = 3
PAD_DUMP0 = TOKENS * TOP_K + RING * EXPERT_BM
PAD_DUMP_BLOCKS = 4
OUT_SLOTS = PAD_DUMP0 + PAD_DUMP_BLOCKS * EXPERT_BM

_F32 = jnp.float32
_BF16 = jnp.bfloat16
_U32 = jnp.uint32
_HI_MASK = 0xFFFF0000


def _rms_mod(x, g, scale, shift):
    y = x * lax.rsqrt(jnp.mean(x * x, axis=-1, keepdims=True) + NORM_EPS)
    return (y * g) * (1.0 + scale) + shift


def _sigmoid(x):
    return 1.0 / (1.0 + jnp.exp(-x))


def _pack_rows(v):
    bits = lambda a: lax.bitcast_convert_type(a.astype(_BF16).astype(_F32), _U32)
    return (bits(v[:, HALF_D:]) & _U32(_HI_MASK)) | (bits(v[:, :HALF_D]) >> 16)


def _store_packed(ref, lead, v):
    rows = v.shape[0]
    words = _pack_rows(v)
    for j in range(ROW_WORDS):
        ref[lead + (pl.ds(j, rows, stride=ROW_WORDS), slice(None))] = words[:, j * LANES:(j + 1) * LANES]


def _load_packed(ref, lead, rows, dtype):
    lo, hi = [], []
    for j in range(ROW_WORDS):
        w = ref[lead + (pl.ds(j, rows, stride=ROW_WORDS), slice(None))]
        lo.append(lax.bitcast_convert_type(w << 16, _F32).astype(dtype))
        hi.append(lax.bitcast_convert_type(w & _U32(_HI_MASK), _F32).astype(dtype))
    return jnp.concatenate(lo + hi, axis=1)


def _mod_kernel(c_ref, w_ref, b_ref, o_ref):
    c = c_ref[...]
    c_act = c * _sigmoid(c)
    o_ref[0] = jnp.dot(c_act, w_ref[0], preferred_element_type=_F32,
                       precision=lax.Precision.HIGHEST) + b_ref[0]


def _modulation(c, mod_w, mod_b):
    tn = 1536
    n = N_MOD * D_MODEL
    return pl.pallas_call(
        _mod_kernel,
        grid=(DEPTH, n // tn),
        in_specs=[
            pl.BlockSpec((BATCH, D_MODEL), lambda i, j: (0, 0)),
            pl.BlockSpec((1, D_MODEL, tn), lambda i, j: (i, 0, j)),
            pl.BlockSpec((1, 1, tn), lambda i, j: (i, 0, j)),
        ],
        out_specs=pl.BlockSpec((1, BATCH, tn), lambda i, j: (i, 0, j)),
        out_shape=jax.ShapeDtypeStruct((DEPTH, BATCH, n), _F32),
        compiler_params=pltpu.CompilerParams(vmem_limit_bytes=VMEM_LIMIT),
        name="modulation",
    )(c, mod_w, mod_b.reshape(DEPTH, 1, n))


def _causal_taps(cbuf, shifted, w_ref, bias_row, out_ref, ntaps, halo, ts):
    off0 = halo - (ntaps - 1)
    for c in range(D_MODEL // LANES):
        cs = slice(c * LANES, (c + 1) * LANES)
        if shifted is not None:
            rows = cbuf.shape[0]
            whole = cbuf[:, cs]
            for r in range(1, SUBLANES):
                shifted[r - 1, :, cs] = pltpu.roll(whole, rows - r, axis=0)
        for rc in range(ts // CONV_RC):
            r0 = rc * CONV_RC
            acc = None
            for k in range(ntaps):
                s = off0 + k
                if shifted is None or s % SUBLANES == 0:
                    window = cbuf[r0 + s:r0 + s + CONV_RC, cs]
                else:
                    a = s - s % SUBLANES
                    window = shifted[s % SUBLANES - 1, r0 + a:r0 + a + CONV_RC, cs]
                term = w_ref[k:k + 1, cs] * window
                acc = term if acc is None else acc + term
            if bias_row is not None:
                acc = acc + bias_row[:, cs]
            out_ref[r0:r0 + CONV_RC, cs] = acc


def _conformer_kernel(x_ref, mod_ref, ng_ref, w1_ref, b1_ref, dw_ref, dwb_ref, lng_ref, lnb_ref,
                      w2_ref, b2_ref, o_ref, cbuf, shifted, vbuf):
    ts = MIX_TS

    @pl.when(pl.program_id(1) == 0)
    def _():
        cbuf[0:CONF_HALO, :] = jnp.zeros((CONF_HALO, D_MODEL), _F32)
        cbuf[CONF_HALO + ts:, :] = jnp.zeros((SUBLANES, D_MODEL), _F32)

    x = x_ref[0]
    m = mod_ref[0]
    h = _rms_mod(x, ng_ref[...], m[1:2], m[0:1])
    u = jnp.dot(h.astype(_BF16), w1_ref[...], preferred_element_type=_F32) + b1_ref[...]
    cbuf[CONF_HALO:CONF_HALO + ts, :] = u[:, :D_MODEL] * _sigmoid(u[:, D_MODEL:])
    _causal_taps(cbuf, shifted, dw_ref, dwb_ref[...], vbuf, CONF_KERNEL, CONF_HALO, ts)
    cbuf[0:CONF_HALO, :] = cbuf[ts:ts + CONF_HALO, :]
    v = vbuf[...]
    mu = jnp.mean(v, axis=-1, keepdims=True)
    vc = v - mu
    var = jnp.mean(vc * vc, axis=-1, keepdims=True)
    y = vc * lax.rsqrt(var + NORM_EPS) * lng_ref[...] + lnb_ref[...]
    y = y * _sigmoid(y)
    mix = jnp.dot(y.astype(_BF16), w2_ref[...], preferred_element_type=_F32) + b2_ref[...]
    o_ref[0] = x + m[2:3] * mix


def _conformer_mixer(x, mod, norm_g, w1, b1, dw, dw_b, ln_g, ln_b, w2, b2):
    d = D_MODEL
    row = lambda a: a.reshape(1, -1)
    const = lambda shape: pl.BlockSpec(shape, lambda b, s: (0,) * len(shape))
    return pl.pallas_call(
        _conformer_kernel,
        grid=(BATCH, SEQ // MIX_TS),
        in_specs=[
            pl.BlockSpec((1, MIX_TS, d), lambda b, s: (b, s, 0)),
            pl.BlockSpec((1, N_MOD, d), lambda b, s: (b, 0, 0)),
            const((1, d)), const((d, 2 * d)), const((1, 2 * d)), const((CONF_KERNEL, d)),
            const((1, d)), const((1, d)), const((1, d)), const((d, d)), const((1, d)),
        ],
        out_specs=pl.BlockSpec((1, MIX_TS, d), lambda b, s: (b, s, 0)),
        out_shape=jax.ShapeDtypeStruct((BATCH, SEQ, d), _F32),
        scratch_shapes=[pltpu.VMEM((CONF_HALO + MIX_TS + SUBLANES, d), _F32),
                        pltpu.VMEM((SUBLANES - 1, CONF_HALO + MIX_TS + SUBLANES, d), _F32),
                        pltpu.VMEM((MIX_TS, d), _F32)],
        compiler_params=pltpu.CompilerParams(
            dimension_semantics=("arbitrary", "arbitrary"), vmem_limit_bytes=VMEM_LIMIT),
        name="conformer_mixer",
    )(x, mod, row(norm_g), w1.astype(_BF16), row(b1), dw, row(dw_b), row(ln_g), row(ln_b),
      w2.astype(_BF16), row(b2))


def _moe_residual(y_refs, gate_ref, x1, gate2, rows):
    gates = gate_ref[...]
    y = None
    for k, yk_ref in enumerate(y_refs):
        term = gates[:, k:k + 1] * _load_packed(yk_ref, (), rows, _F32)
        y = term if y is None else y + term
    return x1 + gate2 * y


def _shortconv_kernel(y0_ref, y1_ref, y2_ref, y3_ref, gate_ref, pmod_ref, x_ref, mod_ref, ng_ref, win_ref, cw_ref,
                      wout_ref, o_ref, cbuf, vbuf):
    ts = MIX_TS
    d = D_MODEL

    @pl.when(pl.program_id(1) == 0)
    def _():
        cbuf[0:SHORT_HALO, :] = jnp.zeros((SHORT_HALO, d), _F32)

    x = _moe_residual((y0_ref, y1_ref, y2_ref, y3_ref), gate_ref, x_ref[0], pmod_ref[0][5:6], ts)
    m = mod_ref[0]
    h = _rms_mod(x, ng_ref[...], m[1:2], m[0:1])
    z = jnp.dot(h.astype(_BF16), win_ref[...], preferred_element_type=_F32)
    cbuf[SHORT_HALO:SHORT_HALO + ts, :] = z[:, d:2 * d] * z[:, 2 * d:]
    _causal_taps(cbuf, None, cw_ref, None, vbuf, SHORT_KERNEL, SHORT_HALO, ts)
    cbuf[0:SHORT_HALO, :] = cbuf[ts:ts + SHORT_HALO, :]
    y = z[:, :d] * vbuf[...]
    mix = jnp.dot(y.astype(_BF16), wout_ref[...], preferred_element_type=_F32)
    o_ref[0] = x + m[2:3] * mix


def _shortconv_mixer(pending, mod, norm_g, w_in, conv_w, w_out):
    ys, x_prev, gates, prev_mod = pending
    d = D_MODEL
    tiles_per_batch = SEQ // MIX_TS
    tiles = TOKENS // MIX_TS
    ys2 = ys.reshape(OUT_SLOTS * ROW_WORDS, LANES)
    const = lambda shape: pl.BlockSpec(shape, lambda b, s: (0,) * len(shape))
    slot_spec = lambda k: pl.BlockSpec((MIX_TS * ROW_WORDS, LANES),
                                       lambda b, s: (k * tiles + b * tiles_per_batch + s, 0))
    return pl.pallas_call(
        _shortconv_kernel,
        grid=(BATCH, SEQ // MIX_TS),
        in_specs=[
            slot_spec(0), slot_spec(1), slot_spec(2), slot_spec(3),
            pl.BlockSpec((MIX_TS, TOP_K), lambda b, s: (b * tiles_per_batch + s, 0)),
            pl.BlockSpec((1, N_MOD, d), lambda b, s: (b, 0, 0)),
            pl.BlockSpec((1, MIX_TS, d), lambda b, s: (b, s, 0)),
            pl.BlockSpec((1, N_MOD, d), lambda b, s: (b, 0, 0)),
            const((1, d)), const((d, 3 * d)), const((SHORT_KERNEL, d)), const((d, d)),
        ],
        out_specs=pl.BlockSpec((1, MIX_TS, d), lambda b, s: (b, s, 0)),
        out_shape=jax.ShapeDtypeStruct((BATCH, SEQ, d), _F32),
        scratch_shapes=[pltpu.VMEM((SHORT_HALO + MIX_TS, d), _F32), pltpu.VMEM((MIX_TS, d), _F32)],
        compiler_params=pltpu.CompilerParams(
            dimension_semantics=("arbitrary", "arbitrary"), vmem_limit_bytes=VMEM_LIMIT),
        name="shortconv_mixer",
    )(ys2, ys2, ys2, ys2, gates, prev_mod, x_prev.reshape(BATCH, SEQ, d), mod, norm_g.reshape(1, d),
      w_in.astype(_BF16), conv_w, w_out.astype(_BF16))


def _route_kernel(x_ref, mod_ref, ng_ref, rw_ref, rb_ref, h_ref, ir_ref, gate_ref, cnt_ref):
    tm = ROUTE_TM

    @pl.when(pl.program_id(0) == 0)
    def _():
        cnt_ref[...] = jnp.zeros((1, N_EXPERTS), _F32)

    x = x_ref[...]
    m = mod_ref[0]
    h = _rms_mod(x, ng_ref[...], m[4:5], m[3:4])
    _store_packed(h_ref, (), h)

    h_hi = h.astype(_BF16)
    h_lo = (h - h_hi.astype(_F32)).astype(_BF16)
    both = jnp.dot(h_hi, rw_ref[...], preferred_element_type=_F32)
    logits = (both[:, :N_EXPERTS] + both[:, N_EXPERTS:]
              + jnp.dot(h_lo, rw_ref[:, :N_EXPERTS], preferred_element_type=_F32)) + rb_ref[...]
    lane = lax.broadcasted_iota(jnp.int32, (tm, N_EXPERTS), 1).astype(_F32)
    work = logits
    vals, idxs, hots = [], [], []
    for _ in range(TOP_K):
        top = jnp.max(work, axis=1, keepdims=True)
        pick = jnp.min(jnp.where(work == top, lane, float(N_EXPERTS)), axis=1, keepdims=True)
        hot = lane == pick
        vals.append(top)
        idxs.append(pick)
        hots.append(hot)
        work = jnp.where(hot, -jnp.inf, work)
    exps = [jnp.exp(v - vals[0]) for v in vals]
    denom = exps[0] + exps[1] + exps[2] + exps[3]

    multi = (hots[0] | hots[1] | hots[2] | hots[3]).astype(_BF16)
    r_i = lax.broadcasted_iota(jnp.int32, (tm, tm), 0)
    c_i = lax.broadcasted_iota(jnp.int32, (tm, tm), 1)
    before = (c_i < r_i).astype(_BF16)
    pos = cnt_ref[...] + jnp.dot(before, multi, preferred_element_type=_F32)
    cnt_ref[...] = cnt_ref[...] + jnp.sum(multi.astype(_F32), axis=0, keepdims=True)

    col = lax.broadcasted_iota(jnp.int32, (tm, LANES), 1)
    gate_col = lax.broadcasted_iota(jnp.int32, (tm, TOP_K), 1)
    ints = jnp.zeros((tm, LANES), _F32)
    gate_out = jnp.zeros((tm, TOP_K), _F32)
    for k in range(TOP_K):
        rank_k = jnp.sum(jnp.where(hots[k], pos, 0.0), axis=1, keepdims=True)
        ints = jnp.where(col == k, idxs[k], ints)
        ints = jnp.where(col == TOP_K + k, rank_k, ints)
        gate_out = jnp.where(gate_col == k, exps[k] / denom, gate_out)
    sel = (lax.broadcasted_iota(jnp.int32, (2 * TOP_K, LANES), 0)
           == lax.broadcasted_iota(jnp.int32, (2 * TOP_K, LANES), 1)).astype(_BF16)
    high = jnp.floor(ints * (1.0 / LANES))
    low = ints - high * LANES
    nt = (((1,), (1,)), ((), ()))
    rows = (lax.dot_general(sel, high.astype(_BF16), nt, preferred_element_type=_F32) * LANES
            + lax.dot_general(sel, low.astype(_BF16), nt, preferred_element_type=_F32))
    ir_ref[...] = rows.astype(jnp.int32)
    gate_ref[...] = gate_out


def _route(x1, mod, norm_g, router_w, router_b):
    d = D_MODEL
    tm = ROUTE_TM
    tiles_per_batch = SEQ // tm
    rw_hi = router_w.astype(_BF16)
    rw_lo = (router_w - rw_hi.astype(_F32)).astype(_BF16)
    return pl.pallas_call(
        _route_kernel,
        grid=(TOKENS // tm,),
        in_specs=[
            pl.BlockSpec((tm, d), lambda i: (i, 0)),
            pl.BlockSpec((1, N_MOD, d), lambda i: (i // tiles_per_batch, 0, 0)),
            pl.BlockSpec((1, d), lambda i: (0, 0)),
            pl.BlockSpec((d, 2 * N_EXPERTS), lambda i: (0, 0)),
            pl.BlockSpec((1, N_EXPERTS), lambda i: (0, 0)),
        ],
        out_specs=[
            pl.BlockSpec((tm * ROW_WORDS, LANES), lambda i: (i, 0)),
            pl.BlockSpec((2 * TOP_K, tm), lambda i: (0, i)),
            pl.BlockSpec((tm, TOP_K), lambda i: (i, 0)),
            pl.BlockSpec((1, N_EXPERTS), lambda i: (0, 0)),
        ],
        out_shape=[
            jax.ShapeDtypeStruct((TOKENS * ROW_WORDS, LANES), _U32),
            jax.ShapeDtypeStruct((2 * TOP_K, TOKENS), jnp.int32),
            jax.ShapeDtypeStruct((TOKENS, TOP_K), _F32),
            jax.ShapeDtypeStruct((1, N_EXPERTS), _F32),
        ],
        compiler_params=pltpu.CompilerParams(
            dimension_semantics=("arbitrary",), vmem_limit_bytes=VMEM_LIMIT),
        name="route",
    )(x1, mod, norm_g.reshape(1, d), jnp.concatenate([rw_hi, rw_lo], axis=1), router_b.reshape(1, N_EXPERTS))


def _dest_kernel(ps_ref, ir_ref, o_ref):
    idx = ir_ref[0:TOP_K, :]
    dest = ir_ref[TOP_K:2 * TOP_K, :]
    for e in range(N_EXPERTS):
        dest = dest + jnp.where(idx == e, ps_ref[e], 0)
    o_ref[...] = dest


def _dest_rows(pad_starts, idx_rank):
    tn = 4096
    return pl.pallas_call(
        _dest_kernel,
        grid_spec=pltpu.PrefetchScalarGridSpec(
            num_scalar_prefetch=1,
            grid=(TOKENS // tn,),
            in_specs=[pl.BlockSpec((2 * TOP_K, tn), lambda i, ps: (0, i))],
            out_specs=pl.BlockSpec((TOP_K, tn), lambda i, ps: (0, i)),
        ),
        out_shape=jax.ShapeDtypeStruct((TOP_K, TOKENS), jnp.int32),
        name="dest_rows",
    )(pad_starts, idx_rank)


def _routing_tables(idx_rank, counts):
    counts = counts.reshape(N_EXPERTS).astype(jnp.int32)
    padded = (counts + EXPERT_BM - 1) // EXPERT_BM * EXPERT_BM
    pad_ends = jnp.cumsum(padded)
    pad_starts = pad_ends - padded
    dest = _dest_rows(pad_starts.astype(jnp.int32), idx_rank)
    block_start = jnp.arange(N_BLOCKS, dtype=jnp.int32) * EXPERT_BM
    block_e = jnp.sum((block_start[:, None] >= pad_ends[None, :]).astype(jnp.int32), axis=1)
    block_e = jnp.minimum(block_e, N_EXPERTS - 1).astype(jnp.int32)
    n_used = (pad_ends[-1:] // EXPERT_BM).astype(jnp.int32)
    ids = jnp.arange(N_EXPERTS, dtype=jnp.int32)
    later_used = (ids[None, :] > ids[:, None]) & (counts[None, :] > 0)
    next_used = jnp.min(jnp.where(later_used, ids[None, :], N_EXPERTS), axis=1).astype(jnp.int32)
    pad_lo = (pad_starts + counts).astype(jnp.int32)
    dest = dest.reshape(TOKENS * TOP_K).astype(jnp.int32)
    return dest, block_e, n_used, next_used, pad_lo, pad_ends.astype(jnp.int32)


def _vmem_row(buf, row):
    start = row * ROW_WORDS
    return buf.at[pl.ds(start if isinstance(row, int) else pl.multiple_of(start, ROW_WORDS), ROW_WORDS)]


def _dispatch_kernel(dest_ref, lo_ref, hi_ref, h_ref, xs_ref, rs_ref, zblk, sem, zsem):
    tm = DISP_TM
    base = pl.program_id(0) * tm

    def set_virtual(r, carry):
        rs_ref[r] = TOKENS * TOP_K + r
        return carry

    def set_dump(r, carry):
        rs_ref[r] = PAD_DUMP0 + (r & (PAD_DUMP_BLOCKS * EXPERT_BM - 1))
        return carry

    @pl.when(pl.program_id(0) == 0)
    def _():
        lax.fori_loop(0, EXPERT_BM, set_virtual, 0)

    def issue(g, carry):
        toks = [g * DISP_UNROLL + u for u in range(DISP_UNROLL)]
        dests = [[dest_ref[k * TOKENS + base + t] for k in range(TOP_K)] for t in toks]
        for t, row in zip(toks, dests):
            for k, d in enumerate(row):
                pltpu.make_async_copy(_vmem_row(h_ref, t), xs_ref.at[d], sem).start(priority=k % 2)
            for k, d in enumerate(row):
                rs_ref[d + EXPERT_BM] = k * TOKENS + base + t
        return carry

    lax.fori_loop(0, tm // DISP_UNROLL, issue, 0)

    @pl.when(pl.program_id(0) == pl.num_programs(0) - 1)
    def _():
        zblk[...] = jnp.zeros((EXPERT_BM * ROW_WORDS, LANES), _U32)

        def pad_copy(r):
            return pltpu.make_async_copy(_vmem_row(zblk, 0), xs_ref.at[r], zsem)

        def tail_copy(b):
            return pltpu.make_async_copy(zblk.reshape(EXPERT_BM, ROW_WORDS, LANES),
                                         xs_ref.at[pl.ds(pl.multiple_of(b * EXPERT_BM, EXPERT_BM), EXPERT_BM)], zsem)

        first_unused = hi_ref[N_EXPERTS - 1] // EXPERT_BM
        for e in range(N_EXPERTS):
            lax.fori_loop(lo_ref[e] + EXPERT_BM, hi_ref[e] + EXPERT_BM, set_dump, 0)
        lax.fori_loop((first_unused + 1) * EXPERT_BM, N_ROWS + EXPERT_BM, set_dump, 0)
        for e in range(N_EXPERTS):
            lax.fori_loop(lo_ref[e], hi_ref[e], lambda r, c: (pad_copy(r).start(), c)[1], 0)
        lax.fori_loop(first_unused, N_BLOCKS, lambda b, c: (tail_copy(b).start(), c)[1], 0)
        for e in range(N_EXPERTS):
            lax.fori_loop(lo_ref[e], hi_ref[e], lambda r, c: (pad_copy(r).wait(), c)[1], 0)
        lax.fori_loop(first_unused, N_BLOCKS, lambda b, c: (tail_copy(b).wait(), c)[1], 0)

    for _ in range(TOP_K):
        pltpu.make_async_copy(h_ref.reshape(tm, ROW_WORDS, LANES), xs_ref.at[pl.ds(0, tm)], sem).wait()


def _dispatch(dest, pad_lo, pad_hi, h_rows):
    tm = DISP_TM
    return pl.pallas_call(
        _dispatch_kernel,
        grid_spec=pltpu.PrefetchScalarGridSpec(
            num_scalar_prefetch=3,
            grid=(TOKENS // tm,),
            in_specs=[pl.BlockSpec((tm * ROW_WORDS, LANES), lambda i, *_: (i, 0))],
            out_specs=[pl.BlockSpec(memory_space=pl.ANY), pl.BlockSpec(memory_space=pltpu.SMEM)],
            scratch_shapes=[pltpu.VMEM((EXPERT_BM * ROW_WORDS, LANES), _U32),
                            pltpu.SemaphoreType.DMA, pltpu.SemaphoreType.DMA],
        ),
        out_shape=[jax.ShapeDtypeStruct((N_ROWS, ROW_WORDS, LANES), _U32),
                   jax.ShapeDtypeStruct((N_ROWS + EXPERT_BM,), jnp.int32)],
        compiler_params=pltpu.CompilerParams(
            dimension_semantics=("arbitrary",), vmem_limit_bytes=VMEM_LIMIT),
        name="dispatch",
    )(dest, pad_lo, pad_hi, h_rows)


def _expert_kernel(be_ref, nu_ref, nx_ref, rs_ref, xs_ref, w1_hbm, b1_ref, w2_hbm, b2_ref, ys_ref,
                   w1f, w2f, w1s, w2s, obuf, wsem, osem, isem, nsw, *, layer):
    bm = EXPERT_BM
    b = pl.program_id(0)
    e = be_ref[b]
    n_used = nu_ref[0]
    half = MXU_COLS // 2

    def fetch(expert, slot):
        return (pltpu.make_async_copy(w1_hbm.at[layer, expert], w1f.at[slot], wsem.at[0, slot]),
                pltpu.make_async_copy(w2_hbm.at[layer, expert], w2f.at[slot], wsem.at[1, slot]))

    def scatter_start(step, slot):
        for i in range(bm):
            pltpu.make_async_copy(_vmem_row(obuf.at[slot], i), ys_ref.at[rs_ref[step * bm + i]],
                                  osem.at[slot]).start(priority=i % 2)

    def block_view(slot):
        return obuf.at[slot].reshape(bm, ROW_WORDS, LANES)

    def scatter_wait(slot):
        pltpu.make_async_copy(block_view(slot), ys_ref.at[pl.ds(0, bm)], osem.at[slot]).wait()

    @pl.when(b == 0)
    def _():
        nsw[0] = 0
        obuf[RING - 1] = jnp.zeros((bm * ROW_WORDS, LANES), _U32)
        for blk in range(PAD_DUMP_BLOCKS):
            pltpu.make_async_copy(block_view(RING - 1), ys_ref.at[pl.ds(PAD_DUMP0 + blk * bm, bm)], isem).start()
        for blk in range(PAD_DUMP_BLOCKS):
            pltpu.make_async_copy(block_view(RING - 1), ys_ref.at[pl.ds(PAD_DUMP0 + blk * bm, bm)], isem).wait()
        for virtual in range(2, RING + 1):
            pltpu.make_async_copy(
                block_view(RING - 1), ys_ref.at[pl.ds(TOKENS * TOP_K + (virtual - 1) * bm, bm)],
                osem.at[RING - virtual]).start()
        for copy in fetch(e, 0):
            copy.start()

    @pl.when(b == n_used)
    def _():
        scatter_start(b, (b + RING - 1) % RING)
        for slot in range(RING):
            scatter_wait(slot)

    @pl.when(b < n_used)
    def _():
        out_slot = b % RING
        scatter_wait(out_slot)

        @pl.when(jnp.logical_or(b == 0, be_ref[jnp.maximum(b - 1, 0)] != e))
        def _():
            slot = nsw[0] % 2
            for copy in fetch(e, slot):
                copy.wait()
            src = lax.broadcasted_iota(jnp.int32, (MXU_COLS, MXU_COLS), 0)
            dst = lax.broadcasted_iota(jnp.int32, (MXU_COLS, MXU_COLS), 1)
            want = jnp.where(dst < half, 2 * dst, 2 * (dst - half) + 1)
            perm = (src == want).astype(_BF16)
            for blk in range(2 * D_EXPERT // MXU_COLS):
                cs = slice(blk * MXU_COLS, (blk + 1) * MXU_COLS)
                w1s[:, cs] = jnp.dot(w1f[slot, :, cs].astype(_BF16), perm,
                                     preferred_element_type=_F32).astype(_BF16)
            w2s[...] = w2f[slot].astype(_BF16)
            nsw[0] = nsw[0] + 1
            nxt = nx_ref[e]

            @pl.when(nxt < N_EXPERTS)
            def _():
                for copy in fetch(nxt, 1 - slot):
                    copy.start(priority=1)

        scatter_start(b, (b + RING - 1) % RING)
        x = _load_packed(xs_ref, (), bm, _BF16)
        u = jnp.dot(x, w1s[...], preferred_element_type=_F32) + b1_ref[0, e]
        acts = []
        for blk in range(2 * D_EXPERT // MXU_COLS):
            glu = jnp.minimum(u[:, blk * MXU_COLS:blk * MXU_COLS + half], SWIGLU_LIMIT)
            lin = jnp.clip(u[:, blk * MXU_COLS + half:(blk + 1) * MXU_COLS], -SWIGLU_LIMIT, SWIGLU_LIMIT)
            acts.append((glu * _sigmoid(SWIGLU_ALPHA * glu) * (lin + 1.0)).astype(_BF16))
        hidden = jnp.concatenate(acts, axis=1)
        y = jnp.dot(hidden, w2s[...], preferred_element_type=_F32) + b2_ref[0, e]
        _store_packed(obuf, (out_slot,), y)


def _experts(layer, block_e, n_used, next_used, row_slot, xs, w1, b1_grouped, w2, b2):
    bm = EXPERT_BM
    d, f = D_MODEL, D_EXPERT
    blk = lambda b, be, nu, nx, rs: jnp.minimum(b, nu[0] - 1)
    return pl.pallas_call(
        functools.partial(_expert_kernel, layer=layer),
        grid_spec=pltpu.PrefetchScalarGridSpec(
            num_scalar_prefetch=4,
            grid=(N_BLOCKS,),
            in_specs=[
                pl.BlockSpec((bm * ROW_WORDS, LANES), lambda *a: (blk(*a), 0)),
                pl.BlockSpec(memory_space=pl.ANY),
                pl.BlockSpec((1, N_EXPERTS, 1, 2 * f), lambda *a: (layer, 0, 0, 0)),
                pl.BlockSpec(memory_space=pl.ANY),
                pl.BlockSpec((1, N_EXPERTS, 1, d), lambda *a: (layer, 0, 0, 0)),
            ],
            out_specs=pl.BlockSpec(memory_space=pl.ANY),
            scratch_shapes=[
                pltpu.VMEM((2, d, 2 * f), _F32), pltpu.VMEM((2, f, d), _F32),
                pltpu.VMEM((d, 2 * f), _BF16), pltpu.VMEM((f, d), _BF16),
                pltpu.VMEM((RING, bm * ROW_WORDS, LANES), _U32),
                pltpu.SemaphoreType.DMA((2, 2)), pltpu.SemaphoreType.DMA((RING,)), pltpu.SemaphoreType.DMA,
                pltpu.SMEM((1,), jnp.int32),
            ],
        ),
        out_shape=jax.ShapeDtypeStruct((OUT_SLOTS, ROW_WORDS, LANES), _U32),
        compiler_params=pltpu.CompilerParams(
            dimension_semantics=("arbitrary",), vmem_limit_bytes=VMEM_LIMIT),
        name="experts",
    )(block_e, n_used, next_used, row_slot, xs.reshape(N_ROWS * ROW_WORDS, LANES), w1, b1_grouped, w2, b2)


def _group_glu_columns(b1):
    half = MXU_COLS // 2
    j = jnp.arange(MXU_COLS)
    want = jnp.where(j < half, 2 * j, 2 * (j - half) + 1)
    cols = (jnp.arange(2 * D_EXPERT // MXU_COLS)[:, None] * MXU_COLS + want[None, :]).reshape(-1)
    return b1[..., cols]


def _combine_kernel(y0_ref, y1_ref, y2_ref, y3_ref, x_ref, gate_ref, mod_ref, fg_ref, o_ref):
    out = _moe_residual((y0_ref, y1_ref, y2_ref, y3_ref), gate_ref, x_ref[...], mod_ref[0][5:6], COMB_TM)
    o_ref[...] = out * lax.rsqrt(jnp.mean(out * out, axis=-1, keepdims=True) + NORM_EPS) * fg_ref[...]


def _combine_final(ys, x1, gates, mod, final_g):
    tm = COMB_TM
    d = D_MODEL
    tiles = TOKENS // tm
    tiles_per_batch = SEQ // tm
    ys2 = ys.reshape(OUT_SLOTS * ROW_WORDS, LANES)
    slot_spec = lambda k: pl.BlockSpec((tm * ROW_WORDS, LANES), lambda i: (k * tiles + i, 0))
    return pl.pallas_call(
        _combine_kernel,
        grid=(tiles,),
        in_specs=[
            slot_spec(0), slot_spec(1), slot_spec(2), slot_spec(3),
            pl.BlockSpec((tm, d), lambda i: (i, 0)),
            pl.BlockSpec((tm, TOP_K), lambda i: (i, 0)),
            pl.BlockSpec((1, N_MOD, d), lambda i: (i // tiles_per_batch, 0, 0)),
            pl.BlockSpec((1, d), lambda i: (0, 0)),
        ],
        out_specs=pl.BlockSpec((tm, d), lambda i: (i, 0)),
        out_shape=jax.ShapeDtypeStruct((TOKENS, d), _F32),
        compiler_params=pltpu.CompilerParams(
            dimension_semantics=("arbitrary",), vmem_limit_bytes=VMEM_LIMIT),
        name="combine_final",
    )(ys2, ys2, ys2, ys2, x1, gates, mod, final_g.reshape(1, d))


def kernel(x, c, mod_w, mod_b, norm_g, conf_w1, conf_b1, conf_dw, conf_dw_b, conf_ln_g, conf_ln_b, conf_w2, conf_b2, sc_w_in, sc_conv, sc_w_out, router_w, router_b, exp_w1, exp_b1, exp_w2, exp_b2, final_g):
    mods = _modulation(c, mod_w, mod_b).reshape(DEPTH, BATCH, N_MOD, D_MODEL)
    b1_grouped = _group_glu_columns(exp_b1).reshape(DEPTH, N_EXPERTS, 1, 2 * D_EXPERT)
    b2_rows = exp_b2.reshape(DEPTH, N_EXPERTS, 1, D_MODEL)
    assert DEPTH == 2, "layer 0's MoE residual is applied inside layer 1's mixer kernel"
    pending = None
    for i in range(DEPTH):
        mod = mods[i]
        j = i // 2
        if i % 2 == 0:
            x1 = _conformer_mixer(x, mod, norm_g[i, 0], conf_w1[j], conf_b1[j], conf_dw[j], conf_dw_b[j],
                                  conf_ln_g[j], conf_ln_b[j], conf_w2[j], conf_b2[j])
        else:
            x1 = _shortconv_mixer(pending, mod, norm_g[i, 0], sc_w_in[j], sc_conv[j], sc_w_out[j])
        x1 = x1.reshape(TOKENS, D_MODEL)
        h_rows, idx_rank, gates, counts = _route(x1, mod, norm_g[i, 1], router_w[i], router_b[i])
        dest, block_e, n_used, next_used, pad_lo, pad_hi = _routing_tables(idx_rank, counts)
        xs, row_slot = _dispatch(dest, pad_lo, pad_hi, h_rows)
        ys = _experts(i, block_e, n_used, next_used, row_slot, xs, exp_w1, b1_grouped, exp_w2, b2_rows)
        pending = (ys, x1, gates, mod)
    out = _combine_final(*pending, final_g)
    return out.reshape(BATCH, SEQ, D_MODEL)
```

```python
import functools

import jax
import jax.numpy as jnp
from jax import lax
from jax.experimental import pallas as pl
from jax.experimental.pallas import tpu as pltpu

D_MODEL = 1024
BATCH = 8
SEQ = 2048
DEPTH = 2
TOKENS = BATCH * SEQ
CONF_KERNEL = 31
SHORT_KERNEL = 3
N_EXPERTS = 32
TOP_K = 4
D_EXPERT = D_MODEL
SWIGLU_ALPHA = 1.702
SWIGLU_LIMIT = 7.0
NORM_EPS = 1e-5
N_MOD = 6

LANES = 128
SUBLANES = 8
MXU_COLS = 256
VMEM_LIMIT = 56 * 1024 * 1024

HALF_D = D_MODEL // 2
ROW_WORDS = HALF_D // LANES
MIX_TS = 256
CONF_HALO = 32
SHORT_HALO = 8
CONV_RC = 64
ROUTE_TM = 512
DISP_TM = 2048
DISP_UNROLL = 16
COMB_TM = 256
EXPERT_BM = 256
N_ROWS = TOKENS * TOP_K + N_EXPERTS * EXPERT_BM
N_BLOCKS = N_ROWS // EXPERT_BM
RING = 3
PAD_DUMP0 = TOKENS * TOP_K + RING * EXPERT_BM
PAD_DUMP_BLOCKS = 4
OUT_SLOTS = PAD_DUMP0 + PAD_DUMP_BLOCKS * EXPERT_BM

_F32 = jnp.float32
_BF16 = jnp.bfloat16
_U32 = jnp.uint32
_HI_MASK = 0xFFFF0000


def _rms_mod(x, g, scale, shift):
    y = x * lax.rsqrt(jnp.mean(x * x, axis=-1, keepdims=True) + NORM_EPS)
    return (y * g) * (1.0 + scale) + shift


def _sigmoid(x):
    return 1.0 / (1.0 + jnp.exp(-x))


def _pack_rows(v):
    bits = lambda a: lax.bitcast_convert_type(a.astype(_BF16).astype(_F32), _U32)
    return (bits(v[:, HALF_D:]) & _U32(_HI_MASK)) | (bits(v[:, :HALF_D]) >> 16)


def _store_packed(ref, lead, v):
    rows = v.shape[0]
    words = _pack_rows(v)
    for j in range(ROW_WORDS):
        ref[lead + (pl.ds(j, rows, stride=ROW_WORDS), slice(None))] = words[:, j * LANES:(j + 1) * LANES]


def _load_packed(ref, lead, rows, dtype):
    lo, hi = [], []
    for j in range(ROW_WORDS):
        w = ref[lead + (pl.ds(j, rows, stride=ROW_WORDS), slice(None))]
        lo.append(lax.bitcast_convert_type(w << 16, _F32).astype(dtype))
        hi.append(lax.bitcast_convert_type(w & _U32(_HI_MASK), _F32).astype(dtype))
    return jnp.concatenate(lo + hi, axis=1)


def _mod_kernel(c_ref, w_ref, b_ref, o_ref):
    c = c_ref[...]
    c_act = c * _sigmoid(c)
    o_ref[0] = jnp.dot(c_act, w_ref[0], preferred_element_type=_F32,
                       precision=lax.Precision.HIGHEST) + b_ref[0]


def _modulation(c, mod_w, mod_b):
    tn = 1536
    n = N_MOD * D_MODEL
    return pl.pallas_call(
        _mod_kernel,
        grid=(DEPTH, n // tn),
        in_specs=[
            pl.BlockSpec((BATCH, D_MODEL), lambda i, j: (0, 0)),
            pl.BlockSpec((1, D_MODEL, tn), lambda i, j: (i, 0, j)),
            pl.BlockSpec((1, 1, tn), lambda i, j: (i, 0, j)),
        ],
        out_specs=pl.BlockSpec((1, BATCH, tn), lambda i, j: (i, 0, j)),
        out_shape=jax.ShapeDtypeStruct((DEPTH, BATCH, n), _F32),
        compiler_params=pltpu.CompilerParams(vmem_limit_bytes=VMEM_LIMIT),
        name="modulation",
    )(c, mod_w, mod_b.reshape(DEPTH, 1, n))


def _causal_taps(cbuf, shifted, w_ref, bias_row, out_ref, ntaps, halo, ts):
    off0 = halo - (ntaps - 1)
    for c in range(D_MODEL // LANES):
        cs = slice(c * LANES, (c + 1) * LANES)
        if shifted is not None:
            rows = cbuf.shape[0]
            whole = cbuf[:, cs]
            for r in range(1, SUBLANES):
                shifted[r - 1, :, cs] = pltpu.roll(whole, rows - r, axis=0)
        for rc in range(ts // CONV_RC):
            r0 = rc * CONV_RC
            acc = None
            for k in range(ntaps):
                s = off0 + k
                if shifted is None or s % SUBLANES == 0:
                    window = cbuf[r0 + s:r0 + s + CONV_RC, cs]
                else:
                    a = s - s % SUBLANES
                    window = shifted[s % SUBLANES - 1, r0 + a:r0 + a + CONV_RC, cs]
                term = w_ref[k:k + 1, cs] * window
                acc = term if acc is None else acc + term
            if bias_row is not None:
                acc = acc + bias_row[:, cs]
            out_ref[r0:r0 + CONV_RC, cs] = acc


def _conformer_kernel(x_ref, mod_ref, ng_ref, w1_ref, b1_ref, dw_ref, dwb_ref, lng_ref, lnb_ref,
                      w2_ref, b2_ref, o_ref, cbuf, shifted, vbuf):
    ts = MIX_TS

    @pl.when(pl.program_id(1) == 0)
    def _():
        cbuf[0:CONF_HALO, :] = jnp.zeros((CONF_HALO, D_MODEL), _F32)
        cbuf[CONF_HALO + ts:, :] = jnp.zeros((SUBLANES, D_MODEL), _F32)

    x = x_ref[0]
    m = mod_ref[0]
    h = _rms_mod(x, ng_ref[...], m[1:2], m[0:1])
    u = jnp.dot(h.astype(_BF16), w1_ref[...], preferred_element_type=_F32) + b1_ref[...]
    cbuf[CONF_HALO:CONF_HALO + ts, :] = u[:, :D_MODEL] * _sigmoid(u[:, D_MODEL:])
    _causal_taps(cbuf, shifted, dw_ref, dwb_ref[...], vbuf, CONF_KERNEL, CONF_HALO, ts)
    cbuf[0:CONF_HALO, :] = cbuf[ts:ts + CONF_HALO, :]
    v = vbuf[...]
    mu = jnp.mean(v, axis=-1, keepdims=True)
    vc = v - mu
    var = jnp.mean(vc * vc, axis=-1, keepdims=True)
    y = vc * lax.rsqrt(var + NORM_EPS) * lng_ref[...] + lnb_ref[...]
    y = y * _sigmoid(y)
    mix = jnp.dot(y.astype(_BF16), w2_ref[...], preferred_element_type=_F32) + b2_ref[...]
    o_ref[0] = x + m[2:3] * mix


def _conformer_mixer(x, mod, norm_g, w1, b1, dw, dw_b, ln_g, ln_b, w2, b2):
    d = D_MODEL
    row = lambda a: a.reshape(1, -1)
    const = lambda shape: pl.BlockSpec(shape, lambda b, s: (0,) * len(shape))
    return pl.pallas_call(
        _conformer_kernel,
        grid=(BATCH, SEQ // MIX_TS),
        in_specs=[
            pl.BlockSpec((1, MIX_TS, d), lambda b, s: (b, s, 0)),
            pl.BlockSpec((1, N_MOD, d), lambda b, s: (b, 0, 0)),
            const((1, d)), const((d, 2 * d)), const((1, 2 * d)), const((CONF_KERNEL, d)),
            const((1, d)), const((1, d)), const((1, d)), const((d, d)), const((1, d)),
        ],
        out_specs=pl.BlockSpec((1, MIX_TS, d), lambda b, s: (b, s, 0)),
        out_shape=jax.ShapeDtypeStruct((BATCH, SEQ, d), _F32),
        scratch_shapes=[pltpu.VMEM((CONF_HALO + MIX_TS + SUBLANES, d), _F32),
                        pltpu.VMEM((SUBLANES - 1, CONF_HALO + MIX_TS + SUBLANES, d), _F32),
                        pltpu.VMEM((MIX_TS, d), _F32)],
        compiler_params=pltpu.CompilerParams(
            dimension_semantics=("arbitrary", "arbitrary"), vmem_limit_bytes=VMEM_LIMIT),
        name="conformer_mixer",
    )(x, mod, row(norm_g), w1.astype(_BF16), row(b1), dw, row(dw_b), row(ln_g), row(ln_b),
      w2.astype(_BF16), row(b2))


def _moe_residual(y_refs, gate_ref, x1, gate2, rows):
    gates = gate_ref[...]
    y = None
    for k, yk_ref in enumerate(y_refs):
        term = gates[:, k:k + 1] * _load_packed(yk_ref, (), rows, _F32)
        y = term if y is None else y + term
    return x1 + gate2 * y


def _shortconv_kernel(y0_ref, y1_ref, y2_ref, y3_ref, gate_ref, pmod_ref, x_ref, mod_ref, ng_ref, win_ref, cw_ref,
                      wout_ref, o_ref, cbuf, vbuf):
    ts = MIX_TS
    d = D_MODEL

    @pl.when(pl.program_id(1) == 0)
    def _():
        cbuf[0:SHORT_HALO, :] = jnp.zeros((SHORT_HALO, d), _F32)

    x = _moe_residual((y0_ref, y1_ref, y2_ref, y3_ref), gate_ref, x_ref[0], pmod_ref[0][5:6], ts)
    m = mod_ref[0]
    h = _rms_mod(x, ng_ref[...], m[1:2], m[0:1])
    z = jnp.dot(h.astype(_BF16), win_ref[...], preferred_element_type=_F32)
    cbuf[SHORT_HALO:SHORT_HALO + ts, :] = z[:, d:2 * d] * z[:, 2 * d:]
    _causal_taps(cbuf, None, cw_ref, None, vbuf, SHORT_KERNEL, SHORT_HALO, ts)
    cbuf[0:SHORT_HALO, :] = cbuf[ts:ts + SHORT_HALO, :]
    y = z[:, :d] * vbuf[...]
    mix = jnp.dot(y.astype(_BF16), wout_ref[...], preferred_element_type=_F32)
    o_ref[0] = x + m[2:3] * mix


def _shortconv_mixer(pending, mod, norm_g, w_in, conv_w, w_out):
    ys, x_prev, gates, prev_mod = pending
    d = D_MODEL
    tiles_per_batch = SEQ // MIX_TS
    tiles = TOKENS // MIX_TS
    ys2 = ys.reshape(OUT_SLOTS * ROW_WORDS, LANES)
    const = lambda shape: pl.BlockSpec(shape, lambda b, s: (0,) * len(shape))
    slot_spec = lambda k: pl.BlockSpec((MIX_TS * ROW_WORDS, LANES),
                                       lambda b, s: (k * tiles + b * tiles_per_batch + s, 0))
    return pl.pallas_call(
        _shortconv_kernel,
        grid=(BATCH, SEQ // MIX_TS),
        in_specs=[
            slot_spec(0), slot_spec(1), slot_spec(2), slot_spec(3),
            pl.BlockSpec((MIX_TS, TOP_K), lambda b, s: (b * tiles_per_batch + s, 0)),
            pl.BlockSpec((1, N_MOD, d), lambda b, s: (b, 0, 0)),
            pl.BlockSpec((1, MIX_TS, d), lambda b, s: (b, s, 0)),
            pl.BlockSpec((1, N_MOD, d), lambda b, s: (b, 0, 0)),
            const((1, d)), const((d, 3 * d)), const((SHORT_KERNEL, d)), const((d, d)),
        ],
        out_specs=pl.BlockSpec((1, MIX_TS, d), lambda b, s: (b, s, 0)),
        out_shape=jax.ShapeDtypeStruct((BATCH, SEQ, d), _F32),
        scratch_shapes=[pltpu.VMEM((SHORT_HALO + MIX_TS, d), _F32), pltpu.VMEM((MIX_TS, d), _F32)],
        compiler_params=pltpu.CompilerParams(
            dimension_semantics=("arbitrary", "arbitrary"), vmem_limit_bytes=VMEM_LIMIT),
        name="shortconv_mixer",
    )(ys2, ys2, ys2, ys2, gates, prev_mod, x_prev.reshape(BATCH, SEQ, d), mod, norm_g.reshape(1, d),
      w_in.astype(_BF16), conv_w, w_out.astype(_BF16))


def _route_kernel(x_ref, mod_ref, ng_ref, rw_ref, rb_ref, h_ref, ir_ref, gate_ref, cnt_ref):
    tm = ROUTE_TM

    @pl.when(pl.program_id(0) == 0)
    def _():
        cnt_ref[...] = jnp.zeros((1, N_EXPERTS), _F32)

    x = x_ref[...]
    m = mod_ref[0]
    h = _rms_mod(x, ng_ref[...], m[4:5], m[3:4])
    _store_packed(h_ref, (), h)

    h_hi = h.astype(_BF16)
    h_lo = (h - h_hi.astype(_F32)).astype(_BF16)
    both = jnp.dot(h_hi, rw_ref[...], preferred_element_type=_F32)
    logits = (both[:, :N_EXPERTS] + both[:, N_EXPERTS:]
              + jnp.dot(h_lo, rw_ref[:, :N_EXPERTS], preferred_element_type=_F32)) + rb_ref[...]
    lane = lax.broadcasted_iota(jnp.int32, (tm, N_EXPERTS), 1).astype(_F32)
    work = logits
    vals, idxs, hots = [], [], []
    for _ in range(TOP_K):
        top = jnp.max(work, axis=1, keepdims=True)
        pick = jnp.min(jnp.where(work == top, lane, float(N_EXPERTS)), axis=1, keepdims=True)
        hot = lane == pick
        vals.append(top)
        idxs.append(pick)
        hots.append(hot)
        work = jnp.where(hot, -jnp.inf, work)
    exps = [jnp.exp(v - vals[0]) for v in vals]
    denom = exps[0] + exps[1] + exps[2] + exps[3]

    multi = (hots[0] | hots[1] | hots[2] | hots[3]).astype(_BF16)
    r_i = lax.broadcasted_iota(jnp.int32, (tm, tm), 0)
    c_i = lax.broadcasted_iota(jnp.int32, (tm, tm), 1)
    before = (c_i < r_i).astype(_BF16)
    pos = cnt_ref[...] + jnp.dot(before, multi, preferred_element_type=_F32)
    cnt_ref[...] = cnt_ref[...] + jnp.sum(multi.astype(_F32), axis=0, keepdims=True)

    col = lax.broadcasted_iota(jnp.int32, (tm, LANES), 1)
    gate_col = lax.broadcasted_iota(jnp.int32, (tm, TOP_K), 1)
    ints = jnp.zeros((tm, LANES), _F32)
    gate_out = jnp.zeros((tm, TOP_K), _F32)
    for k in range(TOP_K):
        rank_k = jnp.sum(jnp.where(hots[k], pos, 0.0), axis=1, keepdims=True)
        ints = jnp.where(col == k, idxs[k], ints)
        ints = jnp.where(col == TOP_K + k, rank_k, ints)
        gate_out = jnp.where(gate_col == k, exps[k] / denom, gate_out)
    sel = (lax.broadcasted_iota(jnp.int32, (2 * TOP_K, LANES), 0)
           == lax.broadcasted_iota(jnp.int32, (2 * TOP_K, LANES), 1)).astype(_BF16)
    high = jnp.floor(ints * (1.0 / LANES))
    low = ints - high * LANES
    nt = (((1,), (1,)), ((), ()))
    rows = (lax.dot_general(sel, high.astype(_BF16), nt, preferred_element_type=_F32) * LANES
            + lax.dot_general(sel, low.astype(_BF16), nt, preferred_element_type=_F32))
    ir_ref[...] = rows.astype(jnp.int32)
    gate_ref[...] = gate_out


def _route(x1, mod, norm_g, router_w, router_b):
    d = D_MODEL
    tm = ROUTE_TM
    tiles_per_batch = SEQ // tm
    rw_hi = router_w.astype(_BF16)
    rw_lo = (router_w - rw_hi.astype(_F32)).astype(_BF16)
    return pl.pallas_call(
        _route_kernel,
        grid=(TOKENS // tm,),
        in_specs=[
            pl.BlockSpec((tm, d), lambda i: (i, 0)),
            pl.BlockSpec((1, N_MOD, d), lambda i: (i // tiles_per_batch, 0, 0)),
            pl.BlockSpec((1, d), lambda i: (0, 0)),
            pl.BlockSpec((d, 2 * N_EXPERTS), lambda i: (0, 0)),
            pl.BlockSpec((1, N_EXPERTS), lambda i: (0, 0)),
        ],
        out_specs=[
            pl.BlockSpec((tm * ROW_WORDS, LANES), lambda i: (i, 0)),
            pl.BlockSpec((2 * TOP_K, tm), lambda i: (0, i)),
            pl.BlockSpec((tm, TOP_K), lambda i: (i, 0)),
            pl.BlockSpec((1, N_EXPERTS), lambda i: (0, 0)),
        ],
        out_shape=[
            jax.ShapeDtypeStruct((TOKENS * ROW_WORDS, LANES), _U32),
            jax.ShapeDtypeStruct((2 * TOP_K, TOKENS), jnp.int32),
            jax.ShapeDtypeStruct((TOKENS, TOP_K), _F32),
            jax.ShapeDtypeStruct((1, N_EXPERTS), _F32),
        ],
        compiler_params=pltpu.CompilerParams(
            dimension_semantics=("arbitrary",), vmem_limit_bytes=VMEM_LIMIT),
        name="route",
    )(x1, mod, norm_g.reshape(1, d), jnp.concatenate([rw_hi, rw_lo], axis=1), router_b.reshape(1, N_EXPERTS))


def _dest_kernel(ps_ref, ir_ref, o_ref):
    idx = ir_ref[0:TOP_K, :]
    dest = ir_ref[TOP_K:2 * TOP_K, :]
    for e in range(N_EXPERTS):
        dest = dest + jnp.where(idx == e, ps_ref[e], 0)
    o_ref[...] = dest


def _dest_rows(pad_starts, idx_rank):
    tn = 4096
    return pl.pallas_call(
        _dest_kernel,
        grid_spec=pltpu.PrefetchScalarGridSpec(
            num_scalar_prefetch=1,
            grid=(TOKENS // tn,),
            in_specs=[pl.BlockSpec((2 * TOP_K, tn), lambda i, ps: (0, i))],
            out_specs=pl.BlockSpec((TOP_K, tn), lambda i, ps: (0, i)),
        ),
        out_shape=jax.ShapeDtypeStruct((TOP_K, TOKENS), jnp.int32),
        name="dest_rows",
    )(pad_starts, idx_rank)


def _routing_tables(idx_rank, counts):
    counts = counts.reshape(N_EXPERTS).astype(jnp.int32)
    padded = (counts + EXPERT_BM - 1) // EXPERT_BM * EXPERT_BM
    pad_ends = jnp.cumsum(padded)
    pad_starts = pad_ends - padded
    dest = _dest_rows(pad_starts.astype(jnp.int32), idx_rank)
    block_start = jnp.arange(N_BLOCKS, dtype=jnp.int32) * EXPERT_BM
    block_e = jnp.sum((block_start[:, None] >= pad_ends[None, :]).astype(jnp.int32), axis=1)
    block_e = jnp.minimum(block_e, N_EXPERTS - 1).astype(jnp.int32)
    n_used = (pad_ends[-1:] // EXPERT_BM).astype(jnp.int32)
    ids = jnp.arange(N_EXPERTS, dtype=jnp.int32)
    later_used = (ids[None, :] > ids[:, None]) & (counts[None, :] > 0)
    next_used = jnp.min(jnp.where(later_used, ids[None, :], N_EXPERTS), axis=1).astype(jnp.int32)
    pad_lo = (pad_starts + counts).astype(jnp.int32)
    dest = dest.reshape(TOKENS * TOP_K).astype(jnp.int32)
    return dest, block_e, n_used, next_used, pad_lo, pad_ends.astype(jnp.int32)


def _vmem_row(buf, row):
    start = row * ROW_WORDS
    return buf.at[pl.ds(start if isinstance(row, int) else pl.multiple_of(start, ROW_WORDS), ROW_WORDS)]


def _dispatch_kernel(dest_ref, lo_ref, hi_ref, h_ref, xs_ref, rs_ref, zblk, sem, zsem):
    tm = DISP_TM
    base = pl.program_id(0) * tm

    def set_virtual(r, carry):
        rs_ref[r] = TOKENS * TOP_K + r
        return carry

    def set_dump(r, carry):
        rs_ref[r] = PAD_DUMP0 + (r & (PAD_DUMP_BLOCKS * EXPERT_BM - 1))
        return carry

    @pl.when(pl.program_id(0) == 0)
    def _():
        lax.fori_loop(0, EXPERT_BM, set_virtual, 0)

    def issue(g, carry):
        toks = [g * DISP_UNROLL + u for u in range(DISP_UNROLL)]
        dests = [[dest_ref[k * TOKENS + base + t] for k in range(TOP_K)] for t in toks]
        for t, row in zip(toks, dests):
            for k, d in enumerate(row):
                pltpu.make_async_copy(_vmem_row(h_ref, t), xs_ref.at[d], sem).start(priority=k % 2)
            for k, d in enumerate(row):
                rs_ref[d + EXPERT_BM] = k * TOKENS + base + t
        return carry

    lax.fori_loop(0, tm // DISP_UNROLL, issue, 0)

    @pl.when(pl.program_id(0) == pl.num_programs(0) - 1)
    def _():
        zblk[...] = jnp.zeros((EXPERT_BM * ROW_WORDS, LANES), _U32)

        def pad_copy(r):
            return pltpu.make_async_copy(_vmem_row(zblk, 0), xs_ref.at[r], zsem)

        def tail_copy(b):
            return pltpu.make_async_copy(zblk.reshape(EXPERT_BM, ROW_WORDS, LANES),
                                         xs_ref.at[pl.ds(pl.multiple_of(b * EXPERT_BM, EXPERT_BM), EXPERT_BM)], zsem)

        def pad_row(r, carry):
            pad_copy(r).start()
            return set_dump(r + EXPERT_BM, carry)

        def wait_rows(n_rows):
            pltpu.make_async_copy(xs_ref.at[pl.ds(0, n_rows)], xs_ref.at[pl.ds(0, n_rows)], zsem).wait()

        first_unused = hi_ref[N_EXPERTS - 1] // EXPERT_BM
        for e in range(N_EXPERTS):
            lax.fori_loop(lo_ref[e], hi_ref[e], pad_row, 0)
        lax.fori_loop((first_unused + 1) * EXPERT_BM, N_ROWS + EXPERT_BM, set_dump, 0)
        lax.fori_loop(first_unused, N_BLOCKS, lambda b, c: (tail_copy(b).start(), c)[1], 0)
        for e in range(N_EXPERTS):
            n_pad = hi_ref[e] - lo_ref[e]
            for bit in range(EXPERT_BM.bit_length() - 1):
                pl.when((n_pad >> bit) & 1 == 1)(functools.partial(wait_rows, 1 << bit))
        lax.fori_loop(first_unused, N_BLOCKS, lambda b, c: (tail_copy(b).wait(), c)[1], 0)

    for _ in range(TOP_K):
        pltpu.make_async_copy(h_ref.reshape(tm, ROW_WORDS, LANES), xs_ref.at[pl.ds(0, tm)], sem).wait()


def _dispatch(dest, pad_lo, pad_hi, h_rows):
    tm = DISP_TM
    return pl.pallas_call(
        _dispatch_kernel,
        grid_spec=pltpu.PrefetchScalarGridSpec(
            num_scalar_prefetch=3,
            grid=(TOKENS // tm,),
            in_specs=[pl.BlockSpec((tm * ROW_WORDS, LANES), lambda i, *_: (i, 0))],
            out_specs=[pl.BlockSpec(memory_space=pl.ANY), pl.BlockSpec(memory_space=pltpu.SMEM)],
            scratch_shapes=[pltpu.VMEM((EXPERT_BM * ROW_WORDS, LANES), _U32),
                            pltpu.SemaphoreType.DMA, pltpu.SemaphoreType.DMA],
        ),
        out_shape=[jax.ShapeDtypeStruct((N_ROWS, ROW_WORDS, LANES), _U32),
                   jax.ShapeDtypeStruct((N_ROWS + EXPERT_BM,), jnp.int32)],
        compiler_params=pltpu.CompilerParams(
            dimension_semantics=("arbitrary",), vmem_limit_bytes=VMEM_LIMIT),
        name="dispatch",
    )(dest, pad_lo, pad_hi, h_rows)


def _expert_kernel(be_ref, nu_ref, nx_ref, rs_ref, xs_ref, w1_hbm, b1_ref, w2_hbm, b2_ref, ys_ref,
                   w1f, w2f, w1s, w2s, obuf, wsem, osem, isem, nsw, *, layer):
    bm = EXPERT_BM
    b = pl.program_id(0)
    e = be_ref[b]
    n_used = nu_ref[0]
    half = MXU_COLS // 2

    def fetch(expert, slot):
        return (pltpu.make_async_copy(w1_hbm.at[layer, expert], w1f.at[slot], wsem.at[0, slot]),
                pltpu.make_async_copy(w2_hbm.at[layer, expert], w2f.at[slot], wsem.at[1, slot]))

    def scatter_start(step, slot):
        for i in range(bm):
            pltpu.make_async_copy(_vmem_row(obuf.at[slot], i), ys_ref.at[rs_ref[step * bm + i]],
                                  osem.at[slot]).start(priority=i % 2)

    def block_view(slot):
        return obuf.at[slot].reshape(bm, ROW_WORDS, LANES)

    def scatter_wait(slot):
        pltpu.make_async_copy(block_view(slot), ys_ref.at[pl.ds(0, bm)], osem.at[slot]).wait()

    @pl.when(b == 0)
    def _():
        nsw[0] = 0
        obuf[RING - 1] = jnp.zeros((bm * ROW_WORDS, LANES), _U32)
        for blk in range(PAD_DUMP_BLOCKS):
            pltpu.make_async_copy(block_view(RING - 1), ys_ref.at[pl.ds(PAD_DUMP0 + blk * bm, bm)], isem).start()
        for blk in range(PAD_DUMP_BLOCKS):
            pltpu.make_async_copy(block_view(RING - 1), ys_ref.at[pl.ds(PAD_DUMP0 + blk * bm, bm)], isem).wait()
        for virtual in range(2, RING + 1):
            pltpu.make_async_copy(
                block_view(RING - 1), ys_ref.at[pl.ds(TOKENS * TOP_K + (virtual - 1) * bm, bm)],
                osem.at[RING - virtual]).start()
        for copy in fetch(e, 0):
            copy.start()

    @pl.when(b == n_used)
    def _():
        scatter_start(b, (b + RING - 1) % RING)
        for slot in range(RING):
            scatter_wait(slot)

    @pl.when(b < n_used)
    def _():
        out_slot = b % RING
        scatter_wait(out_slot)

        @pl.when(jnp.logical_or(b == 0, be_ref[jnp.maximum(b - 1, 0)] != e))
        def _():
            slot = nsw[0] % 2
            for copy in fetch(e, slot):
                copy.wait()
            src = lax.broadcasted_iota(jnp.int32, (MXU_COLS, MXU_COLS), 0)
            dst = lax.broadcasted_iota(jnp.int32, (MXU_COLS, MXU_COLS), 1)
            want = jnp.where(dst < half, 2 * dst, 2 * (dst - half) + 1)
            perm = (src == want).astype(_BF16)
            for blk in range(2 * D_EXPERT // MXU_COLS):
                cs = slice(blk * MXU_COLS, (blk + 1) * MXU_COLS)
                w1s[:, cs] = jnp.dot(w1f[slot, :, cs].astype(_BF16), perm,
                                     preferred_element_type=_F32).astype(_BF16)
            w2s[...] = w2f[slot].astype(_BF16)
            nsw[0] = nsw[0] + 1
            nxt = nx_ref[e]

            @pl.when(nxt < N_EXPERTS)
            def _():
                for copy in fetch(nxt, 1 - slot):
                    copy.start()

        scatter_start(b, (b + RING - 1) % RING)
        x = _load_packed(xs_ref, (), bm, _BF16)
        u = jnp.dot(x, w1s[...], preferred_element_type=_F32) + b1_ref[0, e]
        acts = []
        for blk in range(2 * D_EXPERT // MXU_COLS):
            glu = jnp.minimum(u[:, blk * MXU_COLS:blk * MXU_COLS + half], SWIGLU_LIMIT)
            lin = jnp.clip(u[:, blk * MXU_COLS + half:(blk + 1) * MXU_COLS], -SWIGLU_LIMIT, SWIGLU_LIMIT)
            acts.append((glu * _sigmoid(SWIGLU_ALPHA * glu) * (lin + 1.0)).astype(_BF16))
        hidden = jnp.concatenate(acts, axis=1)
        y = jnp.dot(hidden, w2s[...], preferred_element_type=_F32) + b2_ref[0, e]
        _store_packed(obuf, (out_slot,), y)


def _experts(layer, block_e, n_used, next_used, row_slot, xs, w1, b1_grouped, w2, b2):
    bm = EXPERT_BM
    d, f = D_MODEL, D_EXPERT
    blk = lambda b, be, nu, nx, rs: jnp.minimum(b, nu[0] - 1)
    return pl.pallas_call(
        functools.partial(_expert_kernel, layer=layer),
        grid_spec=pltpu.PrefetchScalarGridSpec(
            num_scalar_prefetch=4,
            grid=(N_BLOCKS,),
            in_specs=[
                pl.BlockSpec((bm * ROW_WORDS, LANES), lambda *a: (blk(*a), 0)),
                pl.BlockSpec(memory_space=pl.ANY),
                pl.BlockSpec((1, N_EXPERTS, 1, 2 * f), lambda *a: (layer, 0, 0, 0)),
                pl.BlockSpec(memory_space=pl.ANY),
                pl.BlockSpec((1, N_EXPERTS, 1, d), lambda *a: (layer, 0, 0, 0)),
            ],
            out_specs=pl.BlockSpec(memory_space=pl.ANY),
            scratch_shapes=[
                pltpu.VMEM((2, d, 2 * f), _F32), pltpu.VMEM((2, f, d), _F32),
                pltpu.VMEM((d, 2 * f), _BF16), pltpu.VMEM((f, d), _BF16),
                pltpu.VMEM((RING, bm * ROW_WORDS, LANES), _U32),
                pltpu.SemaphoreType.DMA((2, 2)), pltpu.SemaphoreType.DMA((RING,)), pltpu.SemaphoreType.DMA,
                pltpu.SMEM((1,), jnp.int32),
            ],
        ),
        out_shape=jax.ShapeDtypeStruct((OUT_SLOTS, ROW_WORDS, LANES), _U32),
        compiler_params=pltpu.CompilerParams(
            dimension_semantics=("arbitrary",), vmem_limit_bytes=VMEM_LIMIT),
        name="experts",
    )(block_e, n_used, next_used, row_slot, xs.reshape(N_ROWS * ROW_WORDS, LANES), w1, b1_grouped, w2, b2)


def _group_glu_columns(b1):
    half = MXU_COLS // 2
    j = jnp.arange(MXU_COLS)
    want = jnp.where(j < half, 2 * j, 2 * (j - half) + 1)
    cols = (jnp.arange(2 * D_EXPERT // MXU_COLS)[:, None] * MXU_COLS + want[None, :]).reshape(-1)
    return b1[..., cols]


def _combine_kernel(y0_ref, y1_ref, y2_ref, y3_ref, x_ref, gate_ref, mod_ref, fg_ref, o_ref):
    out = _moe_residual((y0_ref, y1_ref, y2_ref, y3_ref), gate_ref, x_ref[...], mod_ref[0][5:6], COMB_TM)
    o_ref[...] = out * lax.rsqrt(jnp.mean(out * out, axis=-1, keepdims=True) + NORM_EPS) * fg_ref[...]


def _combine_final(ys, x1, gates, mod, final_g):
    tm = COMB_TM
    d = D_MODEL
    tiles = TOKENS // tm
    tiles_per_batch = SEQ // tm
    ys2 = ys.reshape(OUT_SLOTS * ROW_WORDS, LANES)
    slot_spec = lambda k: pl.BlockSpec((tm * ROW_WORDS, LANES), lambda i: (k * tiles + i, 0))
    return pl.pallas_call(
        _combine_kernel,
        grid=(tiles,),
        in_specs=[
            slot_spec(0), slot_spec(1), slot_spec(2), slot_spec(3),
            pl.BlockSpec((tm, d), lambda i: (i, 0)),
            pl.BlockSpec((tm, TOP_K), lambda i: (i, 0)),
            pl.BlockSpec((1, N_MOD, d), lambda i: (i // tiles_per_batch, 0, 0)),
            pl.BlockSpec((1, d), lambda i: (0, 0)),
        ],
        out_specs=pl.BlockSpec((tm, d), lambda i: (i, 0)),
        out_shape=jax.ShapeDtypeStruct((TOKENS, d), _F32),
        compiler_params=pltpu.CompilerParams(
            dimension_semantics=("arbitrary",), vmem_limit_bytes=VMEM_LIMIT),
        name="combine_final",
    )(ys2, ys2, ys2, ys2, x1, gates, mod, final_g.reshape(1, d))


def kernel(x, c, mod_w, mod_b, norm_g, conf_w1, conf_b1, conf_dw, conf_dw_b, conf_ln_g, conf_ln_b, conf_w2, conf_b2, sc_w_in, sc_conv, sc_w_out, router_w, router_b, exp_w1, exp_b1, exp_w2, exp_b2, final_g):
    mods = _modulation(c, mod_w, mod_b).reshape(DEPTH, BATCH, N_MOD, D_MODEL)
    b1_grouped = _group_glu_columns(exp_b1).reshape(DEPTH, N_EXPERTS, 1, 2 * D_EXPERT)
    b2_rows = exp_b2.reshape(DEPTH, N_EXPERTS, 1, D_MODEL)
    assert DEPTH == 2, "layer 0's MoE residual is applied inside layer 1's mixer kernel"
    pending = None
    for i in range(DEPTH):
        mod = mods[i]
        j = i // 2
        if i % 2 == 0:
            x1 = _conformer_mixer(x, mod, norm_g[i, 0], conf_w1[j], conf_b1[j], conf_dw[j], conf_dw_b[j],
                                  conf_ln_g[j], conf_ln_b[j], conf_w2[j], conf_b2[j])
        else:
            x1 = _shortconv_mixer(pending, mod, norm_g[i, 0], sc_w_in[j], sc_conv[j], sc_w_out[j])
        x1 = x1.reshape(TOKENS, D_MODEL)
        h_rows, idx_rank, gates, counts = _route(x1, mod, norm_g[i, 1], router_w[i], router_b[i])
        dest, block_e, n_used, next_used, pad_lo, pad_hi = _routing_tables(idx_rank, counts)
        xs, row_slot = _dispatch(dest, pad_lo, pad_hi, h_rows)
        ys = _experts(i, block_e, n_used, next_used, row_slot, xs, exp_w1, b1_grouped, exp_w2, b2_rows)
        pending = (ys, x1, gates, mod)
    out = _combine_final(*pending, final_g)
    return out.reshape(BATCH, SEQ, D_MODEL)
```

```python
import functools

import jax
import jax.numpy as jnp
from jax import lax
from jax.experimental import pallas as pl
from jax.experimental.pallas import tpu as pltpu

D_MODEL = 1024
BATCH = 8
SEQ = 2048
DEPTH = 2
TOKENS = BATCH * SEQ
CONF_KERNEL = 31
SHORT_KERNEL = 3
N_EXPERTS = 32
TOP_K = 4
D_EXPERT = D_MODEL
SWIGLU_ALPHA = 1.702
SWIGLU_LIMIT = 7.0
NORM_EPS = 1e-5
N_MOD = 6

LANES = 128
SUBLANES = 8
MXU_COLS = 256
VMEM_LIMIT = 56 * 1024 * 1024

HALF_D = D_MODEL // 2
ROW_WORDS = HALF_D // LANES
MIX_TS = 256
CONF_HALO = 32
SHORT_HALO = 8
CONV_RC = 64
ROUTE_TM = 512
DISP_TM = 2048
DISP_UNROLL = 32
COMB_TM = 256
EXPERT_BM = 256
N_ROWS = TOKENS * TOP_K + N_EXPERTS * EXPERT_BM
N_BLOCKS = N_ROWS // EXPERT_BM
RING = 3
PAD_DUMP0 = TOKENS * TOP_K + RING * EXPERT_BM
PAD_DUMP_BLOCKS = 4
OUT_SLOTS = PAD_DUMP0 + PAD_DUMP_BLOCKS * EXPERT_BM

_F32 = jnp.float32
_BF16 = jnp.bfloat16
_U32 = jnp.uint32
_HI_MASK = 0xFFFF0000


def _rms_mod(x, g, scale, shift):
    y = x * lax.rsqrt(jnp.mean(x * x, axis=-1, keepdims=True) + NORM_EPS)
    return (y * g) * (1.0 + scale) + shift


def _sigmoid(x):
    return 1.0 / (1.0 + jnp.exp(-x))


def _pack_rows(v):
    bits = lambda a: lax.bitcast_convert_type(a.astype(_BF16).astype(_F32), _U32)
    return (bits(v[:, HALF_D:]) & _U32(_HI_MASK)) | (bits(v[:, :HALF_D]) >> 16)


def _store_packed(ref, lead, v):
    rows = v.shape[0]
    words = _pack_rows(v)
    for j in range(ROW_WORDS):
        ref[lead + (pl.ds(j, rows, stride=ROW_WORDS), slice(None))] = words[:, j * LANES:(j + 1) * LANES]


def _load_packed(ref, lead, rows, dtype):
    lo, hi = [], []
    for j in range(ROW_WORDS):
        w = ref[lead + (pl.ds(j, rows, stride=ROW_WORDS), slice(None))]
        lo.append(lax.bitcast_convert_type(w << 16, _F32).astype(dtype))
        hi.append(lax.bitcast_convert_type(w & _U32(_HI_MASK), _F32).astype(dtype))
    return jnp.concatenate(lo + hi, axis=1)


def _mod_kernel(c_ref, w_ref, b_ref, o_ref):
    c = c_ref[...]
    c_act = c * _sigmoid(c)
    o_ref[0] = jnp.dot(c_act, w_ref[0], preferred_element_type=_F32,
                       precision=lax.Precision.HIGHEST) + b_ref[0]


def _modulation(c, mod_w, mod_b):
    tn = 1536
    n = N_MOD * D_MODEL
    return pl.pallas_call(
        _mod_kernel,
        grid=(DEPTH, n // tn),
        in_specs=[
            pl.BlockSpec((BATCH, D_MODEL), lambda i, j: (0, 0)),
            pl.BlockSpec((1, D_MODEL, tn), lambda i, j: (i, 0, j)),
            pl.BlockSpec((1, 1, tn), lambda i, j: (i, 0, j)),
        ],
        out_specs=pl.BlockSpec((1, BATCH, tn), lambda i, j: (i, 0, j)),
        out_shape=jax.ShapeDtypeStruct((DEPTH, BATCH, n), _F32),
        compiler_params=pltpu.CompilerParams(vmem_limit_bytes=VMEM_LIMIT),
        name="modulation",
    )(c, mod_w, mod_b.reshape(DEPTH, 1, n))


def _causal_taps(cbuf, shifted, w_ref, bias_row, out_ref, ntaps, halo, ts):
    off0 = halo - (ntaps - 1)
    for c in range(D_MODEL // LANES):
        cs = slice(c * LANES, (c + 1) * LANES)
        if shifted is not None:
            rows = cbuf.shape[0]
            whole = cbuf[:, cs]
            for r in range(1, SUBLANES):
                shifted[r - 1, :, cs] = pltpu.roll(whole, rows - r, axis=0)
        for rc in range(ts // CONV_RC):
            r0 = rc * CONV_RC
            acc = None
            for k in range(ntaps):
                s = off0 + k
                if shifted is None or s % SUBLANES == 0:
                    window = cbuf[r0 + s:r0 + s + CONV_RC, cs]
                else:
                    a = s - s % SUBLANES
                    window = shifted[s % SUBLANES - 1, r0 + a:r0 + a + CONV_RC, cs]
                term = w_ref[k:k + 1, cs] * window
                acc = term if acc is None else acc + term
            if bias_row is not None:
                acc = acc + bias_row[:, cs]
            out_ref[r0:r0 + CONV_RC, cs] = acc


def _conformer_kernel(x_ref, mod_ref, ng_ref, w1_ref, b1_ref, dw_ref, dwb_ref, lng_ref, lnb_ref,
                      w2_ref, b2_ref, o_ref, cbuf, shifted, vbuf):
    ts = MIX_TS

    @pl.when(pl.program_id(1) == 0)
    def _():
        cbuf[0:CONF_HALO, :] = jnp.zeros((CONF_HALO, D_MODEL), _F32)
        cbuf[CONF_HALO + ts:, :] = jnp.zeros((SUBLANES, D_MODEL), _F32)

    x = x_ref[0]
    m = mod_ref[0]
    h = _rms_mod(x, ng_ref[...], m[1:2], m[0:1])
    u = jnp.dot(h.astype(_BF16), w1_ref[...], preferred_element_type=_F32) + b1_ref[...]
    cbuf[CONF_HALO:CONF_HALO + ts, :] = u[:, :D_MODEL] * _sigmoid(u[:, D_MODEL:])
    _causal_taps(cbuf, shifted, dw_ref, dwb_ref[...], vbuf, CONF_KERNEL, CONF_HALO, ts)
    cbuf[0:CONF_HALO, :] = cbuf[ts:ts + CONF_HALO, :]
    v = vbuf[...]
    mu = jnp.mean(v, axis=-1, keepdims=True)
    vc = v - mu
    var = jnp.mean(vc * vc, axis=-1, keepdims=True)
    y = vc * lax.rsqrt(var + NORM_EPS) * lng_ref[...] + lnb_ref[...]
    y = y * _sigmoid(y)
    mix = jnp.dot(y.astype(_BF16), w2_ref[...], preferred_element_type=_F32) + b2_ref[...]
    o_ref[0] = x + m[2:3] * mix


def _conformer_mixer(x, mod, norm_g, w1, b1, dw, dw_b, ln_g, ln_b, w2, b2):
    d = D_MODEL
    row = lambda a: a.reshape(1, -1)
    const = lambda shape: pl.BlockSpec(shape, lambda b, s: (0,) * len(shape))
    return pl.pallas_call(
        _conformer_kernel,
        grid=(BATCH, SEQ // MIX_TS),
        in_specs=[
            pl.BlockSpec((1, MIX_TS, d), lambda b, s: (b, s, 0)),
            pl.BlockSpec((1, N_MOD, d), lambda b, s: (b, 0, 0)),
            const((1, d)), const((d, 2 * d)), const((1, 2 * d)), const((CONF_KERNEL, d)),
            const((1, d)), const((1, d)), const((1, d)), const((d, d)), const((1, d)),
        ],
        out_specs=pl.BlockSpec((1, MIX_TS, d), lambda b, s: (b, s, 0)),
        out_shape=jax.ShapeDtypeStruct((BATCH, SEQ, d), _F32),
        scratch_shapes=[pltpu.VMEM((CONF_HALO + MIX_TS + SUBLANES, d), _F32),
                        pltpu.VMEM((SUBLANES - 1, CONF_HALO + MIX_TS + SUBLANES, d), _F32),
                        pltpu.VMEM((MIX_TS, d), _F32)],
        compiler_params=pltpu.CompilerParams(
            dimension_semantics=("arbitrary", "arbitrary"), vmem_limit_bytes=VMEM_LIMIT),
        name="conformer_mixer",
    )(x, mod, row(norm_g), w1.astype(_BF16), row(b1), dw, row(dw_b), row(ln_g), row(ln_b),
      w2.astype(_BF16), row(b2))


def _moe_residual(y_refs, gate_ref, x1, gate2, rows):
    gates = gate_ref[...]
    y = None
    for k, yk_ref in enumerate(y_refs):
        term = gates[:, k:k + 1] * _load_packed(yk_ref, (), rows, _F32)
        y = term if y is None else y + term
    return x1 + gate2 * y


def _shortconv_kernel(y0_ref, y1_ref, y2_ref, y3_ref, gate_ref, pmod_ref, x_ref, mod_ref, ng_ref, win_ref, cw_ref,
                      wout_ref, o_ref, cbuf, vbuf):
    ts = MIX_TS
    d = D_MODEL

    @pl.when(pl.program_id(1) == 0)
    def _():
        cbuf[0:SHORT_HALO, :] = jnp.zeros((SHORT_HALO, d), _F32)

    x = _moe_residual((y0_ref, y1_ref, y2_ref, y3_ref), gate_ref, x_ref[0], pmod_ref[0][5:6], ts)
    m = mod_ref[0]
    h = _rms_mod(x, ng_ref[...], m[1:2], m[0:1])
    z = jnp.dot(h.astype(_BF16), win_ref[...], preferred_element_type=_F32)
    cbuf[SHORT_HALO:SHORT_HALO + ts, :] = z[:, d:2 * d] * z[:, 2 * d:]
    _causal_taps(cbuf, None, cw_ref, None, vbuf, SHORT_KERNEL, SHORT_HALO, ts)
    cbuf[0:SHORT_HALO, :] = cbuf[ts:ts + SHORT_HALO, :]
    y = z[:, :d] * vbuf[...]
    mix = jnp.dot(y.astype(_BF16), wout_ref[...], preferred_element_type=_F32)
    o_ref[0] = x + m[2:3] * mix


def _shortconv_mixer(pending, mod, norm_g, w_in, conv_w, w_out):
    ys, x_prev, gates, prev_mod = pending
    d = D_MODEL
    tiles_per_batch = SEQ // MIX_TS
    tiles = TOKENS // MIX_TS
    ys2 = ys.reshape(OUT_SLOTS * ROW_WORDS, LANES)
    const = lambda shape: pl.BlockSpec(shape, lambda b, s: (0,) * len(shape))
    slot_spec = lambda k: pl.BlockSpec((MIX_TS * ROW_WORDS, LANES),
                                       lambda b, s: (k * tiles + b * tiles_per_batch + s, 0))
    return pl.pallas_call(
        _shortconv_kernel,
        grid=(BATCH, SEQ // MIX_TS),
        in_specs=[
            slot_spec(0), slot_spec(1), slot_spec(2), slot_spec(3),
            pl.BlockSpec((MIX_TS, TOP_K), lambda b, s: (b * tiles_per_batch + s, 0)),
            pl.BlockSpec((1, N_MOD, d), lambda b, s: (b, 0, 0)),
            pl.BlockSpec((1, MIX_TS, d), lambda b, s: (b, s, 0)),
            pl.BlockSpec((1, N_MOD, d), lambda b, s: (b, 0, 0)),
            const((1, d)), const((d, 3 * d)), const((SHORT_KERNEL, d)), const((d, d)),
        ],
        out_specs=pl.BlockSpec((1, MIX_TS, d), lambda b, s: (b, s, 0)),
        out_shape=jax.ShapeDtypeStruct((BATCH, SEQ, d), _F32),
        scratch_shapes=[pltpu.VMEM((SHORT_HALO + MIX_TS, d), _F32), pltpu.VMEM((MIX_TS, d), _F32)],
        compiler_params=pltpu.CompilerParams(
            dimension_semantics=("arbitrary", "arbitrary"), vmem_limit_bytes=VMEM_LIMIT),
        name="shortconv_mixer",
    )(ys2, ys2, ys2, ys2, gates, prev_mod, x_prev.reshape(BATCH, SEQ, d), mod, norm_g.reshape(1, d),
      w_in.astype(_BF16), conv_w, w_out.astype(_BF16))


def _route_kernel(x_ref, mod_ref, ng_ref, rw_ref, rb_ref, h_ref, ir_ref, gate_ref, cnt_ref):
    tm = ROUTE_TM

    @pl.when(pl.program_id(0) == 0)
    def _():
        cnt_ref[...] = jnp.zeros((1, N_EXPERTS), _F32)

    x = x_ref[...]
    m = mod_ref[0]
    h = _rms_mod(x, ng_ref[...], m[4:5], m[3:4])
    _store_packed(h_ref, (), h)

    h_hi = h.astype(_BF16)
    h_lo = (h - h_hi.astype(_F32)).astype(_BF16)
    both = jnp.dot(h_hi, rw_ref[...], preferred_element_type=_F32)
    logits = (both[:, :N_EXPERTS] + both[:, N_EXPERTS:]
              + jnp.dot(h_lo, rw_ref[:, :N_EXPERTS], preferred_element_type=_F32)) + rb_ref[...]
    lane = lax.broadcasted_iota(jnp.int32, (tm, N_EXPERTS), 1).astype(_F32)
    work = logits
    vals, idxs, hots = [], [], []
    for _ in range(TOP_K):
        top = jnp.max(work, axis=1, keepdims=True)
        pick = jnp.min(jnp.where(work == top, lane, float(N_EXPERTS)), axis=1, keepdims=True)
        hot = lane == pick
        vals.append(top)
        idxs.append(pick)
        hots.append(hot)
        work = jnp.where(hot, -jnp.inf, work)
    exps = [jnp.exp(v - vals[0]) for v in vals]
    denom = exps[0] + exps[1] + exps[2] + exps[3]

    multi = (hots[0] | hots[1] | hots[2] | hots[3]).astype(_BF16)
    r_i = lax.broadcasted_iota(jnp.int32, (tm, tm), 0)
    c_i = lax.broadcasted_iota(jnp.int32, (tm, tm), 1)
    before = (c_i < r_i).astype(_BF16)
    pos = cnt_ref[...] + jnp.dot(before, multi, preferred_element_type=_F32)
    cnt_ref[...] = cnt_ref[...] + jnp.sum(multi.astype(_F32), axis=0, keepdims=True)

    col = lax.broadcasted_iota(jnp.int32, (tm, LANES), 1)
    gate_col = lax.broadcasted_iota(jnp.int32, (tm, TOP_K), 1)
    ints = jnp.zeros((tm, LANES), _F32)
    gate_out = jnp.zeros((tm, TOP_K), _F32)
    for k in range(TOP_K):
        rank_k = jnp.sum(jnp.where(hots[k], pos, 0.0), axis=1, keepdims=True)
        ints = jnp.where(col == k, idxs[k], ints)
        ints = jnp.where(col == TOP_K + k, rank_k, ints)
        gate_out = jnp.where(gate_col == k, exps[k] / denom, gate_out)
    sel = (lax.broadcasted_iota(jnp.int32, (2 * TOP_K, LANES), 0)
           == lax.broadcasted_iota(jnp.int32, (2 * TOP_K, LANES), 1)).astype(_BF16)
    high = jnp.floor(ints * (1.0 / LANES))
    low = ints - high * LANES
    nt = (((1,), (1,)), ((), ()))
    rows = (lax.dot_general(sel, high.astype(_BF16), nt, preferred_element_type=_F32) * LANES
            + lax.dot_general(sel, low.astype(_BF16), nt, preferred_element_type=_F32))
    ir_ref[...] = rows.astype(jnp.int32)
    gate_ref[...] = gate_out


def _route(x1, mod, norm_g, router_w, router_b):
    d = D_MODEL
    tm = ROUTE_TM
    tiles_per_batch = SEQ // tm
    rw_hi = router_w.astype(_BF16)
    rw_lo = (router_w - rw_hi.astype(_F32)).astype(_BF16)
    return pl.pallas_call(
        _route_kernel,
        grid=(TOKENS // tm,),
        in_specs=[
            pl.BlockSpec((tm, d), lambda i: (i, 0)),
            pl.BlockSpec((1, N_MOD, d), lambda i: (i // tiles_per_batch, 0, 0)),
            pl.BlockSpec((1, d), lambda i: (0, 0)),
            pl.BlockSpec((d, 2 * N_EXPERTS), lambda i: (0, 0)),
            pl.BlockSpec((1, N_EXPERTS), lambda i: (0, 0)),
        ],
        out_specs=[
            pl.BlockSpec((tm * ROW_WORDS, LANES), lambda i: (i, 0)),
            pl.BlockSpec((2 * TOP_K, tm), lambda i: (0, i)),
            pl.BlockSpec((tm, TOP_K), lambda i: (i, 0)),
            pl.BlockSpec((1, N_EXPERTS), lambda i: (0, 0)),
        ],
        out_shape=[
            jax.ShapeDtypeStruct((TOKENS * ROW_WORDS, LANES), _U32),
            jax.ShapeDtypeStruct((2 * TOP_K, TOKENS), jnp.int32),
            jax.ShapeDtypeStruct((TOKENS, TOP_K), _F32),
            jax.ShapeDtypeStruct((1, N_EXPERTS), _F32),
        ],
        compiler_params=pltpu.CompilerParams(
            dimension_semantics=("arbitrary",), vmem_limit_bytes=VMEM_LIMIT),
        name="route",
    )(x1, mod, norm_g.reshape(1, d), jnp.concatenate([rw_hi, rw_lo], axis=1), router_b.reshape(1, N_EXPERTS))


def _dest_kernel(ps_ref, ir_ref, o_ref):
    idx = ir_ref[0:TOP_K, :]
    dest = ir_ref[TOP_K:2 * TOP_K, :] + EXPERT_BM
    for e in range(N_EXPERTS):
        dest = dest + jnp.where(idx == e, ps_ref[e], 0)
    o_ref[...] = dest


def _dest_rows(pad_starts, idx_rank):
    tn = 4096
    return pl.pallas_call(
        _dest_kernel,
        grid_spec=pltpu.PrefetchScalarGridSpec(
            num_scalar_prefetch=1,
            grid=(TOKENS // tn,),
            in_specs=[pl.BlockSpec((2 * TOP_K, tn), lambda i, ps: (0, i))],
            out_specs=pl.BlockSpec((TOP_K, tn), lambda i, ps: (0, i)),
        ),
        out_shape=jax.ShapeDtypeStruct((TOP_K, TOKENS), jnp.int32),
        name="dest_rows",
    )(pad_starts, idx_rank)


def _routing_tables(idx_rank, counts):
    counts = counts.reshape(N_EXPERTS).astype(jnp.int32)
    padded = (counts + EXPERT_BM - 1) // EXPERT_BM * EXPERT_BM
    pad_ends = jnp.cumsum(padded)
    pad_starts = pad_ends - padded
    dest = _dest_rows(pad_starts.astype(jnp.int32), idx_rank)
    block_start = jnp.arange(N_BLOCKS, dtype=jnp.int32) * EXPERT_BM
    block_e = jnp.sum((block_start[:, None] >= pad_ends[None, :]).astype(jnp.int32), axis=1)
    block_e = jnp.minimum(block_e, N_EXPERTS - 1).astype(jnp.int32)
    n_used = (pad_ends[-1:] // EXPERT_BM).astype(jnp.int32)
    ids = jnp.arange(N_EXPERTS, dtype=jnp.int32)
    later_used = (ids[None, :] > ids[:, None]) & (counts[None, :] > 0)
    next_used = jnp.min(jnp.where(later_used, ids[None, :], N_EXPERTS), axis=1).astype(jnp.int32)
    pad_lo = (pad_starts + counts + EXPERT_BM).astype(jnp.int32)
    pad_hi = (pad_ends + EXPERT_BM).astype(jnp.int32)
    dest = dest.reshape(TOKENS * TOP_K).astype(jnp.int32)
    return dest, block_e, n_used, next_used, pad_lo, pad_hi


def _vmem_row(buf, row):
    start = row * ROW_WORDS
    return buf.at[pl.ds(start if isinstance(row, int) else pl.multiple_of(start, ROW_WORDS), ROW_WORDS)]


def _dispatch_kernel(dest_ref, lo_ref, hi_ref, h_ref, xs_ref, rs_ref, zblk, sem, zsem):
    tm = DISP_TM
    base = pl.program_id(0) * tm

    def set_virtual(r, carry):
        rs_ref[r] = TOKENS * TOP_K + r
        return carry

    def set_dump(r, carry):
        rs_ref[r] = PAD_DUMP0 + (r & (PAD_DUMP_BLOCKS * EXPERT_BM - 1))
        return carry

    @pl.when(pl.program_id(0) == 0)
    def _():
        lax.fori_loop(0, EXPERT_BM, set_virtual, 0)

    def issue(g, carry):
        toks = [g * DISP_UNROLL + u for u in range(DISP_UNROLL)]
        dests = [[dest_ref[k * TOKENS + base + t] for k in range(TOP_K)] for t in toks]
        for t, row in zip(toks, dests):
            for k, d in enumerate(row):
                pltpu.make_async_copy(_vmem_row(h_ref, t), xs_ref.at[d], sem).start(priority=k % 2)
            for k, d in enumerate(row):
                rs_ref[d] = k * TOKENS + base + t
        return carry

    lax.fori_loop(0, tm // DISP_UNROLL, issue, 0)

    @pl.when(pl.program_id(0) == pl.num_programs(0) - 1)
    def _():
        zblk[...] = jnp.zeros((EXPERT_BM * ROW_WORDS, LANES), _U32)

        def pad_copy(r):
            return pltpu.make_async_copy(_vmem_row(zblk, 0), xs_ref.at[r], zsem)

        def tail_copy(b):
            return pltpu.make_async_copy(zblk.reshape(EXPERT_BM, ROW_WORDS, LANES),
                                         xs_ref.at[pl.ds(pl.multiple_of(b * EXPERT_BM, EXPERT_BM), EXPERT_BM)], zsem)

        def pad_row(r, carry):
            pad_copy(r).start()
            return set_dump(r, carry)

        def wait_rows(n_rows):
            pltpu.make_async_copy(xs_ref.at[pl.ds(0, n_rows)], xs_ref.at[pl.ds(0, n_rows)], zsem).wait()

        first_unused = hi_ref[N_EXPERTS - 1] // EXPERT_BM
        tail_copy(0).start()
        for e in range(N_EXPERTS):
            lax.fori_loop(lo_ref[e], hi_ref[e], pad_row, 0)
        lax.fori_loop(first_unused * EXPERT_BM, N_ROWS + EXPERT_BM, set_dump, 0)
        lax.fori_loop(first_unused, N_BLOCKS + 1, lambda b, c: (tail_copy(b).start(), c)[1], 0)
        for e in range(N_EXPERTS):
            n_pad = hi_ref[e] - lo_ref[e]
            for bit in range(EXPERT_BM.bit_length() - 1):
                pl.when((n_pad >> bit) & 1 == 1)(functools.partial(wait_rows, 1 << bit))
        lax.fori_loop(first_unused, N_BLOCKS + 1, lambda b, c: (tail_copy(b).wait(), c)[1], 0)
        tail_copy(0).wait()

    for _ in range(TOP_K):
        pltpu.make_async_copy(h_ref.reshape(tm, ROW_WORDS, LANES), xs_ref.at[pl.ds(0, tm)], sem).wait()


def _dispatch(dest, pad_lo, pad_hi, h_rows):
    tm = DISP_TM
    return pl.pallas_call(
        _dispatch_kernel,
        grid_spec=pltpu.PrefetchScalarGridSpec(
            num_scalar_prefetch=3,
            grid=(TOKENS // tm,),
            in_specs=[pl.BlockSpec((tm * ROW_WORDS, LANES), lambda i, *_: (i, 0))],
            out_specs=[pl.BlockSpec(memory_space=pl.ANY), pl.BlockSpec(memory_space=pltpu.SMEM)],
            scratch_shapes=[pltpu.VMEM((EXPERT_BM * ROW_WORDS, LANES), _U32),
                            pltpu.SemaphoreType.DMA, pltpu.SemaphoreType.DMA],
        ),
        out_shape=[jax.ShapeDtypeStruct((N_ROWS + EXPERT_BM, ROW_WORDS, LANES), _U32),
                   jax.ShapeDtypeStruct((N_ROWS + EXPERT_BM,), jnp.int32)],
        compiler_params=pltpu.CompilerParams(
            dimension_semantics=("arbitrary",), vmem_limit_bytes=VMEM_LIMIT),
        name="dispatch",
    )(dest, pad_lo, pad_hi, h_rows)


def _expert_kernel(be_ref, nu_ref, nx_ref, rs_ref, xs_ref, w1_hbm, b1_ref, w2_hbm, b2_ref, ys_ref,
                   w1f, w2f, w1s, w2s, obuf, wsem, osem, isem, nsw, *, layer):
    bm = EXPERT_BM
    b = pl.program_id(0)
    e = be_ref[b]
    n_used = nu_ref[0]
    half = MXU_COLS // 2

    def fetch(expert, slot):
        return (pltpu.make_async_copy(w1_hbm.at[layer, expert], w1f.at[slot], wsem.at[0, slot]),
                pltpu.make_async_copy(w2_hbm.at[layer, expert], w2f.at[slot], wsem.at[1, slot]))

    def scatter_start(step, slot):
        for i in range(bm):
            pltpu.make_async_copy(_vmem_row(obuf.at[slot], i), ys_ref.at[rs_ref[step * bm + i]],
                                  osem.at[slot]).start(priority=i % 2)

    def block_view(slot):
        return obuf.at[slot].reshape(bm, ROW_WORDS, LANES)

    def scatter_wait(slot):
        pltpu.make_async_copy(block_view(slot), ys_ref.at[pl.ds(0, bm)], osem.at[slot]).wait()

    @pl.when(b == 0)
    def _():
        nsw[0] = 0
        obuf[RING - 1] = jnp.zeros((bm * ROW_WORDS, LANES), _U32)
        for blk in range(PAD_DUMP_BLOCKS):
            pltpu.make_async_copy(block_view(RING - 1), ys_ref.at[pl.ds(PAD_DUMP0 + blk * bm, bm)], isem).start()
        for blk in range(PAD_DUMP_BLOCKS):
            pltpu.make_async_copy(block_view(RING - 1), ys_ref.at[pl.ds(PAD_DUMP0 + blk * bm, bm)], isem).wait()
        for virtual in range(2, RING + 1):
            pltpu.make_async_copy(
                block_view(RING - 1), ys_ref.at[pl.ds(TOKENS * TOP_K + (virtual - 1) * bm, bm)],
                osem.at[RING - virtual]).start()
        for copy in fetch(e, 0):
            copy.start()

    @pl.when(b == n_used)
    def _():
        scatter_start(b, (b + RING - 1) % RING)
        for slot in range(RING):
            scatter_wait(slot)

    @pl.when(b < n_used)
    def _():
        out_slot = b % RING
        scatter_wait(out_slot)

        @pl.when(jnp.logical_or(b == 0, be_ref[jnp.maximum(b - 1, 0)] != e))
        def _():
            slot = nsw[0] % 2
            for copy in fetch(e, slot):
                copy.wait()
            src = lax.broadcasted_iota(jnp.int32, (MXU_COLS, MXU_COLS), 0)
            dst = lax.broadcasted_iota(jnp.int32, (MXU_COLS, MXU_COLS), 1)
            want = jnp.where(dst < half, 2 * dst, 2 * (dst - half) + 1)
            perm = (src == want).astype(_BF16)
            for blk in range(2 * D_EXPERT // MXU_COLS):
                cs = slice(blk * MXU_COLS, (blk + 1) * MXU_COLS)
                w1s[:, cs] = jnp.dot(w1f[slot, :, cs].astype(_BF16), perm,
                                     preferred_element_type=_F32).astype(_BF16)
            w2s[...] = w2f[slot].astype(_BF16)
            nsw[0] = nsw[0] + 1
            nxt = nx_ref[e]

            @pl.when(nxt < N_EXPERTS)
            def _():
                for copy in fetch(nxt, 1 - slot):
                    copy.start()

        scatter_start(b, (b + RING - 1) % RING)
        x = _load_packed(xs_ref, (), bm, _BF16)
        u = jnp.dot(x, w1s[...], preferred_element_type=_F32) + b1_ref[0, e]
        acts = []
        for blk in range(2 * D_EXPERT // MXU_COLS):
            glu = jnp.minimum(u[:, blk * MXU_COLS:blk * MXU_COLS + half], SWIGLU_LIMIT)
            lin = jnp.clip(u[:, blk * MXU_COLS + half:(blk + 1) * MXU_COLS], -SWIGLU_LIMIT, SWIGLU_LIMIT)
            acts.append((glu * _sigmoid(SWIGLU_ALPHA * glu) * (lin + 1.0)).astype(_BF16))
        hidden = jnp.concatenate(acts, axis=1)
        y = jnp.dot(hidden, w2s[...], preferred_element_type=_F32) + b2_ref[0, e]
        _store_packed(obuf, (out_slot,), y)


def _experts(layer, block_e, n_used, next_used, row_slot, xs, w1, b1_grouped, w2, b2):
    bm = EXPERT_BM
    d, f = D_MODEL, D_EXPERT
    blk = lambda b, be, nu, nx, rs: jnp.minimum(b, nu[0] - 1)
    return pl.pallas_call(
        functools.partial(_expert_kernel, layer=layer),
        grid_spec=pltpu.PrefetchScalarGridSpec(
            num_scalar_prefetch=4,
            grid=(N_BLOCKS,),
            in_specs=[
                pl.BlockSpec((bm * ROW_WORDS, LANES), lambda *a: (blk(*a) + 1, 0)),
                pl.BlockSpec(memory_space=pl.ANY),
                pl.BlockSpec((1, N_EXPERTS, 1, 2 * f), lambda *a: (layer, 0, 0, 0)),
                pl.BlockSpec(memory_space=pl.ANY),
                pl.BlockSpec((1, N_EXPERTS, 1, d), lambda *a: (layer, 0, 0, 0)),
            ],
            out_specs=pl.BlockSpec(memory_space=pl.ANY),
            scratch_shapes=[
                pltpu.VMEM((2, d, 2 * f), _F32), pltpu.VMEM((2, f, d), _F32),
                pltpu.VMEM((d, 2 * f), _BF16), pltpu.VMEM((f, d), _BF16),
                pltpu.VMEM((RING, bm * ROW_WORDS, LANES), _U32),
                pltpu.SemaphoreType.DMA((2, 2)), pltpu.SemaphoreType.DMA((RING,)), pltpu.SemaphoreType.DMA,
                pltpu.SMEM((1,), jnp.int32),
            ],
        ),
        out_shape=jax.ShapeDtypeStruct((OUT_SLOTS, ROW_WORDS, LANES), _U32),
        compiler_params=pltpu.CompilerParams(
            dimension_semantics=("arbitrary",), vmem_limit_bytes=VMEM_LIMIT),
        name="experts",
    )(block_e, n_used, next_used, row_slot, xs.reshape((N_ROWS + EXPERT_BM) * ROW_WORDS, LANES), w1, b1_grouped,
      w2, b2)


def _group_glu_columns(b1):
    half = MXU_COLS // 2
    j = jnp.arange(MXU_COLS)
    want = jnp.where(j < half, 2 * j, 2 * (j - half) + 1)
    cols = (jnp.arange(2 * D_EXPERT // MXU_COLS)[:, None] * MXU_COLS + want[None, :]).reshape(-1)
    return b1[..., cols]


def _combine_kernel(y0_ref, y1_ref, y2_ref, y3_ref, x_ref, gate_ref, mod_ref, fg_ref, o_ref):
    out = _moe_residual((y0_ref, y1_ref, y2_ref, y3_ref), gate_ref, x_ref[...], mod_ref[0][5:6], COMB_TM)
    o_ref[...] = out * lax.rsqrt(jnp.mean(out * out, axis=-1, keepdims=True) + NORM_EPS) * fg_ref[...]


def _combine_final(ys, x1, gates, mod, final_g):
    tm = COMB_TM
    d = D_MODEL
    tiles = TOKENS // tm
    tiles_per_batch = SEQ // tm
    ys2 = ys.reshape(OUT_SLOTS * ROW_WORDS, LANES)
    slot_spec = lambda k: pl.BlockSpec((tm * ROW_WORDS, LANES), lambda i: (k * tiles + i, 0))
    return pl.pallas_call(
        _combine_kernel,
        grid=(tiles,),
        in_specs=[
            slot_spec(0), slot_spec(1), slot_spec(2), slot_spec(3),
            pl.BlockSpec((tm, d), lambda i: (i, 0)),
            pl.BlockSpec((tm, TOP_K), lambda i: (i, 0)),
            pl.BlockSpec((1, N_MOD, d), lambda i: (i // tiles_per_batch, 0, 0)),
            pl.BlockSpec((1, d), lambda i: (0, 0)),
        ],
        out_specs=pl.BlockSpec((tm, d), lambda i: (i, 0)),
        out_shape=jax.ShapeDtypeStruct((TOKENS, d), _F32),
        compiler_params=pltpu.CompilerParams(
            dimension_semantics=("arbitrary",), vmem_limit_bytes=VMEM_LIMIT),
        name="combine_final",
    )(ys2, ys2, ys2, ys2, x1, gates, mod, final_g.reshape(1, d))


def kernel(x, c, mod_w, mod_b, norm_g, conf_w1, conf_b1, conf_dw, conf_dw_b, conf_ln_g, conf_ln_b, conf_w2, conf_b2, sc_w_in, sc_conv, sc_w_out, router_w, router_b, exp_w1, exp_b1, exp_w2, exp_b2, final_g):
    mods = _modulation(c, mod_w, mod_b).reshape(DEPTH, BATCH, N_MOD, D_MODEL)
    b1_grouped = _group_glu_columns(exp_b1).reshape(DEPTH, N_EXPERTS, 1, 2 * D_EXPERT)
    b2_rows = exp_b2.reshape(DEPTH, N_EXPERTS, 1, D_MODEL)
    assert DEPTH == 2, "layer 0's MoE residual is applied inside layer 1's mixer kernel"
    pending = None
    for i in range(DEPTH):
        mod = mods[i]
        j = i // 2
        if i % 2 == 0:
            x1 = _conformer_mixer(x, mod, norm_g[i, 0], conf_w1[j], conf_b1[j], conf_dw[j], conf_dw_b[j],
                                  conf_ln_g[j], conf_ln_b[j], conf_w2[j], conf_b2[j])
        else:
            x1 = _shortconv_mixer(pending, mod, norm_g[i, 0], sc_w_in[j], sc_conv[j], sc_w_out[j])
        x1 = x1.reshape(TOKENS, D_MODEL)
        h_rows, idx_rank, gates, counts = _route(x1, mod, norm_g[i, 1], router_w[i], router_b[i])
        dest, block_e, n_used, next_used, pad_lo, pad_hi = _routing_tables(idx_rank, counts)
        xs, row_slot = _dispatch(dest, pad_lo, pad_hi, h_rows)
        ys = _experts(i, block_e, n_used, next_used, row_slot, xs, exp_w1, b1_grouped, exp_w2, b2_rows)
        pending = (ys, x1, gates, mod)
    out = _combine_final(*pending, final_g)
    return out.reshape(BATCH, SEQ, D_MODEL)
```

```python
import functools

import jax
import jax.numpy as jnp
from jax import lax
from jax.experimental import pallas as pl
from jax.experimental.pallas import tpu as pltpu

D_MODEL = 1024
BATCH = 8
SEQ = 2048
DEPTH = 2
TOKENS = BATCH * SEQ
CONF_KERNEL = 31
SHORT_KERNEL = 3
N_EXPERTS = 32
TOP_K = 4
D_EXPERT = D_MODEL
SWIGLU_ALPHA = 1.702
SWIGLU_LIMIT = 7.0
NORM_EPS = 1e-5
N_MOD = 6

LANES = 128
SUBLANES = 8
MXU_COLS = 256
VMEM_LIMIT = 56 * 1024 * 1024

HALF_D = D_MODEL // 2
ROW_WORDS = HALF_D // LANES
MIX_TS = 256
CONF_HALO = 32
SHORT_HALO = 8
CONV_RC = 64
ROUTE_TM = 512
DISP_TM = 2048
DISP_UNROLL = 64
COMB_TM = 256
EXPERT_BM = 256
N_ROWS = TOKENS * TOP_K + N_EXPERTS * EXPERT_BM
N_BLOCKS = N_ROWS // EXPERT_BM
RING = 3
PAD_DUMP0 = TOKENS * TOP_K + RING * EXPERT_BM
PAD_DUMP_BLOCKS = 4
OUT_SLOTS = PAD_DUMP0 + PAD_DUMP_BLOCKS * EXPERT_BM

_F32 = jnp.float32
_BF16 = jnp.bfloat16
_U32 = jnp.uint32
_HI_MASK = 0xFFFF0000


def _rms_mod(x, g, scale, shift):
    y = x * lax.rsqrt(jnp.mean(x * x, axis=-1, keepdims=True) + NORM_EPS)
    return (y * g) * (1.0 + scale) + shift


def _sigmoid(x):
    return 1.0 / (1.0 + jnp.exp(-x))


def _pack_rows(v):
    bits = lambda a: lax.bitcast_convert_type(a.astype(_BF16).astype(_F32), _U32)
    return (bits(v[:, HALF_D:]) & _U32(_HI_MASK)) | (bits(v[:, :HALF_D]) >> 16)


def _store_packed(ref, lead, v):
    rows = v.shape[0]
    words = _pack_rows(v)
    for j in range(ROW_WORDS):
        ref[lead + (pl.ds(j, rows, stride=ROW_WORDS), slice(None))] = words[:, j * LANES:(j + 1) * LANES]


def _load_packed(ref, lead, rows, dtype):
    lo, hi = [], []
    for j in range(ROW_WORDS):
        w = ref[lead + (pl.ds(j, rows, stride=ROW_WORDS), slice(None))]
        lo.append(lax.bitcast_convert_type(w << 16, _F32).astype(dtype))
        hi.append(lax.bitcast_convert_type(w & _U32(_HI_MASK), _F32).astype(dtype))
    return jnp.concatenate(lo + hi, axis=1)


def _mod_kernel(c_ref, w_ref, b_ref, o_ref):
    c = c_ref[...]
    c_act = c * _sigmoid(c)
    c_hi = c_act.astype(_BF16)
    c_lo = (c_act - c_hi.astype(_F32)).astype(_BF16)
    w = w_ref[0]
    w_hi = w.astype(_BF16)
    w_lo = (w - w_hi.astype(_F32)).astype(_BF16)
    by_hi = jnp.dot(jnp.concatenate([c_hi, c_lo], axis=0), w_hi, preferred_element_type=_F32)
    o_ref[0] = (by_hi[:BATCH] + by_hi[BATCH:]
                + jnp.dot(c_hi, w_lo, preferred_element_type=_F32)) + b_ref[0]


def _modulation(c, mod_w, mod_b):
    tn = 1536
    n = N_MOD * D_MODEL
    return pl.pallas_call(
        _mod_kernel,
        grid=(DEPTH, n // tn),
        in_specs=[
            pl.BlockSpec((BATCH, D_MODEL), lambda i, j: (0, 0)),
            pl.BlockSpec((1, D_MODEL, tn), lambda i, j: (i, 0, j)),
            pl.BlockSpec((1, 1, tn), lambda i, j: (i, 0, j)),
        ],
        out_specs=pl.BlockSpec((1, BATCH, tn), lambda i, j: (i, 0, j)),
        out_shape=jax.ShapeDtypeStruct((DEPTH, BATCH, n), _F32),
        compiler_params=pltpu.CompilerParams(vmem_limit_bytes=VMEM_LIMIT),
        name="modulation",
    )(c, mod_w, mod_b.reshape(DEPTH, 1, n))


def _causal_taps(cbuf, shifted, w_ref, bias_row, out_ref, ntaps, halo, ts):
    off0 = halo - (ntaps - 1)
    for c in range(D_MODEL // LANES):
        cs = slice(c * LANES, (c + 1) * LANES)
        if shifted is not None:
            rows = cbuf.shape[0]
            whole = cbuf[:, cs]
            for r in range(1, SUBLANES):
                shifted[r - 1, :, cs] = pltpu.roll(whole, rows - r, axis=0)
        for rc in range(ts // CONV_RC):
            r0 = rc * CONV_RC
            acc = None
            for k in range(ntaps):
                s = off0 + k
                if shifted is None or s % SUBLANES == 0:
                    window = cbuf[r0 + s:r0 + s + CONV_RC, cs]
                else:
                    a = s - s % SUBLANES
                    window = shifted[s % SUBLANES - 1, r0 + a:r0 + a + CONV_RC, cs]
                term = w_ref[k:k + 1, cs] * window
                acc = term if acc is None else acc + term
            if bias_row is not None:
                acc = acc + bias_row[:, cs]
            out_ref[r0:r0 + CONV_RC, cs] = acc


def _conformer_kernel(x_ref, mod_ref, ng_ref, w1_ref, b1_ref, dw_ref, dwb_ref, lng_ref, lnb_ref,
                      w2_ref, b2_ref, o_ref, cbuf, shifted, vbuf):
    ts = MIX_TS

    @pl.when(pl.program_id(1) == 0)
    def _():
        cbuf[0:CONF_HALO, :] = jnp.zeros((CONF_HALO, D_MODEL), _F32)
        cbuf[CONF_HALO + ts:, :] = jnp.zeros((SUBLANES, D_MODEL), _F32)

    x = x_ref[0]
    m = mod_ref[0]
    h = _rms_mod(x, ng_ref[...], m[1:2], m[0:1])
    u = jnp.dot(h.astype(_BF16), w1_ref[...], preferred_element_type=_F32) + b1_ref[...]
    cbuf[CONF_HALO:CONF_HALO + ts, :] = u[:, :D_MODEL] * _sigmoid(u[:, D_MODEL:])
    _causal_taps(cbuf, shifted, dw_ref, dwb_ref[...], vbuf, CONF_KERNEL, CONF_HALO, ts)
    cbuf[0:CONF_HALO, :] = cbuf[ts:ts + CONF_HALO, :]
    v = vbuf[...]
    mu = jnp.mean(v, axis=-1, keepdims=True)
    vc = v - mu
    var = jnp.mean(vc * vc, axis=-1, keepdims=True)
    y = vc * lax.rsqrt(var + NORM_EPS) * lng_ref[...] + lnb_ref[...]
    y = y * _sigmoid(y)
    mix = jnp.dot(y.astype(_BF16), w2_ref[...], preferred_element_type=_F32) + b2_ref[...]
    o_ref[0] = x + m[2:3] * mix


def _conformer_mixer(x, mod, norm_g, w1, b1, dw, dw_b, ln_g, ln_b, w2, b2):
    d = D_MODEL
    row = lambda a: a.reshape(1, -1)
    const = lambda shape: pl.BlockSpec(shape, lambda b, s: (0,) * len(shape))
    return pl.pallas_call(
        _conformer_kernel,
        grid=(BATCH, SEQ // MIX_TS),
        in_specs=[
            pl.BlockSpec((1, MIX_TS, d), lambda b, s: (b, s, 0)),
            pl.BlockSpec((1, N_MOD, d), lambda b, s: (b, 0, 0)),
            const((1, d)), const((d, 2 * d)), const((1, 2 * d)), const((CONF_KERNEL, d)),
            const((1, d)), const((1, d)), const((1, d)), const((d, d)), const((1, d)),
        ],
        out_specs=pl.BlockSpec((1, MIX_TS, d), lambda b, s: (b, s, 0)),
        out_shape=jax.ShapeDtypeStruct((BATCH, SEQ, d), _F32),
        scratch_shapes=[pltpu.VMEM((CONF_HALO + MIX_TS + SUBLANES, d), _F32),
                        pltpu.VMEM((SUBLANES - 1, CONF_HALO + MIX_TS + SUBLANES, d), _F32),
                        pltpu.VMEM((MIX_TS, d), _F32)],
        compiler_params=pltpu.CompilerParams(
            dimension_semantics=("arbitrary", "arbitrary"), vmem_limit_bytes=VMEM_LIMIT),
        name="conformer_mixer",
    )(x, mod, row(norm_g), w1.astype(_BF16), row(b1), dw, row(dw_b), row(ln_g), row(ln_b),
      w2.astype(_BF16), row(b2))


def _moe_residual(y_refs, gate_ref, x1, gate2, rows):
    gates = gate_ref[...]
    y = None
    for k, yk_ref in enumerate(y_refs):
        term = gates[:, k:k + 1] * _load_packed(yk_ref, (), rows, _F32)
        y = term if y is None else y + term
    return x1 + gate2 * y


def _shortconv_kernel(y0_ref, y1_ref, y2_ref, y3_ref, gate_ref, pmod_ref, x_ref, mod_ref, ng_ref, win_ref, cw_ref,
                      wout_ref, o_ref, cbuf, vbuf):
    ts = MIX_TS
    d = D_MODEL

    @pl.when(pl.program_id(1) == 0)
    def _():
        cbuf[0:SHORT_HALO, :] = jnp.zeros((SHORT_HALO, d), _F32)

    x = _moe_residual((y0_ref, y1_ref, y2_ref, y3_ref), gate_ref, x_ref[0], pmod_ref[0][5:6], ts)
    m = mod_ref[0]
    h = _rms_mod(x, ng_ref[...], m[1:2], m[0:1])
    z = jnp.dot(h.astype(_BF16), win_ref[...], preferred_element_type=_F32)
    cbuf[SHORT_HALO:SHORT_HALO + ts, :] = z[:, d:2 * d] * z[:, 2 * d:]
    _causal_taps(cbuf, None, cw_ref, None, vbuf, SHORT_KERNEL, SHORT_HALO, ts)
    cbuf[0:SHORT_HALO, :] = cbuf[ts:ts + SHORT_HALO, :]
    y = z[:, :d] * vbuf[...]
    mix = jnp.dot(y.astype(_BF16), wout_ref[...], preferred_element_type=_F32)
    o_ref[0] = x + m[2:3] * mix


def _shortconv_mixer(pending, mod, norm_g, w_in, conv_w, w_out):
    ys, x_prev, gates, prev_mod = pending
    d = D_MODEL
    tiles_per_batch = SEQ // MIX_TS
    tiles = TOKENS // MIX_TS
    ys2 = ys.reshape(OUT_SLOTS * ROW_WORDS, LANES)
    const = lambda shape: pl.BlockSpec(shape, lambda b, s: (0,) * len(shape))
    slot_spec = lambda k: pl.BlockSpec((MIX_TS * ROW_WORDS, LANES),
                                       lambda b, s: (k * tiles + b * tiles_per_batch + s, 0))
    return pl.pallas_call(
        _shortconv_kernel,
        grid=(BATCH, SEQ // MIX_TS),
        in_specs=[
            slot_spec(0), slot_spec(1), slot_spec(2), slot_spec(3),
            pl.BlockSpec((MIX_TS, TOP_K), lambda b, s: (b * tiles_per_batch + s, 0)),
            pl.BlockSpec((1, N_MOD, d), lambda b, s: (b, 0, 0)),
            pl.BlockSpec((1, MIX_TS, d), lambda b, s: (b, s, 0)),
            pl.BlockSpec((1, N_MOD, d), lambda b, s: (b, 0, 0)),
            const((1, d)), const((d, 3 * d)), const((SHORT_KERNEL, d)), const((d, d)),
        ],
        out_specs=pl.BlockSpec((1, MIX_TS, d), lambda b, s: (b, s, 0)),
        out_shape=jax.ShapeDtypeStruct((BATCH, SEQ, d), _F32),
        scratch_shapes=[pltpu.VMEM((SHORT_HALO + MIX_TS, d), _F32), pltpu.VMEM((MIX_TS, d), _F32)],
        compiler_params=pltpu.CompilerParams(
            dimension_semantics=("arbitrary", "arbitrary"), vmem_limit_bytes=VMEM_LIMIT),
        name="shortconv_mixer",
    )(ys2, ys2, ys2, ys2, gates, prev_mod, x_prev.reshape(BATCH, SEQ, d), mod, norm_g.reshape(1, d),
      w_in.astype(_BF16), conv_w, w_out.astype(_BF16))


def _route_kernel(x_ref, mod_ref, ng_ref, rw_ref, rb_ref, h_ref, ir_ref, gate_ref, cnt_ref):
    tm = ROUTE_TM

    @pl.when(pl.program_id(0) == 0)
    def _():
        cnt_ref[...] = jnp.zeros((1, N_EXPERTS), _F32)

    x = x_ref[...]
    m = mod_ref[0]
    h = _rms_mod(x, ng_ref[...], m[4:5], m[3:4])
    _store_packed(h_ref, (), h)

    h_hi = h.astype(_BF16)
    h_lo = (h - h_hi.astype(_F32)).astype(_BF16)
    both = jnp.dot(h_hi, rw_ref[...], preferred_element_type=_F32)
    logits = (both[:, :N_EXPERTS] + both[:, N_EXPERTS:]
              + jnp.dot(h_lo, rw_ref[:, :N_EXPERTS], preferred_element_type=_F32)) + rb_ref[...]
    lane = lax.broadcasted_iota(jnp.int32, (tm, N_EXPERTS), 1).astype(_F32)
    work = logits
    vals, idxs, hots = [], [], []
    for _ in range(TOP_K):
        top = jnp.max(work, axis=1, keepdims=True)
        pick = jnp.min(jnp.where(work == top, lane, float(N_EXPERTS)), axis=1, keepdims=True)
        hot = lane == pick
        vals.append(top)
        idxs.append(pick)
        hots.append(hot)
        work = jnp.where(hot, -jnp.inf, work)
    exps = [jnp.exp(v - vals[0]) for v in vals]
    denom = exps[0] + exps[1] + exps[2] + exps[3]

    multi = (hots[0] | hots[1] | hots[2] | hots[3]).astype(_BF16)
    r_i = lax.broadcasted_iota(jnp.int32, (tm, tm), 0)
    c_i = lax.broadcasted_iota(jnp.int32, (tm, tm), 1)
    before = (c_i < r_i).astype(_BF16)
    pos = cnt_ref[...] + jnp.dot(before, multi, preferred_element_type=_F32)
    cnt_ref[...] = cnt_ref[...] + jnp.sum(multi.astype(_F32), axis=0, keepdims=True)

    col = lax.broadcasted_iota(jnp.int32, (tm, LANES), 1)
    gate_col = lax.broadcasted_iota(jnp.int32, (tm, TOP_K), 1)
    ints = jnp.zeros((tm, LANES), _F32)
    gate_out = jnp.zeros((tm, TOP_K), _F32)
    for k in range(TOP_K):
        rank_k = jnp.sum(jnp.where(hots[k], pos, 0.0), axis=1, keepdims=True)
        ints = jnp.where(col == k, idxs[k], ints)
        ints = jnp.where(col == TOP_K + k, rank_k, ints)
        gate_out = jnp.where(gate_col == k, exps[k] / denom, gate_out)
    sel = (lax.broadcasted_iota(jnp.int32, (2 * TOP_K, LANES), 0)
           == lax.broadcasted_iota(jnp.int32, (2 * TOP_K, LANES), 1)).astype(_BF16)
    high = jnp.floor(ints * (1.0 / LANES))
    low = ints - high * LANES
    nt = (((1,), (1,)), ((), ()))
    rows = (lax.dot_general(sel, high.astype(_BF16), nt, preferred_element_type=_F32) * LANES
            + lax.dot_general(sel, low.astype(_BF16), nt, preferred_element_type=_F32))
    ir_ref[...] = rows.astype(jnp.int32)
    gate_ref[...] = gate_out


def _route(x1, mod, norm_g, router_w, router_b):
    d = D_MODEL
    tm = ROUTE_TM
    tiles_per_batch = SEQ // tm
    rw_hi = router_w.astype(_BF16)
    rw_lo = (router_w - rw_hi.astype(_F32)).astype(_BF16)
    return pl.pallas_call(
        _route_kernel,
        grid=(TOKENS // tm,),
        in_specs=[
            pl.BlockSpec((tm, d), lambda i: (i, 0)),
            pl.BlockSpec((1, N_MOD, d), lambda i: (i // tiles_per_batch, 0, 0)),
            pl.BlockSpec((1, d), lambda i: (0, 0)),
            pl.BlockSpec((d, 2 * N_EXPERTS), lambda i: (0, 0)),
            pl.BlockSpec((1, N_EXPERTS), lambda i: (0, 0)),
        ],
        out_specs=[
            pl.BlockSpec((tm * ROW_WORDS, LANES), lambda i: (i, 0)),
            pl.BlockSpec((2 * TOP_K, tm), lambda i: (0, i)),
            pl.BlockSpec((tm, TOP_K), lambda i: (i, 0)),
            pl.BlockSpec((1, N_EXPERTS), lambda i: (0, 0)),
        ],
        out_shape=[
            jax.ShapeDtypeStruct((TOKENS * ROW_WORDS, LANES), _U32),
            jax.ShapeDtypeStruct((2 * TOP_K, TOKENS), jnp.int32),
            jax.ShapeDtypeStruct((TOKENS, TOP_K), _F32),
            jax.ShapeDtypeStruct((1, N_EXPERTS), _F32),
        ],
        compiler_params=pltpu.CompilerParams(
            dimension_semantics=("arbitrary",), vmem_limit_bytes=VMEM_LIMIT),
        name="route",
    )(x1, mod, norm_g.reshape(1, d), jnp.concatenate([rw_hi, rw_lo], axis=1), router_b.reshape(1, N_EXPERTS))


def _dest_kernel(ps_ref, ir_ref, o_ref):
    idx = ir_ref[0:TOP_K, :]
    dest = ir_ref[TOP_K:2 * TOP_K, :] + EXPERT_BM
    for e in range(N_EXPERTS):
        dest = dest + jnp.where(idx == e, ps_ref[e], 0)
    o_ref[...] = dest


def _dest_rows(pad_starts, idx_rank):
    tn = 4096
    return pl.pallas_call(
        _dest_kernel,
        grid_spec=pltpu.PrefetchScalarGridSpec(
            num_scalar_prefetch=1,
            grid=(TOKENS // tn,),
            in_specs=[pl.BlockSpec((2 * TOP_K, tn), lambda i, ps: (0, i))],
            out_specs=pl.BlockSpec((TOP_K, tn), lambda i, ps: (0, i)),
        ),
        out_shape=jax.ShapeDtypeStruct((TOP_K, TOKENS), jnp.int32),
        name="dest_rows",
    )(pad_starts, idx_rank)


def _routing_tables(idx_rank, counts):
    counts = counts.reshape(N_EXPERTS).astype(jnp.int32)
    padded = (counts + EXPERT_BM - 1) // EXPERT_BM * EXPERT_BM
    pad_ends = jnp.cumsum(padded)
    pad_starts = pad_ends - padded
    dest = _dest_rows(pad_starts.astype(jnp.int32), idx_rank)
    block_start = jnp.arange(N_BLOCKS, dtype=jnp.int32) * EXPERT_BM
    block_e = jnp.sum((block_start[:, None] >= pad_ends[None, :]).astype(jnp.int32), axis=1)
    block_e = jnp.minimum(block_e, N_EXPERTS - 1).astype(jnp.int32)
    n_used = (pad_ends[-1:] // EXPERT_BM).astype(jnp.int32)
    ids = jnp.arange(N_EXPERTS, dtype=jnp.int32)
    later_used = (ids[None, :] > ids[:, None]) & (counts[None, :] > 0)
    next_used = jnp.min(jnp.where(later_used, ids[None, :], N_EXPERTS), axis=1).astype(jnp.int32)
    pad_lo = (pad_starts + counts + EXPERT_BM).astype(jnp.int32)
    pad_hi = (pad_ends + EXPERT_BM).astype(jnp.int32)
    dest = dest.reshape(TOKENS * TOP_K).astype(jnp.int32)
    return dest, block_e, n_used, next_used, pad_lo, pad_hi


def _vmem_row(buf, row):
    start = row * ROW_WORDS
    return buf.at[pl.ds(start if isinstance(row, int) else pl.multiple_of(start, ROW_WORDS), ROW_WORDS)]


def _dispatch_kernel(dest_ref, lo_ref, hi_ref, h_ref, xs_ref, rs_ref, zblk, sem, zsem):
    tm = DISP_TM
    base = pl.program_id(0) * tm

    def set_virtual(r, carry):
        rs_ref[r] = TOKENS * TOP_K + r
        return carry

    def set_dump(r, carry):
        rs_ref[r] = PAD_DUMP0 + (r & (PAD_DUMP_BLOCKS * EXPERT_BM - 1))
        return carry

    @pl.when(pl.program_id(0) == 0)
    def _():
        lax.fori_loop(0, EXPERT_BM, set_virtual, 0)

    def issue(g, carry):
        toks = [g * DISP_UNROLL + u for u in range(DISP_UNROLL)]
        dests = [[dest_ref[k * TOKENS + base + t] for k in range(TOP_K)] for t in toks]
        for t, row in zip(toks, dests):
            for k, d in enumerate(row):
                pltpu.make_async_copy(_vmem_row(h_ref, t), xs_ref.at[d], sem).start(priority=k % 2)
            for k, d in enumerate(row):
                rs_ref[d] = k * TOKENS + base + t
        return carry

    lax.fori_loop(0, tm // DISP_UNROLL, issue, 0)

    @pl.when(pl.program_id(0) == pl.num_programs(0) - 1)
    def _():
        zblk[...] = jnp.zeros((EXPERT_BM * ROW_WORDS, LANES), _U32)

        def pad_copy(r):
            return pltpu.make_async_copy(_vmem_row(zblk, 0), xs_ref.at[r], zsem)

        def tail_copy(b):
            return pltpu.make_async_copy(zblk.reshape(EXPERT_BM, ROW_WORDS, LANES),
                                         xs_ref.at[pl.ds(pl.multiple_of(b * EXPERT_BM, EXPERT_BM), EXPERT_BM)], zsem)

        def pad_row(r, carry):
            pad_copy(r).start()
            return set_dump(r, carry)

        def wait_rows(n_rows):
            pltpu.make_async_copy(xs_ref.at[pl.ds(0, n_rows)], xs_ref.at[pl.ds(0, n_rows)], zsem).wait()

        first_unused = hi_ref[N_EXPERTS - 1] // EXPERT_BM
        tail_copy(0).start()
        for e in range(N_EXPERTS):
            lax.fori_loop(lo_ref[e], hi_ref[e], pad_row, 0)
        lax.fori_loop(first_unused * EXPERT_BM, N_ROWS + EXPERT_BM, set_dump, 0)
        lax.fori_loop(first_unused, N_BLOCKS + 1, lambda b, c: (tail_copy(b).start(), c)[1], 0)
        for e in range(N_EXPERTS):
            n_pad = hi_ref[e] - lo_ref[e]
            for bit in range(EXPERT_BM.bit_length() - 1):
                pl.when((n_pad >> bit) & 1 == 1)(functools.partial(wait_rows, 1 << bit))
        lax.fori_loop(first_unused, N_BLOCKS + 1, lambda b, c: (tail_copy(b).wait(), c)[1], 0)
        tail_copy(0).wait()

    for _ in range(TOP_K):
        pltpu.make_async_copy(h_ref.reshape(tm, ROW_WORDS, LANES), xs_ref.at[pl.ds(0, tm)], sem).wait()


def _dispatch(dest, pad_lo, pad_hi, h_rows):
    tm = DISP_TM
    return pl.pallas_call(
        _dispatch_kernel,
        grid_spec=pltpu.PrefetchScalarGridSpec(
            num_scalar_prefetch=3,
            grid=(TOKENS // tm,),
            in_specs=[pl.BlockSpec((tm * ROW_WORDS, LANES), lambda i, *_: (i, 0))],
            out_specs=[pl.BlockSpec(memory_space=pl.ANY), pl.BlockSpec(memory_space=pltpu.SMEM)],
            scratch_shapes=[pltpu.VMEM((EXPERT_BM * ROW_WORDS, LANES), _U32),
                            pltpu.SemaphoreType.DMA, pltpu.SemaphoreType.DMA],
        ),
        out_shape=[jax.ShapeDtypeStruct((N_ROWS + EXPERT_BM, ROW_WORDS, LANES), _U32),
                   jax.ShapeDtypeStruct((N_ROWS + EXPERT_BM,), jnp.int32)],
        compiler_params=pltpu.CompilerParams(
            dimension_semantics=("arbitrary",), vmem_limit_bytes=VMEM_LIMIT),
        name="dispatch",
    )(dest, pad_lo, pad_hi, h_rows)


def _expert_kernel(be_ref, nu_ref, nx_ref, rs_ref, xs_ref, w1_hbm, b1_ref, w2_hbm, b2_ref, ys_ref,
                   w1f, w2f, w1s, w2s, obuf, wsem, osem, isem, nsw, *, layer):
    bm = EXPERT_BM
    b = pl.program_id(0)
    e = be_ref[b]
    n_used = nu_ref[0]
    half = MXU_COLS // 2

    def fetch(expert, slot):
        return (pltpu.make_async_copy(w1_hbm.at[layer, expert], w1f.at[slot], wsem.at[0, slot]),
                pltpu.make_async_copy(w2_hbm.at[layer, expert], w2f.at[slot], wsem.at[1, slot]))

    def scatter_start(step, slot):
        for i in range(bm):
            pltpu.make_async_copy(_vmem_row(obuf.at[slot], i), ys_ref.at[rs_ref[step * bm + i]],
                                  osem.at[slot]).start(priority=i % 2)

    def block_view(slot):
        return obuf.at[slot].reshape(bm, ROW_WORDS, LANES)

    def scatter_wait(slot):
        pltpu.make_async_copy(block_view(slot), ys_ref.at[pl.ds(0, bm)], osem.at[slot]).wait()

    @pl.when(b == 0)
    def _():
        nsw[0] = 0
        obuf[RING - 1] = jnp.zeros((bm * ROW_WORDS, LANES), _U32)
        for blk in range(PAD_DUMP_BLOCKS):
            pltpu.make_async_copy(block_view(RING - 1), ys_ref.at[pl.ds(PAD_DUMP0 + blk * bm, bm)], isem).start()
        for blk in range(PAD_DUMP_BLOCKS):
            pltpu.make_async_copy(block_view(RING - 1), ys_ref.at[pl.ds(PAD_DUMP0 + blk * bm, bm)], isem).wait()
        for virtual in range(2, RING + 1):
            pltpu.make_async_copy(
                block_view(RING - 1), ys_ref.at[pl.ds(TOKENS * TOP_K + (virtual - 1) * bm, bm)],
                osem.at[RING - virtual]).start()
        for copy in fetch(e, 0):
            copy.start()

    @pl.when(b == n_used)
    def _():
        scatter_start(b, (b + RING - 1) % RING)
        for slot in range(RING):
            scatter_wait(slot)

    @pl.when(b < n_used)
    def _():
        out_slot = b % RING
        scatter_wait(out_slot)

        @pl.when(jnp.logical_or(b == 0, be_ref[jnp.maximum(b - 1, 0)] != e))
        def _():
            slot = nsw[0] % 2
            for copy in fetch(e, slot):
                copy.wait()
            src = lax.broadcasted_iota(jnp.int32, (MXU_COLS, MXU_COLS), 0)
            dst = lax.broadcasted_iota(jnp.int32, (MXU_COLS, MXU_COLS), 1)
            want = jnp.where(dst < half, 2 * dst, 2 * (dst - half) + 1)
            perm = (src == want).astype(_BF16)
            for blk in range(2 * D_EXPERT // MXU_COLS):
                cs = slice(blk * MXU_COLS, (blk + 1) * MXU_COLS)
                w1s[:, cs] = jnp.dot(w1f[slot, :, cs].astype(_BF16), perm,
                                     preferred_element_type=_F32).astype(_BF16)
            w2s[...] = w2f[slot].astype(_BF16)
            nsw[0] = nsw[0] + 1
            nxt = nx_ref[e]

            @pl.when(nxt < N_EXPERTS)
            def _():
                for copy in fetch(nxt, 1 - slot):
                    copy.start()

        scatter_start(b, (b + RING - 1) % RING)
        x = _load_packed(xs_ref, (), bm, _BF16)
        u = jnp.dot(x, w1s[...], preferred_element_type=_F32) + b1_ref[0, e]
        acts = []
        for blk in range(2 * D_EXPERT // MXU_COLS):
            glu = jnp.minimum(u[:, blk * MXU_COLS:blk * MXU_COLS + half], SWIGLU_LIMIT)
            lin = jnp.clip(u[:, blk * MXU_COLS + half:(blk + 1) * MXU_COLS], -SWIGLU_LIMIT, SWIGLU_LIMIT)
            acts.append((glu * _sigmoid(SWIGLU_ALPHA * glu) * (lin + 1.0)).astype(_BF16))
        hidden = jnp.concatenate(acts, axis=1)
        y = jnp.dot(hidden, w2s[...], preferred_element_type=_F32) + b2_ref[0, e]
        _store_packed(obuf, (out_slot,), y)


def _experts(layer, block_e, n_used, next_used, row_slot, xs, w1, b1_grouped, w2, b2):
    bm = EXPERT_BM
    d, f = D_MODEL, D_EXPERT
    blk = lambda b, be, nu, nx, rs: jnp.minimum(b, nu[0] - 1)
    return pl.pallas_call(
        functools.partial(_expert_kernel, layer=layer),
        grid_spec=pltpu.PrefetchScalarGridSpec(
            num_scalar_prefetch=4,
            grid=(N_BLOCKS,),
            in_specs=[
                pl.BlockSpec((bm * ROW_WORDS, LANES), lambda *a: (blk(*a) + 1, 0)),
                pl.BlockSpec(memory_space=pl.ANY),
                pl.BlockSpec((1, N_EXPERTS, 1, 2 * f), lambda *a: (layer, 0, 0, 0)),
                pl.BlockSpec(memory_space=pl.ANY),
                pl.BlockSpec((1, N_EXPERTS, 1, d), lambda *a: (layer, 0, 0, 0)),
            ],
            out_specs=pl.BlockSpec(memory_space=pl.ANY),
            scratch_shapes=[
                pltpu.VMEM((2, d, 2 * f), _F32), pltpu.VMEM((2, f, d), _F32),
                pltpu.VMEM((d, 2 * f), _BF16), pltpu.VMEM((f, d), _BF16),
                pltpu.VMEM((RING, bm * ROW_WORDS, LANES), _U32),
                pltpu.SemaphoreType.DMA((2, 2)), pltpu.SemaphoreType.DMA((RING,)), pltpu.SemaphoreType.DMA,
                pltpu.SMEM((1,), jnp.int32),
            ],
        ),
        out_shape=jax.ShapeDtypeStruct((OUT_SLOTS, ROW_WORDS, LANES), _U32),
        compiler_params=pltpu.CompilerParams(
            dimension_semantics=("arbitrary",), vmem_limit_bytes=VMEM_LIMIT),
        name="experts",
    )(block_e, n_used, next_used, row_slot, xs.reshape((N_ROWS + EXPERT_BM) * ROW_WORDS, LANES), w1, b1_grouped,
      w2, b2)


def _group_glu_columns(b1):
    half = MXU_COLS // 2
    j = jnp.arange(MXU_COLS)
    want = jnp.where(j < half, 2 * j, 2 * (j - half) + 1)
    cols = (jnp.arange(2 * D_EXPERT // MXU_COLS)[:, None] * MXU_COLS + want[None, :]).reshape(-1)
    return b1[..., cols]


def _combine_kernel(y0_ref, y1_ref, y2_ref, y3_ref, x_ref, gate_ref, mod_ref, fg_ref, o_ref):
    out = _moe_residual((y0_ref, y1_ref, y2_ref, y3_ref), gate_ref, x_ref[...], mod_ref[0][5:6], COMB_TM)
    o_ref[...] = out * lax.rsqrt(jnp.mean(out * out, axis=-1, keepdims=True) + NORM_EPS) * fg_ref[...]


def _combine_final(ys, x1, gates, mod, final_g):
    tm = COMB_TM
    d = D_MODEL
    tiles = TOKENS // tm
    tiles_per_batch = SEQ // tm
    ys2 = ys.reshape(OUT_SLOTS * ROW_WORDS, LANES)
    slot_spec = lambda k: pl.BlockSpec((tm * ROW_WORDS, LANES), lambda i: (k * tiles + i, 0))
    return pl.pallas_call(
        _combine_kernel,
        grid=(tiles,),
        in_specs=[
            slot_spec(0), slot_spec(1), slot_spec(2), slot_spec(3),
            pl.BlockSpec((tm, d), lambda i: (i, 0)),
            pl.BlockSpec((tm, TOP_K), lambda i: (i, 0)),
            pl.BlockSpec((1, N_MOD, d), lambda i: (i // tiles_per_batch, 0, 0)),
            pl.BlockSpec((1, d), lambda i: (0, 0)),
        ],
        out_specs=pl.BlockSpec((tm, d), lambda i: (i, 0)),
        out_shape=jax.ShapeDtypeStruct((TOKENS, d), _F32),
        compiler_params=pltpu.CompilerParams(
            dimension_semantics=("arbitrary",), vmem_limit_bytes=VMEM_LIMIT),
        name="combine_final",
    )(ys2, ys2, ys2, ys2, x1, gates, mod, final_g.reshape(1, d))


def kernel(x, c, mod_w, mod_b, norm_g, conf_w1, conf_b1, conf_dw, conf_dw_b, conf_ln_g, conf_ln_b, conf_w2, conf_b2, sc_w_in, sc_conv, sc_w_out, router_w, router_b, exp_w1, exp_b1, exp_w2, exp_b2, final_g):
    mods = _modulation(c, mod_w, mod_b).reshape(DEPTH, BATCH, N_MOD, D_MODEL)
    b1_grouped = _group_glu_columns(exp_b1).reshape(DEPTH, N_EXPERTS, 1, 2 * D_EXPERT)
    b2_rows = exp_b2.reshape(DEPTH, N_EXPERTS, 1, D_MODEL)
    assert DEPTH == 2, "layer 0's MoE residual is applied inside layer 1's mixer kernel"
    pending = None
    for i in range(DEPTH):
        mod = mods[i]
        j = i // 2
        if i % 2 == 0:
            x1 = _conformer_mixer(x, mod, norm_g[i, 0], conf_w1[j], conf_b1[j], conf_dw[j], conf_dw_b[j],
                                  conf_ln_g[j], conf_ln_b[j], conf_w2[j], conf_b2[j])
        else:
            x1 = _shortconv_mixer(pending, mod, norm_g[i, 0], sc_w_in[j], sc_conv[j], sc_w_out[j])
        x1 = x1.reshape(TOKENS, D_MODEL)
        h_rows, idx_rank, gates, counts = _route(x1, mod, norm_g[i, 1], router_w[i], router_b[i])
        dest, block_e, n_used, next_used, pad_lo, pad_hi = _routing_tables(idx_rank, counts)
        xs, row_slot = _dispatch(dest, pad_lo, pad_hi, h_rows)
        ys = _experts(i, block_e, n_used, next_used, row_slot, xs, exp_w1, b1_grouped, exp_w2, b2_rows)
        pending = (ys, x1, gates, mod)
    out = _combine_final(*pending, final_g)
    return out.reshape(BATCH, SEQ, D_MODEL)
```
